```python
import jax, jax.numpy as jnp
from jax import lax
import numpy as np

D_MODEL = 1024
BATCH = 8
SEQ = 2048
DEPTH = 1
DEC_BATCH = 128
DEC_SEQ = 1
PAST_LEN = 16384
PAGE_SIZE = 128

CONV_DIM = D_MODEL // 2
CONV_WIDTH = 31
RET_HEADS = 4
RET_QK_DIM = 128
RET_V_DIM = 256
RET_QK_WIDTH = RET_HEADS * RET_QK_DIM
RET_V_WIDTH = RET_HEADS * RET_V_DIM
RET_CHUNK = 128
FFN_DIM = 4 * D_MODEL
ROPE_BASE = 10000.0
EPS = 1e-6
IN_SIZES = (CONV_DIM, CONV_DIM, RET_QK_WIDTH, RET_QK_WIDTH, RET_V_WIDTH, RET_V_WIDTH, D_MODEL, D_MODEL)
IN_COLS = sum(IN_SIZES)
IN_SPLITS = [int(s) for s in np.cumsum(IN_SIZES)[:-1]]

kernel_name = "gated_conformer_retention_decoder_step"


def rms_norm(x, w):
    xf = x.astype(jnp.float32)
    y = xf * lax.rsqrt(jnp.mean(xf * xf, axis=-1, keepdims=True) + EPS)
    return (y * w.astype(jnp.float32)).astype(x.dtype)


def layer_norm(x, w, b):
    xf = x.astype(jnp.float32)
    mu = jnp.mean(xf, axis=-1, keepdims=True)
    xc = xf - mu
    y = xc * lax.rsqrt(jnp.mean(xc * xc, axis=-1, keepdims=True) + EPS)
    return (y * w.astype(jnp.float32) + b.astype(jnp.float32)).astype(x.dtype)


def rotary(x, pos):
    half = x.shape[-1] // 2
    freqs = 1.0 / (ROPE_BASE ** jnp.linspace(0.0, 1.0, half, dtype=jnp.float32))
    ang = pos[:, None] * freqs[None, :]
    c = jnp.cos(ang)[None, :, None, :]
    s = jnp.sin(ang)[None, :, None, :]
    x1, x2 = x[..., :half], x[..., half:]
    return jnp.concatenate([x1 * c - x2 * s, x2 * c + x1 * s], axis=-1)


def log_gammas():
    return jnp.log1p(-jnp.exp2(-5.0 - jnp.arange(RET_HEADS, dtype=jnp.float32)))


def retention_chunk(S, q, k, v):
    C = q.shape[1]
    lg = log_gammas()
    idx = jnp.arange(C, dtype=jnp.float32)
    diff = idx[:, None] - idx[None, :]
    causal = diff >= 0
    expo = jnp.where(causal, diff, 0.0)
    decay = jnp.where(causal[None], jnp.exp(expo[None] * lg[:, None, None]), 0.0)
    scores = jnp.einsum('bihd,bjhd->bhij', q, k) * decay[None]
    intra = jnp.einsum('bhij,bjhe->bihe', scores, v)
    q_decay = jnp.exp((idx + 1.0)[:, None] * lg[None, :])
    cross = jnp.einsum('bihd,bhde->bihe', q, S) * q_decay[None, :, :, None]
    k_decay = jnp.exp((C - 1.0 - idx)[:, None] * lg[None, :])
    S_new = jnp.exp(C * lg)[None, :, None, None] * S + jnp.einsum(
        'bjhd,bjhe->bhde', k * k_decay[None, :, :, None], v)
    return S_new, intra + cross


def retention_scan(q, k, v, S0):
    B, T = q.shape[0], q.shape[1]
    chunk = RET_CHUNK if T % RET_CHUNK == 0 else T
    n = T // chunk

    def to_chunks(a):
        return jnp.moveaxis(a.reshape(B, n, chunk, a.shape[2], a.shape[3]), 1, 0)

    def step(S, qkv):
        qc, kc, vc = qkv
        return retention_chunk(S, qc, kc, vc)

    S_fin, o = lax.scan(step, S0, (to_chunks(q), to_chunks(k), to_chunks(v)))
    o = jnp.moveaxis(o, 0, 1).reshape(B, T, RET_HEADS, RET_V_DIM)
    return S_fin, o


def mixer(xn, conv_buf, ret_state, pos, w_in, w_dw, b_dw, conv_ln_w, conv_ln_b,
          w_conv_out, w_ret_out, w_o):
    B, T, _ = xn.shape
    proj = xn @ w_in
    a_val, a_gate, q, k, v, g, gate_a, gate_b = jnp.split(proj, IN_SPLITS, axis=-1)

    u = a_val * jax.nn.sigmoid(a_gate)
    full = jnp.concatenate([conv_buf.astype(u.dtype), u], axis=1)
    conv = lax.conv_general_dilated(
        full, w_dw[:, None, :], window_strides=(1,), padding='VALID',
        dimension_numbers=('NWC', 'WIO', 'NWC'), feature_group_count=CONV_DIM) + b_dw
    new_buf = full[:, full.shape[1] - (CONV_WIDTH - 1):]
    a_out = jax.nn.silu(layer_norm(conv, conv_ln_w, conv_ln_b)) @ w_conv_out

    qf = rotary(q.reshape(B, T, RET_HEADS, RET_QK_DIM).astype(jnp.float32), pos)
    kf = rotary(k.reshape(B, T, RET_HEADS, RET_QK_DIM).astype(jnp.float32), pos) * (RET_QK_DIM ** -0.5)
    vf = v.reshape(B, T, RET_HEADS, RET_V_DIM).astype(jnp.float32)
    S_new, o = retention_scan(qf, kf, vf, ret_state.astype(jnp.float32))
    o = o * lax.rsqrt(jnp.mean(o * o, axis=-1, keepdims=True) + EPS)
    o = o.reshape(B, T, RET_V_WIDTH).astype(xn.dtype)
    b_out = (jax.nn.silu(g) * o) @ w_ret_out

    merged = jax.nn.sigmoid(gate_a) * a_out + jax.nn.sigmoid(gate_b) * b_out
    return merged @ w_o, new_buf, S_new


def block(x, conv_buf, ret_state, pos, p):
    (n_mix_pre, n_mix_post, w_in, w_dw, b_dw, conv_ln_w, conv_ln_b, w_conv_out,
     w_ret_out, w_o, n_ffn_pre, n_ffn_post, w_up, w_down) = p
    m, new_buf, S_new = mixer(rms_norm(x, n_mix_pre), conv_buf, ret_state, pos, w_in, w_dw, b_dw,
                              conv_ln_w, conv_ln_b, w_conv_out, w_ret_out, w_o)
    h = x + rms_norm(m, n_mix_post)
    f = jnp.square(jax.nn.relu(rms_norm(h, n_ffn_pre) @ w_up)) @ w_down
    return h + rms_norm(f, n_ffn_post), new_buf, S_new


def setup_inputs(seed: int = 0) -> dict:
    key = jax.random.key(seed)
    ks = jax.random.split(key, 20)
    f32 = jnp.float32

    def nrm(k, shape, scale):
        return jax.random.normal(k, shape, f32) * scale

    def gain(k, n):
        return 1.0 + nrm(k, (DEPTH, n), 0.02)

    return {
        "x_prompt": nrm(ks[0], (BATCH, SEQ, D_MODEL), 1.0),
        "x_sample": nrm(ks[1], (DEC_BATCH, DEC_SEQ, D_MODEL), 1.0),
        "cache_conv": nrm(ks[2], (DEPTH, DEC_BATCH, CONV_WIDTH - 1, CONV_DIM), 0.5),
        "state_ret": nrm(ks[3], (DEPTH, DEC_BATCH, RET_HEADS, RET_QK_DIM, RET_V_DIM), 0.5),
        "norm_mix_pre": gain(ks[4], D_MODEL),
        "norm_mix_post": gain(ks[5], D_MODEL),
        "w_in": nrm(ks[6], (DEPTH, D_MODEL, IN_COLS), D_MODEL ** -0.5),
        "w_dw": nrm(ks[7], (DEPTH, CONV_WIDTH, CONV_DIM), CONV_WIDTH ** -0.5),
        "b_dw": nrm(ks[8], (DEPTH, CONV_DIM), 0.01),
        "conv_ln_w": gain(ks[9], CONV_DIM),
        "conv_ln_b": nrm(ks[10], (DEPTH, CONV_DIM), 0.01),
        "w_conv_out": nrm(ks[11], (DEPTH, CONV_DIM, D_MODEL), CONV_DIM ** -0.5),
        "w_ret_out": nrm(ks[12], (DEPTH, RET_V_WIDTH, D_MODEL), RET_V_WIDTH ** -0.5),
        "w_o": nrm(ks[13], (DEPTH, D_MODEL, D_MODEL), D_MODEL ** -0.5),
        "norm_ffn_pre": gain(ks[14], D_MODEL),
        "norm_ffn_post": gain(ks[15], D_MODEL),
        "w_ffn_up": nrm(ks[16], (DEPTH, D_MODEL, FFN_DIM), D_MODEL ** -0.5),
        "w_ffn_down": nrm(ks[17], (DEPTH, FFN_DIM, D_MODEL), FFN_DIM ** -0.5),
    }


def reference(x_prompt, x_sample, cache_conv, state_ret, norm_mix_pre, norm_mix_post, w_in,
              w_dw, b_dw, conv_ln_w, conv_ln_b, w_conv_out, w_ret_out, w_o, norm_ffn_pre,
              norm_ffn_post, w_ffn_up, w_ffn_down):
    B, T = x_prompt.shape[0], x_prompt.shape[1]
    Bs, Ts = x_sample.shape[0], x_sample.shape[1]
    pos_p = jnp.arange(T, dtype=jnp.float32)
    pos_s = PAST_LEN + jnp.arange(Ts, dtype=jnp.float32)
    zero_buf = jnp.zeros((B, CONV_WIDTH - 1, CONV_DIM), x_prompt.dtype)
    zero_S = jnp.zeros((B, RET_HEADS, RET_QK_DIM, RET_V_DIM), jnp.float32)

    xp, xs = x_prompt, x_sample
    conv_p, ret_p, conv_s, ret_s = [], [], [], []
    for l in range(DEPTH):
        p = (norm_mix_pre[l], norm_mix_post[l], w_in[l], w_dw[l], b_dw[l], conv_ln_w[l],
             conv_ln_b[l], w_conv_out[l], w_ret_out[l], w_o[l], norm_ffn_pre[l],
             norm_ffn_post[l], w_ffn_up[l], w_ffn_down[l])
        xp, cb, sp = block(xp, zero_buf, zero_S, pos_p, p)
        xs, cbs, ss = block(xs, cache_conv[l], state_ret[l], pos_s, p)
        conv_p.append(cb)
        ret_p.append(sp)
        conv_s.append(cbs)
        ret_s.append(ss)

    new_conv_prompt = jnp.stack(conv_p)
    new_ret_prompt = jnp.stack(ret_p)
    new_conv_sample = jnp.stack(conv_s)
    new_ret_sample = jnp.stack(ret_s)
    return (xp, xs, new_conv_prompt, new_ret_prompt, new_conv_sample, new_ret_sample)
```

```python
import functools
import math

import jax
import jax.numpy as jnp
import numpy as np
from jax import lax
from jax.experimental import pallas as pl
from jax.experimental.pallas import tpu as pltpu

F32 = jnp.float32
BF16 = jnp.bfloat16

D_MODEL = 1024
CONV_DIM = 512
CONV_WIDTH = 31
HALO = CONV_WIDTH - 1
HEADS = 4
DK = 128
DV = 256
QK_W = HEADS * DK
V_W = HEADS * DV
CHUNK = 128
FFN_DIM = 4 * D_MODEL
EPS = 1e-6
ROPE_BASE = 10000.0
PAST_LEN = 16384
K_SCALE = DK ** -0.5

C_CONV = 0
C_Q = 2 * CONV_DIM
C_K = C_Q + QK_W
C_V = C_K + QK_W
C_G = C_V + V_W
C_GATES = C_G + V_W
IN_COLS = C_GATES + 2 * D_MODEL

LOG_GAMMA = tuple(math.log1p(-(2.0 ** (-5 - h))) for h in range(HEADS))
GAMMA = tuple(math.exp(lg) for lg in LOG_GAMMA)
GAMMA_CHUNK = tuple(math.exp(CHUNK * lg) for lg in LOG_GAMMA)

VMEM_LIMIT_BYTES = 56 * 1024 * 1024
HALO_PAD = 32
MIX_ROWS = 256
FFN_ROWS = 256
CONV_ROWS = 32
SAMPLE_BLOCK = 8

NT_DIMS = (((1,), (1,)), ((), ()))
TN_DIMS = (((0,), (0,)), ((), ()))


def _dot(a, b):
    return jnp.dot(a, b, preferred_element_type=F32)


def _rms(x, w):
    return x * lax.rsqrt(jnp.mean(x * x, axis=-1, keepdims=True) + EPS) * w


def _layer_norm(x, w, b):
    mu = jnp.mean(x, axis=-1, keepdims=True)
    xc = x - mu
    return xc * lax.rsqrt(jnp.mean(xc * xc, axis=-1, keepdims=True) + EPS) * w + b


def _silu(x):
    return x * jax.nn.sigmoid(x)


def _group_norm(o):
    return o * lax.rsqrt(jnp.mean(o * o, axis=-1, keepdims=True) + EPS)


def _const_spec(shape):
    zeros = (0,) * len(shape)
    return pl.BlockSpec(shape, lambda *_: zeros, pipeline_mode=pl.Buffered(1))


def _mixer_prompt_kernel(x_ref, cos_ref, sin_ref, npre_ref, npost_ref, win_ref, wdw_ref,
                         bdw_ref, lnw_ref, lnb_ref, wco_ref, wro_ref, wo_ref,
                         h_ref, nc_ref, s_ref,
                         fb_ref, conv_ref, y_ref, dec_ref, qd_ref, kd_ref):
    i = pl.program_id(1)
    rows = x_ref.shape[1]

    @pl.when(i == 0)
    def _init():
        fb_ref[0:HALO_PAD, :] = jnp.zeros((HALO_PAD, CONV_DIM), F32)
        s_ref[...] = jnp.zeros(s_ref.shape, F32)
        ii = lax.broadcasted_iota(jnp.int32, (CHUNK, CHUNK), 0)
        jj = lax.broadcasted_iota(jnp.int32, (CHUNK, CHUNK), 1)
        diff = (ii - jj).astype(F32)
        row_v = lax.broadcasted_iota(jnp.int32, (CHUNK, DV), 0).astype(F32)
        row_k = ii.astype(F32)
        for h in range(HEADS):
            lg = LOG_GAMMA[h]
            dec_ref[h] = jnp.where(diff >= 0.0, jnp.exp(jnp.maximum(diff, 0.0) * lg), 0.0)
            qd_ref[h] = jnp.exp((row_v + 1.0) * lg)
            kd_ref[h] = jnp.exp((CHUNK - 1.0 - row_k) * lg)

    x = x_ref[0]
    xn = _rms(x, npre_ref[...]).astype(BF16)

    pc = _dot(xn, win_ref[:, C_CONV:C_CONV + 2 * CONV_DIM])
    fb_ref[HALO_PAD:HALO_PAD + rows, :] = pc[:, :CONV_DIM] * jax.nn.sigmoid(pc[:, CONV_DIM:])
    first = HALO_PAD - HALO
    for r in range(rows // CONV_ROWS):
        base = first + r * CONV_ROWS
        acc = jnp.broadcast_to(bdw_ref[...], (CONV_ROWS, CONV_DIM))
        for j in range(CONV_WIDTH):
            acc = acc + fb_ref[base + j:base + j + CONV_ROWS, :] * wdw_ref[j:j + 1, :]
        conv_ref[r * CONV_ROWS:(r + 1) * CONV_ROWS, :] = acc

    @pl.when(i == pl.num_programs(1) - 1)
    def _emit_conv_state():
        nc_ref[0] = fb_ref[HALO_PAD + rows - HALO:HALO_PAD + rows, :]

    fb_ref[0:HALO_PAD, :] = fb_ref[rows:rows + HALO_PAD, :]
    a_act = _silu(_layer_norm(conv_ref[...], lnw_ref[...], lnb_ref[...])).astype(BF16)
    a_out = _dot(a_act, wco_ref[...])

    cos2 = cos_ref[...]
    sin2 = sin_ref[...]

    def rot(t):
        return t * cos2 + pltpu.roll(t, DK // 2, 1) * sin2

    pqk = _dot(xn, win_ref[:, C_Q:C_Q + 2 * QK_W])
    pv = _dot(xn, win_ref[:, C_V:C_V + V_W]).astype(BF16)
    pg = _dot(xn, win_ref[:, C_G:C_G + V_W])
    for h in range(HEADS):
        qh = rot(pqk[:, h * DK:(h + 1) * DK]).astype(BF16)
        kh = rot(pqk[:, QK_W + h * DK:QK_W + (h + 1) * DK]) * K_SCALE
        vh = pv[:, h * DV:(h + 1) * DV]
        gh = _silu(pg[:, h * DV:(h + 1) * DV])
        for c in range(rows // CHUNK):
            rs = slice(c * CHUNK, (c + 1) * CHUNK)
            qc, kc, vc = qh[rs], kh[rs], vh[rs]
            state = s_ref[0, h]
            scores = lax.dot_general(qc, kc.astype(BF16), NT_DIMS,
                                     preferred_element_type=F32) * dec_ref[h]
            o = _dot(scores.astype(BF16), vc) + _dot(qc, state.astype(BF16)) * qd_ref[h]
            s_ref[0, h] = GAMMA_CHUNK[h] * state + lax.dot_general(
                (kc * kd_ref[h]).astype(BF16), vc, TN_DIMS, preferred_element_type=F32)
            y_ref[rs, h * DV:(h + 1) * DV] = (gh[rs] * _group_norm(o)).astype(BF16)
    b_out = _dot(y_ref[...], wro_ref[...])

    pgate = _dot(xn, win_ref[:, C_GATES:C_GATES + 2 * D_MODEL])
    merged = (jax.nn.sigmoid(pgate[:, :D_MODEL]) * a_out
              + jax.nn.sigmoid(pgate[:, D_MODEL:]) * b_out)
    m = _dot(merged.astype(BF16), wo_ref[...])
    h_ref[0] = x + _rms(m, npost_ref[...])


def _mixer_prompt(x, cos2, sin2, npre, npost, win, wdw, bdw, lnw, lnb, wco, wro, wo):
    batch, seq, _ = x.shape
    rows = MIX_ROWS
    grid = (batch, seq // rows)
    in_specs = [
        pl.BlockSpec((1, rows, D_MODEL), lambda b, i: (b, i, 0)),
        pl.BlockSpec((rows, DK), lambda b, i: (i, 0)),
        pl.BlockSpec((rows, DK), lambda b, i: (i, 0)),
        _const_spec((1, D_MODEL)),
        _const_spec((1, D_MODEL)),
        _const_spec((D_MODEL, IN_COLS)),
        _const_spec((CONV_WIDTH, CONV_DIM)),
        _const_spec((1, CONV_DIM)),
        _const_spec((1, CONV_DIM)),
        _const_spec((1, CONV_DIM)),
        _const_spec((CONV_DIM, D_MODEL)),
        _const_spec((V_W, D_MODEL)),
        _const_spec((D_MODEL, D_MODEL)),
    ]
    out_specs = [
        pl.BlockSpec((1, rows, D_MODEL), lambda b, i: (b, i, 0)),
        pl.BlockSpec((1, HALO, CONV_DIM), lambda b, i: (b, 0, 0)),
        pl.BlockSpec((1, HEADS, DK, DV), lambda b, i: (b, 0, 0, 0)),
    ]
    out_shape = [
        jax.ShapeDtypeStruct((batch, seq, D_MODEL), F32),
        jax.ShapeDtypeStruct((batch, HALO, CONV_DIM), F32),
        jax.ShapeDtypeStruct((batch, HEADS, DK, DV), F32),
    ]
    scratch = [
        pltpu.VMEM((HALO_PAD + rows, CONV_DIM), F32),
        pltpu.VMEM((rows, CONV_DIM), F32),
        pltpu.VMEM((rows, V_W), BF16),
        pltpu.VMEM((HEADS, CHUNK, CHUNK), F32),
        pltpu.VMEM((HEADS, CHUNK, DV), F32),
        pltpu.VMEM((HEADS, CHUNK, DK), F32),
    ]
    return pl.pallas_call(
        _mixer_prompt_kernel,
        grid=grid, in_specs=in_specs, out_specs=out_specs, out_shape=out_shape,
        scratch_shapes=scratch,
        compiler_params=pltpu.CompilerParams(
            dimension_semantics=("parallel", "arbitrary"),
            vmem_limit_bytes=VMEM_LIMIT_BYTES),
        name="mixer_prompt",
    )(x, cos2, sin2, npre, npost, win, wdw, bdw, lnw, lnb, wco, wro, wo)


def _ffn_body(h, npre, npost, wup_ref, wdn_ref):
    hn = _rms(h, npre).astype(BF16)
    up = _dot(hn, wup_ref[...])
    act = jnp.square(jnp.maximum(up, 0.0)).astype(BF16)
    return h + _rms(_dot(act, wdn_ref[...]), npost)


def _ffn_kernel(h_ref, npre_ref, npost_ref, wup_ref, wdn_ref, o_ref):
    o_ref[...] = _ffn_body(h_ref[...], npre_ref[...], npost_ref[...], wup_ref, wdn_ref)


def _ffn(h, npre, npost, wup, wdn):
    n = h.shape[0]
    rows = FFN_ROWS
    return pl.pallas_call(
        _ffn_kernel,
        grid=(n // rows,),
        in_specs=[
            pl.BlockSpec((rows, D_MODEL), lambda i: (i, 0)),
            _const_spec((1, D_MODEL)),
            _const_spec((1, D_MODEL)),
            _const_spec((D_MODEL, FFN_DIM)),
            _const_spec((FFN_DIM, D_MODEL)),
        ],
        out_specs=pl.BlockSpec((rows, D_MODEL), lambda i: (i, 0)),
        out_shape=jax.ShapeDtypeStruct((n, D_MODEL), F32),
        compiler_params=pltpu.CompilerParams(
            dimension_semantics=("parallel",),
            vmem_limit_bytes=VMEM_LIMIT_BYTES),
        name="ffn_prompt",
    )(h, npre, npost, wup, wdn)


def _sample_pre_kernel(x_ref, cos_ref, sin_ref, npre_ref, win_ref, wqkt_ref,
                       u_ref, qt_ref, kt_ref, v_ref, g_ref, ga_ref, gb_ref):
    xn = _rms(x_ref[...], npre_ref[...]).astype(BF16)
    pc = _dot(xn, win_ref[:, C_CONV:C_CONV + 2 * CONV_DIM])
    u_ref[...] = pc[:, :CONV_DIM] * jax.nn.sigmoid(pc[:, CONV_DIM:])
    v_ref[...] = _dot(xn, win_ref[:, C_V:C_V + V_W])
    g_ref[...] = _dot(xn, win_ref[:, C_G:C_G + V_W])
    pgate = _dot(xn, win_ref[:, C_GATES:C_GATES + 2 * D_MODEL])
    ga_ref[...] = pgate[:, :D_MODEL]
    gb_ref[...] = pgate[:, D_MODEL:]
    qkt = lax.dot_general(wqkt_ref[...], xn, NT_DIMS, preferred_element_type=F32)
    cos_t = cos_ref[...]
    sin_t = sin_ref[...]
    half = DK // 2
    for g in range(2 * HEADS):
        x1 = qkt[g * DK:g * DK + half]
        x2 = qkt[g * DK + half:(g + 1) * DK]
        o1 = x1 * cos_t - x2 * sin_t
        o2 = x2 * cos_t + x1 * sin_t
        if g < HEADS:
            qt_ref[g * DK:g * DK + half, :] = o1
            qt_ref[g * DK + half:(g + 1) * DK, :] = o2
        else:
            k0 = (g - HEADS) * DK
            kt_ref[k0:k0 + half, :] = o1 * K_SCALE
            kt_ref[k0 + half:k0 + DK, :] = o2 * K_SCALE


def _sample_pre(x, cos_t, sin_t, npre, win, wqkt):
    n = x.shape[0]
    out_shape = [
        jax.ShapeDtypeStruct((n, CONV_DIM), F32),
        jax.ShapeDtypeStruct((QK_W, n), F32),
        jax.ShapeDtypeStruct((QK_W, n), F32),
        jax.ShapeDtypeStruct((n, V_W), F32),
        jax.ShapeDtypeStruct((n, V_W), F32),
        jax.ShapeDtypeStruct((n, D_MODEL), F32),
        jax.ShapeDtypeStruct((n, D_MODEL), F32),
    ]
    return pl.pallas_call(
        _sample_pre_kernel, out_shape=out_shape,
        compiler_params=pltpu.CompilerParams(vmem_limit_bytes=VMEM_LIMIT_BYTES),
        name="sample_pre",
    )(x, cos_t, sin_t, npre, win, wqkt)


def _sample_state_kernel(cache_ref, u_ref, wdw_ref, bdw_ref, qt_ref, kt_ref, v_ref, st_ref,
                         conv_ref, nc_ref, o_ref, so_ref):
    n_seq = qt_ref.shape[1]
    blk = pl.program_id(0)
    w_hist = wdw_ref[0:HALO, :]
    w_last = wdw_ref[HALO:HALO + 1, :]
    bias = bdw_ref[...]
    for s in range(SAMPLE_BLOCK):
        hist = cache_ref[s]
        u_row = u_ref[s:s + 1, :]
        conv_ref[s:s + 1, :] = (jnp.sum(hist * w_hist, axis=0, keepdims=True)
                                + u_row * w_last + bias)
        nc_ref[s, 0:HALO - 1, :] = cache_ref[s, 1:HALO, :]
        nc_ref[s, HALO - 1:HALO, :] = u_row

    shift = (n_seq - blk * SAMPLE_BLOCK) % n_seq
    qt = pltpu.roll(qt_ref[...], shift, 1)
    kt = pltpu.roll(kt_ref[...], shift, 1)
    for s in range(SAMPLE_BLOCK):
        for h in range(HEADS):
            k_col = kt[h * DK:(h + 1) * DK, s:s + 1]
            q_col = qt[h * DK:(h + 1) * DK, s:s + 1]
            v_row = v_ref[s:s + 1, h * DV:(h + 1) * DV]
            new_state = GAMMA[h] * st_ref[s, h] + k_col * v_row
            so_ref[s, h] = new_state
            o_ref[s:s + 1, h * DV:(h + 1) * DV] = jnp.sum(new_state * q_col, axis=0,
                                                          keepdims=True)


def _sample_state(cache, u, wdw, bdw, qt, kt, v, state):
    n = u.shape[0]
    sb = SAMPLE_BLOCK
    in_specs = [
        pl.BlockSpec((sb, HALO, CONV_DIM), lambda i: (i, 0, 0)),
        pl.BlockSpec((sb, CONV_DIM), lambda i: (i, 0)),
        pl.BlockSpec((CONV_WIDTH, CONV_DIM), lambda i: (0, 0)),
        pl.BlockSpec((1, CONV_DIM), lambda i: (0, 0)),
        pl.BlockSpec((QK_W, n), lambda i: (0, 0)),
        pl.BlockSpec((QK_W, n), lambda i: (0, 0)),
        pl.BlockSpec((sb, V_W), lambda i: (i, 0)),
        pl.BlockSpec((sb, HEADS, DK, DV), lambda i: (i, 0, 0, 0)),
    ]
    out_specs = [
        pl.BlockSpec((sb, CONV_DIM), lambda i: (i, 0)),
        pl.BlockSpec((sb, HALO, CONV_DIM), lambda i: (i, 0, 0)),
        pl.BlockSpec((sb, V_W), lambda i: (i, 0)),
        pl.BlockSpec((sb, HEADS, DK, DV), lambda i: (i, 0, 0, 0)),
    ]
    out_shape = [
        jax.ShapeDtypeStruct((n, CONV_DIM), F32),
        jax.ShapeDtypeStruct((n, HALO, CONV_DIM), F32),
        jax.ShapeDtypeStruct((n, V_W), F32),
        jax.ShapeDtypeStruct((n, HEADS, DK, DV), F32),
    ]
    return pl.pallas_call(
        _sample_state_kernel,
        grid=(n // sb,), in_specs=in_specs, out_specs=out_specs, out_shape=out_shape,
        compiler_params=pltpu.CompilerParams(
            dimension_semantics=("parallel",),
            vmem_limit_bytes=VMEM_LIMIT_BYTES),
        name="sample_state",
    )(cache, u, wdw, bdw, qt, kt, v, state)


def _sample_post_kernel(x_ref, conv_ref, o_ref, g_ref, ga_ref, gb_ref, lnw_ref, lnb_ref,
                        wco_ref, wro_ref, wo_ref, npost_ref, nfpre_ref, nfpost_ref,
                        wup_ref, wdn_ref, y_ref):
    a_act = _silu(_layer_norm(conv_ref[...], lnw_ref[...], lnb_ref[...])).astype(BF16)
    a_out = _dot(a_act, wco_ref[...])
    b_out = jnp.zeros_like(a_out)
    for h in range(HEADS):
        cols = slice(h * DV, (h + 1) * DV)
        yh = (_silu(g_ref[:, cols]) * _group_norm(o_ref[:, cols])).astype(BF16)
        b_out = b_out + _dot(yh, wro_ref[cols, :])
    merged = jax.nn.sigmoid(ga_ref[...]) * a_out + jax.nn.sigmoid(gb_ref[...]) * b_out
    m = _dot(merged.astype(BF16), wo_ref[...])
    hres = x_ref[...] + _rms(m, npost_ref[...])
    y_ref[...] = _ffn_body(hres, nfpre_ref[...], nfpost_ref[...], wup_ref, wdn_ref)


def _sample_post(x, conv, o, g, ga, gb, lnw, lnb, wco, wro, wo, npost, nfpre, nfpost, wup, wdn):
    return pl.pallas_call(
        _sample_post_kernel,
        out_shape=jax.ShapeDtypeStruct(x.shape, F32),
        compiler_params=pltpu.CompilerParams(vmem_limit_bytes=VMEM_LIMIT_BYTES),
        name="sample_post",
    )(x, conv, o, g, ga, gb, lnw, lnb, wco, wro, wo, npost, nfpre, nfpost, wup, wdn)


def _rope_angles(pos):
    half = DK // 2
    freqs = 1.0 / (ROPE_BASE ** jnp.linspace(0.0, 1.0, half, dtype=F32))
    return pos[:, None] * freqs[None, :]


def kernel(x_prompt, x_sample, cache_conv, state_ret, norm_mix_pre, norm_mix_post, w_in, w_dw, b_dw, conv_ln_w, conv_ln_b, w_conv_out, w_ret_out, w_o, norm_ffn_pre, norm_ffn_post, w_ffn_up, w_ffn_down):
    batch, seq, _ = x_prompt.shape
    n_seq, dec_seq, _ = x_sample.shape
    depth = w_in.shape[0]
    assert dec_seq == 1 and seq % MIX_ROWS == 0 and (batch * seq) % FFN_ROWS == 0
    assert n_seq % SAMPLE_BLOCK == 0

    ang_p = _rope_angles(jnp.arange(seq, dtype=F32))
    cos_p = jnp.concatenate([jnp.cos(ang_p), jnp.cos(ang_p)], axis=1)
    sin_p = jnp.concatenate([-jnp.sin(ang_p), jnp.sin(ang_p)], axis=1)
    ang_s = _rope_angles(PAST_LEN + jnp.arange(dec_seq, dtype=F32))
    cos_s = jnp.broadcast_to(jnp.cos(ang_s).T, (DK // 2, n_seq))
    sin_s = jnp.broadcast_to(jnp.sin(ang_s).T, (DK // 2, n_seq))

    xp = x_prompt
    xs = x_sample.reshape(n_seq, D_MODEL)
    conv_p, ret_p, conv_s, ret_s = [], [], [], []
    for l in range(depth):
        npre = norm_mix_pre[l][None]
        npost = norm_mix_post[l][None]
        nfpre = norm_ffn_pre[l][None]
        nfpost = norm_ffn_post[l][None]
        win = w_in[l].astype(BF16)
        wqkt = w_in[l][:, C_Q:C_Q + 2 * QK_W].T.astype(BF16)
        wdw = w_dw[l]
        bdw = b_dw[l][None]
        lnw = conv_ln_w[l][None]
        lnb = conv_ln_b[l][None]
        wco = w_conv_out[l].astype(BF16)
        wro = w_ret_out[l].astype(BF16)
        wo = w_o[l].astype(BF16)
        wup = w_ffn_up[l].astype(BF16)
        wdn = w_ffn_down[l].astype(BF16)

        h_p, nc_p, s_p = _mixer_prompt(xp, cos_p, sin_p, npre, npost, win, wdw, bdw, lnw, lnb,
                                       wco, wro, wo)
        xp = _ffn(h_p.reshape(batch * seq, D_MODEL), nfpre, nfpost, wup, wdn
                  ).reshape(batch, seq, D_MODEL)

        u, qt, kt, v, g, ga, gb = _sample_pre(xs, cos_s, sin_s, npre, win, wqkt)
        conv, nc_s, o, s_s = _sample_state(cache_conv[l], u, wdw, bdw, qt, kt, v, state_ret[l])
        xs = _sample_post(xs, conv, o, g, ga, gb, lnw, lnb, wco, wro, wo, npost, nfpre, nfpost,
                          wup, wdn)
        conv_p.append(nc_p)
        ret_p.append(s_p)
        conv_s.append(nc_s)
        ret_s.append(s_s)

    return (xp, xs.reshape(n_seq, dec_seq, D_MODEL), jnp.stack(conv_p), jnp.stack(ret_p),
            jnp.stack(conv_s), jnp.stack(ret_s))
```

```python
import math

import jax
import jax.numpy as jnp
from jax import lax
from jax.experimental import pallas as pl
from jax.experimental.pallas import tpu as pltpu

F32 = jnp.float32
BF16 = jnp.bfloat16

D_MODEL = 1024
CONV_DIM = 512
CONV_WIDTH = 31
HALO = CONV_WIDTH - 1
HEADS = 4
DK = 128
DV = 256
QK_W = HEADS * DK
V_W = HEADS * DV
CHUNK = 128
FFN_DIM = 4 * D_MODEL
EPS = 1e-6
ROPE_BASE = 10000.0
PAST_LEN = 16384
K_SCALE = DK ** -0.5

C_CONV = 0
C_Q = 2 * CONV_DIM
C_K = C_Q + QK_W
C_V = C_K + QK_W
C_G = C_V + V_W
C_GATES = C_G + V_W
IN_COLS = C_GATES + 2 * D_MODEL

LOG_GAMMA = tuple(math.log1p(-(2.0 ** (-5 - h))) for h in range(HEADS))
GAMMA = tuple(math.exp(lg) for lg in LOG_GAMMA)
GAMMA_CHUNK = tuple(math.exp(CHUNK * lg) for lg in LOG_GAMMA)

VMEM_LIMIT_BYTES = 56 * 1024 * 1024
SUBLANES = 8
HALO_PAD = 32
MIX_ROWS = 256
FFN_ROWS = 256
CONV_ROWS = 32
SAMPLE_BLOCK = 8

NT_DIMS = (((1,), (1,)), ((), ()))
TN_DIMS = (((0,), (0,)), ((), ()))


def _dot(a, b):
    return jnp.dot(a, b, preferred_element_type=F32)


def _rms(x, w):
    return x * lax.rsqrt(jnp.mean(x * x, axis=-1, keepdims=True) + EPS) * w


def _layer_norm(x, w, b):
    mu = jnp.mean(x, axis=-1, keepdims=True)
    xc = x - mu
    return xc * lax.rsqrt(jnp.mean(xc * xc, axis=-1, keepdims=True) + EPS) * w + b


def _silu(x):
    return x * jax.nn.sigmoid(x)


def _group_norm(o):
    return o * lax.rsqrt(jnp.mean(o * o, axis=-1, keepdims=True) + EPS)


def _const_spec(shape):
    zeros = (0,) * len(shape)
    return pl.BlockSpec(shape, lambda *_: zeros, pipeline_mode=pl.Buffered(1))


def _mixer_prompt_kernel(x_ref, cos_ref, sin_ref, npre_ref, npost_ref, win_ref, wdw_ref,
                         bdw_ref, lnw_ref, lnb_ref, wco_ref, wro_ref, wo_ref,
                         h_ref, nc_ref, s_ref,
                         fb_ref, conv_ref, y_ref, dec_ref, qd_ref, kd_ref):
    i = pl.program_id(1)
    rows = x_ref.shape[1]

    @pl.when(i == 0)
    def _init():
        fb_ref[0, 0:HALO_PAD, :] = jnp.zeros((HALO_PAD, CONV_DIM), F32)
        s_ref[...] = jnp.zeros(s_ref.shape, F32)
        ii = lax.broadcasted_iota(jnp.int32, (CHUNK, CHUNK), 0)
        jj = lax.broadcasted_iota(jnp.int32, (CHUNK, CHUNK), 1)
        diff = (ii - jj).astype(F32)
        row_v = lax.broadcasted_iota(jnp.int32, (CHUNK, DV), 0).astype(F32)
        row_k = ii.astype(F32)
        for h in range(HEADS):
            lg = LOG_GAMMA[h]
            dec_ref[h] = jnp.where(diff >= 0.0, jnp.exp(jnp.maximum(diff, 0.0) * lg), 0.0)
            qd_ref[h] = jnp.exp((row_v + 1.0) * lg)
            kd_ref[h] = jnp.exp((CHUNK - 1.0 - row_k) * lg)

    x = x_ref[0]
    xn = _rms(x, npre_ref[...]).astype(BF16)

    pc = _dot(xn, win_ref[:, C_CONV:C_CONV + 2 * CONV_DIM])
    fb_ref[0, HALO_PAD:HALO_PAD + rows, :] = (pc[:, :CONV_DIM]
                                              * jax.nn.sigmoid(pc[:, CONV_DIM:]))
    span = rows + HALO_PAD - SUBLANES
    for r in range(1, SUBLANES):
        fb_ref[r, 0:span, :] = fb_ref[0, r:r + span, :]

    pqk = _dot(xn, win_ref[:, C_Q:C_Q + 2 * QK_W])
    pv = _dot(xn, win_ref[:, C_V:C_V + V_W]).astype(BF16)
    pg = _dot(xn, win_ref[:, C_G:C_G + V_W])
    pgate = _dot(xn, win_ref[:, C_GATES:C_GATES + 2 * D_MODEL])

    first = HALO_PAD - HALO
    for rb in range(rows // CONV_ROWS):
        acc = jnp.broadcast_to(bdw_ref[...], (CONV_ROWS, CONV_DIM))
        for j in range(CONV_WIDTH):
            off = first + j
            base = rb * CONV_ROWS + off - off % SUBLANES
            acc = acc + fb_ref[off % SUBLANES, base:base + CONV_ROWS, :] * wdw_ref[j:j + 1, :]
        conv_ref[rb * CONV_ROWS:(rb + 1) * CONV_ROWS, :] = acc
    a_act = _silu(_layer_norm(conv_ref[...], lnw_ref[...], lnb_ref[...])).astype(BF16)
    a_out = _dot(a_act, wco_ref[...])

    cos2 = cos_ref[...]
    sin2 = sin_ref[...]

    def rot(t):
        return t * cos2 + pltpu.roll(t, DK // 2, 1) * sin2

    for h in range(HEADS):
        qh = rot(pqk[:, h * DK:(h + 1) * DK]).astype(BF16)
        kh = rot(pqk[:, QK_W + h * DK:QK_W + (h + 1) * DK]) * K_SCALE
        vh = pv[:, h * DV:(h + 1) * DV]
        gh = _silu(pg[:, h * DV:(h + 1) * DV])
        for c in range(rows // CHUNK):
            rs = slice(c * CHUNK, (c + 1) * CHUNK)
            qc, kc, vc = qh[rs], kh[rs], vh[rs]
            state = s_ref[0, h]
            scores = lax.dot_general(qc, kc.astype(BF16), NT_DIMS,
                                     preferred_element_type=F32) * dec_ref[h]
            o = _dot(scores.astype(BF16), vc) + _dot(qc, state.astype(BF16)) * qd_ref[h]
            s_ref[0, h] = GAMMA_CHUNK[h] * state + lax.dot_general(
                (kc * kd_ref[h]).astype(BF16), vc, TN_DIMS, preferred_element_type=F32)
            y_ref[rs, h * DV:(h + 1) * DV] = (gh[rs] * _group_norm(o)).astype(BF16)
    b_out = _dot(y_ref[...], wro_ref[...])

    merged = (jax.nn.sigmoid(pgate[:, :D_MODEL]) * a_out
              + jax.nn.sigmoid(pgate[:, D_MODEL:]) * b_out)
    m = _dot(merged.astype(BF16), wo_ref[...])
    h_ref[0] = x + _rms(m, npost_ref[...])

    @pl.when(i == pl.num_programs(1) - 1)
    def _emit_conv_state():
        nc_ref[0] = fb_ref[0, HALO_PAD + rows - HALO:HALO_PAD + rows, :]

    fb_ref[0, 0:HALO_PAD, :] = fb_ref[0, rows:rows + HALO_PAD, :]


def _mixer_prompt(x, cos2, sin2, npre, npost, win, wdw, bdw, lnw, lnb, wco, wro, wo):
    batch, seq, _ = x.shape
    rows = MIX_ROWS
    grid = (batch, seq // rows)
    in_specs = [
        pl.BlockSpec((1, rows, D_MODEL), lambda b, i: (b, i, 0)),
        pl.BlockSpec((rows, DK), lambda b, i: (i, 0)),
        pl.BlockSpec((rows, DK), lambda b, i: (i, 0)),
        _const_spec((1, D_MODEL)),
        _const_spec((1, D_MODEL)),
        _const_spec((D_MODEL, IN_COLS)),
        _const_spec((CONV_WIDTH, CONV_DIM)),
        _const_spec((1, CONV_DIM)),
        _const_spec((1, CONV_DIM)),
        _const_spec((1, CONV_DIM)),
        _const_spec((CONV_DIM, D_MODEL)),
        _const_spec((V_W, D_MODEL)),
        _const_spec((D_MODEL, D_MODEL)),
    ]
    out_specs = [
        pl.BlockSpec((1, rows, D_MODEL), lambda b, i: (b, i, 0)),
        pl.BlockSpec((1, HALO, CONV_DIM), lambda b, i: (b, 0, 0)),
        pl.BlockSpec((1, HEADS, DK, DV), lambda b, i: (b, 0, 0, 0)),
    ]
    out_shape = [
        jax.ShapeDtypeStruct((batch, seq, D_MODEL), F32),
        jax.ShapeDtypeStruct((batch, HALO, CONV_DIM), F32),
        jax.ShapeDtypeStruct((batch, HEADS, DK, DV), F32),
    ]
    scratch = [
        pltpu.VMEM((SUBLANES, HALO_PAD + rows, CONV_DIM), F32),
        pltpu.VMEM((rows, CONV_DIM), F32),
        pltpu.VMEM((rows, V_W), BF16),
        pltpu.VMEM((HEADS, CHUNK, CHUNK), F32),
        pltpu.VMEM((HEADS, CHUNK, DV), F32),
        pltpu.VMEM((HEADS, CHUNK, DK), F32),
    ]
    return pl.pallas_call(
        _mixer_prompt_kernel,
        grid=grid, in_specs=in_specs, out_specs=out_specs, out_shape=out_shape,
        scratch_shapes=scratch,
        compiler_params=pltpu.CompilerParams(
            dimension_semantics=("parallel", "arbitrary"),
            vmem_limit_bytes=VMEM_LIMIT_BYTES),
        name="mixer_prompt",
    )(x, cos2, sin2, npre, npost, win, wdw, bdw, lnw, lnb, wco, wro, wo)


def _ffn_body(h, npre, npost, wup_ref, wdn_ref):
    hn = _rms(h, npre).astype(BF16)
    up = _dot(hn, wup_ref[...])
    act = jnp.square(jnp.maximum(up, 0.0)).astype(BF16)
    return h + _rms(_dot(act, wdn_ref[...]), npost)


def _ffn_kernel(h_ref, npre_ref, npost_ref, wup_ref, wdn_ref, o_ref):
    o_ref[...] = _ffn_body(h_ref[...], npre_ref[...], npost_ref[...], wup_ref, wdn_ref)


def _ffn(h, npre, npost, wup, wdn):
    n = h.shape[0]
    rows = FFN_ROWS
    return pl.pallas_call(
        _ffn_kernel,
        grid=(n // rows,),
        in_specs=[
            pl.BlockSpec((rows, D_MODEL), lambda i: (i, 0)),
            _const_spec((1, D_MODEL)),
            _const_spec((1, D_MODEL)),
            _const_spec((D_MODEL, FFN_DIM)),
            _const_spec((FFN_DIM, D_MODEL)),
        ],
        out_specs=pl.BlockSpec((rows, D_MODEL), lambda i: (i, 0)),
        out_shape=jax.ShapeDtypeStruct((n, D_MODEL), F32),
        compiler_params=pltpu.CompilerParams(
            dimension_semantics=("parallel",),
            vmem_limit_bytes=VMEM_LIMIT_BYTES),
        name="ffn_prompt",
    )(h, npre, npost, wup, wdn)


def _sample_pre_kernel(x_ref, cos_ref, sin_ref, npre_ref, win_ref,
                       u_ref, qt_ref, kt_ref, v_ref, g_ref, ga_ref, gb_ref):
    xn = _rms(x_ref[...], npre_ref[...]).astype(BF16)
    pc = _dot(xn, win_ref[:, C_CONV:C_CONV + 2 * CONV_DIM])
    u_ref[...] = pc[:, :CONV_DIM] * jax.nn.sigmoid(pc[:, CONV_DIM:])
    v_ref[...] = _dot(xn, win_ref[:, C_V:C_V + V_W])
    g_ref[...] = _dot(xn, win_ref[:, C_G:C_G + V_W])
    pgate = _dot(xn, win_ref[:, C_GATES:C_GATES + 2 * D_MODEL])
    ga_ref[...] = pgate[:, :D_MODEL]
    gb_ref[...] = pgate[:, D_MODEL:]
    qkt = _dot(xn, win_ref[:, C_Q:C_Q + 2 * QK_W]).T
    cos_t = cos_ref[...]
    sin_t = sin_ref[...]
    half = DK // 2
    for g in range(2 * HEADS):
        x1 = qkt[g * DK:g * DK + half]
        x2 = qkt[g * DK + half:(g + 1) * DK]
        o1 = x1 * cos_t - x2 * sin_t
        o2 = x2 * cos_t + x1 * sin_t
        if g < HEADS:
            qt_ref[g * DK:g * DK + half, :] = o1
            qt_ref[g * DK + half:(g + 1) * DK, :] = o2
        else:
            k0 = (g - HEADS) * DK
            kt_ref[k0:k0 + half, :] = o1 * K_SCALE
            kt_ref[k0 + half:k0 + DK, :] = o2 * K_SCALE


def _sample_pre(x, cos_t, sin_t, npre, win):
    n = x.shape[0]
    out_shape = [
        jax.ShapeDtypeStruct((n, CONV_DIM), F32),
        jax.ShapeDtypeStruct((QK_W, n), F32),
        jax.ShapeDtypeStruct((QK_W, n), F32),
        jax.ShapeDtypeStruct((n, V_W), F32),
        jax.ShapeDtypeStruct((n, V_W), F32),
        jax.ShapeDtypeStruct((n, D_MODEL), F32),
        jax.ShapeDtypeStruct((n, D_MODEL), F32),
    ]
    return pl.pallas_call(
        _sample_pre_kernel, out_shape=out_shape,
        compiler_params=pltpu.CompilerParams(vmem_limit_bytes=VMEM_LIMIT_BYTES),
        name="sample_pre",
    )(x, cos_t, sin_t, npre, win)


def _sample_state_kernel(cache_ref, u_ref, wdw_ref, bdw_ref, qt_ref, kt_ref, v_ref, st_ref,
                         conv_ref, nc_ref, o_ref, so_ref):
    n_seq = qt_ref.shape[1]
    blk = pl.program_id(0)
    u = u_ref[...]
    acc = u * wdw_ref[HALO:HALO + 1, :] + bdw_ref[...]
    for j in range(HALO):
        acc = acc + cache_ref[j] * wdw_ref[j:j + 1, :]
    conv_ref[...] = acc
    nc_ref[0:HALO - 1] = cache_ref[1:HALO]
    nc_ref[HALO - 1] = u

    shift = (n_seq - blk * SAMPLE_BLOCK) % n_seq
    qt = pltpu.roll(qt_ref[...], shift, 1)
    kt = pltpu.roll(kt_ref[...], shift, 1)
    for s in range(SAMPLE_BLOCK):
        for h in range(HEADS):
            k_col = kt[h * DK:(h + 1) * DK, s:s + 1]
            q_col = qt[h * DK:(h + 1) * DK, s:s + 1]
            v_row = v_ref[s:s + 1, h * DV:(h + 1) * DV]
            new_state = GAMMA[h] * st_ref[s, h] + k_col * v_row
            so_ref[s, h] = new_state
            o_ref[s:s + 1, h * DV:(h + 1) * DV] = jnp.sum(new_state * q_col, axis=0,
                                                          keepdims=True)


def _sample_state(cache, u, wdw, bdw, qt, kt, v, state):
    n = u.shape[0]
    sb = SAMPLE_BLOCK
    in_specs = [
        pl.BlockSpec((HALO, sb, CONV_DIM), lambda i: (0, i, 0)),
        pl.BlockSpec((sb, CONV_DIM), lambda i: (i, 0)),
        pl.BlockSpec((CONV_WIDTH, CONV_DIM), lambda i: (0, 0)),
        pl.BlockSpec((1, CONV_DIM), lambda i: (0, 0)),
        pl.BlockSpec((QK_W, n), lambda i: (0, 0)),
        pl.BlockSpec((QK_W, n), lambda i: (0, 0)),
        pl.BlockSpec((sb, V_W), lambda i: (i, 0)),
        pl.BlockSpec((sb, HEADS, DK, DV), lambda i: (i, 0, 0, 0)),
    ]
    out_specs = [
        pl.BlockSpec((sb, CONV_DIM), lambda i: (i, 0)),
        pl.BlockSpec((HALO, sb, CONV_DIM), lambda i: (0, i, 0)),
        pl.BlockSpec((sb, V_W), lambda i: (i, 0)),
        pl.BlockSpec((sb, HEADS, DK, DV), lambda i: (i, 0, 0, 0)),
    ]
    out_shape = [
        jax.ShapeDtypeStruct((n, CONV_DIM), F32),
        jax.ShapeDtypeStruct((HALO, n, CONV_DIM), F32),
        jax.ShapeDtypeStruct((n, V_W), F32),
        jax.ShapeDtypeStruct((n, HEADS, DK, DV), F32),
    ]
    return pl.pallas_call(
        _sample_state_kernel,
        grid=(n // sb,), in_specs=in_specs, out_specs=out_specs, out_shape=out_shape,
        compiler_params=pltpu.CompilerParams(
            dimension_semantics=("parallel",),
            vmem_limit_bytes=VMEM_LIMIT_BYTES),
        name="sample_state",
    )(cache, u, wdw, bdw, qt, kt, v, state)


def _sample_post_kernel(x_ref, conv_ref, o_ref, g_ref, ga_ref, gb_ref, lnw_ref, lnb_ref,
                        wco_ref, wro_ref, wo_ref, npost_ref, nfpre_ref, nfpost_ref,
                        wup_ref, wdn_ref, y_ref):
    a_act = _silu(_layer_norm(conv_ref[...], lnw_ref[...], lnb_ref[...])).astype(BF16)
    a_out = _dot(a_act, wco_ref[...])
    b_out = jnp.zeros_like(a_out)
    for h in range(HEADS):
        cols = slice(h * DV, (h + 1) * DV)
        yh = (_silu(g_ref[:, cols]) * _group_norm(o_ref[:, cols])).astype(BF16)
        b_out = b_out + _dot(yh, wro_ref[cols, :])
    merged = jax.nn.sigmoid(ga_ref[...]) * a_out + jax.nn.sigmoid(gb_ref[...]) * b_out
    m = _dot(merged.astype(BF16), wo_ref[...])
    hres = x_ref[...] + _rms(m, npost_ref[...])
    y_ref[...] = _ffn_body(hres, nfpre_ref[...], nfpost_ref[...], wup_ref, wdn_ref)


def _sample_post(x, conv, o, g, ga, gb, lnw, lnb, wco, wro, wo, npost, nfpre, nfpost, wup, wdn):
    return pl.pallas_call(
        _sample_post_kernel,
        out_shape=jax.ShapeDtypeStruct(x.shape, F32),
        compiler_params=pltpu.CompilerParams(vmem_limit_bytes=VMEM_LIMIT_BYTES),
        name="sample_post",
    )(x, conv, o, g, ga, gb, lnw, lnb, wco, wro, wo, npost, nfpre, nfpost, wup, wdn)


def _rope_angles(pos):
    half = DK // 2
    freqs = 1.0 / (ROPE_BASE ** jnp.linspace(0.0, 1.0, half, dtype=F32))
    return pos[:, None] * freqs[None, :]


def kernel(x_prompt, x_sample, cache_conv, state_ret, norm_mix_pre, norm_mix_post, w_in, w_dw, b_dw, conv_ln_w, conv_ln_b, w_conv_out, w_ret_out, w_o, norm_ffn_pre, norm_ffn_post, w_ffn_up, w_ffn_down):
    batch, seq, _ = x_prompt.shape
    n_seq, dec_seq, _ = x_sample.shape
    depth = w_in.shape[0]
    assert dec_seq == 1 and seq % MIX_ROWS == 0 and (batch * seq) % FFN_ROWS == 0
    assert n_seq % SAMPLE_BLOCK == 0

    ang_p = _rope_angles(jnp.arange(seq, dtype=F32))
    cos_p = jnp.concatenate([jnp.cos(ang_p), jnp.cos(ang_p)], axis=1)
    sin_p = jnp.concatenate([-jnp.sin(ang_p), jnp.sin(ang_p)], axis=1)
    ang_s = _rope_angles(PAST_LEN + jnp.arange(dec_seq, dtype=F32))
    cos_s = jnp.broadcast_to(jnp.cos(ang_s).T, (DK // 2, n_seq))
    sin_s = jnp.broadcast_to(jnp.sin(ang_s).T, (DK // 2, n_seq))

    xp = x_prompt
    xs = x_sample.reshape(n_seq, D_MODEL)
    conv_p, ret_p, conv_s, ret_s = [], [], [], []
    for l in range(depth):
        npre = norm_mix_pre[l][None]
        npost = norm_mix_post[l][None]
        nfpre = norm_ffn_pre[l][None]
        nfpost = norm_ffn_post[l][None]
        win = w_in[l].astype(BF16)
        wdw = w_dw[l]
        bdw = b_dw[l][None]
        lnw = conv_ln_w[l][None]
        lnb = conv_ln_b[l][None]
        wco = w_conv_out[l].astype(BF16)
        wro = w_ret_out[l].astype(BF16)
        wo = w_o[l].astype(BF16)
        wup = w_ffn_up[l].astype(BF16)
        wdn = w_ffn_down[l].astype(BF16)

        h_p, nc_p, s_p = _mixer_prompt(xp, cos_p, sin_p, npre, npost, win, wdw, bdw, lnw, lnb,
                                       wco, wro, wo)
        xp = _ffn(h_p.reshape(batch * seq, D_MODEL), nfpre, nfpost, wup, wdn
                  ).reshape(batch, seq, D_MODEL)

        u, qt, kt, v, g, ga, gb = _sample_pre(xs, cos_s, sin_s, npre, win)
        cache_t = jnp.transpose(cache_conv[l], (1, 0, 2))
        conv, nc_t, o, s_s = _sample_state(cache_t, u, wdw, bdw, qt, kt, v, state_ret[l])
        nc_s = jnp.transpose(nc_t, (1, 0, 2))
        xs = _sample_post(xs, conv, o, g, ga, gb, lnw, lnb, wco, wro, wo, npost, nfpre, nfpost,
                          wup, wdn)
        conv_p.append(nc_p)
        ret_p.append(s_p)
        conv_s.append(nc_s)
        ret_s.append(s_s)

    return (xp, xs.reshape(n_seq, dec_seq, D_MODEL), jnp.stack(conv_p), jnp.stack(ret_p),
            jnp.stack(conv_s), jnp.stack(ret_s))
```

```python
import functools
import math

import jax
import jax.numpy as jnp
from jax import lax
from jax.experimental import pallas as pl
from jax.experimental.pallas import tpu as pltpu

F32 = jnp.float32
BF16 = jnp.bfloat16

D_MODEL = 1024
CONV_DIM = 512
CONV_WIDTH = 31
HALO = CONV_WIDTH - 1
HEADS = 4
DK = 128
DV = 256
QK_W = HEADS * DK
V_W = HEADS * DV
CHUNK = 128
FFN_DIM = 4 * D_MODEL
EPS = 1e-6
ROPE_BASE = 10000.0
PAST_LEN = 16384
K_SCALE = DK ** -0.5

C_CONV = 0
C_Q = 2 * CONV_DIM
C_K = C_Q + QK_W
C_V = C_K + QK_W
C_G = C_V + V_W
C_GATES = C_G + V_W
IN_COLS = C_GATES + 2 * D_MODEL

LOG_GAMMA = tuple(math.log1p(-(2.0 ** (-5 - h))) for h in range(HEADS))
GAMMA = tuple(math.exp(lg) for lg in LOG_GAMMA)
GAMMA_CHUNK = tuple(math.exp(CHUNK * lg) for lg in LOG_GAMMA)

VMEM_LIMIT_BYTES = 56 * 1024 * 1024
SUBLANES = 8
HALO_PAD = 32
MIX_ROWS = 256
FFN_ROWS = 512
CONV_ROWS = 32
SAMPLE_BLOCK = 8

NT_DIMS = (((1,), (1,)), ((), ()))
TN_DIMS = (((0,), (0,)), ((), ()))


def _dot(a, b):
    return jnp.dot(a, b, preferred_element_type=F32)


def _rms(x, w):
    return x * lax.rsqrt(jnp.mean(x * x, axis=-1, keepdims=True) + EPS) * w


def _layer_norm(x, w, b):
    mu = jnp.mean(x, axis=-1, keepdims=True)
    xc = x - mu
    return xc * lax.rsqrt(jnp.mean(xc * xc, axis=-1, keepdims=True) + EPS) * w + b


def _silu(x):
    return x * jax.nn.sigmoid(x)


def _group_norm(o):
    return o * lax.rsqrt(jnp.mean(o * o, axis=-1, keepdims=True) + EPS)


def _const_spec(shape):
    zeros = (0,) * len(shape)
    return pl.BlockSpec(shape, lambda *_: zeros, pipeline_mode=pl.Buffered(1))


def _mixer_prompt_kernel(x_ref, cos_ref, sin_ref, npre_ref, npost_ref, win_ref, wdw_ref,
                         bdw_ref, lnw_ref, lnb_ref, wco_ref, wro_ref, wo_ref,
                         h_ref, nc_ref, sout_ref,
                         fb_ref, conv_ref, y_ref, dec_ref, qd_ref, kd_ref, wb_ref, s_ref,
                         aact_ref, gate_ref, bout_ref, xprev_ref, *, tiles_per_seq, n_tiles):
    g = pl.program_id(0)
    i = jnp.minimum(g, n_tiles - 1) % tiles_per_seq
    rows = x_ref.shape[1]

    @pl.when(g == 0)
    def _init_tables():
        aact_ref[...] = jnp.zeros(aact_ref.shape, BF16)
        gate_ref[...] = jnp.zeros(gate_ref.shape, F32)
        bout_ref[...] = jnp.zeros(bout_ref.shape, F32)
        xprev_ref[...] = jnp.zeros(xprev_ref.shape, F32)
        ii = lax.broadcasted_iota(jnp.int32, (CHUNK, CHUNK), 0)
        jj = lax.broadcasted_iota(jnp.int32, (CHUNK, CHUNK), 1)
        diff = (ii - jj).astype(F32)
        row_v = lax.broadcasted_iota(jnp.int32, (CHUNK, DV), 0).astype(F32)
        row_k = ii.astype(F32)
        for h in range(HEADS):
            lg = LOG_GAMMA[h]
            dec_ref[h] = jnp.where(diff >= 0.0, jnp.exp(jnp.maximum(diff, 0.0) * lg), 0.0)
            qd_ref[h] = jnp.exp((row_v + 1.0) * lg)
            kd_ref[h] = jnp.exp((CHUNK - 1.0 - row_k) * lg)
        for j in range(CONV_WIDTH):
            wb_ref[j] = jnp.broadcast_to(wdw_ref[j:j + 1, :], (SUBLANES, CONV_DIM))

    @pl.when(i == 0)
    def _start_sequence():
        fb_ref[0, 0:HALO_PAD, :] = jnp.zeros((HALO_PAD, CONV_DIM), F32)
        s_ref[...] = jnp.zeros(s_ref.shape, F32)

    a_out = _dot(aact_ref[...], wco_ref[...])
    merged = (jax.nn.sigmoid(gate_ref[:, :D_MODEL]) * a_out
              + jax.nn.sigmoid(gate_ref[:, D_MODEL:]) * bout_ref[...])
    m = _dot(merged.astype(BF16), wo_ref[...])
    h_ref[0] = xprev_ref[...] + _rms(m, npost_ref[...])

    x = x_ref[0]
    xprev_ref[...] = x
    xn = _rms(x, npre_ref[...]).astype(BF16)

    pc = _dot(xn, win_ref[:, C_CONV:C_CONV + 2 * CONV_DIM])
    fb_ref[0, HALO_PAD:HALO_PAD + rows, :] = (pc[:, :CONV_DIM]
                                              * jax.nn.sigmoid(pc[:, CONV_DIM:]))
    span = rows + HALO_PAD - SUBLANES
    for r in range(1, SUBLANES):
        fb_ref[r, 0:span, :] = fb_ref[0, r:r + span, :]

    pqk = _dot(xn, win_ref[:, C_Q:C_Q + 2 * QK_W])
    pv = _dot(xn, win_ref[:, C_V:C_V + V_W]).astype(BF16)
    pg = _dot(xn, win_ref[:, C_G:C_G + V_W])

    first = HALO_PAD - HALO
    for rb in range(rows // CONV_ROWS):
        acc = jnp.broadcast_to(bdw_ref[...], (CONV_ROWS, CONV_DIM))
        for j in range(CONV_WIDTH):
            off = first + j
            base = rb * CONV_ROWS + off - off % SUBLANES
            slab = fb_ref[off % SUBLANES, base:base + CONV_ROWS, :]
            acc = acc + (slab.reshape(CONV_ROWS // SUBLANES, SUBLANES, CONV_DIM)
                         * wb_ref[j][None]).reshape(CONV_ROWS, CONV_DIM)
        conv_ref[rb * CONV_ROWS:(rb + 1) * CONV_ROWS, :] = acc
    aact_ref[...] = _silu(_layer_norm(conv_ref[...], lnw_ref[...], lnb_ref[...])).astype(BF16)

    cos2 = cos_ref[...]
    sin2 = sin_ref[...]

    def rot(t):
        return t * cos2 + pltpu.roll(t, DK // 2, 1) * sin2

    for h in range(HEADS):
        qh = rot(pqk[:, h * DK:(h + 1) * DK]).astype(BF16)
        kh = rot(pqk[:, QK_W + h * DK:QK_W + (h + 1) * DK]) * K_SCALE
        vh = pv[:, h * DV:(h + 1) * DV]
        gh = _silu(pg[:, h * DV:(h + 1) * DV])
        for c in range(rows // CHUNK):
            rs = slice(c * CHUNK, (c + 1) * CHUNK)
            qc, kc, vc = qh[rs], kh[rs], vh[rs]
            state = s_ref[h]
            scores = lax.dot_general(qc, kc.astype(BF16), NT_DIMS,
                                     preferred_element_type=F32) * dec_ref[h]
            o = _dot(scores.astype(BF16), vc) + _dot(qc, state.astype(BF16)) * qd_ref[h]
            s_ref[h] = GAMMA_CHUNK[h] * state + lax.dot_general(
                (kc * kd_ref[h]).astype(BF16), vc, TN_DIMS, preferred_element_type=F32)
            y_ref[rs, h * DV:(h + 1) * DV] = (gh[rs] * _group_norm(o)).astype(BF16)
    bout_ref[...] = _dot(y_ref[...], wro_ref[...])
    gate_ref[...] = _dot(xn, win_ref[:, C_GATES:C_GATES + 2 * D_MODEL])

    @pl.when(jnp.logical_and(i == tiles_per_seq - 1, g < n_tiles))
    def _emit_sequence_state():
        nc_ref[0] = fb_ref[0, HALO_PAD + rows - HALO:HALO_PAD + rows, :]
        sout_ref[0] = s_ref[...]

    fb_ref[0, 0:HALO_PAD, :] = fb_ref[0, rows:rows + HALO_PAD, :]


def _mixer_prompt(x, cos2, sin2, npre, npost, win, wdw, bdw, lnw, lnb, wco, wro, wo):
    batch, seq, _ = x.shape
    rows = MIX_ROWS
    nt = seq // rows
    n_tiles = batch * nt

    def front(g):
        return jnp.minimum(g, n_tiles - 1)

    def back(g):
        return jnp.maximum(g - 1, 0)

    in_specs = [
        pl.BlockSpec((1, rows, D_MODEL), lambda g: (front(g) // nt, front(g) % nt, 0)),
        pl.BlockSpec((rows, DK), lambda g: (front(g) % nt, 0)),
        pl.BlockSpec((rows, DK), lambda g: (front(g) % nt, 0)),
        _const_spec((1, D_MODEL)),
        _const_spec((1, D_MODEL)),
        _const_spec((D_MODEL, IN_COLS)),
        _const_spec((CONV_WIDTH, CONV_DIM)),
        _const_spec((1, CONV_DIM)),
        _const_spec((1, CONV_DIM)),
        _const_spec((1, CONV_DIM)),
        _const_spec((CONV_DIM, D_MODEL)),
        _const_spec((V_W, D_MODEL)),
        _const_spec((D_MODEL, D_MODEL)),
    ]
    out_specs = [
        pl.BlockSpec((1, rows, D_MODEL), lambda g: (back(g) // nt, back(g) % nt, 0)),
        pl.BlockSpec((1, HALO, CONV_DIM), lambda g: (front(g) // nt, 0, 0)),
        pl.BlockSpec((1, HEADS, DK, DV), lambda g: (front(g) // nt, 0, 0, 0)),
    ]
    out_shape = [
        jax.ShapeDtypeStruct((batch, seq, D_MODEL), F32),
        jax.ShapeDtypeStruct((batch, HALO, CONV_DIM), F32),
        jax.ShapeDtypeStruct((batch, HEADS, DK, DV), F32),
    ]
    scratch = [
        pltpu.VMEM((SUBLANES, HALO_PAD + rows, CONV_DIM), F32),
        pltpu.VMEM((rows, CONV_DIM), F32),
        pltpu.VMEM((rows, V_W), BF16),
        pltpu.VMEM((HEADS, CHUNK, CHUNK), F32),
        pltpu.VMEM((HEADS, CHUNK, DV), F32),
        pltpu.VMEM((HEADS, CHUNK, DK), F32),
        pltpu.VMEM((CONV_WIDTH, SUBLANES, CONV_DIM), F32),
        pltpu.VMEM((HEADS, DK, DV), F32),
        pltpu.VMEM((rows, CONV_DIM), BF16),
        pltpu.VMEM((rows, 2 * D_MODEL), F32),
        pltpu.VMEM((rows, D_MODEL), F32),
        pltpu.VMEM((rows, D_MODEL), F32),
    ]
    return pl.pallas_call(
        functools.partial(_mixer_prompt_kernel, tiles_per_seq=nt, n_tiles=n_tiles),
        grid=(n_tiles + 1,), in_specs=in_specs, out_specs=out_specs, out_shape=out_shape,
        scratch_shapes=scratch,
        compiler_params=pltpu.CompilerParams(
            dimension_semantics=("arbitrary",),
            vmem_limit_bytes=VMEM_LIMIT_BYTES),
        name="mixer_prompt",
    )(x, cos2, sin2, npre, npost, win, wdw, bdw, lnw, lnb, wco, wro, wo)


def _ffn_body(h, npre, npost, wup_ref, wdn_ref):
    hn = _rms(h, npre).astype(BF16)
    up = _dot(hn, wup_ref[...])
    act = jnp.square(jnp.maximum(up, 0.0)).astype(BF16)
    return h + _rms(_dot(act, wdn_ref[...]), npost)


def _ffn_kernel(h_ref, npre_ref, npost_ref, wup_ref, wdn_ref, o_ref):
    o_ref[...] = _ffn_body(h_ref[...], npre_ref[...], npost_ref[...], wup_ref, wdn_ref)


def _ffn(h, npre, npost, wup, wdn):
    n = h.shape[0]
    rows = FFN_ROWS
    return pl.pallas_call(
        _ffn_kernel,
        grid=(n // rows,),
        in_specs=[
            pl.BlockSpec((rows, D_MODEL), lambda i: (i, 0)),
            _const_spec((1, D_MODEL)),
            _const_spec((1, D_MODEL)),
            _const_spec((D_MODEL, FFN_DIM)),
            _const_spec((FFN_DIM, D_MODEL)),
        ],
        out_specs=pl.BlockSpec((rows, D_MODEL), lambda i: (i, 0)),
        out_shape=jax.ShapeDtypeStruct((n, D_MODEL), F32),
        compiler_params=pltpu.CompilerParams(
            dimension_semantics=("parallel",),
            vmem_limit_bytes=VMEM_LIMIT_BYTES),
        name="ffn_prompt",
    )(h, npre, npost, wup, wdn)


def _sample_pre_kernel(x_ref, cos_ref, sin_ref, npre_ref, win_ref,
                       u_ref, qt_ref, kt_ref, v_ref, g_ref, ga_ref, gb_ref):
    xn = _rms(x_ref[...], npre_ref[...]).astype(BF16)
    pc = _dot(xn, win_ref[:, C_CONV:C_CONV + 2 * CONV_DIM])
    u_ref[...] = pc[:, :CONV_DIM] * jax.nn.sigmoid(pc[:, CONV_DIM:])
    v_ref[...] = _dot(xn, win_ref[:, C_V:C_V + V_W])
    g_ref[...] = _dot(xn, win_ref[:, C_G:C_G + V_W])
    pgate = _dot(xn, win_ref[:, C_GATES:C_GATES + 2 * D_MODEL])
    ga_ref[...] = pgate[:, :D_MODEL]
    gb_ref[...] = pgate[:, D_MODEL:]
    qkt = _dot(xn, win_ref[:, C_Q:C_Q + 2 * QK_W]).T
    cos_t = cos_ref[...]
    sin_t = sin_ref[...]
    half = DK // 2
    for g in range(2 * HEADS):
        x1 = qkt[g * DK:g * DK + half]
        x2 = qkt[g * DK + half:(g + 1) * DK]
        o1 = x1 * cos_t - x2 * sin_t
        o2 = x2 * cos_t + x1 * sin_t
        if g < HEADS:
            qt_ref[g * DK:g * DK + half, :] = o1
            qt_ref[g * DK + half:(g + 1) * DK, :] = o2
        else:
            k0 = (g - HEADS) * DK
            kt_ref[k0:k0 + half, :] = o1 * K_SCALE
            kt_ref[k0 + half:k0 + DK, :] = o2 * K_SCALE


def _sample_pre(x, cos_t, sin_t, npre, win):
    n = x.shape[0]
    out_shape = [
        jax.ShapeDtypeStruct((n, CONV_DIM), F32),
        jax.ShapeDtypeStruct((QK_W, n), F32),
        jax.ShapeDtypeStruct((QK_W, n), F32),
        jax.ShapeDtypeStruct((n, V_W), F32),
        jax.ShapeDtypeStruct((n, V_W), F32),
        jax.ShapeDtypeStruct((n, D_MODEL), F32),
        jax.ShapeDtypeStruct((n, D_MODEL), F32),
    ]
    return pl.pallas_call(
        _sample_pre_kernel, out_shape=out_shape,
        compiler_params=pltpu.CompilerParams(vmem_limit_bytes=VMEM_LIMIT_BYTES),
        name="sample_pre",
    )(x, cos_t, sin_t, npre, win)


def _sample_state_kernel(cache_ref, u_ref, wdw_ref, bdw_ref, qt_ref, kt_ref, v_ref, st_ref,
                         conv_ref, nc_ref, o_ref, so_ref):
    n_seq = qt_ref.shape[1]
    blk = pl.program_id(0)
    u = u_ref[...]
    acc = u * wdw_ref[HALO:HALO + 1, :] + bdw_ref[...]
    for j in range(HALO):
        acc = acc + cache_ref[j] * wdw_ref[j:j + 1, :]
    conv_ref[...] = acc
    nc_ref[0:HALO - 1] = cache_ref[1:HALO]
    nc_ref[HALO - 1] = u

    shift = (n_seq - blk * SAMPLE_BLOCK) % n_seq
    qt = pltpu.roll(qt_ref[...], shift, 1)
    kt = pltpu.roll(kt_ref[...], shift, 1)
    for s in range(SAMPLE_BLOCK):
        for h in range(HEADS):
            k_col = kt[h * DK:(h + 1) * DK, s:s + 1]
            q_col = qt[h * DK:(h + 1) * DK, s:s + 1]
            v_row = v_ref[s:s + 1, h * DV:(h + 1) * DV]
            new_state = GAMMA[h] * st_ref[s, h] + k_col * v_row
            so_ref[s, h] = new_state
            o_ref[s:s + 1, h * DV:(h + 1) * DV] = jnp.sum(new_state * q_col, axis=0,
                                                          keepdims=True)


def _sample_state(cache, u, wdw, bdw, qt, kt, v, state):
    n = u.shape[0]
    sb = SAMPLE_BLOCK
    in_specs = [
        pl.BlockSpec((HALO, sb, CONV_DIM), lambda i: (0, i, 0)),
        pl.BlockSpec((sb, CONV_DIM), lambda i: (i, 0)),
        pl.BlockSpec((CONV_WIDTH, CONV_DIM), lambda i: (0, 0)),
        pl.BlockSpec((1, CONV_DIM), lambda i: (0, 0)),
        pl.BlockSpec((QK_W, n), lambda i: (0, 0)),
        pl.BlockSpec((QK_W, n), lambda i: (0, 0)),
        pl.BlockSpec((sb, V_W), lambda i: (i, 0)),
        pl.BlockSpec((sb, HEADS, DK, DV), lambda i: (i, 0, 0, 0)),
    ]
    out_specs = [
        pl.BlockSpec((sb, CONV_DIM), lambda i: (i, 0)),
        pl.BlockSpec((HALO, sb, CONV_DIM), lambda i: (0, i, 0)),
        pl.BlockSpec((sb, V_W), lambda i: (i, 0)),
        pl.BlockSpec((sb, HEADS, DK, DV), lambda i: (i, 0, 0, 0)),
    ]
    out_shape = [
        jax.ShapeDtypeStruct((n, CONV_DIM), F32),
        jax.ShapeDtypeStruct((HALO, n, CONV_DIM), F32),
        jax.ShapeDtypeStruct((n, V_W), F32),
        jax.ShapeDtypeStruct((n, HEADS, DK, DV), F32),
    ]
    return pl.pallas_call(
        _sample_state_kernel,
        grid=(n // sb,), in_specs=in_specs, out_specs=out_specs, out_shape=out_shape,
        compiler_params=pltpu.CompilerParams(
            dimension_semantics=("parallel",),
            vmem_limit_bytes=VMEM_LIMIT_BYTES),
        name="sample_state",
    )(cache, u, wdw, bdw, qt, kt, v, state)


def _sample_post_kernel(x_ref, conv_ref, o_ref, g_ref, ga_ref, gb_ref, lnw_ref, lnb_ref,
                        wco_ref, wro_ref, wo_ref, npost_ref, nfpre_ref, nfpost_ref,
                        wup_ref, wdn_ref, y_ref):
    a_act = _silu(_layer_norm(conv_ref[...], lnw_ref[...], lnb_ref[...])).astype(BF16)
    a_out = _dot(a_act, wco_ref[...])
    b_out = jnp.zeros_like(a_out)
    for h in range(HEADS):
        cols = slice(h * DV, (h + 1) * DV)
        yh = (_silu(g_ref[:, cols]) * _group_norm(o_ref[:, cols])).astype(BF16)
        b_out = b_out + _dot(yh, wro_ref[cols, :])
    merged = jax.nn.sigmoid(ga_ref[...]) * a_out + jax.nn.sigmoid(gb_ref[...]) * b_out
    m = _dot(merged.astype(BF16), wo_ref[...])
    hres = x_ref[...] + _rms(m, npost_ref[...])
    y_ref[...] = _ffn_body(hres, nfpre_ref[...], nfpost_ref[...], wup_ref, wdn_ref)


def _sample_post(x, conv, o, g, ga, gb, lnw, lnb, wco, wro, wo, npost, nfpre, nfpost, wup, wdn):
    return pl.pallas_call(
        _sample_post_kernel,
        out_shape=jax.ShapeDtypeStruct(x.shape, F32),
        compiler_params=pltpu.CompilerParams(vmem_limit_bytes=VMEM_LIMIT_BYTES),
        name="sample_post",
    )(x, conv, o, g, ga, gb, lnw, lnb, wco, wro, wo, npost, nfpre, nfpost, wup, wdn)


def _rope_angles(pos):
    half = DK // 2
    freqs = 1.0 / (ROPE_BASE ** jnp.linspace(0.0, 1.0, half, dtype=F32))
    return pos[:, None] * freqs[None, :]


def kernel(x_prompt, x_sample, cache_conv, state_ret, norm_mix_pre, norm_mix_post, w_in, w_dw, b_dw, conv_ln_w, conv_ln_b, w_conv_out, w_ret_out, w_o, norm_ffn_pre, norm_ffn_post, w_ffn_up, w_ffn_down):
    batch, seq, _ = x_prompt.shape
    n_seq, dec_seq, _ = x_sample.shape
    depth = w_in.shape[0]
    assert dec_seq == 1 and seq % MIX_ROWS == 0 and (batch * seq) % FFN_ROWS == 0
    assert n_seq % SAMPLE_BLOCK == 0

    ang_p = _rope_angles(jnp.arange(seq, dtype=F32))
    cos_p = jnp.concatenate([jnp.cos(ang_p), jnp.cos(ang_p)], axis=1)
    sin_p = jnp.concatenate([-jnp.sin(ang_p), jnp.sin(ang_p)], axis=1)
    ang_s = _rope_angles(PAST_LEN + jnp.arange(dec_seq, dtype=F32))
    cos_s = jnp.broadcast_to(jnp.cos(ang_s).T, (DK // 2, n_seq))
    sin_s = jnp.broadcast_to(jnp.sin(ang_s).T, (DK // 2, n_seq))

    xp = x_prompt
    xs = x_sample.reshape(n_seq, D_MODEL)
    conv_p, ret_p, conv_s, ret_s = [], [], [], []
    for l in range(depth):
        npre = norm_mix_pre[l][None]
        npost = norm_mix_post[l][None]
        nfpre = norm_ffn_pre[l][None]
        nfpost = norm_ffn_post[l][None]
        win = w_in[l].astype(BF16)
        wdw = w_dw[l]
        bdw = b_dw[l][None]
        lnw = conv_ln_w[l][None]
        lnb = conv_ln_b[l][None]
        wco = w_conv_out[l].astype(BF16)
        wro = w_ret_out[l].astype(BF16)
        wo = w_o[l].astype(BF16)
        wup = w_ffn_up[l].astype(BF16)
        wdn = w_ffn_down[l].astype(BF16)

        h_p, nc_p, s_p = _mixer_prompt(xp, cos_p, sin_p, npre, npost, win, wdw, bdw, lnw, lnb,
                                       wco, wro, wo)
        xp = _ffn(h_p.reshape(batch * seq, D_MODEL), nfpre, nfpost, wup, wdn
                  ).reshape(batch, seq, D_MODEL)

        u, qt, kt, v, g, ga, gb = _sample_pre(xs, cos_s, sin_s, npre, win)
        cache_t = jnp.transpose(cache_conv[l], (1, 0, 2))
        conv, nc_t, o, s_s = _sample_state(cache_t, u, wdw, bdw, qt, kt, v, state_ret[l])
        nc_s = jnp.transpose(nc_t, (1, 0, 2))
        xs = _sample_post(xs, conv, o, g, ga, gb, lnw, lnb, wco, wro, wo, npost, nfpre, nfpost,
                          wup, wdn)
        conv_p.append(nc_p)
        ret_p.append(s_p)
        conv_s.append(nc_s)
        ret_s.append(s_s)

    return (xp, xs.reshape(n_seq, dec_seq, D_MODEL), jnp.stack(conv_p), jnp.stack(ret_p),
            jnp.stack(conv_s), jnp.stack(ret_s))
```

```python
import functools
import math

import jax
import jax.numpy as jnp
from jax import lax
from jax.experimental import pallas as pl
from jax.experimental.pallas import tpu as pltpu

F32 = jnp.float32
BF16 = jnp.bfloat16

D_MODEL = 1024
CONV_DIM = 512
CONV_WIDTH = 31
HALO = CONV_WIDTH - 1
HEADS = 4
DK = 128
DV = 256
QK_W = HEADS * DK
V_W = HEADS * DV
CHUNK = 128
FFN_DIM = 4 * D_MODEL
EPS = 1e-6
ROPE_BASE = 10000.0
PAST_LEN = 16384
K_SCALE = DK ** -0.5

C_CONV = 0
C_Q = 2 * CONV_DIM
C_K = C_Q + QK_W
C_V = C_K + QK_W
C_G = C_V + V_W
C_GATES = C_G + V_W
IN_COLS = C_GATES + 2 * D_MODEL

LOG_GAMMA = tuple(math.log1p(-(2.0 ** (-5 - h))) for h in range(HEADS))
GAMMA = tuple(math.exp(lg) for lg in LOG_GAMMA)
GAMMA_CHUNK = tuple(math.exp(CHUNK * lg) for lg in LOG_GAMMA)

VMEM_LIMIT_BYTES = 56 * 1024 * 1024
SUBLANES = 8
HALO_PAD = 32
MIX_ROWS = 256
FFN_ROWS = 512
CONV_ROWS = 32
SAMPLE_BLOCK = 8

NT_DIMS = (((1,), (1,)), ((), ()))
TN_DIMS = (((0,), (0,)), ((), ()))


def _dot(a, b):
    return jnp.dot(a, b, preferred_element_type=F32)


def _rms(x, w):
    return x * lax.rsqrt(jnp.mean(x * x, axis=-1, keepdims=True) + EPS) * w


def _layer_norm(x, w, b):
    mu = jnp.mean(x, axis=-1, keepdims=True)
    xc = x - mu
    return xc * lax.rsqrt(jnp.mean(xc * xc, axis=-1, keepdims=True) + EPS) * w + b


def _silu(x):
    return x * jax.nn.sigmoid(x)


def _group_norm(o):
    return o * lax.rsqrt(jnp.mean(o * o, axis=-1, keepdims=True) + EPS)


def _const_spec(shape):
    zeros = (0,) * len(shape)
    return pl.BlockSpec(shape, lambda *_: zeros, pipeline_mode=pl.Buffered(1))


def _mixer_prompt_kernel(x_ref, cos_ref, sin_ref, npre_ref, npost_ref, win_ref, wdw_ref,
                         bdw_ref, lnw_ref, lnb_ref, wco_ref, wro_ref, wo_ref,
                         h_ref, nc_ref, sout_ref,
                         fb_ref, conv_ref, y_ref, dec_ref, qd_ref, kd_ref, wb_ref, s_ref,
                         aact_ref, gate_ref, bout_ref, xprev_ref, *, tiles_per_seq, n_tiles):
    g = pl.program_id(0)
    i = jnp.minimum(g, n_tiles - 1) % tiles_per_seq
    rows = x_ref.shape[1]

    @pl.when(g == 0)
    def _init_tables():
        aact_ref[...] = jnp.zeros(aact_ref.shape, BF16)
        gate_ref[...] = jnp.zeros(gate_ref.shape, F32)
        bout_ref[...] = jnp.zeros(bout_ref.shape, F32)
        xprev_ref[...] = jnp.zeros(xprev_ref.shape, F32)
        ii = lax.broadcasted_iota(jnp.int32, (CHUNK, CHUNK), 0)
        jj = lax.broadcasted_iota(jnp.int32, (CHUNK, CHUNK), 1)
        diff = (ii - jj).astype(F32)
        row_v = lax.broadcasted_iota(jnp.int32, (CHUNK, DV), 0).astype(F32)
        row_k = ii.astype(F32)
        for h in range(HEADS):
            lg = LOG_GAMMA[h]
            dec_ref[h] = jnp.where(diff >= 0.0, jnp.exp(jnp.maximum(diff, 0.0) * lg), 0.0)
            qd_ref[h] = jnp.exp((row_v + 1.0) * lg)
            kd_ref[h] = jnp.exp((CHUNK - 1.0 - row_k) * lg)
        for j in range(CONV_WIDTH):
            wb_ref[j] = jnp.broadcast_to(wdw_ref[j:j + 1, :], (SUBLANES, CONV_DIM))

    @pl.when(i == 0)
    def _start_sequence():
        fb_ref[0, 0:HALO_PAD, :] = jnp.zeros((HALO_PAD, CONV_DIM), F32)
        s_ref[...] = jnp.zeros(s_ref.shape, F32)

    a_out = _dot(aact_ref[...], wco_ref[...])
    x = x_ref[0]
    xn = _rms(x, npre_ref[...]).astype(BF16)
    pc = _dot(xn, win_ref[:, C_CONV:C_CONV + 2 * CONV_DIM])
    merged = (jax.nn.sigmoid(gate_ref[:, :D_MODEL]) * a_out
              + jax.nn.sigmoid(gate_ref[:, D_MODEL:]) * bout_ref[...])
    m = _dot(merged.astype(BF16), wo_ref[...])
    h_ref[0] = xprev_ref[...] + _rms(m, npost_ref[...])

    xprev_ref[...] = x

    fb_ref[0, HALO_PAD:HALO_PAD + rows, :] = (pc[:, :CONV_DIM]
                                              * jax.nn.sigmoid(pc[:, CONV_DIM:]))
    span = rows + HALO_PAD - SUBLANES
    for r in range(1, SUBLANES):
        fb_ref[r, 0:span, :] = fb_ref[0, r:r + span, :]

    pqk = _dot(xn, win_ref[:, C_Q:C_Q + 2 * QK_W])
    pv = _dot(xn, win_ref[:, C_V:C_V + V_W]).astype(BF16)
    pg = _dot(xn, win_ref[:, C_G:C_G + V_W])

    first = HALO_PAD - HALO
    for rb in range(rows // CONV_ROWS):
        acc = jnp.broadcast_to(bdw_ref[...], (CONV_ROWS, CONV_DIM))
        for j in range(CONV_WIDTH):
            off = first + j
            base = rb * CONV_ROWS + off - off % SUBLANES
            slab = fb_ref[off % SUBLANES, base:base + CONV_ROWS, :]
            acc = acc + (slab.reshape(CONV_ROWS // SUBLANES, SUBLANES, CONV_DIM)
                         * wb_ref[j][None]).reshape(CONV_ROWS, CONV_DIM)
        conv_ref[rb * CONV_ROWS:(rb + 1) * CONV_ROWS, :] = acc
    aact_ref[...] = _silu(_layer_norm(conv_ref[...], lnw_ref[...], lnb_ref[...])).astype(BF16)

    cos2 = cos_ref[...]
    sin2 = sin_ref[...]

    def rot(t):
        return t * cos2 + pltpu.roll(t, DK // 2, 1) * sin2

    heads = range(HEADS)
    q_rot = [rot(pqk[:, h * DK:(h + 1) * DK]).astype(BF16) for h in heads]
    k_rot = [rot(pqk[:, QK_W + h * DK:QK_W + (h + 1) * DK]) * K_SCALE for h in heads]
    k_bf = [k.astype(BF16) for k in k_rot]
    g_act = [_silu(pg[:, h * DV:(h + 1) * DV]) for h in heads]

    n_chunks = rows // CHUNK
    n_fill = 2 * n_chunks
    gate_cols = 2 * D_MODEL // n_fill

    def gate_slice(f):
        gate_ref[:, f * gate_cols:(f + 1) * gate_cols] = _dot(
            xn, win_ref[:, C_GATES + f * gate_cols:C_GATES + (f + 1) * gate_cols])

    for c in range(n_chunks):
        rs = slice(c * CHUNK, (c + 1) * CHUNK)
        v_c = [pv[rs, h * DV:(h + 1) * DV] for h in heads]
        state = [s_ref[h] for h in heads]
        scores = [lax.dot_general(q_rot[h][rs], k_bf[h][rs], NT_DIMS,
                                  preferred_element_type=F32) for h in heads]
        cross = [_dot(q_rot[h][rs], state[h].astype(BF16)) for h in heads]
        kv = [lax.dot_general((k_rot[h][rs] * kd_ref[h]).astype(BF16), v_c[h], TN_DIMS,
                              preferred_element_type=F32) for h in heads]
        gate_slice(2 * c)
        intra = [_dot((scores[h] * dec_ref[h]).astype(BF16), v_c[h]) for h in heads]
        gate_slice(2 * c + 1)
        for h in heads:
            s_ref[h] = GAMMA_CHUNK[h] * state[h] + kv[h]
            o = intra[h] + cross[h] * qd_ref[h]
            y_ref[rs, h * DV:(h + 1) * DV] = (g_act[h][rs] * _group_norm(o)).astype(BF16)
    bout_ref[...] = _dot(y_ref[...], wro_ref[...])

    @pl.when(jnp.logical_and(i == tiles_per_seq - 1, g < n_tiles))
    def _emit_sequence_state():
        nc_ref[0] = fb_ref[0, HALO_PAD + rows - HALO:HALO_PAD + rows, :]
        sout_ref[0] = s_ref[...]

    fb_ref[0, 0:HALO_PAD, :] = fb_ref[0, rows:rows + HALO_PAD, :]


def _mixer_prompt(x, cos2, sin2, npre, npost, win, wdw, bdw, lnw, lnb, wco, wro, wo):
    batch, seq, _ = x.shape
    rows = MIX_ROWS
    nt = seq // rows
    n_tiles = batch * nt

    def front(g):
        return jnp.minimum(g, n_tiles - 1)

    def back(g):
        return jnp.maximum(g - 1, 0)

    in_specs = [
        pl.BlockSpec((1, rows, D_MODEL), lambda g: (front(g) // nt, front(g) % nt, 0)),
        pl.BlockSpec((rows, DK), lambda g: (front(g) % nt, 0)),
        pl.BlockSpec((rows, DK), lambda g: (front(g) % nt, 0)),
        _const_spec((1, D_MODEL)),
        _const_spec((1, D_MODEL)),
        _const_spec((D_MODEL, IN_COLS)),
        _const_spec((CONV_WIDTH, CONV_DIM)),
        _const_spec((1, CONV_DIM)),
        _const_spec((1, CONV_DIM)),
        _const_spec((1, CONV_DIM)),
        _const_spec((CONV_DIM, D_MODEL)),
        _const_spec((V_W, D_MODEL)),
        _const_spec((D_MODEL, D_MODEL)),
    ]
    out_specs = [
        pl.BlockSpec((1, rows, D_MODEL), lambda g: (back(g) // nt, back(g) % nt, 0)),
        pl.BlockSpec((1, HALO, CONV_DIM), lambda g: (front(g) // nt, 0, 0)),
        pl.BlockSpec((1, HEADS, DK, DV), lambda g: (front(g) // nt, 0, 0, 0)),
    ]
    out_shape = [
        jax.ShapeDtypeStruct((batch, seq, D_MODEL), F32),
        jax.ShapeDtypeStruct((batch, HALO, CONV_DIM), F32),
        jax.ShapeDtypeStruct((batch, HEADS, DK, DV), F32),
    ]
    scratch = [
        pltpu.VMEM((SUBLANES, HALO_PAD + rows, CONV_DIM), F32),
        pltpu.VMEM((rows, CONV_DIM), F32),
        pltpu.VMEM((rows, V_W), BF16),
        pltpu.VMEM((HEADS, CHUNK, CHUNK), F32),
        pltpu.VMEM((HEADS, CHUNK, DV), F32),
        pltpu.VMEM((HEADS, CHUNK, DK), F32),
        pltpu.VMEM((CONV_WIDTH, SUBLANES, CONV_DIM), F32),
        pltpu.VMEM((HEADS, DK, DV), F32),
        pltpu.VMEM((rows, CONV_DIM), BF16),
        pltpu.VMEM((rows, 2 * D_MODEL), F32),
        pltpu.VMEM((rows, D_MODEL), F32),
        pltpu.VMEM((rows, D_MODEL), F32),
    ]
    return pl.pallas_call(
        functools.partial(_mixer_prompt_kernel, tiles_per_seq=nt, n_tiles=n_tiles),
        grid=(n_tiles + 1,), in_specs=in_specs, out_specs=out_specs, out_shape=out_shape,
        scratch_shapes=scratch,
        compiler_params=pltpu.CompilerParams(
            dimension_semantics=("arbitrary",),
            vmem_limit_bytes=VMEM_LIMIT_BYTES),
        name="mixer_prompt",
    )(x, cos2, sin2, npre, npost, win, wdw, bdw, lnw, lnb, wco, wro, wo)


def _ffn_body(h, npre, npost, wup_ref, wdn_ref):
    hn = _rms(h, npre).astype(BF16)
    up = _dot(hn, wup_ref[...])
    act = jnp.square(jnp.maximum(up, 0.0)).astype(BF16)
    return h + _rms(_dot(act, wdn_ref[...]), npost)


def _ffn_kernel(h_ref, npre_ref, npost_ref, wup_ref, wdn_ref, o_ref):
    o_ref[...] = _ffn_body(h_ref[...], npre_ref[...], npost_ref[...], wup_ref, wdn_ref)


def _ffn(h, npre, npost, wup, wdn):
    n = h.shape[0]
    rows = FFN_ROWS
    return pl.pallas_call(
        _ffn_kernel,
        grid=(n // rows,),
        in_specs=[
            pl.BlockSpec((rows, D_MODEL), lambda i: (i, 0)),
            _const_spec((1, D_MODEL)),
            _const_spec((1, D_MODEL)),
            _const_spec((D_MODEL, FFN_DIM)),
            _const_spec((FFN_DIM, D_MODEL)),
        ],
        out_specs=pl.BlockSpec((rows, D_MODEL), lambda i: (i, 0)),
        out_shape=jax.ShapeDtypeStruct((n, D_MODEL), F32),
        compiler_params=pltpu.CompilerParams(
            dimension_semantics=("parallel",),
            vmem_limit_bytes=VMEM_LIMIT_BYTES),
        name="ffn_prompt",
    )(h, npre, npost, wup, wdn)


def _sample_pre_kernel(x_ref, cos_ref, sin_ref, npre_ref, win_ref,
                       u_ref, qt_ref, kt_ref, v_ref, g_ref, ga_ref, gb_ref):
    xn = _rms(x_ref[...], npre_ref[...]).astype(BF16)
    pc = _dot(xn, win_ref[:, C_CONV:C_CONV + 2 * CONV_DIM])
    u_ref[...] = pc[:, :CONV_DIM] * jax.nn.sigmoid(pc[:, CONV_DIM:])
    v_ref[...] = _dot(xn, win_ref[:, C_V:C_V + V_W])
    g_ref[...] = _dot(xn, win_ref[:, C_G:C_G + V_W])
    pgate = _dot(xn, win_ref[:, C_GATES:C_GATES + 2 * D_MODEL])
    ga_ref[...] = pgate[:, :D_MODEL]
    gb_ref[...] = pgate[:, D_MODEL:]
    qkt = _dot(xn, win_ref[:, C_Q:C_Q + 2 * QK_W]).T
    cos_t = cos_ref[...]
    sin_t = sin_ref[...]
    half = DK // 2
    for g in range(2 * HEADS):
        x1 = qkt[g * DK:g * DK + half]
        x2 = qkt[g * DK + half:(g + 1) * DK]
        o1 = x1 * cos_t - x2 * sin_t
        o2 = x2 * cos_t + x1 * sin_t
        if g < HEADS:
            qt_ref[g * DK:g * DK + half, :] = o1
            qt_ref[g * DK + half:(g + 1) * DK, :] = o2
        else:
            k0 = (g - HEADS) * DK
            kt_ref[k0:k0 + half, :] = o1 * K_SCALE
            kt_ref[k0 + half:k0 + DK, :] = o2 * K_SCALE


def _sample_pre(x, cos_t, sin_t, npre, win):
    n = x.shape[0]
    out_shape = [
        jax.ShapeDtypeStruct((n, CONV_DIM), F32),
        jax.ShapeDtypeStruct((QK_W, n), F32),
        jax.ShapeDtypeStruct((QK_W, n), F32),
        jax.ShapeDtypeStruct((n, V_W), F32),
        jax.ShapeDtypeStruct((n, V_W), F32),
        jax.ShapeDtypeStruct((n, D_MODEL), F32),
        jax.ShapeDtypeStruct((n, D_MODEL), F32),
    ]
    return pl.pallas_call(
        _sample_pre_kernel, out_shape=out_shape,
        compiler_params=pltpu.CompilerParams(vmem_limit_bytes=VMEM_LIMIT_BYTES),
        name="sample_pre",
    )(x, cos_t, sin_t, npre, win)


def _sample_state_kernel(cache_ref, u_ref, wdw_ref, bdw_ref, qt_ref, kt_ref, v_ref, st_ref,
                         conv_ref, nc_ref, o_ref, so_ref):
    n_seq = qt_ref.shape[1]
    blk = pl.program_id(0)
    u = u_ref[...]
    acc = u * wdw_ref[HALO:HALO + 1, :] + bdw_ref[...]
    for j in range(HALO):
        acc = acc + cache_ref[j] * wdw_ref[j:j + 1, :]
    conv_ref[...] = acc
    nc_ref[0:HALO - 1] = cache_ref[1:HALO]
    nc_ref[HALO - 1] = u

    shift = (n_seq - blk * SAMPLE_BLOCK) % n_seq
    qt = pltpu.roll(qt_ref[...], shift, 1)
    kt = pltpu.roll(kt_ref[...], shift, 1)
    for s in range(SAMPLE_BLOCK):
        for h in range(HEADS):
            k_col = kt[h * DK:(h + 1) * DK, s:s + 1]
            q_col = qt[h * DK:(h + 1) * DK, s:s + 1]
            v_row = v_ref[s:s + 1, h * DV:(h + 1) * DV]
            new_state = GAMMA[h] * st_ref[s, h] + k_col * v_row
            so_ref[s, h] = new_state
            o_ref[s:s + 1, h * DV:(h + 1) * DV] = jnp.sum(new_state * q_col, axis=0,
                                                          keepdims=True)


def _sample_state(cache, u, wdw, bdw, qt, kt, v, state):
    n = u.shape[0]
    sb = SAMPLE_BLOCK
    in_specs = [
        pl.BlockSpec((HALO, sb, CONV_DIM), lambda i: (0, i, 0)),
        pl.BlockSpec((sb, CONV_DIM), lambda i: (i, 0)),
        pl.BlockSpec((CONV_WIDTH, CONV_DIM), lambda i: (0, 0)),
        pl.BlockSpec((1, CONV_DIM), lambda i: (0, 0)),
        pl.BlockSpec((QK_W, n), lambda i: (0, 0)),
        pl.BlockSpec((QK_W, n), lambda i: (0, 0)),
        pl.BlockSpec((sb, V_W), lambda i: (i, 0)),
        pl.BlockSpec((sb, HEADS, DK, DV), lambda i: (i, 0, 0, 0)),
    ]
    out_specs = [
        pl.BlockSpec((sb, CONV_DIM), lambda i: (i, 0)),
        pl.BlockSpec((HALO, sb, CONV_DIM), lambda i: (0, i, 0)),
        pl.BlockSpec((sb, V_W), lambda i: (i, 0)),
        pl.BlockSpec((sb, HEADS, DK, DV), lambda i: (i, 0, 0, 0)),
    ]
    out_shape = [
        jax.ShapeDtypeStruct((n, CONV_DIM), F32),
        jax.ShapeDtypeStruct((HALO, n, CONV_DIM), F32),
        jax.ShapeDtypeStruct((n, V_W), F32),
        jax.ShapeDtypeStruct((n, HEADS, DK, DV), F32),
    ]
    return pl.pallas_call(
        _sample_state_kernel,
        grid=(n // sb,), in_specs=in_specs, out_specs=out_specs, out_shape=out_shape,
        compiler_params=pltpu.CompilerParams(
            dimension_semantics=("parallel",),
            vmem_limit_bytes=VMEM_LIMIT_BYTES),
        name="sample_state",
    )(cache, u, wdw, bdw, qt, kt, v, state)


def _sample_post_kernel(x_ref, conv_ref, o_ref, g_ref, ga_ref, gb_ref, lnw_ref, lnb_ref,
                        wco_ref, wro_ref, wo_ref, npost_ref, nfpre_ref, nfpost_ref,
                        wup_ref, wdn_ref, y_ref):
    a_act = _silu(_layer_norm(conv_ref[...], lnw_ref[...], lnb_ref[...])).astype(BF16)
    a_out = _dot(a_act, wco_ref[...])
    b_out = jnp.zeros_like(a_out)
    for h in range(HEADS):
        cols = slice(h * DV, (h + 1) * DV)
        yh = (_silu(g_ref[:, cols]) * _group_norm(o_ref[:, cols])).astype(BF16)
        b_out = b_out + _dot(yh, wro_ref[cols, :])
    merged = jax.nn.sigmoid(ga_ref[...]) * a_out + jax.nn.sigmoid(gb_ref[...]) * b_out
    m = _dot(merged.astype(BF16), wo_ref[...])
    hres = x_ref[...] + _rms(m, npost_ref[...])
    y_ref[...] = _ffn_body(hres, nfpre_ref[...], nfpost_ref[...], wup_ref, wdn_ref)


def _sample_post(x, conv, o, g, ga, gb, lnw, lnb, wco, wro, wo, npost, nfpre, nfpost, wup, wdn):
    return pl.pallas_call(
        _sample_post_kernel,
        out_shape=jax.ShapeDtypeStruct(x.shape, F32),
        compiler_params=pltpu.CompilerParams(vmem_limit_bytes=VMEM_LIMIT_BYTES),
        name="sample_post",
    )(x, conv, o, g, ga, gb, lnw, lnb, wco, wro, wo, npost, nfpre, nfpost, wup, wdn)


def _rope_angles(pos):
    half = DK // 2
    freqs = 1.0 / (ROPE_BASE ** jnp.linspace(0.0, 1.0, half, dtype=F32))
    return pos[:, None] * freqs[None, :]


def kernel(x_prompt, x_sample, cache_conv, state_ret, norm_mix_pre, norm_mix_post, w_in, w_dw, b_dw, conv_ln_w, conv_ln_b, w_conv_out, w_ret_out, w_o, norm_ffn_pre, norm_ffn_post, w_ffn_up, w_ffn_down):
    batch, seq, _ = x_prompt.shape
    n_seq, dec_seq, _ = x_sample.shape
    depth = w_in.shape[0]
    assert dec_seq == 1 and seq % MIX_ROWS == 0 and (batch * seq) % FFN_ROWS == 0
    assert n_seq % SAMPLE_BLOCK == 0

    ang_p = _rope_angles(jnp.arange(seq, dtype=F32))
    cos_p = jnp.concatenate([jnp.cos(ang_p), jnp.cos(ang_p)], axis=1)
    sin_p = jnp.concatenate([-jnp.sin(ang_p), jnp.sin(ang_p)], axis=1)
    ang_s = _rope_angles(PAST_LEN + jnp.arange(dec_seq, dtype=F32))
    cos_s = jnp.broadcast_to(jnp.cos(ang_s).T, (DK // 2, n_seq))
    sin_s = jnp.broadcast_to(jnp.sin(ang_s).T, (DK // 2, n_seq))

    xp = x_prompt
    xs = x_sample.reshape(n_seq, D_MODEL)
    conv_p, ret_p, conv_s, ret_s = [], [], [], []
    for l in range(depth):
        npre = norm_mix_pre[l][None]
        npost = norm_mix_post[l][None]
        nfpre = norm_ffn_pre[l][None]
        nfpost = norm_ffn_post[l][None]
        win = w_in[l].astype(BF16)
        wdw = w_dw[l]
        bdw = b_dw[l][None]
        lnw = conv_ln_w[l][None]
        lnb = conv_ln_b[l][None]
        wco = w_conv_out[l].astype(BF16)
        wro = w_ret_out[l].astype(BF16)
        wo = w_o[l].astype(BF16)
        wup = w_ffn_up[l].astype(BF16)
        wdn = w_ffn_down[l].astype(BF16)

        h_p, nc_p, s_p = _mixer_prompt(xp, cos_p, sin_p, npre, npost, win, wdw, bdw, lnw, lnb,
                                       wco, wro, wo)
        xp = _ffn(h_p.reshape(batch * seq, D_MODEL), nfpre, nfpost, wup, wdn
                  ).reshape(batch, seq, D_MODEL)

        u, qt, kt, v, g, ga, gb = _sample_pre(xs, cos_s, sin_s, npre, win)
        cache_t = jnp.transpose(cache_conv[l], (1, 0, 2))
        conv, nc_t, o, s_s = _sample_state(cache_t, u, wdw, bdw, qt, kt, v, state_ret[l])
        nc_s = jnp.transpose(nc_t, (1, 0, 2))
        xs = _sample_post(xs, conv, o, g, ga, gb, lnw, lnb, wco, wro, wo, npost, nfpre, nfpost,
                          wup, wdn)
        conv_p.append(nc_p)
        ret_p.append(s_p)
        conv_s.append(nc_s)
        ret_s.append(s_s)

    return (xp, xs.reshape(n_seq, dec_seq, D_MODEL), jnp.stack(conv_p), jnp.stack(ret_p),
            jnp.stack(conv_s), jnp.stack(ret_s))
```

```python
import functools
import math

import jax
import jax.numpy as jnp
from jax import lax
from jax.experimental import pallas as pl
from jax.experimental.pallas import tpu as pltpu

F32 = jnp.float32
BF16 = jnp.bfloat16

D_MODEL = 1024
CONV_DIM = 512
CONV_WIDTH = 31
HALO = CONV_WIDTH - 1
HEADS = 4
DK = 128
DV = 256
QK_W = HEADS * DK
V_W = HEADS * DV
CHUNK = 128
FFN_DIM = 4 * D_MODEL
EPS = 1e-6
ROPE_BASE = 10000.0
PAST_LEN = 16384
K_SCALE = DK ** -0.5

C_CONV = 0
C_Q = 2 * CONV_DIM
C_K = C_Q + QK_W
C_V = C_K + QK_W
C_G = C_V + V_W
C_GATES = C_G + V_W
IN_COLS = C_GATES + 2 * D_MODEL

LOG_GAMMA = tuple(math.log1p(-(2.0 ** (-5 - h))) for h in range(HEADS))
GAMMA = tuple(math.exp(lg) for lg in LOG_GAMMA)
GAMMA_CHUNK = tuple(math.exp(CHUNK * lg) for lg in LOG_GAMMA)

VMEM_LIMIT_BYTES = 56 * 1024 * 1024
SUBLANES = 8
HALO_PAD = 32
MIX_ROWS = 256
FFN_ROWS = 512
CONV_ROWS = 32
SAMPLE_BLOCK = 8

NT_DIMS = (((1,), (1,)), ((), ()))
TN_DIMS = (((0,), (0,)), ((), ()))


def _dot(a, b):
    return jnp.dot(a, b, preferred_element_type=F32)


def _rms(x, w):
    return x * lax.rsqrt(jnp.mean(x * x, axis=-1, keepdims=True) + EPS) * w


def _layer_norm(x, w, b):
    mu = jnp.mean(x, axis=-1, keepdims=True)
    xc = x - mu
    return xc * lax.rsqrt(jnp.mean(xc * xc, axis=-1, keepdims=True) + EPS) * w + b


def _silu(x):
    return x * jax.nn.sigmoid(x)


def _group_norm(o):
    return o * lax.rsqrt(jnp.mean(o * o, axis=-1, keepdims=True) + EPS)


def _const_spec(shape):
    zeros = (0,) * len(shape)
    return pl.BlockSpec(shape, lambda *_: zeros, pipeline_mode=pl.Buffered(1))


def _mixer_prompt_kernel(x_ref, cos_ref, sin_ref, npre_ref, npost_ref, win_ref, wdw_ref,
                         bdw_ref, lnw_ref, lnb_ref, wco_ref, wro_ref, wo_ref,
                         h_ref, nc_ref, sout_ref,
                         fb_ref, conv_ref, y_ref, dec_ref, qd_ref, kd_ref, wb_ref, s_ref,
                         aact_ref, gate_ref, bout_ref, xprev_ref, *, tiles_per_seq, n_tiles):
    g = pl.program_id(0)
    i = jnp.minimum(g, n_tiles - 1) % tiles_per_seq
    rows = x_ref.shape[1]

    @pl.when(g == 0)
    def _init_tables():
        aact_ref[...] = jnp.zeros(aact_ref.shape, BF16)
        gate_ref[...] = jnp.zeros(gate_ref.shape, F32)
        bout_ref[...] = jnp.zeros(bout_ref.shape, F32)
        xprev_ref[...] = jnp.zeros(xprev_ref.shape, F32)
        ii = lax.broadcasted_iota(jnp.int32, (CHUNK, CHUNK), 0)
        jj = lax.broadcasted_iota(jnp.int32, (CHUNK, CHUNK), 1)
        diff = (ii - jj).astype(F32)
        row_v = lax.broadcasted_iota(jnp.int32, (CHUNK, DV), 0).astype(F32)
        row_k = ii.astype(F32)
        for h in range(HEADS):
            lg = LOG_GAMMA[h]
            dec_ref[h] = jnp.where(diff >= 0.0, jnp.exp(jnp.maximum(diff, 0.0) * lg), 0.0)
            qd_ref[h] = jnp.exp((row_v + 1.0) * lg)
            kd_ref[h] = jnp.exp((CHUNK - 1.0 - row_k) * lg)
        for j in range(CONV_WIDTH):
            wb_ref[j] = jnp.broadcast_to(wdw_ref[j:j + 1, :], (SUBLANES, CONV_DIM))

    @pl.when(i == 0)
    def _start_sequence():
        fb_ref[0, 0:HALO_PAD, :] = jnp.zeros((HALO_PAD, CONV_DIM), F32)
        s_ref[...] = jnp.zeros(s_ref.shape, F32)

    a_out = _dot(aact_ref[...], wco_ref[...])
    x = x_ref[0]
    xn = _rms(x, npre_ref[...]).astype(BF16)
    pc = _dot(xn, win_ref[:, C_CONV:C_CONV + 2 * CONV_DIM])
    merged = (jax.nn.sigmoid(gate_ref[:, :D_MODEL]) * a_out
              + jax.nn.sigmoid(gate_ref[:, D_MODEL:]) * bout_ref[...])
    m = _dot(merged.astype(BF16), wo_ref[...])
    h_ref[0] = xprev_ref[...] + _rms(m, npost_ref[...])

    xprev_ref[...] = x

    fb_ref[0, HALO_PAD:HALO_PAD + rows, :] = (pc[:, :CONV_DIM]
                                              * jax.nn.sigmoid(pc[:, CONV_DIM:]))
    span = rows + HALO_PAD - SUBLANES
    for r in range(1, SUBLANES):
        fb_ref[r, 0:span, :] = fb_ref[0, r:r + span, :]

    pqk = _dot(xn, win_ref[:, C_Q:C_Q + 2 * QK_W])
    pv = _dot(xn, win_ref[:, C_V:C_V + V_W]).astype(BF16)
    pg = _dot(xn, win_ref[:, C_G:C_G + V_W])

    first = HALO_PAD - HALO
    for rb in range(rows // CONV_ROWS):
        acc = jnp.broadcast_to(bdw_ref[...], (CONV_ROWS, CONV_DIM))
        for j in range(CONV_WIDTH):
            off = first + j
            base = rb * CONV_ROWS + off - off % SUBLANES
            slab = fb_ref[off % SUBLANES, base:base + CONV_ROWS, :]
            acc = acc + (slab.reshape(CONV_ROWS // SUBLANES, SUBLANES, CONV_DIM)
                         * wb_ref[j][None]).reshape(CONV_ROWS, CONV_DIM)
        conv_ref[rb * CONV_ROWS:(rb + 1) * CONV_ROWS, :] = acc
    aact_ref[...] = _silu(_layer_norm(conv_ref[...], lnw_ref[...], lnb_ref[...])).astype(BF16)

    cos2 = cos_ref[...]
    sin2 = sin_ref[...]

    def rot(t):
        return t * cos2 + pltpu.roll(t, DK // 2, 1) * sin2

    heads = range(HEADS)
    q_rot = [rot(pqk[:, h * DK:(h + 1) * DK]).astype(BF16) for h in heads]
    k_rot = [rot(pqk[:, QK_W + h * DK:QK_W + (h + 1) * DK]) * K_SCALE for h in heads]
    k_bf = [k.astype(BF16) for k in k_rot]
    g_act = [_silu(pg[:, h * DV:(h + 1) * DV]) for h in heads]

    n_chunks = rows // CHUNK
    n_fill = 2 * n_chunks
    gate_cols = 2 * D_MODEL // n_fill

    def gate_slice(f):
        gate_ref[:, f * gate_cols:(f + 1) * gate_cols] = _dot(
            xn, win_ref[:, C_GATES + f * gate_cols:C_GATES + (f + 1) * gate_cols])

    for c in range(n_chunks):
        rs = slice(c * CHUNK, (c + 1) * CHUNK)
        v_c = [pv[rs, h * DV:(h + 1) * DV] for h in heads]
        state = [s_ref[h] for h in heads]
        scores = [lax.dot_general(q_rot[h][rs], k_bf[h][rs], NT_DIMS,
                                  preferred_element_type=F32) for h in heads]
        cross = [_dot(q_rot[h][rs], state[h].astype(BF16)) for h in heads]
        kv = [lax.dot_general((k_rot[h][rs] * kd_ref[h]).astype(BF16), v_c[h], TN_DIMS,
                              preferred_element_type=F32) for h in heads]
        gate_slice(2 * c)
        intra = [_dot((scores[h] * dec_ref[h]).astype(BF16), v_c[h]) for h in heads]
        gate_slice(2 * c + 1)
        for h in heads:
            s_ref[h] = GAMMA_CHUNK[h] * state[h] + kv[h]
            o = intra[h] + cross[h] * qd_ref[h]
            y_ref[rs, h * DV:(h + 1) * DV] = (g_act[h][rs] * _group_norm(o)).astype(BF16)
    bout_ref[...] = _dot(y_ref[...], wro_ref[...])

    @pl.when(jnp.logical_and(i == tiles_per_seq - 1, g < n_tiles))
    def _emit_sequence_state():
        nc_ref[0] = fb_ref[0, HALO_PAD + rows - HALO:HALO_PAD + rows, :]
        sout_ref[0] = s_ref[...]

    fb_ref[0, 0:HALO_PAD, :] = fb_ref[0, rows:rows + HALO_PAD, :]


def _mixer_prompt(x, cos2, sin2, npre, npost, win, wdw, bdw, lnw, lnb, wco, wro, wo):
    batch, seq, _ = x.shape
    rows = MIX_ROWS
    nt = seq // rows
    n_tiles = batch * nt

    def front(g):
        return jnp.minimum(g, n_tiles - 1)

    def back(g):
        return jnp.maximum(g - 1, 0)

    in_specs = [
        pl.BlockSpec((1, rows, D_MODEL), lambda g: (front(g) // nt, front(g) % nt, 0)),
        pl.BlockSpec((rows, DK), lambda g: (front(g) % nt, 0)),
        pl.BlockSpec((rows, DK), lambda g: (front(g) % nt, 0)),
        _const_spec((1, D_MODEL)),
        _const_spec((1, D_MODEL)),
        _const_spec((D_MODEL, IN_COLS)),
        _const_spec((CONV_WIDTH, CONV_DIM)),
        _const_spec((1, CONV_DIM)),
        _const_spec((1, CONV_DIM)),
        _const_spec((1, CONV_DIM)),
        _const_spec((CONV_DIM, D_MODEL)),
        _const_spec((V_W, D_MODEL)),
        _const_spec((D_MODEL, D_MODEL)),
    ]
    out_specs = [
        pl.BlockSpec((1, rows, D_MODEL), lambda g: (back(g) // nt, back(g) % nt, 0)),
        pl.BlockSpec((1, HALO, CONV_DIM), lambda g: (front(g) // nt, 0, 0)),
        pl.BlockSpec((1, HEADS, DK, DV), lambda g: (front(g) // nt, 0, 0, 0)),
    ]
    out_shape = [
        jax.ShapeDtypeStruct((batch, seq, D_MODEL), F32),
        jax.ShapeDtypeStruct((batch, HALO, CONV_DIM), F32),
        jax.ShapeDtypeStruct((batch, HEADS, DK, DV), F32),
    ]
    scratch = [
        pltpu.VMEM((SUBLANES, HALO_PAD + rows, CONV_DIM), F32),
        pltpu.VMEM((rows, CONV_DIM), F32),
        pltpu.VMEM((rows, V_W), BF16),
        pltpu.VMEM((HEADS, CHUNK, CHUNK), F32),
        pltpu.VMEM((HEADS, CHUNK, DV), F32),
        pltpu.VMEM((HEADS, CHUNK, DK), F32),
        pltpu.VMEM((CONV_WIDTH, SUBLANES, CONV_DIM), F32),
        pltpu.VMEM((HEADS, DK, DV), F32),
        pltpu.VMEM((rows, CONV_DIM), BF16),
        pltpu.VMEM((rows, 2 * D_MODEL), F32),
        pltpu.VMEM((rows, D_MODEL), F32),
        pltpu.VMEM((rows, D_MODEL), F32),
    ]
    return pl.pallas_call(
        functools.partial(_mixer_prompt_kernel, tiles_per_seq=nt, n_tiles=n_tiles),
        grid=(n_tiles + 1,), in_specs=in_specs, out_specs=out_specs, out_shape=out_shape,
        scratch_shapes=scratch,
        compiler_params=pltpu.CompilerParams(
            dimension_semantics=("arbitrary",),
            vmem_limit_bytes=VMEM_LIMIT_BYTES),
        name="mixer_prompt",
    )(x, cos2, sin2, npre, npost, win, wdw, bdw, lnw, lnb, wco, wro, wo)


def _ffn_body(h, npre, npost, wup_ref, wdn_ref):
    hn = _rms(h, npre).astype(BF16)
    up = _dot(hn, wup_ref[...])
    act = jnp.square(jnp.maximum(up, 0.0)).astype(BF16)
    return h + _rms(_dot(act, wdn_ref[...]), npost)


def _sample_state_body(blk, cache_ref, u_ref, wdw_ref, bdw_ref, qt_ref, kt_ref, v_ref, st_ref,
                       conv_ref, nc_ref, o_ref, so_ref):
    n_seq = qt_ref.shape[1]
    u = u_ref[...]
    acc = u * wdw_ref[HALO:HALO + 1, :] + bdw_ref[...]
    for j in range(HALO):
        acc = acc + cache_ref[j] * wdw_ref[j:j + 1, :]
    conv_ref[...] = acc
    nc_ref[0:HALO - 1] = cache_ref[1:HALO]
    nc_ref[HALO - 1] = u

    shift = (n_seq - blk * SAMPLE_BLOCK) % n_seq
    qt = pltpu.roll(qt_ref[...], shift, 1)
    kt = pltpu.roll(kt_ref[...], shift, 1)
    for s in range(SAMPLE_BLOCK):
        for h in range(HEADS):
            k_col = kt[h * DK:(h + 1) * DK, s:s + 1]
            q_col = qt[h * DK:(h + 1) * DK, s:s + 1]
            v_row = v_ref[s:s + 1, h * DV:(h + 1) * DV]
            new_state = GAMMA[h] * st_ref[s, h] + k_col * v_row
            so_ref[s, h] = new_state
            o_ref[s:s + 1, h * DV:(h + 1) * DV] = jnp.sum(new_state * q_col, axis=0,
                                                          keepdims=True)


def _ffn_state_kernel(h_ref, npre_ref, npost_ref, wup_ref, wdn_ref,
                      cache_ref, u_ref, wdw_ref, bdw_ref, qt_ref, kt_ref, v_ref, st_ref,
                      y_ref, conv_ref, nc_ref, o_ref, so_ref, *, steps_per_block):
    y_ref[...] = _ffn_body(h_ref[...], npre_ref[...], npost_ref[...], wup_ref, wdn_ref)
    _sample_state_body(pl.program_id(0) // steps_per_block, cache_ref, u_ref, wdw_ref, bdw_ref,
                       qt_ref, kt_ref, v_ref, st_ref, conv_ref, nc_ref, o_ref, so_ref)


def _ffn_state(h, npre, npost, wup, wdn, cache, u, wdw, bdw, qt, kt, v, state):
    n = h.shape[0]
    rows = FFN_ROWS
    steps = n // rows
    n_seq = u.shape[0]
    sb = SAMPLE_BLOCK
    spb = steps // (n_seq // sb)
    assert spb * (n_seq // sb) == steps
    in_specs = [
        pl.BlockSpec((rows, D_MODEL), lambda i: (i, 0)),
        _const_spec((1, D_MODEL)),
        _const_spec((1, D_MODEL)),
        _const_spec((D_MODEL, FFN_DIM)),
        _const_spec((FFN_DIM, D_MODEL)),
        pl.BlockSpec((HALO, sb, CONV_DIM), lambda i: (0, i // spb, 0)),
        pl.BlockSpec((sb, CONV_DIM), lambda i: (i // spb, 0)),
        _const_spec((CONV_WIDTH, CONV_DIM)),
        _const_spec((1, CONV_DIM)),
        _const_spec((QK_W, n_seq)),
        _const_spec((QK_W, n_seq)),
        pl.BlockSpec((sb, V_W), lambda i: (i // spb, 0)),
        pl.BlockSpec((sb, HEADS, DK, DV), lambda i: (i // spb, 0, 0, 0)),
    ]
    out_specs = [
        pl.BlockSpec((rows, D_MODEL), lambda i: (i, 0)),
        pl.BlockSpec((sb, CONV_DIM), lambda i: (i // spb, 0)),
        pl.BlockSpec((HALO, sb, CONV_DIM), lambda i: (0, i // spb, 0)),
        pl.BlockSpec((sb, V_W), lambda i: (i // spb, 0)),
        pl.BlockSpec((sb, HEADS, DK, DV), lambda i: (i // spb, 0, 0, 0)),
    ]
    out_shape = [
        jax.ShapeDtypeStruct((n, D_MODEL), F32),
        jax.ShapeDtypeStruct((n_seq, CONV_DIM), F32),
        jax.ShapeDtypeStruct((HALO, n_seq, CONV_DIM), F32),
        jax.ShapeDtypeStruct((n_seq, V_W), F32),
        jax.ShapeDtypeStruct((n_seq, HEADS, DK, DV), F32),
    ]
    return pl.pallas_call(
        functools.partial(_ffn_state_kernel, steps_per_block=spb),
        grid=(steps,), in_specs=in_specs, out_specs=out_specs, out_shape=out_shape,
        compiler_params=pltpu.CompilerParams(
            dimension_semantics=("arbitrary",),
            vmem_limit_bytes=VMEM_LIMIT_BYTES),
        name="ffn_prompt_sample_state",
    )(h, npre, npost, wup, wdn, cache, u, wdw, bdw, qt, kt, v, state)


def _sample_pre_kernel(x_ref, cos_ref, sin_ref, npre_ref, win_ref,
                       u_ref, qt_ref, kt_ref, v_ref, g_ref, ga_ref, gb_ref):
    xn = _rms(x_ref[...], npre_ref[...]).astype(BF16)
    pc = _dot(xn, win_ref[:, C_CONV:C_CONV + 2 * CONV_DIM])
    u_ref[...] = pc[:, :CONV_DIM] * jax.nn.sigmoid(pc[:, CONV_DIM:])
    v_ref[...] = _dot(xn, win_ref[:, C_V:C_V + V_W])
    g_ref[...] = _dot(xn, win_ref[:, C_G:C_G + V_W])
    pgate = _dot(xn, win_ref[:, C_GATES:C_GATES + 2 * D_MODEL])
    ga_ref[...] = pgate[:, :D_MODEL]
    gb_ref[...] = pgate[:, D_MODEL:]
    qkt = _dot(xn, win_ref[:, C_Q:C_Q + 2 * QK_W]).T
    cos_t = cos_ref[...]
    sin_t = sin_ref[...]
    half = DK // 2
    for g in range(2 * HEADS):
        x1 = qkt[g * DK:g * DK + half]
        x2 = qkt[g * DK + half:(g + 1) * DK]
        o1 = x1 * cos_t - x2 * sin_t
        o2 = x2 * cos_t + x1 * sin_t
        if g < HEADS:
            qt_ref[g * DK:g * DK + half, :] = o1
            qt_ref[g * DK + half:(g + 1) * DK, :] = o2
        else:
            k0 = (g - HEADS) * DK
            kt_ref[k0:k0 + half, :] = o1 * K_SCALE
            kt_ref[k0 + half:k0 + DK, :] = o2 * K_SCALE


def _sample_pre(x, cos_t, sin_t, npre, win):
    n = x.shape[0]
    out_shape = [
        jax.ShapeDtypeStruct((n, CONV_DIM), F32),
        jax.ShapeDtypeStruct((QK_W, n), F32),
        jax.ShapeDtypeStruct((QK_W, n), F32),
        jax.ShapeDtypeStruct((n, V_W), F32),
        jax.ShapeDtypeStruct((n, V_W), F32),
        jax.ShapeDtypeStruct((n, D_MODEL), F32),
        jax.ShapeDtypeStruct((n, D_MODEL), F32),
    ]
    return pl.pallas_call(
        _sample_pre_kernel, out_shape=out_shape,
        compiler_params=pltpu.CompilerParams(vmem_limit_bytes=VMEM_LIMIT_BYTES),
        name="sample_pre",
    )(x, cos_t, sin_t, npre, win)


def _sample_post_kernel(x_ref, conv_ref, o_ref, g_ref, ga_ref, gb_ref, lnw_ref, lnb_ref,
                        wco_ref, wro_ref, wo_ref, npost_ref, nfpre_ref, nfpost_ref,
                        wup_ref, wdn_ref, y_ref):
    a_act = _silu(_layer_norm(conv_ref[...], lnw_ref[...], lnb_ref[...])).astype(BF16)
    a_out = _dot(a_act, wco_ref[...])
    b_out = jnp.zeros_like(a_out)
    for h in range(HEADS):
        cols = slice(h * DV, (h + 1) * DV)
        yh = (_silu(g_ref[:, cols]) * _group_norm(o_ref[:, cols])).astype(BF16)
        b_out = b_out + _dot(yh, wro_ref[cols, :])
    merged = jax.nn.sigmoid(ga_ref[...]) * a_out + jax.nn.sigmoid(gb_ref[...]) * b_out
    m = _dot(merged.astype(BF16), wo_ref[...])
    hres = x_ref[...] + _rms(m, npost_ref[...])
    y_ref[...] = _ffn_body(hres, nfpre_ref[...], nfpost_ref[...], wup_ref, wdn_ref)


def _sample_post(x, conv, o, g, ga, gb, lnw, lnb, wco, wro, wo, npost, nfpre, nfpost, wup, wdn):
    return pl.pallas_call(
        _sample_post_kernel,
        out_shape=jax.ShapeDtypeStruct(x.shape, F32),
        compiler_params=pltpu.CompilerParams(vmem_limit_bytes=VMEM_LIMIT_BYTES),
        name="sample_post",
    )(x, conv, o, g, ga, gb, lnw, lnb, wco, wro, wo, npost, nfpre, nfpost, wup, wdn)


def _rope_angles(pos):
    half = DK // 2
    freqs = 1.0 / (ROPE_BASE ** jnp.linspace(0.0, 1.0, half, dtype=F32))
    return pos[:, None] * freqs[None, :]


def kernel(x_prompt, x_sample, cache_conv, state_ret, norm_mix_pre, norm_mix_post, w_in, w_dw, b_dw, conv_ln_w, conv_ln_b, w_conv_out, w_ret_out, w_o, norm_ffn_pre, norm_ffn_post, w_ffn_up, w_ffn_down):
    batch, seq, _ = x_prompt.shape
    n_seq, dec_seq, _ = x_sample.shape
    depth = w_in.shape[0]
    assert dec_seq == 1 and seq % MIX_ROWS == 0 and (batch * seq) % FFN_ROWS == 0
    assert n_seq % SAMPLE_BLOCK == 0

    ang_p = _rope_angles(jnp.arange(seq, dtype=F32))
    cos_p = jnp.concatenate([jnp.cos(ang_p), jnp.cos(ang_p)], axis=1)
    sin_p = jnp.concatenate([-jnp.sin(ang_p), jnp.sin(ang_p)], axis=1)
    ang_s = _rope_angles(PAST_LEN + jnp.arange(dec_seq, dtype=F32))
    cos_s = jnp.broadcast_to(jnp.cos(ang_s).T, (DK // 2, n_seq))
    sin_s = jnp.broadcast_to(jnp.sin(ang_s).T, (DK // 2, n_seq))

    xp = x_prompt
    xs = x_sample.reshape(n_seq, D_MODEL)
    conv_p, ret_p, conv_s, ret_s = [], [], [], []
    for l in range(depth):
        npre = norm_mix_pre[l][None]
        npost = norm_mix_post[l][None]
        nfpre = norm_ffn_pre[l][None]
        nfpost = norm_ffn_post[l][None]
        win = w_in[l].astype(BF16)
        wdw = w_dw[l]
        bdw = b_dw[l][None]
        lnw = conv_ln_w[l][None]
        lnb = conv_ln_b[l][None]
        wco = w_conv_out[l].astype(BF16)
        wro = w_ret_out[l].astype(BF16)
        wo = w_o[l].astype(BF16)
        wup = w_ffn_up[l].astype(BF16)
        wdn = w_ffn_down[l].astype(BF16)

        u, qt, kt, v, g, ga, gb = _sample_pre(xs, cos_s, sin_s, npre, win)
        cache_t = jnp.transpose(cache_conv[l], (1, 0, 2))
        h_p, nc_p, s_p = _mixer_prompt(xp, cos_p, sin_p, npre, npost, win, wdw, bdw, lnw, lnb,
                                       wco, wro, wo)
        y_p, conv, nc_t, o, s_s = _ffn_state(h_p.reshape(batch * seq, D_MODEL), nfpre, nfpost,
                                             wup, wdn, cache_t, u, wdw, bdw, qt, kt, v,
                                             state_ret[l])
        xp = y_p.reshape(batch, seq, D_MODEL)
        nc_s = jnp.transpose(nc_t, (1, 0, 2))
        xs = _sample_post(xs, conv, o, g, ga, gb, lnw, lnb, wco, wro, wo, npost, nfpre, nfpost,
                          wup, wdn)
        conv_p.append(nc_p)
        ret_p.append(s_p)
        conv_s.append(nc_s)
        ret_s.append(s_s)

    return (xp, xs.reshape(n_seq, dec_seq, D_MODEL), jnp.stack(conv_p), jnp.stack(ret_p),
            jnp.stack(conv_s), jnp.stack(ret_s))
```

```python
import functools
import math

import jax
import jax.numpy as jnp
from jax import lax
from jax.experimental import pallas as pl
from jax.experimental.pallas import tpu as pltpu

F32 = jnp.float32
BF16 = jnp.bfloat16

D_MODEL = 1024
CONV_DIM = 512
CONV_WIDTH = 31
HALO = CONV_WIDTH - 1
HEADS = 4
DK = 128
DV = 256
QK_W = HEADS * DK
V_W = HEADS * DV
CHUNK = 128
FFN_DIM = 4 * D_MODEL
EPS = 1e-6
ROPE_BASE = 10000.0
PAST_LEN = 16384
K_SCALE = DK ** -0.5

C_CONV = 0
C_Q = 2 * CONV_DIM
C_K = C_Q + QK_W
C_V = C_K + QK_W
C_G = C_V + V_W
C_GATES = C_G + V_W
IN_COLS = C_GATES + 2 * D_MODEL

LOG_GAMMA = tuple(math.log1p(-(2.0 ** (-5 - h))) for h in range(HEADS))
GAMMA = tuple(math.exp(lg) for lg in LOG_GAMMA)
GAMMA_CHUNK = tuple(math.exp(CHUNK * lg) for lg in LOG_GAMMA)

VMEM_LIMIT_BYTES = 60 * 1024 * 1024
SUBLANES = 8
HALO_PAD = 32
MIX_ROWS = 256
FFN_ROWS = 1024
FFN_CHUNK = 1024
CONV_ROWS = 32
SAMPLE_BLOCK = 8

NT_DIMS = (((1,), (1,)), ((), ()))
TN_DIMS = (((0,), (0,)), ((), ()))


def _dot(a, b):
    return jnp.dot(a, b, preferred_element_type=F32)


def _rms(x, w):
    return x * lax.rsqrt(jnp.mean(x * x, axis=-1, keepdims=True) + EPS) * w


def _layer_norm(x, w, b):
    mu = jnp.mean(x, axis=-1, keepdims=True)
    xc = x - mu
    return xc * lax.rsqrt(jnp.mean(xc * xc, axis=-1, keepdims=True) + EPS) * w + b


def _silu(x):
    return x * jax.nn.sigmoid(x)


def _group_norm(o):
    return o * lax.rsqrt(jnp.mean(o * o, axis=-1, keepdims=True) + EPS)


def _const_spec(shape):
    zeros = (0,) * len(shape)
    return pl.BlockSpec(shape, lambda *_: zeros, pipeline_mode=pl.Buffered(1))


def _mixer_prompt_kernel(x_ref, cos_ref, sin_ref, npre_ref, npost_ref, win_ref, wdw_ref,
                         bdw_ref, lnw_ref, lnb_ref, wco_ref, wro_ref, wo_ref,
                         h_ref, nc_ref, sout_ref,
                         fb_ref, conv_ref, y_ref, dec_ref, qd_ref, kd_ref, wb_ref, s_ref,
                         aact_ref, gate_ref, bout_ref, xprev_ref, *, tiles_per_seq, n_tiles):
    g = pl.program_id(0)
    i = jnp.minimum(g, n_tiles - 1) % tiles_per_seq
    rows = x_ref.shape[1]

    @pl.when(g == 0)
    def _init_tables():
        aact_ref[...] = jnp.zeros(aact_ref.shape, BF16)
        gate_ref[...] = jnp.zeros(gate_ref.shape, F32)
        bout_ref[...] = jnp.zeros(bout_ref.shape, F32)
        xprev_ref[...] = jnp.zeros(xprev_ref.shape, F32)
        ii = lax.broadcasted_iota(jnp.int32, (CHUNK, CHUNK), 0)
        jj = lax.broadcasted_iota(jnp.int32, (CHUNK, CHUNK), 1)
        diff = (ii - jj).astype(F32)
        row_v = lax.broadcasted_iota(jnp.int32, (CHUNK, DV), 0).astype(F32)
        row_k = ii.astype(F32)
        for h in range(HEADS):
            lg = LOG_GAMMA[h]
            dec_ref[h] = jnp.where(diff >= 0.0, jnp.exp(jnp.maximum(diff, 0.0) * lg), 0.0)
            qd_ref[h] = jnp.exp((row_v + 1.0) * lg)
            kd_ref[h] = jnp.exp((CHUNK - 1.0 - row_k) * lg)
        for j in range(CONV_WIDTH):
            wb_ref[j] = jnp.broadcast_to(wdw_ref[j:j + 1, :], (SUBLANES, CONV_DIM))

    @pl.when(i == 0)
    def _start_sequence():
        fb_ref[0, 0:HALO_PAD, :] = jnp.zeros((HALO_PAD, CONV_DIM), F32)
        s_ref[...] = jnp.zeros(s_ref.shape, F32)

    a_out = _dot(aact_ref[...], wco_ref[...])
    x = x_ref[0]
    xn = _rms(x, npre_ref[...]).astype(BF16)
    pc = _dot(xn, win_ref[:, C_CONV:C_CONV + 2 * CONV_DIM])
    merged = (jax.nn.sigmoid(gate_ref[:, :D_MODEL]) * a_out
              + jax.nn.sigmoid(gate_ref[:, D_MODEL:]) * bout_ref[...])
    m = _dot(merged.astype(BF16), wo_ref[...])
    h_ref[0] = xprev_ref[...] + _rms(m, npost_ref[...])

    xprev_ref[...] = x

    fb_ref[0, HALO_PAD:HALO_PAD + rows, :] = (pc[:, :CONV_DIM]
                                              * jax.nn.sigmoid(pc[:, CONV_DIM:]))
    span = rows + HALO_PAD - SUBLANES
    for r in range(1, SUBLANES):
        fb_ref[r, 0:span, :] = fb_ref[0, r:r + span, :]

    pqk = _dot(xn, win_ref[:, C_Q:C_Q + 2 * QK_W])
    pv = _dot(xn, win_ref[:, C_V:C_V + V_W]).astype(BF16)
    pg = _dot(xn, win_ref[:, C_G:C_G + V_W])

    first = HALO_PAD - HALO
    for rb in range(rows // CONV_ROWS):
        acc = jnp.broadcast_to(bdw_ref[...], (CONV_ROWS, CONV_DIM))
        for j in range(CONV_WIDTH):
            off = first + j
            base = rb * CONV_ROWS + off - off % SUBLANES
            slab = fb_ref[off % SUBLANES, base:base + CONV_ROWS, :]
            acc = acc + (slab.reshape(CONV_ROWS // SUBLANES, SUBLANES, CONV_DIM)
                         * wb_ref[j][None]).reshape(CONV_ROWS, CONV_DIM)
        conv_ref[rb * CONV_ROWS:(rb + 1) * CONV_ROWS, :] = acc
    aact_ref[...] = _silu(_layer_norm(conv_ref[...], lnw_ref[...], lnb_ref[...])).astype(BF16)

    cos2 = cos_ref[...]
    sin2 = sin_ref[...]

    def rot(t):
        return t * cos2 + pltpu.roll(t, DK // 2, 1) * sin2

    heads = range(HEADS)
    q_rot = [rot(pqk[:, h * DK:(h + 1) * DK]).astype(BF16) for h in heads]
    k_rot = [rot(pqk[:, QK_W + h * DK:QK_W + (h + 1) * DK]) * K_SCALE for h in heads]
    k_bf = [k.astype(BF16) for k in k_rot]
    g_act = [_silu(pg[:, h * DV:(h + 1) * DV]) for h in heads]

    n_chunks = rows // CHUNK
    n_fill = 2 * n_chunks
    gate_cols = 2 * D_MODEL // n_fill

    def gate_slice(f):
        gate_ref[:, f * gate_cols:(f + 1) * gate_cols] = _dot(
            xn, win_ref[:, C_GATES + f * gate_cols:C_GATES + (f + 1) * gate_cols])

    for c in range(n_chunks):
        rs = slice(c * CHUNK, (c + 1) * CHUNK)
        v_c = [pv[rs, h * DV:(h + 1) * DV] for h in heads]
        state = [s_ref[h] for h in heads]
        scores = [lax.dot_general(q_rot[h][rs], k_bf[h][rs], NT_DIMS,
                                  preferred_element_type=F32) for h in heads]
        cross = [_dot(q_rot[h][rs], state[h].astype(BF16)) for h in heads]
        kv = [lax.dot_general((k_rot[h][rs] * kd_ref[h]).astype(BF16), v_c[h], TN_DIMS,
                              preferred_element_type=F32) for h in heads]
        gate_slice(2 * c)
        intra = [_dot((scores[h] * dec_ref[h]).astype(BF16), v_c[h]) for h in heads]
        gate_slice(2 * c + 1)
        for h in heads:
            s_ref[h] = GAMMA_CHUNK[h] * state[h] + kv[h]
            o = intra[h] + cross[h] * qd_ref[h]
            y_ref[rs, h * DV:(h + 1) * DV] = (g_act[h][rs] * _group_norm(o)).astype(BF16)
    bout_ref[...] = _dot(y_ref[...], wro_ref[...])

    @pl.when(jnp.logical_and(i == tiles_per_seq - 1, g < n_tiles))
    def _emit_sequence_state():
        nc_ref[0] = fb_ref[0, HALO_PAD + rows - HALO:HALO_PAD + rows, :]
        sout_ref[0] = s_ref[...]

    fb_ref[0, 0:HALO_PAD, :] = fb_ref[0, rows:rows + HALO_PAD, :]


def _mixer_prompt(x, cos2, sin2, npre, npost, win, wdw, bdw, lnw, lnb, wco, wro, wo):
    batch, seq, _ = x.shape
    rows = MIX_ROWS
    nt = seq // rows
    n_tiles = batch * nt

    def front(g):
        return jnp.minimum(g, n_tiles - 1)

    def back(g):
        return jnp.maximum(g - 1, 0)

    in_specs = [
        pl.BlockSpec((1, rows, D_MODEL), lambda g: (front(g) // nt, front(g) % nt, 0)),
        pl.BlockSpec((rows, DK), lambda g: (front(g) % nt, 0)),
        pl.BlockSpec((rows, DK), lambda g: (front(g) % nt, 0)),
        _const_spec((1, D_MODEL)),
        _const_spec((1, D_MODEL)),
        _const_spec((D_MODEL, IN_COLS)),
        _const_spec((CONV_WIDTH, CONV_DIM)),
        _const_spec((1, CONV_DIM)),
        _const_spec((1, CONV_DIM)),
        _const_spec((1, CONV_DIM)),
        _const_spec((CONV_DIM, D_MODEL)),
        _const_spec((V_W, D_MODEL)),
        _const_spec((D_MODEL, D_MODEL)),
    ]
    out_specs = [
        pl.BlockSpec((1, rows, D_MODEL), lambda g: (back(g) // nt, back(g) % nt, 0)),
        pl.BlockSpec((1, HALO, CONV_DIM), lambda g: (front(g) // nt, 0, 0)),
        pl.BlockSpec((1, HEADS, DK, DV), lambda g: (front(g) // nt, 0, 0, 0)),
    ]
    out_shape = [
        jax.ShapeDtypeStruct((batch, seq, D_MODEL), F32),
        jax.ShapeDtypeStruct((batch, HALO, CONV_DIM), F32),
        jax.ShapeDtypeStruct((batch, HEADS, DK, DV), F32),
    ]
    scratch = [
        pltpu.VMEM((SUBLANES, HALO_PAD + rows, CONV_DIM), F32),
        pltpu.VMEM((rows, CONV_DIM), F32),
        pltpu.VMEM((rows, V_W), BF16),
        pltpu.VMEM((HEADS, CHUNK, CHUNK), F32),
        pltpu.VMEM((HEADS, CHUNK, DV), F32),
        pltpu.VMEM((HEADS, CHUNK, DK), F32),
        pltpu.VMEM((CONV_WIDTH, SUBLANES, CONV_DIM), F32),
        pltpu.VMEM((HEADS, DK, DV), F32),
        pltpu.VMEM((rows, CONV_DIM), BF16),
        pltpu.VMEM((rows, 2 * D_MODEL), F32),
        pltpu.VMEM((rows, D_MODEL), F32),
        pltpu.VMEM((rows, D_MODEL), F32),
    ]
    return pl.pallas_call(
        functools.partial(_mixer_prompt_kernel, tiles_per_seq=nt, n_tiles=n_tiles),
        grid=(n_tiles + 1,), in_specs=in_specs, out_specs=out_specs, out_shape=out_shape,
        scratch_shapes=scratch,
        compiler_params=pltpu.CompilerParams(
            dimension_semantics=("arbitrary",),
            vmem_limit_bytes=VMEM_LIMIT_BYTES),
        name="mixer_prompt",
    )(x, cos2, sin2, npre, npost, win, wdw, bdw, lnw, lnb, wco, wro, wo)


def _ffn_body(h, npre, npost, wup_ref, wdn_ref):
    hn = _rms(h, npre).astype(BF16)
    f = None
    for c in range(FFN_DIM // FFN_CHUNK):
        cols = slice(c * FFN_CHUNK, (c + 1) * FFN_CHUNK)
        up = _dot(hn, wup_ref[:, cols])
        act = jnp.square(jnp.maximum(up, 0.0)).astype(BF16)
        part = _dot(act, wdn_ref[cols, :])
        f = part if f is None else f + part
    return h + _rms(f, npost)


def _sample_state_body(blk, cache_ref, u_ref, wdw_ref, bdw_ref, qt_ref, kt_ref, v_ref, st_ref,
                       conv_ref, nc_ref, o_ref, so_ref):
    n_seq = qt_ref.shape[1]
    u = u_ref[...]
    acc = u * wdw_ref[HALO:HALO + 1, :] + bdw_ref[...]
    for j in range(HALO):
        acc = acc + cache_ref[j] * wdw_ref[j:j + 1, :]
    conv_ref[...] = acc
    nc_ref[0:HALO - 1] = cache_ref[1:HALO]
    nc_ref[HALO - 1] = u

    shift = (n_seq - blk * SAMPLE_BLOCK) % n_seq
    qt = pltpu.roll(qt_ref[...], shift, 1)
    kt = pltpu.roll(kt_ref[...], shift, 1)
    for s in range(SAMPLE_BLOCK):
        for h in range(HEADS):
            k_col = kt[h * DK:(h + 1) * DK, s:s + 1]
            q_col = qt[h * DK:(h + 1) * DK, s:s + 1]
            v_row = v_ref[s:s + 1, h * DV:(h + 1) * DV]
            new_state = GAMMA[h] * st_ref[s, h] + k_col * v_row
            so_ref[s, h] = new_state
            o_ref[s:s + 1, h * DV:(h + 1) * DV] = jnp.sum(new_state * q_col, axis=0,
                                                          keepdims=True)


def _ffn_state_kernel(h_ref, npre_ref, npost_ref, wup_ref, wdn_ref,
                      cache_ref, u_ref, wdw_ref, bdw_ref, qt_ref, kt_ref, v_ref, st_ref,
                      y_ref, conv_ref, nc_ref, o_ref, so_ref, *, steps_per_block):
    y_ref[...] = _ffn_body(h_ref[...], npre_ref[...], npost_ref[...], wup_ref, wdn_ref)
    _sample_state_body(pl.program_id(0) // steps_per_block, cache_ref, u_ref, wdw_ref, bdw_ref,
                       qt_ref, kt_ref, v_ref, st_ref, conv_ref, nc_ref, o_ref, so_ref)


def _ffn_state(h, npre, npost, wup, wdn, cache, u, wdw, bdw, qt, kt, v, state):
    n = h.shape[0]
    rows = FFN_ROWS
    steps = n // rows
    n_seq = u.shape[0]
    sb = SAMPLE_BLOCK
    spb = steps // (n_seq // sb)
    assert spb * (n_seq // sb) == steps
    in_specs = [
        pl.BlockSpec((rows, D_MODEL), lambda i: (i, 0)),
        _const_spec((1, D_MODEL)),
        _const_spec((1, D_MODEL)),
        _const_spec((D_MODEL, FFN_DIM)),
        _const_spec((FFN_DIM, D_MODEL)),
        pl.BlockSpec((HALO, sb, CONV_DIM), lambda i: (0, i // spb, 0)),
        pl.BlockSpec((sb, CONV_DIM), lambda i: (i // spb, 0)),
        _const_spec((CONV_WIDTH, CONV_DIM)),
        _const_spec((1, CONV_DIM)),
        _const_spec((QK_W, n_seq)),
        _const_spec((QK_W, n_seq)),
        pl.BlockSpec((sb, V_W), lambda i: (i // spb, 0)),
        pl.BlockSpec((sb, HEADS, DK, DV), lambda i: (i // spb, 0, 0, 0)),
    ]
    out_specs = [
        pl.BlockSpec((rows, D_MODEL), lambda i: (i, 0)),
        pl.BlockSpec((sb, CONV_DIM), lambda i: (i // spb, 0)),
        pl.BlockSpec((HALO, sb, CONV_DIM), lambda i: (0, i // spb, 0)),
        pl.BlockSpec((sb, V_W), lambda i: (i // spb, 0)),
        pl.BlockSpec((sb, HEADS, DK, DV), lambda i: (i // spb, 0, 0, 0)),
    ]
    out_shape = [
        jax.ShapeDtypeStruct((n, D_MODEL), F32),
        jax.ShapeDtypeStruct((n_seq, CONV_DIM), F32),
        jax.ShapeDtypeStruct((HALO, n_seq, CONV_DIM), F32),
        jax.ShapeDtypeStruct((n_seq, V_W), F32),
        jax.ShapeDtypeStruct((n_seq, HEADS, DK, DV), F32),
    ]
    return pl.pallas_call(
        functools.partial(_ffn_state_kernel, steps_per_block=spb),
        grid=(steps,), in_specs=in_specs, out_specs=out_specs, out_shape=out_shape,
        compiler_params=pltpu.CompilerParams(
            dimension_semantics=("arbitrary",),
            vmem_limit_bytes=VMEM_LIMIT_BYTES),
        name="ffn_prompt_sample_state",
    )(h, npre, npost, wup, wdn, cache, u, wdw, bdw, qt, kt, v, state)


def _sample_pre_kernel(x_ref, cos_ref, sin_ref, npre_ref, win_ref,
                       u_ref, qt_ref, kt_ref, v_ref, g_ref, ga_ref, gb_ref):
    xn = _rms(x_ref[...], npre_ref[...]).astype(BF16)
    pc = _dot(xn, win_ref[:, C_CONV:C_CONV + 2 * CONV_DIM])
    u_ref[...] = pc[:, :CONV_DIM] * jax.nn.sigmoid(pc[:, CONV_DIM:])
    v_ref[...] = _dot(xn, win_ref[:, C_V:C_V + V_W])
    g_ref[...] = _dot(xn, win_ref[:, C_G:C_G + V_W])
    pgate = _dot(xn, win_ref[:, C_GATES:C_GATES + 2 * D_MODEL])
    ga_ref[...] = pgate[:, :D_MODEL]
    gb_ref[...] = pgate[:, D_MODEL:]
    qkt = _dot(xn, win_ref[:, C_Q:C_Q + 2 * QK_W]).T
    cos_t = cos_ref[...]
    sin_t = sin_ref[...]
    half = DK // 2
    for g in range(2 * HEADS):
        x1 = qkt[g * DK:g * DK + half]
        x2 = qkt[g * DK + half:(g + 1) * DK]
        o1 = x1 * cos_t - x2 * sin_t
        o2 = x2 * cos_t + x1 * sin_t
        if g < HEADS:
            qt_ref[g * DK:g * DK + half, :] = o1
            qt_ref[g * DK + half:(g + 1) * DK, :] = o2
        else:
            k0 = (g - HEADS) * DK
            kt_ref[k0:k0 + half, :] = o1 * K_SCALE
            kt_ref[k0 + half:k0 + DK, :] = o2 * K_SCALE


def _sample_pre(x, cos_t, sin_t, npre, win):
    n = x.shape[0]
    out_shape = [
        jax.ShapeDtypeStruct((n, CONV_DIM), F32),
        jax.ShapeDtypeStruct((QK_W, n), F32),
        jax.ShapeDtypeStruct((QK_W, n), F32),
        jax.ShapeDtypeStruct((n, V_W), F32),
        jax.ShapeDtypeStruct((n, V_W), F32),
        jax.ShapeDtypeStruct((n, D_MODEL), F32),
        jax.ShapeDtypeStruct((n, D_MODEL), F32),
    ]
    return pl.pallas_call(
        _sample_pre_kernel, out_shape=out_shape,
        compiler_params=pltpu.CompilerParams(vmem_limit_bytes=VMEM_LIMIT_BYTES),
        name="sample_pre",
    )(x, cos_t, sin_t, npre, win)


def _sample_post_kernel(x_ref, conv_ref, o_ref, g_ref, ga_ref, gb_ref, lnw_ref, lnb_ref,
                        wco_ref, wro_ref, wo_ref, npost_ref, nfpre_ref, nfpost_ref,
                        wup_ref, wdn_ref, y_ref):
    a_act = _silu(_layer_norm(conv_ref[...], lnw_ref[...], lnb_ref[...])).astype(BF16)
    a_out = _dot(a_act, wco_ref[...])
    b_out = jnp.zeros_like(a_out)
    for h in range(HEADS):
        cols = slice(h * DV, (h + 1) * DV)
        yh = (_silu(g_ref[:, cols]) * _group_norm(o_ref[:, cols])).astype(BF16)
        b_out = b_out + _dot(yh, wro_ref[cols, :])
    merged = jax.nn.sigmoid(ga_ref[...]) * a_out + jax.nn.sigmoid(gb_ref[...]) * b_out
    m = _dot(merged.astype(BF16), wo_ref[...])
    hres = x_ref[...] + _rms(m, npost_ref[...])
    y_ref[...] = _ffn_body(hres, nfpre_ref[...], nfpost_ref[...], wup_ref, wdn_ref)


def _sample_post(x, conv, o, g, ga, gb, lnw, lnb, wco, wro, wo, npost, nfpre, nfpost, wup, wdn):
    return pl.pallas_call(
        _sample_post_kernel,
        out_shape=jax.ShapeDtypeStruct(x.shape, F32),
        compiler_params=pltpu.CompilerParams(vmem_limit_bytes=VMEM_LIMIT_BYTES),
        name="sample_post",
    )(x, conv, o, g, ga, gb, lnw, lnb, wco, wro, wo, npost, nfpre, nfpost, wup, wdn)


def _rope_angles(pos):
    half = DK // 2
    freqs = 1.0 / (ROPE_BASE ** jnp.linspace(0.0, 1.0, half, dtype=F32))
    return pos[:, None] * freqs[None, :]


def kernel(x_prompt, x_sample, cache_conv, state_ret, norm_mix_pre, norm_mix_post, w_in, w_dw, b_dw, conv_ln_w, conv_ln_b, w_conv_out, w_ret_out, w_o, norm_ffn_pre, norm_ffn_post, w_ffn_up, w_ffn_down):
    batch, seq, _ = x_prompt.shape
    n_seq, dec_seq, _ = x_sample.shape
    depth = w_in.shape[0]
    assert dec_seq == 1 and seq % MIX_ROWS == 0 and (batch * seq) % FFN_ROWS == 0
    assert n_seq % SAMPLE_BLOCK == 0

    ang_p = _rope_angles(jnp.arange(seq, dtype=F32))
    cos_p = jnp.concatenate([jnp.cos(ang_p), jnp.cos(ang_p)], axis=1)
    sin_p = jnp.concatenate([-jnp.sin(ang_p), jnp.sin(ang_p)], axis=1)
    ang_s = _rope_angles(PAST_LEN + jnp.arange(dec_seq, dtype=F32))
    cos_s = jnp.broadcast_to(jnp.cos(ang_s).T, (DK // 2, n_seq))
    sin_s = jnp.broadcast_to(jnp.sin(ang_s).T, (DK // 2, n_seq))

    xp = x_prompt
    xs = x_sample.reshape(n_seq, D_MODEL)
    conv_p, ret_p, conv_s, ret_s = [], [], [], []
    for l in range(depth):
        npre = norm_mix_pre[l][None]
        npost = norm_mix_post[l][None]
        nfpre = norm_ffn_pre[l][None]
        nfpost = norm_ffn_post[l][None]
        win = w_in[l].astype(BF16)
        wdw = w_dw[l]
        bdw = b_dw[l][None]
        lnw = conv_ln_w[l][None]
        lnb = conv_ln_b[l][None]
        wco = w_conv_out[l].astype(BF16)
        wro = w_ret_out[l].astype(BF16)
        wo = w_o[l].astype(BF16)
        wup = w_ffn_up[l].astype(BF16)
        wdn = w_ffn_down[l].astype(BF16)

        u, qt, kt, v, g, ga, gb = _sample_pre(xs, cos_s, sin_s, npre, win)
        cache_t = jnp.transpose(cache_conv[l], (1, 0, 2))
        h_p, nc_p, s_p = _mixer_prompt(xp, cos_p, sin_p, npre, npost, win, wdw, bdw, lnw, lnb,
                                       wco, wro, wo)
        y_p, conv, nc_t, o, s_s = _ffn_state(h_p.reshape(batch * seq, D_MODEL), nfpre, nfpost,
                                             wup, wdn, cache_t, u, wdw, bdw, qt, kt, v,
                                             state_ret[l])
        xp = y_p.reshape(batch, seq, D_MODEL)
        nc_s = jnp.transpose(nc_t, (1, 0, 2))
        xs = _sample_post(xs, conv, o, g, ga, gb, lnw, lnb, wco, wro, wo, npost, nfpre, nfpost,
                          wup, wdn)
        conv_p.append(nc_p)
        ret_p.append(s_p)
        conv_s.append(nc_s)
        ret_s.append(s_s)

    return (xp, xs.reshape(n_seq, dec_seq, D_MODEL), jnp.stack(conv_p), jnp.stack(ret_p),
            jnp.stack(conv_s), jnp.stack(ret_s))
```

```python
import functools
import math

import jax
import jax.numpy as jnp
from jax import lax
from jax.experimental import pallas as pl
from jax.experimental.pallas import tpu as pltpu

F32 = jnp.float32
BF16 = jnp.bfloat16

D_MODEL = 1024
CONV_DIM = 512
CONV_WIDTH = 31
HALO = CONV_WIDTH - 1
HEADS = 4
DK = 128
DV = 256
QK_W = HEADS * DK
V_W = HEADS * DV
CHUNK = 128
FFN_DIM = 4 * D_MODEL
EPS = 1e-6
ROPE_BASE = 10000.0
PAST_LEN = 16384
K_SCALE = DK ** -0.5

C_CONV = 0
C_Q = 2 * CONV_DIM
C_K = C_Q + QK_W
C_V = C_K + QK_W
C_G = C_V + V_W
C_GATES = C_G + V_W
IN_COLS = C_GATES + 2 * D_MODEL

LOG_GAMMA = tuple(math.log1p(-(2.0 ** (-5 - h))) for h in range(HEADS))
GAMMA = tuple(math.exp(lg) for lg in LOG_GAMMA)
GAMMA_CHUNK = tuple(math.exp(CHUNK * lg) for lg in LOG_GAMMA)

VMEM_LIMIT_BYTES = 60 * 1024 * 1024
SUBLANES = 8
HALO_PAD = 32
MIX_ROWS = 512
FFN_ROWS = 1024
FFN_CHUNK = 1024
CONV_ROWS = 32
SAMPLE_BLOCK = 8

NT_DIMS = (((1,), (1,)), ((), ()))
TN_DIMS = (((0,), (0,)), ((), ()))


def _dot(a, b):
    return jnp.dot(a, b, preferred_element_type=F32)


def _rms(x, w):
    return x * lax.rsqrt(jnp.mean(x * x, axis=-1, keepdims=True) + EPS) * w


def _layer_norm(x, w, b):
    mu = jnp.mean(x, axis=-1, keepdims=True)
    xc = x - mu
    return xc * lax.rsqrt(jnp.mean(xc * xc, axis=-1, keepdims=True) + EPS) * w + b


def _silu(x):
    return x * jax.nn.sigmoid(x)


def _group_norm(o):
    return o * lax.rsqrt(jnp.mean(o * o, axis=-1, keepdims=True) + EPS)


def _const_spec(shape):
    zeros = (0,) * len(shape)
    return pl.BlockSpec(shape, lambda *_: zeros, pipeline_mode=pl.Buffered(1))


def _mixer_prompt_kernel(x_ref, cos_ref, sin_ref, npre_ref, npost_ref, win_ref, wdw_ref,
                         bdw_ref, lnw_ref, lnb_ref, wco_ref, wro_ref, wo_ref,
                         h_ref, nc_ref, sout_ref,
                         fb_ref, conv_ref, y_ref, dec_ref, qd_ref, kd_ref, wb_ref, s_ref,
                         aact_ref, gate_ref, bout_ref, xprev_ref, *, tiles_per_seq, n_tiles):
    g = pl.program_id(0)
    i = jnp.minimum(g, n_tiles - 1) % tiles_per_seq
    rows = x_ref.shape[1]

    @pl.when(g == 0)
    def _init_tables():
        aact_ref[...] = jnp.zeros(aact_ref.shape, BF16)
        gate_ref[...] = jnp.zeros(gate_ref.shape, F32)
        bout_ref[...] = jnp.zeros(bout_ref.shape, F32)
        xprev_ref[...] = jnp.zeros(xprev_ref.shape, F32)
        ii = lax.broadcasted_iota(jnp.int32, (CHUNK, CHUNK), 0)
        jj = lax.broadcasted_iota(jnp.int32, (CHUNK, CHUNK), 1)
        diff = (ii - jj).astype(F32)
        row_v = lax.broadcasted_iota(jnp.int32, (CHUNK, DV), 0).astype(F32)
        row_k = ii.astype(F32)
        for h in range(HEADS):
            lg = LOG_GAMMA[h]
            dec_ref[h] = jnp.where(diff >= 0.0, jnp.exp(jnp.maximum(diff, 0.0) * lg), 0.0)
            qd_ref[h] = jnp.exp((row_v + 1.0) * lg)
            kd_ref[h] = jnp.exp((CHUNK - 1.0 - row_k) * lg)
        for j in range(CONV_WIDTH):
            wb_ref[j] = jnp.broadcast_to(wdw_ref[j:j + 1, :], (SUBLANES, CONV_DIM))

    @pl.when(i == 0)
    def _start_sequence():
        fb_ref[0, 0:HALO_PAD, :] = jnp.zeros((HALO_PAD, CONV_DIM), F32)
        s_ref[...] = jnp.zeros(s_ref.shape, F32)

    a_out = _dot(aact_ref[...], wco_ref[...])
    x = x_ref[0]
    xn = _rms(x, npre_ref[...]).astype(BF16)
    pc = _dot(xn, win_ref[:, C_CONV:C_CONV + 2 * CONV_DIM])
    merged = (jax.nn.sigmoid(gate_ref[:, :D_MODEL]) * a_out
              + jax.nn.sigmoid(gate_ref[:, D_MODEL:]) * bout_ref[...])
    m = _dot(merged.astype(BF16), wo_ref[...])
    h_ref[0] = xprev_ref[...] + _rms(m, npost_ref[...])

    xprev_ref[...] = x

    fb_ref[0, HALO_PAD:HALO_PAD + rows, :] = (pc[:, :CONV_DIM]
                                              * jax.nn.sigmoid(pc[:, CONV_DIM:]))
    span = rows + HALO_PAD - SUBLANES
    for r in range(1, SUBLANES):
        fb_ref[r, 0:span, :] = fb_ref[0, r:r + span, :]

    pqk = _dot(xn, win_ref[:, C_Q:C_Q + 2 * QK_W])
    pv = _dot(xn, win_ref[:, C_V:C_V + V_W]).astype(BF16)
    pg = _dot(xn, win_ref[:, C_G:C_G + V_W])

    first = HALO_PAD - HALO
    for rb in range(rows // CONV_ROWS):
        acc = jnp.broadcast_to(bdw_ref[...], (CONV_ROWS, CONV_DIM))
        for j in range(CONV_WIDTH):
            off = first + j
            base = rb * CONV_ROWS + off - off % SUBLANES
            slab = fb_ref[off % SUBLANES, base:base + CONV_ROWS, :]
            acc = acc + (slab.reshape(CONV_ROWS // SUBLANES, SUBLANES, CONV_DIM)
                         * wb_ref[j][None]).reshape(CONV_ROWS, CONV_DIM)
        conv_ref[rb * CONV_ROWS:(rb + 1) * CONV_ROWS, :] = acc
    aact_ref[...] = _silu(_layer_norm(conv_ref[...], lnw_ref[...], lnb_ref[...])).astype(BF16)

    cos2 = cos_ref[...]
    sin2 = sin_ref[...]

    def rot(t):
        return t * cos2 + pltpu.roll(t, DK // 2, 1) * sin2

    heads = range(HEADS)
    q_rot = [rot(pqk[:, h * DK:(h + 1) * DK]).astype(BF16) for h in heads]
    k_rot = [rot(pqk[:, QK_W + h * DK:QK_W + (h + 1) * DK]) * K_SCALE for h in heads]
    k_bf = [k.astype(BF16) for k in k_rot]
    g_act = [_silu(pg[:, h * DV:(h + 1) * DV]) for h in heads]

    n_chunks = rows // CHUNK
    n_fill = 2 * n_chunks
    gate_cols = 2 * D_MODEL // n_fill

    def gate_slice(f):
        gate_ref[:, f * gate_cols:(f + 1) * gate_cols] = _dot(
            xn, win_ref[:, C_GATES + f * gate_cols:C_GATES + (f + 1) * gate_cols])

    for c in range(n_chunks):
        rs = slice(c * CHUNK, (c + 1) * CHUNK)
        v_c = [pv[rs, h * DV:(h + 1) * DV] for h in heads]
        state = [s_ref[h] for h in heads]
        scores = [lax.dot_general(q_rot[h][rs], k_bf[h][rs], NT_DIMS,
                                  preferred_element_type=F32) for h in heads]
        cross = [_dot(q_rot[h][rs], state[h].astype(BF16)) for h in heads]
        kv = [lax.dot_general((k_rot[h][rs] * kd_ref[h]).astype(BF16), v_c[h], TN_DIMS,
                              preferred_element_type=F32) for h in heads]
        gate_slice(2 * c)
        intra = [_dot((scores[h] * dec_ref[h]).astype(BF16), v_c[h]) for h in heads]
        gate_slice(2 * c + 1)
        for h in heads:
            s_ref[h] = GAMMA_CHUNK[h] * state[h] + kv[h]
            o = intra[h] + cross[h] * qd_ref[h]
            y_ref[rs, h * DV:(h + 1) * DV] = (g_act[h][rs] * _group_norm(o)).astype(BF16)
    bout_ref[...] = _dot(y_ref[...], wro_ref[...])

    @pl.when(jnp.logical_and(i == tiles_per_seq - 1, g < n_tiles))
    def _emit_sequence_state():
        nc_ref[0] = fb_ref[0, HALO_PAD + rows - HALO:HALO_PAD + rows, :]
        sout_ref[0] = s_ref[...]

    fb_ref[0, 0:HALO_PAD, :] = fb_ref[0, rows:rows + HALO_PAD, :]


def _mixer_prompt(x, cos2, sin2, npre, npost, win, wdw, bdw, lnw, lnb, wco, wro, wo):
    batch, seq, _ = x.shape
    rows = MIX_ROWS
    nt = seq // rows
    n_tiles = batch * nt

    def front(g):
        return jnp.minimum(g, n_tiles - 1)

    def back(g):
        return jnp.maximum(g - 1, 0)

    in_specs = [
        pl.BlockSpec((1, rows, D_MODEL), lambda g: (front(g) // nt, front(g) % nt, 0)),
        pl.BlockSpec((rows, DK), lambda g: (front(g) % nt, 0)),
        pl.BlockSpec((rows, DK), lambda g: (front(g) % nt, 0)),
        _const_spec((1, D_MODEL)),
        _const_spec((1, D_MODEL)),
        _const_spec((D_MODEL, IN_COLS)),
        _const_spec((CONV_WIDTH, CONV_DIM)),
        _const_spec((1, CONV_DIM)),
        _const_spec((1, CONV_DIM)),
        _const_spec((1, CONV_DIM)),
        _const_spec((CONV_DIM, D_MODEL)),
        _const_spec((V_W, D_MODEL)),
        _const_spec((D_MODEL, D_MODEL)),
    ]
    out_specs = [
        pl.BlockSpec((1, rows, D_MODEL), lambda g: (back(g) // nt, back(g) % nt, 0)),
        pl.BlockSpec((1, HALO, CONV_DIM), lambda g: (front(g) // nt, 0, 0)),
        pl.BlockSpec((1, HEADS, DK, DV), lambda g: (front(g) // nt, 0, 0, 0)),
    ]
    out_shape = [
        jax.ShapeDtypeStruct((batch, seq, D_MODEL), F32),
        jax.ShapeDtypeStruct((batch, HALO, CONV_DIM), F32),
        jax.ShapeDtypeStruct((batch, HEADS, DK, DV), F32),
    ]
    scratch = [
        pltpu.VMEM((SUBLANES, HALO_PAD + rows, CONV_DIM), F32),
        pltpu.VMEM((rows, CONV_DIM), F32),
        pltpu.VMEM((rows, V_W), BF16),
        pltpu.VMEM((HEADS, CHUNK, CHUNK), F32),
        pltpu.VMEM((HEADS, CHUNK, DV), F32),
        pltpu.VMEM((HEADS, CHUNK, DK), F32),
        pltpu.VMEM((CONV_WIDTH, SUBLANES, CONV_DIM), F32),
        pltpu.VMEM((HEADS, DK, DV), F32),
        pltpu.VMEM((rows, CONV_DIM), BF16),
        pltpu.VMEM((rows, 2 * D_MODEL), F32),
        pltpu.VMEM((rows, D_MODEL), F32),
        pltpu.VMEM((rows, D_MODEL), F32),
    ]
    return pl.pallas_call(
        functools.partial(_mixer_prompt_kernel, tiles_per_seq=nt, n_tiles=n_tiles),
        grid=(n_tiles + 1,), in_specs=in_specs, out_specs=out_specs, out_shape=out_shape,
        scratch_shapes=scratch,
        compiler_params=pltpu.CompilerParams(
            dimension_semantics=("arbitrary",),
            vmem_limit_bytes=VMEM_LIMIT_BYTES),
        name="mixer_prompt",
    )(x, cos2, sin2, npre, npost, win, wdw, bdw, lnw, lnb, wco, wro, wo)


def _ffn_body(h, npre, npost, wup_ref, wdn_ref):
    hn = _rms(h, npre).astype(BF16)
    f = None
    for c in range(FFN_DIM // FFN_CHUNK):
        cols = slice(c * FFN_CHUNK, (c + 1) * FFN_CHUNK)
        up = _dot(hn, wup_ref[:, cols])
        act = jnp.square(jnp.maximum(up, 0.0)).astype(BF16)
        part = _dot(act, wdn_ref[cols, :])
        f = part if f is None else f + part
    return h + _rms(f, npost)


def _sample_state_body(blk, cache_ref, u_ref, wdw_ref, bdw_ref, qt_ref, kt_ref, v_ref, st_ref,
                       conv_ref, nc_ref, o_ref, so_ref):
    n_seq = qt_ref.shape[1]
    u = u_ref[...]
    acc = u * wdw_ref[HALO:HALO + 1, :] + bdw_ref[...]
    for j in range(HALO):
        acc = acc + cache_ref[j] * wdw_ref[j:j + 1, :]
    conv_ref[...] = acc
    nc_ref[0:HALO - 1] = cache_ref[1:HALO]
    nc_ref[HALO - 1] = u

    shift = (n_seq - blk * SAMPLE_BLOCK) % n_seq
    qt = pltpu.roll(qt_ref[...], shift, 1)
    kt = pltpu.roll(kt_ref[...], shift, 1)
    for s in range(SAMPLE_BLOCK):
        for h in range(HEADS):
            k_col = kt[h * DK:(h + 1) * DK, s:s + 1]
            q_col = qt[h * DK:(h + 1) * DK, s:s + 1]
            v_row = v_ref[s:s + 1, h * DV:(h + 1) * DV]
            new_state = GAMMA[h] * st_ref[s, h] + k_col * v_row
            so_ref[s, h] = new_state
            o_ref[s:s + 1, h * DV:(h + 1) * DV] = jnp.sum(new_state * q_col, axis=0,
                                                          keepdims=True)


def _ffn_state_kernel(h_ref, npre_ref, npost_ref, wup_ref, wdn_ref,
                      cache_ref, u_ref, wdw_ref, bdw_ref, qt_ref, kt_ref, v_ref, st_ref,
                      y_ref, conv_ref, nc_ref, o_ref, so_ref, *, steps_per_block):
    y_ref[...] = _ffn_body(h_ref[...], npre_ref[...], npost_ref[...], wup_ref, wdn_ref)
    _sample_state_body(pl.program_id(0) // steps_per_block, cache_ref, u_ref, wdw_ref, bdw_ref,
                       qt_ref, kt_ref, v_ref, st_ref, conv_ref, nc_ref, o_ref, so_ref)


def _ffn_state(h, npre, npost, wup, wdn, cache, u, wdw, bdw, qt, kt, v, state):
    n = h.shape[0]
    rows = FFN_ROWS
    steps = n // rows
    n_seq = u.shape[0]
    sb = SAMPLE_BLOCK
    spb = steps // (n_seq // sb)
    assert spb * (n_seq // sb) == steps
    in_specs = [
        pl.BlockSpec((rows, D_MODEL), lambda i: (i, 0)),
        _const_spec((1, D_MODEL)),
        _const_spec((1, D_MODEL)),
        _const_spec((D_MODEL, FFN_DIM)),
        _const_spec((FFN_DIM, D_MODEL)),
        pl.BlockSpec((HALO, sb, CONV_DIM), lambda i: (0, i // spb, 0)),
        pl.BlockSpec((sb, CONV_DIM), lambda i: (i // spb, 0)),
        _const_spec((CONV_WIDTH, CONV_DIM)),
        _const_spec((1, CONV_DIM)),
        _const_spec((QK_W, n_seq)),
        _const_spec((QK_W, n_seq)),
        pl.BlockSpec((sb, V_W), lambda i: (i // spb, 0)),
        pl.BlockSpec((sb, HEADS, DK, DV), lambda i: (i // spb, 0, 0, 0)),
    ]
    out_specs = [
        pl.BlockSpec((rows, D_MODEL), lambda i: (i, 0)),
        pl.BlockSpec((sb, CONV_DIM), lambda i: (i // spb, 0)),
        pl.BlockSpec((HALO, sb, CONV_DIM), lambda i: (0, i // spb, 0)),
        pl.BlockSpec((sb, V_W), lambda i: (i // spb, 0)),
        pl.BlockSpec((sb, HEADS, DK, DV), lambda i: (i // spb, 0, 0, 0)),
    ]
    out_shape = [
        jax.ShapeDtypeStruct((n, D_MODEL), F32),
        jax.ShapeDtypeStruct((n_seq, CONV_DIM), F32),
        jax.ShapeDtypeStruct((HALO, n_seq, CONV_DIM), F32),
        jax.ShapeDtypeStruct((n_seq, V_W), F32),
        jax.ShapeDtypeStruct((n_seq, HEADS, DK, DV), F32),
    ]
    return pl.pallas_call(
        functools.partial(_ffn_state_kernel, steps_per_block=spb),
        grid=(steps,), in_specs=in_specs, out_specs=out_specs, out_shape=out_shape,
        compiler_params=pltpu.CompilerParams(
            dimension_semantics=("arbitrary",),
            vmem_limit_bytes=VMEM_LIMIT_BYTES),
        name="ffn_prompt_sample_state",
    )(h, npre, npost, wup, wdn, cache, u, wdw, bdw, qt, kt, v, state)


def _sample_pre_kernel(x_ref, cos_ref, sin_ref, npre_ref, win_ref,
                       u_ref, qt_ref, kt_ref, v_ref, g_ref, ga_ref, gb_ref):
    xn = _rms(x_ref[...], npre_ref[...]).astype(BF16)
    pc = _dot(xn, win_ref[:, C_CONV:C_CONV + 2 * CONV_DIM])
    u_ref[...] = pc[:, :CONV_DIM] * jax.nn.sigmoid(pc[:, CONV_DIM:])
    v_ref[...] = _dot(xn, win_ref[:, C_V:C_V + V_W])
    g_ref[...] = _dot(xn, win_ref[:, C_G:C_G + V_W])
    pgate = _dot(xn, win_ref[:, C_GATES:C_GATES + 2 * D_MODEL])
    ga_ref[...] = pgate[:, :D_MODEL]
    gb_ref[...] = pgate[:, D_MODEL:]
    qkt = _dot(xn, win_ref[:, C_Q:C_Q + 2 * QK_W]).T
    cos_t = cos_ref[...]
    sin_t = sin_ref[...]
    half = DK // 2
    for g in range(2 * HEADS):
        x1 = qkt[g * DK:g * DK + half]
        x2 = qkt[g * DK + half:(g + 1) * DK]
        o1 = x1 * cos_t - x2 * sin_t
        o2 = x2 * cos_t + x1 * sin_t
        if g < HEADS:
            qt_ref[g * DK:g * DK + half, :] = o1
            qt_ref[g * DK + half:(g + 1) * DK, :] = o2
        else:
            k0 = (g - HEADS) * DK
            kt_ref[k0:k0 + half, :] = o1 * K_SCALE
            kt_ref[k0 + half:k0 + DK, :] = o2 * K_SCALE


def _sample_pre(x, cos_t, sin_t, npre, win):
    n = x.shape[0]
    out_shape = [
        jax.ShapeDtypeStruct((n, CONV_DIM), F32),
        jax.ShapeDtypeStruct((QK_W, n), F32),
        jax.ShapeDtypeStruct((QK_W, n), F32),
        jax.ShapeDtypeStruct((n, V_W), F32),
        jax.ShapeDtypeStruct((n, V_W), F32),
        jax.ShapeDtypeStruct((n, D_MODEL), F32),
        jax.ShapeDtypeStruct((n, D_MODEL), F32),
    ]
    return pl.pallas_call(
        _sample_pre_kernel, out_shape=out_shape,
        compiler_params=pltpu.CompilerParams(vmem_limit_bytes=VMEM_LIMIT_BYTES),
        name="sample_pre",
    )(x, cos_t, sin_t, npre, win)


def _sample_post_kernel(x_ref, conv_ref, o_ref, g_ref, ga_ref, gb_ref, lnw_ref, lnb_ref,
                        wco_ref, wro_ref, wo_ref, npost_ref, nfpre_ref, nfpost_ref,
                        wup_ref, wdn_ref, y_ref):
    a_act = _silu(_layer_norm(conv_ref[...], lnw_ref[...], lnb_ref[...])).astype(BF16)
    a_out = _dot(a_act, wco_ref[...])
    b_out = jnp.zeros_like(a_out)
    for h in range(HEADS):
        cols = slice(h * DV, (h + 1) * DV)
        yh = (_silu(g_ref[:, cols]) * _group_norm(o_ref[:, cols])).astype(BF16)
        b_out = b_out + _dot(yh, wro_ref[cols, :])
    merged = jax.nn.sigmoid(ga_ref[...]) * a_out + jax.nn.sigmoid(gb_ref[...]) * b_out
    m = _dot(merged.astype(BF16), wo_ref[...])
    hres = x_ref[...] + _rms(m, npost_ref[...])
    y_ref[...] = _ffn_body(hres, nfpre_ref[...], nfpost_ref[...], wup_ref, wdn_ref)


def _sample_post(x, conv, o, g, ga, gb, lnw, lnb, wco, wro, wo, npost, nfpre, nfpost, wup, wdn):
    return pl.pallas_call(
        _sample_post_kernel,
        out_shape=jax.ShapeDtypeStruct(x.shape, F32),
        compiler_params=pltpu.CompilerParams(vmem_limit_bytes=VMEM_LIMIT_BYTES),
        name="sample_post",
    )(x, conv, o, g, ga, gb, lnw, lnb, wco, wro, wo, npost, nfpre, nfpost, wup, wdn)


def _rope_angles(pos):
    half = DK // 2
    freqs = 1.0 / (ROPE_BASE ** jnp.linspace(0.0, 1.0, half, dtype=F32))
    return pos[:, None] * freqs[None, :]


def kernel(x_prompt, x_sample, cache_conv, state_ret, norm_mix_pre, norm_mix_post, w_in, w_dw, b_dw, conv_ln_w, conv_ln_b, w_conv_out, w_ret_out, w_o, norm_ffn_pre, norm_ffn_post, w_ffn_up, w_ffn_down):
    batch, seq, _ = x_prompt.shape
    n_seq, dec_seq, _ = x_sample.shape
    depth = w_in.shape[0]
    assert dec_seq == 1 and seq % MIX_ROWS == 0 and (batch * seq) % FFN_ROWS == 0
    assert n_seq % SAMPLE_BLOCK == 0

    ang_p = _rope_angles(jnp.arange(seq, dtype=F32))
    cos_p = jnp.concatenate([jnp.cos(ang_p), jnp.cos(ang_p)], axis=1)
    sin_p = jnp.concatenate([-jnp.sin(ang_p), jnp.sin(ang_p)], axis=1)
    ang_s = _rope_angles(PAST_LEN + jnp.arange(dec_seq, dtype=F32))
    cos_s = jnp.broadcast_to(jnp.cos(ang_s).T, (DK // 2, n_seq))
    sin_s = jnp.broadcast_to(jnp.sin(ang_s).T, (DK // 2, n_seq))

    xp = x_prompt
    xs = x_sample.reshape(n_seq, D_MODEL)
    conv_p, ret_p, conv_s, ret_s = [], [], [], []
    for l in range(depth):
        npre = norm_mix_pre[l][None]
        npost = norm_mix_post[l][None]
        nfpre = norm_ffn_pre[l][None]
        nfpost = norm_ffn_post[l][None]
        win = w_in[l].astype(BF16)
        wdw = w_dw[l]
        bdw = b_dw[l][None]
        lnw = conv_ln_w[l][None]
        lnb = conv_ln_b[l][None]
        wco = w_conv_out[l].astype(BF16)
        wro = w_ret_out[l].astype(BF16)
        wo = w_o[l].astype(BF16)
        wup = w_ffn_up[l].astype(BF16)
        wdn = w_ffn_down[l].astype(BF16)

        u, qt, kt, v, g, ga, gb = _sample_pre(xs, cos_s, sin_s, npre, win)
        cache_t = jnp.transpose(cache_conv[l], (1, 0, 2))
        h_p, nc_p, s_p = _mixer_prompt(xp, cos_p, sin_p, npre, npost, win, wdw, bdw, lnw, lnb,
                                       wco, wro, wo)
        y_p, conv, nc_t, o, s_s = _ffn_state(h_p.reshape(batch * seq, D_MODEL), nfpre, nfpost,
                                             wup, wdn, cache_t, u, wdw, bdw, qt, kt, v,
                                             state_ret[l])
        xp = y_p.reshape(batch, seq, D_MODEL)
        nc_s = jnp.transpose(nc_t, (1, 0, 2))
        xs = _sample_post(xs, conv, o, g, ga, gb, lnw, lnb, wco, wro, wo, npost, nfpre, nfpost,
                          wup, wdn)
        conv_p.append(nc_p)
        ret_p.append(s_p)
        conv_s.append(nc_s)
        ret_s.append(s_s)

    return (xp, xs.reshape(n_seq, dec_seq, D_MODEL), jnp.stack(conv_p), jnp.stack(ret_p),
            jnp.stack(conv_s), jnp.stack(ret_s))
```

```python
import functools
import math

import jax
import jax.numpy as jnp
from jax import lax
from jax.experimental import pallas as pl
from jax.experimental.pallas import tpu as pltpu

F32 = jnp.float32
BF16 = jnp.bfloat16

D_MODEL = 1024
CONV_DIM = 512
CONV_WIDTH = 31
HALO = CONV_WIDTH - 1
HEADS = 4
DK = 128
DV = 256
QK_W = HEADS * DK
V_W = HEADS * DV
CHUNK = 128
FFN_DIM = 4 * D_MODEL
EPS = 1e-6
ROPE_BASE = 10000.0
PAST_LEN = 16384
K_SCALE = DK ** -0.5

C_CONV = 0
C_Q = 2 * CONV_DIM
C_K = C_Q + QK_W
C_V = C_K + QK_W
C_G = C_V + V_W
C_GATES = C_G + V_W
IN_COLS = C_GATES + 2 * D_MODEL

LOG_GAMMA = tuple(math.log1p(-(2.0 ** (-5 - h))) for h in range(HEADS))
GAMMA = tuple(math.exp(lg) for lg in LOG_GAMMA)
GAMMA_CHUNK = tuple(math.exp(CHUNK * lg) for lg in LOG_GAMMA)

VMEM_LIMIT_BYTES = 60 * 1024 * 1024
SUBLANES = 8
HALO_PAD = 32
MIX_ROWS = 512
FFN_ROWS = 1024
FFN_CHUNK = 1024
CONV_TILE = 256
CONV_ROWS = 32
SAMPLE_BLOCK = 8

NT_DIMS = (((1,), (1,)), ((), ()))
TN_DIMS = (((0,), (0,)), ((), ()))


def _dot(a, b):
    return jnp.dot(a, b, preferred_element_type=F32)


def _rms(x, w):
    return x * lax.rsqrt(jnp.mean(x * x, axis=-1, keepdims=True) + EPS) * w


def _layer_norm(x, w, b):
    mu = jnp.mean(x, axis=-1, keepdims=True)
    xc = x - mu
    return xc * lax.rsqrt(jnp.mean(xc * xc, axis=-1, keepdims=True) + EPS) * w + b


def _silu(x):
    return x * jax.nn.sigmoid(x)


def _group_norm(o):
    return o * lax.rsqrt(jnp.mean(o * o, axis=-1, keepdims=True) + EPS)


def _const_spec(shape):
    zeros = (0,) * len(shape)
    return pl.BlockSpec(shape, lambda *_: zeros, pipeline_mode=pl.Buffered(1))


def _mixer_prompt_kernel(x_ref, cos_ref, sin_ref, npre_ref, npost_ref, win_ref, wdw_ref,
                         bdw_ref, lnw_ref, lnb_ref, wco_ref, wro_ref, wo_ref,
                         h_ref, nc_ref, sout_ref,
                         fb_ref, u_ref, y_ref, dec_ref, qd_ref, kd_ref, wb_ref, s_ref,
                         aact_ref, gate_ref, bout_ref, xprev_ref, *, tiles_per_seq, n_tiles):
    g = pl.program_id(0)
    i = jnp.minimum(g, n_tiles - 1) % tiles_per_seq
    rows = x_ref.shape[1]

    @pl.when(g == 0)
    def _init_tables():
        aact_ref[...] = jnp.zeros(aact_ref.shape, BF16)
        gate_ref[...] = jnp.zeros(gate_ref.shape, F32)
        bout_ref[...] = jnp.zeros(bout_ref.shape, F32)
        xprev_ref[...] = jnp.zeros(xprev_ref.shape, F32)
        ii = lax.broadcasted_iota(jnp.int32, (CHUNK, CHUNK), 0)
        jj = lax.broadcasted_iota(jnp.int32, (CHUNK, CHUNK), 1)
        diff = (ii - jj).astype(F32)
        row_k = lax.broadcasted_iota(jnp.int32, (CHUNK, DK), 0).astype(F32)
        for h in range(HEADS):
            lg = LOG_GAMMA[h]
            dec_ref[h] = jnp.where(diff >= 0.0, jnp.exp(jnp.maximum(diff, 0.0) * lg), 0.0)
            qd_ref[h] = jnp.exp((row_k + 1.0) * lg)
            kd_ref[h] = jnp.exp((CHUNK - 1.0 - row_k) * lg)
        for j in range(CONV_WIDTH):
            wb_ref[j] = jnp.broadcast_to(wdw_ref[j:j + 1, :], (SUBLANES, CONV_DIM))

    @pl.when(i == 0)
    def _start_sequence():
        fb_ref[0, 0:HALO_PAD, :] = jnp.zeros((HALO_PAD, CONV_DIM), F32)
        s_ref[...] = jnp.zeros(s_ref.shape, F32)

    a_out = _dot(aact_ref[...], wco_ref[...])
    x = x_ref[0]
    xn = _rms(x, npre_ref[...]).astype(BF16)
    pc = _dot(xn, win_ref[:, C_CONV:C_CONV + 2 * CONV_DIM])
    merged = (jax.nn.sigmoid(gate_ref[:, :D_MODEL]) * a_out
              + jax.nn.sigmoid(gate_ref[:, D_MODEL:]) * bout_ref[...])
    m = _dot(merged.astype(BF16), wo_ref[...])
    h_ref[0] = xprev_ref[...] + _rms(m, npost_ref[...])

    xprev_ref[...] = x

    u_ref[...] = pc[:, :CONV_DIM] * jax.nn.sigmoid(pc[:, CONV_DIM:])

    pqk = _dot(xn, win_ref[:, C_Q:C_Q + 2 * QK_W])
    pv = _dot(xn, win_ref[:, C_V:C_V + V_W]).astype(BF16)
    pg = _dot(xn, win_ref[:, C_G:C_G + V_W])

    first = HALO_PAD - HALO
    span = CONV_TILE + HALO_PAD - SUBLANES
    for t in range(rows // CONV_TILE):
        fb_ref[0, HALO_PAD:HALO_PAD + CONV_TILE, :] = u_ref[t * CONV_TILE:(t + 1) * CONV_TILE, :]
        for r in range(1, SUBLANES):
            fb_ref[r, 0:span, :] = fb_ref[0, r:r + span, :]
        for rb in range(CONV_TILE // CONV_ROWS):
            acc = jnp.broadcast_to(bdw_ref[...], (CONV_ROWS, CONV_DIM))
            for j in range(CONV_WIDTH):
                off = first + j
                base = rb * CONV_ROWS + off - off % SUBLANES
                slab = fb_ref[off % SUBLANES, base:base + CONV_ROWS, :]
                acc = acc + (slab.reshape(CONV_ROWS // SUBLANES, SUBLANES, CONV_DIM)
                             * wb_ref[j][None]).reshape(CONV_ROWS, CONV_DIM)
            r0 = t * CONV_TILE + rb * CONV_ROWS
            aact_ref[r0:r0 + CONV_ROWS, :] = _silu(
                _layer_norm(acc, lnw_ref[...], lnb_ref[...])).astype(BF16)
        fb_ref[0, 0:HALO_PAD, :] = fb_ref[0, CONV_TILE:CONV_TILE + HALO_PAD, :]

    cos2 = cos_ref[...]
    sin2 = sin_ref[...]

    def rot(t):
        return t * cos2 + pltpu.roll(t, DK // 2, 1) * sin2

    heads = range(HEADS)
    q_rot = [rot(pqk[:, h * DK:(h + 1) * DK]) for h in heads]
    q_bf = [q.astype(BF16) for q in q_rot]
    k_rot = [rot(pqk[:, QK_W + h * DK:QK_W + (h + 1) * DK]) * K_SCALE for h in heads]
    k_bf = [k.astype(BF16) for k in k_rot]
    g_act = [_silu(pg[:, h * DV:(h + 1) * DV]) for h in heads]

    n_chunks = rows // CHUNK
    n_fill = 2 * n_chunks
    gate_cols = 2 * D_MODEL // n_fill

    def gate_slice(f):
        gate_ref[:, f * gate_cols:(f + 1) * gate_cols] = _dot(
            xn, win_ref[:, C_GATES + f * gate_cols:C_GATES + (f + 1) * gate_cols])

    for c in range(n_chunks):
        rs = slice(c * CHUNK, (c + 1) * CHUNK)
        v_c = [pv[rs, h * DV:(h + 1) * DV] for h in heads]
        state = [s_ref[h] for h in heads]
        scores = [lax.dot_general(q_bf[h][rs], k_bf[h][rs], NT_DIMS,
                                  preferred_element_type=F32) for h in heads]
        kv = [lax.dot_general((k_rot[h][rs] * kd_ref[h]).astype(BF16), v_c[h], TN_DIMS,
                              preferred_element_type=F32) for h in heads]
        gate_slice(2 * c)
        lhs = [jnp.concatenate([(scores[h] * dec_ref[h]).astype(BF16),
                                (q_rot[h][rs] * qd_ref[h]).astype(BF16)], axis=1)
               for h in heads]
        rhs = [jnp.concatenate([v_c[h], state[h].astype(BF16)], axis=0) for h in heads]
        out = [_dot(lhs[h], rhs[h]) for h in heads]
        gate_slice(2 * c + 1)
        for h in heads:
            s_ref[h] = GAMMA_CHUNK[h] * state[h] + kv[h]
            y_ref[rs, h * DV:(h + 1) * DV] = (g_act[h][rs] * _group_norm(out[h])).astype(BF16)
    bout_ref[...] = _dot(y_ref[...], wro_ref[...])

    @pl.when(jnp.logical_and(i == tiles_per_seq - 1, g < n_tiles))
    def _emit_sequence_state():
        nc_ref[0] = fb_ref[0, HALO_PAD - HALO:HALO_PAD, :]
        sout_ref[0] = s_ref[...]


def _mixer_prompt(x, cos2, sin2, npre, npost, win, wdw, bdw, lnw, lnb, wco, wro, wo):
    batch, seq, _ = x.shape
    rows = MIX_ROWS
    nt = seq // rows
    n_tiles = batch * nt

    def front(g):
        return jnp.minimum(g, n_tiles - 1)

    def back(g):
        return jnp.maximum(g - 1, 0)

    in_specs = [
        pl.BlockSpec((1, rows, D_MODEL), lambda g: (front(g) // nt, front(g) % nt, 0)),
        pl.BlockSpec((rows, DK), lambda g: (front(g) % nt, 0)),
        pl.BlockSpec((rows, DK), lambda g: (front(g) % nt, 0)),
        _const_spec((1, D_MODEL)),
        _const_spec((1, D_MODEL)),
        _const_spec((D_MODEL, IN_COLS)),
        _const_spec((CONV_WIDTH, CONV_DIM)),
        _const_spec((1, CONV_DIM)),
        _const_spec((1, CONV_DIM)),
        _const_spec((1, CONV_DIM)),
        _const_spec((CONV_DIM, D_MODEL)),
        _const_spec((V_W, D_MODEL)),
        _const_spec((D_MODEL, D_MODEL)),
    ]
    out_specs = [
        pl.BlockSpec((1, rows, D_MODEL), lambda g: (back(g) // nt, back(g) % nt, 0)),
        pl.BlockSpec((1, HALO, CONV_DIM), lambda g: (front(g) // nt, 0, 0)),
        pl.BlockSpec((1, HEADS, DK, DV), lambda g: (front(g) // nt, 0, 0, 0)),
    ]
    out_shape = [
        jax.ShapeDtypeStruct((batch, seq, D_MODEL), F32),
        jax.ShapeDtypeStruct((batch, HALO, CONV_DIM), F32),
        jax.ShapeDtypeStruct((batch, HEADS, DK, DV), F32),
    ]
    scratch = [
        pltpu.VMEM((SUBLANES, HALO_PAD + CONV_TILE, CONV_DIM), F32),
        pltpu.VMEM((rows, CONV_DIM), F32),
        pltpu.VMEM((rows, V_W), BF16),
        pltpu.VMEM((HEADS, CHUNK, CHUNK), F32),
        pltpu.VMEM((HEADS, CHUNK, DK), F32),
        pltpu.VMEM((HEADS, CHUNK, DK), F32),
        pltpu.VMEM((CONV_WIDTH, SUBLANES, CONV_DIM), F32),
        pltpu.VMEM((HEADS, DK, DV), F32),
        pltpu.VMEM((rows, CONV_DIM), BF16),
        pltpu.VMEM((rows, 2 * D_MODEL), F32),
        pltpu.VMEM((rows, D_MODEL), F32),
        pltpu.VMEM((rows, D_MODEL), F32),
    ]
    return pl.pallas_call(
        functools.partial(_mixer_prompt_kernel, tiles_per_seq=nt, n_tiles=n_tiles),
        grid=(n_tiles + 1,), in_specs=in_specs, out_specs=out_specs, out_shape=out_shape,
        scratch_shapes=scratch,
        compiler_params=pltpu.CompilerParams(
            dimension_semantics=("arbitrary",),
            vmem_limit_bytes=VMEM_LIMIT_BYTES),
        name="mixer_prompt",
    )(x, cos2, sin2, npre, npost, win, wdw, bdw, lnw, lnb, wco, wro, wo)


def _ffn_body(h, npre, npost, wup_ref, wdn_ref):
    hn = _rms(h, npre).astype(BF16)
    f = None
    for c in range(FFN_DIM // FFN_CHUNK):
        cols = slice(c * FFN_CHUNK, (c + 1) * FFN_CHUNK)
        up = _dot(hn, wup_ref[:, cols])
        act = jnp.square(jnp.maximum(up, 0.0)).astype(BF16)
        part = _dot(act, wdn_ref[cols, :])
        f = part if f is None else f + part
    return h + _rms(f, npost)


def _sample_state_body(blk, cache_ref, u_ref, wdw_ref, bdw_ref, qt_ref, kt_ref, v_ref, st_ref,
                       conv_ref, nc_ref, o_ref, so_ref):
    n_seq = qt_ref.shape[1]
    u = u_ref[...]
    acc = u * wdw_ref[HALO:HALO + 1, :] + bdw_ref[...]
    for j in range(HALO):
        acc = acc + cache_ref[j] * wdw_ref[j:j + 1, :]
    conv_ref[...] = acc
    nc_ref[0:HALO - 1] = cache_ref[1:HALO]
    nc_ref[HALO - 1] = u

    shift = (n_seq - blk * SAMPLE_BLOCK) % n_seq
    qt = pltpu.roll(qt_ref[...], shift, 1)
    kt = pltpu.roll(kt_ref[...], shift, 1)
    for s in range(SAMPLE_BLOCK):
        for h in range(HEADS):
            k_col = kt[h * DK:(h + 1) * DK, s:s + 1]
            q_col = qt[h * DK:(h + 1) * DK, s:s + 1]
            v_row = v_ref[s:s + 1, h * DV:(h + 1) * DV]
            new_state = GAMMA[h] * st_ref[s, h] + k_col * v_row
            so_ref[s, h] = new_state
            o_ref[s:s + 1, h * DV:(h + 1) * DV] = jnp.sum(new_state * q_col, axis=0,
                                                          keepdims=True)


def _ffn_state_kernel(h_ref, npre_ref, npost_ref, wup_ref, wdn_ref,
                      cache_ref, u_ref, wdw_ref, bdw_ref, qt_ref, kt_ref, v_ref, st_ref,
                      y_ref, conv_ref, nc_ref, o_ref, so_ref, *, steps_per_block):
    y_ref[...] = _ffn_body(h_ref[...], npre_ref[...], npost_ref[...], wup_ref, wdn_ref)
    _sample_state_body(pl.program_id(0) // steps_per_block, cache_ref, u_ref, wdw_ref, bdw_ref,
                       qt_ref, kt_ref, v_ref, st_ref, conv_ref, nc_ref, o_ref, so_ref)


def _ffn_state(h, npre, npost, wup, wdn, cache, u, wdw, bdw, qt, kt, v, state):
    n = h.shape[0]
    rows = FFN_ROWS
    steps = n // rows
    n_seq = u.shape[0]
    sb = SAMPLE_BLOCK
    spb = steps // (n_seq // sb)
    assert spb * (n_seq // sb) == steps
    in_specs = [
        pl.BlockSpec((rows, D_MODEL), lambda i: (i, 0)),
        _const_spec((1, D_MODEL)),
        _const_spec((1, D_MODEL)),
        _const_spec((D_MODEL, FFN_DIM)),
        _const_spec((FFN_DIM, D_MODEL)),
        pl.BlockSpec((HALO, sb, CONV_DIM), lambda i: (0, i // spb, 0)),
        pl.BlockSpec((sb, CONV_DIM), lambda i: (i // spb, 0)),
        _const_spec((CONV_WIDTH, CONV_DIM)),
        _const_spec((1, CONV_DIM)),
        _const_spec((QK_W, n_seq)),
        _const_spec((QK_W, n_seq)),
        pl.BlockSpec((sb, V_W), lambda i: (i // spb, 0)),
        pl.BlockSpec((sb, HEADS, DK, DV), lambda i: (i // spb, 0, 0, 0)),
    ]
    out_specs = [
        pl.BlockSpec((rows, D_MODEL), lambda i: (i, 0)),
        pl.BlockSpec((sb, CONV_DIM), lambda i: (i // spb, 0)),
        pl.BlockSpec((HALO, sb, CONV_DIM), lambda i: (0, i // spb, 0)),
        pl.BlockSpec((sb, V_W), lambda i: (i // spb, 0)),
        pl.BlockSpec((sb, HEADS, DK, DV), lambda i: (i // spb, 0, 0, 0)),
    ]
    out_shape = [
        jax.ShapeDtypeStruct((n, D_MODEL), F32),
        jax.ShapeDtypeStruct((n_seq, CONV_DIM), F32),
        jax.ShapeDtypeStruct((HALO, n_seq, CONV_DIM), F32),
        jax.ShapeDtypeStruct((n_seq, V_W), F32),
        jax.ShapeDtypeStruct((n_seq, HEADS, DK, DV), F32),
    ]
    return pl.pallas_call(
        functools.partial(_ffn_state_kernel, steps_per_block=spb),
        grid=(steps,), in_specs=in_specs, out_specs=out_specs, out_shape=out_shape,
        compiler_params=pltpu.CompilerParams(
            dimension_semantics=("arbitrary",),
            vmem_limit_bytes=VMEM_LIMIT_BYTES),
        name="ffn_prompt_sample_state",
    )(h, npre, npost, wup, wdn, cache, u, wdw, bdw, qt, kt, v, state)


def _sample_pre_kernel(x_ref, cos_ref, sin_ref, npre_ref, win_ref,
                       u_ref, qt_ref, kt_ref, v_ref, g_ref, ga_ref, gb_ref):
    xn = _rms(x_ref[...], npre_ref[...]).astype(BF16)
    pc = _dot(xn, win_ref[:, C_CONV:C_CONV + 2 * CONV_DIM])
    u_ref[...] = pc[:, :CONV_DIM] * jax.nn.sigmoid(pc[:, CONV_DIM:])
    v_ref[...] = _dot(xn, win_ref[:, C_V:C_V + V_W])
    g_ref[...] = _dot(xn, win_ref[:, C_G:C_G + V_W])
    pgate = _dot(xn, win_ref[:, C_GATES:C_GATES + 2 * D_MODEL])
    ga_ref[...] = pgate[:, :D_MODEL]
    gb_ref[...] = pgate[:, D_MODEL:]
    qkt = _dot(xn, win_ref[:, C_Q:C_Q + 2 * QK_W]).T
    cos_t = cos_ref[...]
    sin_t = sin_ref[...]
    half = DK // 2
    for g in range(2 * HEADS):
        x1 = qkt[g * DK:g * DK + half]
        x2 = qkt[g * DK + half:(g + 1) * DK]
        o1 = x1 * cos_t - x2 * sin_t
        o2 = x2 * cos_t + x1 * sin_t
        if g < HEADS:
            qt_ref[g * DK:g * DK + half, :] = o1
            qt_ref[g * DK + half:(g + 1) * DK, :] = o2
        else:
            k0 = (g - HEADS) * DK
            kt_ref[k0:k0 + half, :] = o1 * K_SCALE
            kt_ref[k0 + half:k0 + DK, :] = o2 * K_SCALE


def _sample_pre(x, cos_t, sin_t, npre, win):
    n = x.shape[0]
    out_shape = [
        jax.ShapeDtypeStruct((n, CONV_DIM), F32),
        jax.ShapeDtypeStruct((QK_W, n), F32),
        jax.ShapeDtypeStruct((QK_W, n), F32),
        jax.ShapeDtypeStruct((n, V_W), F32),
        jax.ShapeDtypeStruct((n, V_W), F32),
        jax.ShapeDtypeStruct((n, D_MODEL), F32),
        jax.ShapeDtypeStruct((n, D_MODEL), F32),
    ]
    return pl.pallas_call(
        _sample_pre_kernel, out_shape=out_shape,
        compiler_params=pltpu.CompilerParams(vmem_limit_bytes=VMEM_LIMIT_BYTES),
        name="sample_pre",
    )(x, cos_t, sin_t, npre, win)


def _sample_post_kernel(x_ref, conv_ref, o_ref, g_ref, ga_ref, gb_ref, lnw_ref, lnb_ref,
                        wco_ref, wro_ref, wo_ref, npost_ref, nfpre_ref, nfpost_ref,
                        wup_ref, wdn_ref, y_ref):
    a_act = _silu(_layer_norm(conv_ref[...], lnw_ref[...], lnb_ref[...])).astype(BF16)
    a_out = _dot(a_act, wco_ref[...])
    b_out = jnp.zeros_like(a_out)
    for h in range(HEADS):
        cols = slice(h * DV, (h + 1) * DV)
        yh = (_silu(g_ref[:, cols]) * _group_norm(o_ref[:, cols])).astype(BF16)
        b_out = b_out + _dot(yh, wro_ref[cols, :])
    merged = jax.nn.sigmoid(ga_ref[...]) * a_out + jax.nn.sigmoid(gb_ref[...]) * b_out
    m = _dot(merged.astype(BF16), wo_ref[...])
    hres = x_ref[...] + _rms(m, npost_ref[...])
    y_ref[...] = _ffn_body(hres, nfpre_ref[...], nfpost_ref[...], wup_ref, wdn_ref)


def _sample_post(x, conv, o, g, ga, gb, lnw, lnb, wco, wro, wo, npost, nfpre, nfpost, wup, wdn):
    return pl.pallas_call(
        _sample_post_kernel,
        out_shape=jax.ShapeDtypeStruct(x.shape, F32),
        compiler_params=pltpu.CompilerParams(vmem_limit_bytes=VMEM_LIMIT_BYTES),
        name="sample_post",
    )(x, conv, o, g, ga, gb, lnw, lnb, wco, wro, wo, npost, nfpre, nfpost, wup, wdn)


def _rope_angles(pos):
    half = DK // 2
    freqs = 1.0 / (ROPE_BASE ** jnp.linspace(0.0, 1.0, half, dtype=F32))
    return pos[:, None] * freqs[None, :]


def kernel(x_prompt, x_sample, cache_conv, state_ret, norm_mix_pre, norm_mix_post, w_in, w_dw, b_dw, conv_ln_w, conv_ln_b, w_conv_out, w_ret_out, w_o, norm_ffn_pre, norm_ffn_post, w_ffn_up, w_ffn_down):
    batch, seq, _ = x_prompt.shape
    n_seq, dec_seq, _ = x_sample.shape
    depth = w_in.shape[0]
    assert dec_seq == 1 and seq % MIX_ROWS == 0 and (batch * seq) % FFN_ROWS == 0
    assert n_seq % SAMPLE_BLOCK == 0

    ang_p = _rope_angles(jnp.arange(seq, dtype=F32))
    cos_p = jnp.concatenate([jnp.cos(ang_p), jnp.cos(ang_p)], axis=1)
    sin_p = jnp.concatenate([-jnp.sin(ang_p), jnp.sin(ang_p)], axis=1)
    ang_s = _rope_angles(PAST_LEN + jnp.arange(dec_seq, dtype=F32))
    cos_s = jnp.broadcast_to(jnp.cos(ang_s).T, (DK // 2, n_seq))
    sin_s = jnp.broadcast_to(jnp.sin(ang_s).T, (DK // 2, n_seq))

    xp = x_prompt
    xs = x_sample.reshape(n_seq, D_MODEL)
    conv_p, ret_p, conv_s, ret_s = [], [], [], []
    for l in range(depth):
        npre = norm_mix_pre[l][None]
        npost = norm_mix_post[l][None]
        nfpre = norm_ffn_pre[l][None]
        nfpost = norm_ffn_post[l][None]
        win = w_in[l].astype(BF16)
        wdw = w_dw[l]
        bdw = b_dw[l][None]
        lnw = conv_ln_w[l][None]
        lnb = conv_ln_b[l][None]
        wco = w_conv_out[l].astype(BF16)
        wro = w_ret_out[l].astype(BF16)
        wo = w_o[l].astype(BF16)
        wup = w_ffn_up[l].astype(BF16)
        wdn = w_ffn_down[l].astype(BF16)

        u, qt, kt, v, g, ga, gb = _sample_pre(xs, cos_s, sin_s, npre, win)
        cache_t = jnp.transpose(cache_conv[l], (1, 0, 2))
        h_p, nc_p, s_p = _mixer_prompt(xp, cos_p, sin_p, npre, npost, win, wdw, bdw, lnw, lnb,
                                       wco, wro, wo)
        y_p, conv, nc_t, o, s_s = _ffn_state(h_p.reshape(batch * seq, D_MODEL), nfpre, nfpost,
                                             wup, wdn, cache_t, u, wdw, bdw, qt, kt, v,
                                             state_ret[l])
        xp = y_p.reshape(batch, seq, D_MODEL)
        nc_s = jnp.transpose(nc_t, (1, 0, 2))
        xs = _sample_post(xs, conv, o, g, ga, gb, lnw, lnb, wco, wro, wo, npost, nfpre, nfpost,
                          wup, wdn)
        conv_p.append(nc_p)
        ret_p.append(s_p)
        conv_s.append(nc_s)
        ret_s.append(s_s)

    return (xp, xs.reshape(n_seq, dec_seq, D_MODEL), jnp.stack(conv_p), jnp.stack(ret_p),
            jnp.stack(conv_s), jnp.stack(ret_s))
```

```python
import functools
import math

import jax
import jax.numpy as jnp
from jax import lax
from jax.experimental import pallas as pl
from jax.experimental.pallas import tpu as pltpu

F32 = jnp.float32
BF16 = jnp.bfloat16

D_MODEL = 1024
CONV_DIM = 512
CONV_WIDTH = 31
HALO = CONV_WIDTH - 1
HEADS = 4
DK = 128
DV = 256
QK_W = HEADS * DK
V_W = HEADS * DV
CHUNK = 128
FFN_DIM = 4 * D_MODEL
EPS = 1e-6
ROPE_BASE = 10000.0
PAST_LEN = 16384
K_SCALE = DK ** -0.5

C_CONV = 0
C_Q = 2 * CONV_DIM
C_K = C_Q + QK_W
C_V = C_K + QK_W
C_G = C_V + V_W
C_GATES = C_G + V_W
IN_COLS = C_GATES + 2 * D_MODEL

LOG_GAMMA = tuple(math.log1p(-(2.0 ** (-5 - h))) for h in range(HEADS))
GAMMA = tuple(math.exp(lg) for lg in LOG_GAMMA)
GAMMA_CHUNK = tuple(math.exp(CHUNK * lg) for lg in LOG_GAMMA)

VMEM_LIMIT_BYTES = 60 * 1024 * 1024
SUBLANES = 8
HALO_PAD = 32
MIX_ROWS = 512
FFN_ROWS = 1024
FFN_CHUNK = 1024
CONV_TILE = 256
CONV_ROWS = 32
SAMPLE_BLOCK = 8

NT_DIMS = (((1,), (1,)), ((), ()))
TN_DIMS = (((0,), (0,)), ((), ()))


def _dot(a, b):
    return jnp.dot(a, b, preferred_element_type=F32)


def _rms(x, w):
    return x * lax.rsqrt(jnp.mean(x * x, axis=-1, keepdims=True) + EPS) * w


def _layer_norm(x, w, b):
    mu = jnp.mean(x, axis=-1, keepdims=True)
    xc = x - mu
    return xc * lax.rsqrt(jnp.mean(xc * xc, axis=-1, keepdims=True) + EPS) * w + b


def _silu(x):
    return x * jax.nn.sigmoid(x)


def _group_norm(o):
    return o * lax.rsqrt(jnp.mean(o * o, axis=-1, keepdims=True) + EPS)


def _const_spec(shape):
    zeros = (0,) * len(shape)
    return pl.BlockSpec(shape, lambda *_: zeros, pipeline_mode=pl.Buffered(1))


def _mixer_prompt_kernel(x_ref, cos_ref, sin_ref, npre_ref, npost_ref, win_ref, wdw_ref,
                         bdw_ref, lnw_ref, lnb_ref, wco_ref, wro_ref, wo_ref,
                         h_ref, nc_ref, sout_ref,
                         fb_ref, u_ref, y_ref, dec_ref, qd_ref, kd_ref, wb_ref, s_ref,
                         aact_ref, gate_ref, bout_ref, xprev_ref, *, tiles_per_seq, n_tiles):
    g = pl.program_id(0)
    i = jnp.minimum(g, n_tiles - 1) % tiles_per_seq
    rows = x_ref.shape[1]

    @pl.when(g == 0)
    def _init_tables():
        aact_ref[...] = jnp.zeros(aact_ref.shape, BF16)
        gate_ref[...] = jnp.zeros(gate_ref.shape, F32)
        bout_ref[...] = jnp.zeros(bout_ref.shape, F32)
        xprev_ref[...] = jnp.zeros(xprev_ref.shape, F32)
        ii = lax.broadcasted_iota(jnp.int32, (CHUNK, CHUNK), 0)
        jj = lax.broadcasted_iota(jnp.int32, (CHUNK, CHUNK), 1)
        diff = (ii - jj).astype(F32)
        row_k = lax.broadcasted_iota(jnp.int32, (CHUNK, DK), 0).astype(F32)
        for h in range(HEADS):
            lg = LOG_GAMMA[h]
            dec_ref[h] = jnp.where(diff >= 0.0, jnp.exp(jnp.maximum(diff, 0.0) * lg), 0.0)
            qd_ref[h] = jnp.exp((row_k + 1.0) * lg)
            kd_ref[h] = jnp.exp((CHUNK - 1.0 - row_k) * lg)
        for j in range(CONV_WIDTH):
            wb_ref[j] = jnp.broadcast_to(wdw_ref[j:j + 1, :], (SUBLANES, CONV_DIM))

    @pl.when(i == 0)
    def _start_sequence():
        fb_ref[0, 0:HALO_PAD, :] = jnp.zeros((HALO_PAD, CONV_DIM), F32)
        s_ref[...] = jnp.zeros(s_ref.shape, F32)

    a_out = _dot(aact_ref[...], wco_ref[...])
    x = x_ref[0]
    xn = _rms(x, npre_ref[...]).astype(BF16)
    pc = _dot(xn, win_ref[:, C_CONV:C_CONV + 2 * CONV_DIM])
    merged = (jax.nn.sigmoid(gate_ref[:, :D_MODEL]) * a_out
              + jax.nn.sigmoid(gate_ref[:, D_MODEL:]) * bout_ref[...])
    m = _dot(merged.astype(BF16), wo_ref[...])
    h_ref[0] = xprev_ref[...] + _rms(m, npost_ref[...])

    xprev_ref[...] = x

    u_ref[...] = pc[:, :CONV_DIM] * jax.nn.sigmoid(pc[:, CONV_DIM:])

    pqk = _dot(xn, win_ref[:, C_Q:C_Q + 2 * QK_W])
    pv = _dot(xn, win_ref[:, C_V:C_V + V_W]).astype(BF16)
    pg = _dot(xn, win_ref[:, C_G:C_G + V_W])

    first = HALO_PAD - HALO
    span = CONV_TILE + HALO_PAD - SUBLANES
    for t in range(rows // CONV_TILE):
        fb_ref[0, HALO_PAD:HALO_PAD + CONV_TILE, :] = u_ref[t * CONV_TILE:(t + 1) * CONV_TILE, :]
        for r in range(1, SUBLANES):
            fb_ref[r, 0:span, :] = fb_ref[0, r:r + span, :]
        for rb in range(CONV_TILE // CONV_ROWS):
            acc = jnp.broadcast_to(bdw_ref[...], (CONV_ROWS, CONV_DIM))
            for j in range(CONV_WIDTH):
                off = first + j
                base = rb * CONV_ROWS + off - off % SUBLANES
                slab = fb_ref[off % SUBLANES, base:base + CONV_ROWS, :]
                acc = acc + (slab.reshape(CONV_ROWS // SUBLANES, SUBLANES, CONV_DIM)
                             * wb_ref[j][None]).reshape(CONV_ROWS, CONV_DIM)
            r0 = t * CONV_TILE + rb * CONV_ROWS
            aact_ref[r0:r0 + CONV_ROWS, :] = _silu(
                _layer_norm(acc, lnw_ref[...], lnb_ref[...])).astype(BF16)
        fb_ref[0, 0:HALO_PAD, :] = fb_ref[0, CONV_TILE:CONV_TILE + HALO_PAD, :]

    cos2 = cos_ref[...]
    sin2 = sin_ref[...]

    def rot(t):
        return t * cos2 + pltpu.roll(t, DK // 2, 1) * sin2

    heads = range(HEADS)
    q_rot = [rot(pqk[:, h * DK:(h + 1) * DK]) for h in heads]
    q_bf = [q.astype(BF16) for q in q_rot]
    k_rot = [rot(pqk[:, QK_W + h * DK:QK_W + (h + 1) * DK]) * K_SCALE for h in heads]
    k_bf = [k.astype(BF16) for k in k_rot]
    g_act = [_silu(pg[:, h * DV:(h + 1) * DV]) for h in heads]

    n_chunks = rows // CHUNK
    n_fill = 2 * n_chunks
    gate_cols = 2 * D_MODEL // n_fill

    def gate_slice(f):
        gate_ref[:, f * gate_cols:(f + 1) * gate_cols] = _dot(
            xn, win_ref[:, C_GATES + f * gate_cols:C_GATES + (f + 1) * gate_cols])

    for c in range(n_chunks):
        rs = slice(c * CHUNK, (c + 1) * CHUNK)
        v_c = [pv[rs, h * DV:(h + 1) * DV] for h in heads]
        state = [s_ref[h] for h in heads]
        scores = []
        zeros = jnp.zeros((CHUNK, DK), BF16)
        for a in range(0, HEADS, 2):
            q_pair = jnp.concatenate([q_bf[a][rs], q_bf[a + 1][rs]], axis=1)
            k_pair = jnp.concatenate(
                [jnp.concatenate([k_bf[a][rs], zeros], axis=1),
                 jnp.concatenate([zeros, k_bf[a + 1][rs]], axis=1)], axis=0)
            s_pair = lax.dot_general(q_pair, k_pair, NT_DIMS, preferred_element_type=F32)
            scores += [s_pair[:, :CHUNK], s_pair[:, CHUNK:]]
        kv = [lax.dot_general((k_rot[h][rs] * kd_ref[h]).astype(BF16), v_c[h], TN_DIMS,
                              preferred_element_type=F32) for h in heads]
        gate_slice(2 * c)
        lhs = [jnp.concatenate([(scores[h] * dec_ref[h]).astype(BF16),
                                (q_rot[h][rs] * qd_ref[h]).astype(BF16)], axis=1)
               for h in heads]
        rhs = [jnp.concatenate([v_c[h], state[h].astype(BF16)], axis=0) for h in heads]
        out = [_dot(lhs[h], rhs[h]) for h in heads]
        gate_slice(2 * c + 1)
        for h in heads:
            s_ref[h] = GAMMA_CHUNK[h] * state[h] + kv[h]
            y_ref[rs, h * DV:(h + 1) * DV] = (g_act[h][rs] * _group_norm(out[h])).astype(BF16)
    bout_ref[...] = _dot(y_ref[...], wro_ref[...])

    @pl.when(jnp.logical_and(i == tiles_per_seq - 1, g < n_tiles))
    def _emit_sequence_state():
        nc_ref[0] = fb_ref[0, HALO_PAD - HALO:HALO_PAD, :]
        sout_ref[0] = s_ref[...]


def _mixer_prompt(x, cos2, sin2, npre, npost, win, wdw, bdw, lnw, lnb, wco, wro, wo):
    batch, seq, _ = x.shape
    rows = MIX_ROWS
    nt = seq // rows
    n_tiles = batch * nt

    def front(g):
        return jnp.minimum(g, n_tiles - 1)

    def back(g):
        return jnp.maximum(g - 1, 0)

    in_specs = [
        pl.BlockSpec((1, rows, D_MODEL), lambda g: (front(g) // nt, front(g) % nt, 0)),
        pl.BlockSpec((rows, DK), lambda g: (front(g) % nt, 0)),
        pl.BlockSpec((rows, DK), lambda g: (front(g) % nt, 0)),
        _const_spec((1, D_MODEL)),
        _const_spec((1, D_MODEL)),
        _const_spec((D_MODEL, IN_COLS)),
        _const_spec((CONV_WIDTH, CONV_DIM)),
        _const_spec((1, CONV_DIM)),
        _const_spec((1, CONV_DIM)),
        _const_spec((1, CONV_DIM)),
        _const_spec((CONV_DIM, D_MODEL)),
        _const_spec((V_W, D_MODEL)),
        _const_spec((D_MODEL, D_MODEL)),
    ]
    out_specs = [
        pl.BlockSpec((1, rows, D_MODEL), lambda g: (back(g) // nt, back(g) % nt, 0)),
        pl.BlockSpec((1, HALO, CONV_DIM), lambda g: (front(g) // nt, 0, 0)),
        pl.BlockSpec((1, HEADS, DK, DV), lambda g: (front(g) // nt, 0, 0, 0)),
    ]
    out_shape = [
        jax.ShapeDtypeStruct((batch, seq, D_MODEL), F32),
        jax.ShapeDtypeStruct((batch, HALO, CONV_DIM), F32),
        jax.ShapeDtypeStruct((batch, HEADS, DK, DV), F32),
    ]
    scratch = [
        pltpu.VMEM((SUBLANES, HALO_PAD + CONV_TILE, CONV_DIM), F32),
        pltpu.VMEM((rows, CONV_DIM), F32),
        pltpu.VMEM((rows, V_W), BF16),
        pltpu.VMEM((HEADS, CHUNK, CHUNK), F32),
        pltpu.VMEM((HEADS, CHUNK, DK), F32),
        pltpu.VMEM((HEADS, CHUNK, DK), F32),
        pltpu.VMEM((CONV_WIDTH, SUBLANES, CONV_DIM), F32),
        pltpu.VMEM((HEADS, DK, DV), F32),
        pltpu.VMEM((rows, CONV_DIM), BF16),
        pltpu.VMEM((rows, 2 * D_MODEL), F32),
        pltpu.VMEM((rows, D_MODEL), F32),
        pltpu.VMEM((rows, D_MODEL), F32),
    ]
    return pl.pallas_call(
        functools.partial(_mixer_prompt_kernel, tiles_per_seq=nt, n_tiles=n_tiles),
        grid=(n_tiles + 1,), in_specs=in_specs, out_specs=out_specs, out_shape=out_shape,
        scratch_shapes=scratch,
        compiler_params=pltpu.CompilerParams(
            dimension_semantics=("arbitrary",),
            vmem_limit_bytes=VMEM_LIMIT_BYTES),
        name="mixer_prompt",
    )(x, cos2, sin2, npre, npost, win, wdw, bdw, lnw, lnb, wco, wro, wo)


def _ffn_body(h, npre, npost, wup_ref, wdn_ref):
    hn = _rms(h, npre).astype(BF16)
    f = None
    for c in range(FFN_DIM // FFN_CHUNK):
        cols = slice(c * FFN_CHUNK, (c + 1) * FFN_CHUNK)
        up = _dot(hn, wup_ref[:, cols])
        act = jnp.square(jnp.maximum(up, 0.0)).astype(BF16)
        part = _dot(act, wdn_ref[cols, :])
        f = part if f is None else f + part
    return h + _rms(f, npost)


def _sample_state_body(blk, cache_ref, u_ref, wdw_ref, bdw_ref, qt_ref, kt_ref, v_ref, st_ref,
                       conv_ref, nc_ref, o_ref, so_ref):
    n_seq = qt_ref.shape[1]
    u = u_ref[...]
    acc = u * wdw_ref[HALO:HALO + 1, :] + bdw_ref[...]
    for j in range(HALO):
        acc = acc + cache_ref[j] * wdw_ref[j:j + 1, :]
    conv_ref[...] = acc
    nc_ref[0:HALO - 1] = cache_ref[1:HALO]
    nc_ref[HALO - 1] = u

    shift = (n_seq - blk * SAMPLE_BLOCK) % n_seq
    qt = pltpu.roll(qt_ref[...], shift, 1)
    kt = pltpu.roll(kt_ref[...], shift, 1)
    for s in range(SAMPLE_BLOCK):
        for h in range(HEADS):
            k_col = kt[h * DK:(h + 1) * DK, s:s + 1]
            q_col = qt[h * DK:(h + 1) * DK, s:s + 1]
            v_row = v_ref[s:s + 1, h * DV:(h + 1) * DV]
            new_state = GAMMA[h] * st_ref[s, h] + k_col * v_row
            so_ref[s, h] = new_state
            o_ref[s:s + 1, h * DV:(h + 1) * DV] = jnp.sum(new_state * q_col, axis=0,
                                                          keepdims=True)


def _ffn_state_kernel(h_ref, npre_ref, npost_ref, wup_ref, wdn_ref,
                      cache_ref, u_ref, wdw_ref, bdw_ref, qt_ref, kt_ref, v_ref, st_ref,
                      y_ref, conv_ref, nc_ref, o_ref, so_ref, *, steps_per_block):
    y_ref[...] = _ffn_body(h_ref[...], npre_ref[...], npost_ref[...], wup_ref, wdn_ref)
    _sample_state_body(pl.program_id(0) // steps_per_block, cache_ref, u_ref, wdw_ref, bdw_ref,
                       qt_ref, kt_ref, v_ref, st_ref, conv_ref, nc_ref, o_ref, so_ref)


def _ffn_state(h, npre, npost, wup, wdn, cache, u, wdw, bdw, qt, kt, v, state):
    n = h.shape[0]
    rows = FFN_ROWS
    steps = n // rows
    n_seq = u.shape[0]
    sb = SAMPLE_BLOCK
    spb = steps // (n_seq // sb)
    assert spb * (n_seq // sb) == steps
    in_specs = [
        pl.BlockSpec((rows, D_MODEL), lambda i: (i, 0)),
        _const_spec((1, D_MODEL)),
        _const_spec((1, D_MODEL)),
        _const_spec((D_MODEL, FFN_DIM)),
        _const_spec((FFN_DIM, D_MODEL)),
        pl.BlockSpec((HALO, sb, CONV_DIM), lambda i: (0, i // spb, 0)),
        pl.BlockSpec((sb, CONV_DIM), lambda i: (i // spb, 0)),
        _const_spec((CONV_WIDTH, CONV_DIM)),
        _const_spec((1, CONV_DIM)),
        _const_spec((QK_W, n_seq)),
        _const_spec((QK_W, n_seq)),
        pl.BlockSpec((sb, V_W), lambda i: (i // spb, 0)),
        pl.BlockSpec((sb, HEADS, DK, DV), lambda i: (i // spb, 0, 0, 0)),
    ]
    out_specs = [
        pl.BlockSpec((rows, D_MODEL), lambda i: (i, 0)),
        pl.BlockSpec((sb, CONV_DIM), lambda i: (i // spb, 0)),
        pl.BlockSpec((HALO, sb, CONV_DIM), lambda i: (0, i // spb, 0)),
        pl.BlockSpec((sb, V_W), lambda i: (i // spb, 0)),
        pl.BlockSpec((sb, HEADS, DK, DV), lambda i: (i // spb, 0, 0, 0)),
    ]
    out_shape = [
        jax.ShapeDtypeStruct((n, D_MODEL), F32),
        jax.ShapeDtypeStruct((n_seq, CONV_DIM), F32),
        jax.ShapeDtypeStruct((HALO, n_seq, CONV_DIM), F32),
        jax.ShapeDtypeStruct((n_seq, V_W), F32),
        jax.ShapeDtypeStruct((n_seq, HEADS, DK, DV), F32),
    ]
    return pl.pallas_call(
        functools.partial(_ffn_state_kernel, steps_per_block=spb),
        grid=(steps,), in_specs=in_specs, out_specs=out_specs, out_shape=out_shape,
        compiler_params=pltpu.CompilerParams(
            dimension_semantics=("arbitrary",),
            vmem_limit_bytes=VMEM_LIMIT_BYTES),
        name="ffn_prompt_sample_state",
    )(h, npre, npost, wup, wdn, cache, u, wdw, bdw, qt, kt, v, state)


def _sample_pre_kernel(x_ref, cos_ref, sin_ref, npre_ref, win_ref,
                       u_ref, qt_ref, kt_ref, v_ref, g_ref, ga_ref, gb_ref):
    xn = _rms(x_ref[...], npre_ref[...]).astype(BF16)
    pc = _dot(xn, win_ref[:, C_CONV:C_CONV + 2 * CONV_DIM])
    u_ref[...] = pc[:, :CONV_DIM] * jax.nn.sigmoid(pc[:, CONV_DIM:])
    v_ref[...] = _dot(xn, win_ref[:, C_V:C_V + V_W])
    g_ref[...] = _dot(xn, win_ref[:, C_G:C_G + V_W])
    pgate = _dot(xn, win_ref[:, C_GATES:C_GATES + 2 * D_MODEL])
    ga_ref[...] = pgate[:, :D_MODEL]
    gb_ref[...] = pgate[:, D_MODEL:]
    qkt = _dot(xn, win_ref[:, C_Q:C_Q + 2 * QK_W]).T
    cos_t = cos_ref[...]
    sin_t = sin_ref[...]
    half = DK // 2
    for g in range(2 * HEADS):
        x1 = qkt[g * DK:g * DK + half]
        x2 = qkt[g * DK + half:(g + 1) * DK]
        o1 = x1 * cos_t - x2 * sin_t
        o2 = x2 * cos_t + x1 * sin_t
        if g < HEADS:
            qt_ref[g * DK:g * DK + half, :] = o1
            qt_ref[g * DK + half:(g + 1) * DK, :] = o2
        else:
            k0 = (g - HEADS) * DK
            kt_ref[k0:k0 + half, :] = o1 * K_SCALE
            kt_ref[k0 + half:k0 + DK, :] = o2 * K_SCALE


def _sample_pre(x, cos_t, sin_t, npre, win):
    n = x.shape[0]
    out_shape = [
        jax.ShapeDtypeStruct((n, CONV_DIM), F32),
        jax.ShapeDtypeStruct((QK_W, n), F32),
        jax.ShapeDtypeStruct((QK_W, n), F32),
        jax.ShapeDtypeStruct((n, V_W), F32),
        jax.ShapeDtypeStruct((n, V_W), F32),
        jax.ShapeDtypeStruct((n, D_MODEL), F32),
        jax.ShapeDtypeStruct((n, D_MODEL), F32),
    ]
    return pl.pallas_call(
        _sample_pre_kernel, out_shape=out_shape,
        compiler_params=pltpu.CompilerParams(vmem_limit_bytes=VMEM_LIMIT_BYTES),
        name="sample_pre",
    )(x, cos_t, sin_t, npre, win)


def _sample_post_kernel(x_ref, conv_ref, o_ref, g_ref, ga_ref, gb_ref, lnw_ref, lnb_ref,
                        wco_ref, wro_ref, wo_ref, npost_ref, nfpre_ref, nfpost_ref,
                        wup_ref, wdn_ref, y_ref):
    a_act = _silu(_layer_norm(conv_ref[...], lnw_ref[...], lnb_ref[...])).astype(BF16)
    a_out = _dot(a_act, wco_ref[...])
    b_out = jnp.zeros_like(a_out)
    for h in range(HEADS):
        cols = slice(h * DV, (h + 1) * DV)
        yh = (_silu(g_ref[:, cols]) * _group_norm(o_ref[:, cols])).astype(BF16)
        b_out = b_out + _dot(yh, wro_ref[cols, :])
    merged = jax.nn.sigmoid(ga_ref[...]) * a_out + jax.nn.sigmoid(gb_ref[...]) * b_out
    m = _dot(merged.astype(BF16), wo_ref[...])
    hres = x_ref[...] + _rms(m, npost_ref[...])
    y_ref[...] = _ffn_body(hres, nfpre_ref[...], nfpost_ref[...], wup_ref, wdn_ref)


def _sample_post(x, conv, o, g, ga, gb, lnw, lnb, wco, wro, wo, npost, nfpre, nfpost, wup, wdn):
    return pl.pallas_call(
        _sample_post_kernel,
        out_shape=jax.ShapeDtypeStruct(x.shape, F32),
        compiler_params=pltpu.CompilerParams(vmem_limit_bytes=VMEM_LIMIT_BYTES),
        name="sample_post",
    )(x, conv, o, g, ga, gb, lnw, lnb, wco, wro, wo, npost, nfpre, nfpost, wup, wdn)


def _rope_angles(pos):
    half = DK // 2
    freqs = 1.0 / (ROPE_BASE ** jnp.linspace(0.0, 1.0, half, dtype=F32))
    return pos[:, None] * freqs[None, :]


def kernel(x_prompt, x_sample, cache_conv, state_ret, norm_mix_pre, norm_mix_post, w_in, w_dw, b_dw, conv_ln_w, conv_ln_b, w_conv_out, w_ret_out, w_o, norm_ffn_pre, norm_ffn_post, w_ffn_up, w_ffn_down):
    batch, seq, _ = x_prompt.shape
    n_seq, dec_seq, _ = x_sample.shape
    depth = w_in.shape[0]
    assert dec_seq == 1 and seq % MIX_ROWS == 0 and (batch * seq) % FFN_ROWS == 0
    assert n_seq % SAMPLE_BLOCK == 0

    ang_p = _rope_angles(jnp.arange(seq, dtype=F32))
    cos_p = jnp.concatenate([jnp.cos(ang_p), jnp.cos(ang_p)], axis=1)
    sin_p = jnp.concatenate([-jnp.sin(ang_p), jnp.sin(ang_p)], axis=1)
    ang_s = _rope_angles(PAST_LEN + jnp.arange(dec_seq, dtype=F32))
    cos_s = jnp.broadcast_to(jnp.cos(ang_s).T, (DK // 2, n_seq))
    sin_s = jnp.broadcast_to(jnp.sin(ang_s).T, (DK // 2, n_seq))

    xp = x_prompt
    xs = x_sample.reshape(n_seq, D_MODEL)
    conv_p, ret_p, conv_s, ret_s = [], [], [], []
    for l in range(depth):
        npre = norm_mix_pre[l][None]
        npost = norm_mix_post[l][None]
        nfpre = norm_ffn_pre[l][None]
        nfpost = norm_ffn_post[l][None]
        win = w_in[l].astype(BF16)
        wdw = w_dw[l]
        bdw = b_dw[l][None]
        lnw = conv_ln_w[l][None]
        lnb = conv_ln_b[l][None]
        wco = w_conv_out[l].astype(BF16)
        wro = w_ret_out[l].astype(BF16)
        wo = w_o[l].astype(BF16)
        wup = w_ffn_up[l].astype(BF16)
        wdn = w_ffn_down[l].astype(BF16)

        u, qt, kt, v, g, ga, gb = _sample_pre(xs, cos_s, sin_s, npre, win)
        cache_t = jnp.transpose(cache_conv[l], (1, 0, 2))
        h_p, nc_p, s_p = _mixer_prompt(xp, cos_p, sin_p, npre, npost, win, wdw, bdw, lnw, lnb,
                                       wco, wro, wo)
        y_p, conv, nc_t, o, s_s = _ffn_state(h_p.reshape(batch * seq, D_MODEL), nfpre, nfpost,
                                             wup, wdn, cache_t, u, wdw, bdw, qt, kt, v,
                                             state_ret[l])
        xp = y_p.reshape(batch, seq, D_MODEL)
        nc_s = jnp.transpose(nc_t, (1, 0, 2))
        xs = _sample_post(xs, conv, o, g, ga, gb, lnw, lnb, wco, wro, wo, npost, nfpre, nfpost,
                          wup, wdn)
        conv_p.append(nc_p)
        ret_p.append(s_p)
        conv_s.append(nc_s)
        ret_s.append(s_s)

    return (xp, xs.reshape(n_seq, dec_seq, D_MODEL), jnp.stack(conv_p), jnp.stack(ret_p),
            jnp.stack(conv_s), jnp.stack(ret_s))
```

```python
import functools
import math

import jax
import jax.numpy as jnp
from jax import lax
from jax.experimental import pallas as pl
from jax.experimental.pallas import tpu as pltpu

F32 = jnp.float32
BF16 = jnp.bfloat16

D_MODEL = 1024
CONV_DIM = 512
CONV_WIDTH = 31
HALO = CONV_WIDTH - 1
HEADS = 4
DK = 128
DV = 256
QK_W = HEADS * DK
V_W = HEADS * DV
CHUNK = 128
FFN_DIM = 4 * D_MODEL
EPS = 1e-6
ROPE_BASE = 10000.0
PAST_LEN = 16384
K_SCALE = DK ** -0.5

C_CONV = 0
C_Q = 2 * CONV_DIM
C_K = C_Q + QK_W
C_V = C_K + QK_W
C_G = C_V + V_W
C_GATES = C_G + V_W
IN_COLS = C_GATES + 2 * D_MODEL

LOG_GAMMA = tuple(math.log1p(-(2.0 ** (-5 - h))) for h in range(HEADS))
GAMMA = tuple(math.exp(lg) for lg in LOG_GAMMA)
GAMMA_CHUNK = tuple(math.exp(CHUNK * lg) for lg in LOG_GAMMA)

VMEM_LIMIT_BYTES = 60 * 1024 * 1024
SUBLANES = 8
HALO_PAD = 32
MIX_ROWS = 512
FFN_ROWS = 1024
FFN_CHUNK = 1024
CONV_TILE = 256
CONV_ROWS = 32
SAMPLE_BLOCK = 8

NT_DIMS = (((1,), (1,)), ((), ()))
TN_DIMS = (((0,), (0,)), ((), ()))


def _dot(a, b):
    return jnp.dot(a, b, preferred_element_type=F32)


def _rms(x, w):
    return x * lax.rsqrt(jnp.mean(x * x, axis=-1, keepdims=True) + EPS) * w


def _layer_norm(x, w, b):
    mu = jnp.mean(x, axis=-1, keepdims=True)
    xc = x - mu
    return xc * lax.rsqrt(jnp.mean(xc * xc, axis=-1, keepdims=True) + EPS) * w + b


def _silu(x):
    return x * jax.nn.sigmoid(x)


def _group_norm(o):
    return o * lax.rsqrt(jnp.mean(o * o, axis=-1, keepdims=True) + EPS)


def _const_spec(shape):
    zeros = (0,) * len(shape)
    return pl.BlockSpec(shape, lambda *_: zeros, pipeline_mode=pl.Buffered(1))


def _mixer_prompt_kernel(x_ref, cos_ref, sin_ref, npre_ref, npost_ref, win_ref, wdw_ref,
                         bdw_ref, lnw_ref, lnb_ref, wco_ref, wro_ref, wo_ref,
                         h_ref, nc_ref, sout_ref,
                         fb_ref, u_ref, y_ref, dec_ref, qd_ref, kd_ref, wb_ref, s_ref,
                         aact_ref, gate_ref, bout_ref, xprev_ref, *, tiles_per_seq, n_tiles):
    g = pl.program_id(0)
    i = jnp.minimum(g, n_tiles - 1) % tiles_per_seq
    rows = x_ref.shape[1]

    @pl.when(g == 0)
    def _init_tables():
        aact_ref[...] = jnp.zeros(aact_ref.shape, BF16)
        gate_ref[...] = jnp.zeros(gate_ref.shape, F32)
        bout_ref[...] = jnp.zeros(bout_ref.shape, F32)
        xprev_ref[...] = jnp.zeros(xprev_ref.shape, F32)
        ii = lax.broadcasted_iota(jnp.int32, (CHUNK, CHUNK), 0)
        jj = lax.broadcasted_iota(jnp.int32, (CHUNK, CHUNK), 1)
        diff = (ii - jj).astype(F32)
        row_k = lax.broadcasted_iota(jnp.int32, (CHUNK, DK), 0).astype(F32)
        for h in range(HEADS):
            lg = LOG_GAMMA[h]
            dec_ref[h] = jnp.where(diff >= 0.0, jnp.exp(jnp.maximum(diff, 0.0) * lg), 0.0)
            qd_ref[h] = jnp.exp((row_k + 1.0) * lg)
            kd_ref[h] = jnp.exp((CHUNK - 1.0 - row_k) * lg)
        for j in range(CONV_WIDTH):
            wb_ref[j] = jnp.broadcast_to(wdw_ref[j], (SUBLANES, CONV_DIM))

    @pl.when(i == 0)
    def _start_sequence():
        fb_ref[0, 0:HALO_PAD, :] = jnp.zeros((HALO_PAD, CONV_DIM), F32)
        s_ref[...] = jnp.zeros(s_ref.shape, F32)

    a_out = _dot(aact_ref[...], wco_ref[...])
    x = x_ref[0]
    xn = _rms(x, npre_ref[...]).astype(BF16)
    pc = _dot(xn, win_ref[:, C_CONV:C_CONV + 2 * CONV_DIM])
    merged = (jax.nn.sigmoid(gate_ref[:, :D_MODEL]) * a_out
              + jax.nn.sigmoid(gate_ref[:, D_MODEL:]) * bout_ref[...])
    m = _dot(merged.astype(BF16), wo_ref[...])
    h_ref[0] = xprev_ref[...] + _rms(m, npost_ref[...])

    xprev_ref[...] = x

    u_ref[...] = pc[:, :CONV_DIM] * jax.nn.sigmoid(pc[:, CONV_DIM:])

    pqk = _dot(xn, win_ref[:, C_Q:C_Q + 2 * QK_W])
    pv = _dot(xn, win_ref[:, C_V:C_V + V_W]).astype(BF16)
    pg = _dot(xn, win_ref[:, C_G:C_G + V_W])

    first = HALO_PAD - HALO
    span = CONV_TILE + HALO_PAD - SUBLANES
    for t in range(rows // CONV_TILE):
        fb_ref[0, HALO_PAD:HALO_PAD + CONV_TILE, :] = u_ref[t * CONV_TILE:(t + 1) * CONV_TILE, :]
        for r in range(1, SUBLANES):
            fb_ref[r, 0:span, :] = fb_ref[0, r:r + span, :]
        for rb in range(CONV_TILE // CONV_ROWS):
            acc = jnp.broadcast_to(bdw_ref[...], (CONV_ROWS, CONV_DIM))
            for j in range(CONV_WIDTH):
                off = first + j
                base = rb * CONV_ROWS + off - off % SUBLANES
                slab = fb_ref[off % SUBLANES, base:base + CONV_ROWS, :]
                acc = acc + (slab.reshape(CONV_ROWS // SUBLANES, SUBLANES, CONV_DIM)
                             * wb_ref[j][None]).reshape(CONV_ROWS, CONV_DIM)
            r0 = t * CONV_TILE + rb * CONV_ROWS
            aact_ref[r0:r0 + CONV_ROWS, :] = _silu(
                _layer_norm(acc, lnw_ref[...], lnb_ref[...])).astype(BF16)
        fb_ref[0, 0:HALO_PAD, :] = fb_ref[0, CONV_TILE:CONV_TILE + HALO_PAD, :]

    cos2 = cos_ref[...]
    sin2 = sin_ref[...]

    def rot(t):
        return t * cos2 + pltpu.roll(t, DK // 2, 1) * sin2

    heads = range(HEADS)
    q_rot = [rot(pqk[:, h * DK:(h + 1) * DK]) for h in heads]
    q_bf = [q.astype(BF16) for q in q_rot]
    k_rot = [rot(pqk[:, QK_W + h * DK:QK_W + (h + 1) * DK]) * K_SCALE for h in heads]
    k_bf = [k.astype(BF16) for k in k_rot]
    g_act = [_silu(pg[:, h * DV:(h + 1) * DV]) for h in heads]

    n_chunks = rows // CHUNK
    n_fill = 2 * n_chunks
    gate_cols = 2 * D_MODEL // n_fill

    def gate_slice(f):
        gate_ref[:, f * gate_cols:(f + 1) * gate_cols] = _dot(
            xn, win_ref[:, C_GATES + f * gate_cols:C_GATES + (f + 1) * gate_cols])

    for c in range(n_chunks):
        rs = slice(c * CHUNK, (c + 1) * CHUNK)
        v_c = [pv[rs, h * DV:(h + 1) * DV] for h in heads]
        state = [s_ref[h] for h in heads]
        scores = []
        zeros = jnp.zeros((CHUNK, DK), BF16)
        for a in range(0, HEADS, 2):
            q_pair = jnp.concatenate([q_bf[a][rs], q_bf[a + 1][rs]], axis=1)
            k_pair = jnp.concatenate(
                [jnp.concatenate([k_bf[a][rs], zeros], axis=1),
                 jnp.concatenate([zeros, k_bf[a + 1][rs]], axis=1)], axis=0)
            s_pair = lax.dot_general(q_pair, k_pair, NT_DIMS, preferred_element_type=F32)
            scores += [s_pair[:, :CHUNK], s_pair[:, CHUNK:]]
        kv = [lax.dot_general((k_rot[h][rs] * kd_ref[h]).astype(BF16), v_c[h], TN_DIMS,
                              preferred_element_type=F32) for h in heads]
        gate_slice(2 * c)
        lhs = [jnp.concatenate([(scores[h] * dec_ref[h]).astype(BF16),
                                (q_rot[h][rs] * qd_ref[h]).astype(BF16)], axis=1)
               for h in heads]
        rhs = [jnp.concatenate([v_c[h], state[h].astype(BF16)], axis=0) for h in heads]
        out = [_dot(lhs[h], rhs[h]) for h in heads]
        gate_slice(2 * c + 1)
        for h in heads:
            s_ref[h] = GAMMA_CHUNK[h] * state[h] + kv[h]
            y_ref[rs, h * DV:(h + 1) * DV] = (g_act[h][rs] * _group_norm(out[h])).astype(BF16)
    bout_ref[...] = _dot(y_ref[...], wro_ref[...])

    @pl.when(jnp.logical_and(i == tiles_per_seq - 1, g < n_tiles))
    def _emit_sequence_state():
        nc_ref[0] = fb_ref[0, HALO_PAD - HALO:HALO_PAD, :]
        sout_ref[0] = s_ref[...]


def _mixer_prompt(x, cos2, sin2, npre, npost, win, wdw, bdw, lnw, lnb, wco, wro, wo):
    batch, seq, _ = x.shape
    rows = MIX_ROWS
    nt = seq // rows
    n_tiles = batch * nt

    def front(g):
        return jnp.minimum(g, n_tiles - 1)

    def back(g):
        return jnp.maximum(g - 1, 0)

    in_specs = [
        pl.BlockSpec((1, rows, D_MODEL), lambda g: (front(g) // nt, front(g) % nt, 0)),
        pl.BlockSpec((rows, DK), lambda g: (front(g) % nt, 0)),
        pl.BlockSpec((rows, DK), lambda g: (front(g) % nt, 0)),
        _const_spec((1, D_MODEL)),
        _const_spec((1, D_MODEL)),
        _const_spec((D_MODEL, IN_COLS)),
        _const_spec((CONV_WIDTH, 1, CONV_DIM)),
        _const_spec((1, CONV_DIM)),
        _const_spec((1, CONV_DIM)),
        _const_spec((1, CONV_DIM)),
        _const_spec((CONV_DIM, D_MODEL)),
        _const_spec((V_W, D_MODEL)),
        _const_spec((D_MODEL, D_MODEL)),
    ]
    out_specs = [
        pl.BlockSpec((1, rows, D_MODEL), lambda g: (back(g) // nt, back(g) % nt, 0)),
        pl.BlockSpec((1, HALO, CONV_DIM), lambda g: (front(g) // nt, 0, 0)),
        pl.BlockSpec((1, HEADS, DK, DV), lambda g: (front(g) // nt, 0, 0, 0)),
    ]
    out_shape = [
        jax.ShapeDtypeStruct((batch, seq, D_MODEL), F32),
        jax.ShapeDtypeStruct((batch, HALO, CONV_DIM), F32),
        jax.ShapeDtypeStruct((batch, HEADS, DK, DV), F32),
    ]
    scratch = [
        pltpu.VMEM((SUBLANES, HALO_PAD + CONV_TILE, CONV_DIM), F32),
        pltpu.VMEM((rows, CONV_DIM), F32),
        pltpu.VMEM((rows, V_W), BF16),
        pltpu.VMEM((HEADS, CHUNK, CHUNK), F32),
        pltpu.VMEM((HEADS, CHUNK, DK), F32),
        pltpu.VMEM((HEADS, CHUNK, DK), F32),
        pltpu.VMEM((CONV_WIDTH, SUBLANES, CONV_DIM), F32),
        pltpu.VMEM((HEADS, DK, DV), F32),
        pltpu.VMEM((rows, CONV_DIM), BF16),
        pltpu.VMEM((rows, 2 * D_MODEL), F32),
        pltpu.VMEM((rows, D_MODEL), F32),
        pltpu.VMEM((rows, D_MODEL), F32),
    ]
    return pl.pallas_call(
        functools.partial(_mixer_prompt_kernel, tiles_per_seq=nt, n_tiles=n_tiles),
        grid=(n_tiles + 1,), in_specs=in_specs, out_specs=out_specs, out_shape=out_shape,
        scratch_shapes=scratch,
        compiler_params=pltpu.CompilerParams(
            dimension_semantics=("arbitrary",),
            vmem_limit_bytes=VMEM_LIMIT_BYTES),
        name="mixer_prompt",
    )(x, cos2, sin2, npre, npost, win, wdw, bdw, lnw, lnb, wco, wro, wo)


def _ffn_body(h, npre, npost, wup_ref, wdn_ref):
    hn = _rms(h, npre).astype(BF16)
    f = None
    for c in range(FFN_DIM // FFN_CHUNK):
        cols = slice(c * FFN_CHUNK, (c + 1) * FFN_CHUNK)
        up = _dot(hn, wup_ref[:, cols])
        act = jnp.square(jnp.maximum(up, 0.0)).astype(BF16)
        part = _dot(act, wdn_ref[cols, :])
        f = part if f is None else f + part
    return h + _rms(f, npost)


def _sample_state_body(blk, cache_ref, u_ref, wdw_ref, bdw_ref, qt_ref, kt_ref, v_ref, st_ref,
                       conv_ref, nc_ref, o_ref, so_ref):
    n_seq = qt_ref.shape[1]
    u = u_ref[...]
    acc = u * wdw_ref[HALO] + bdw_ref[...]
    for j in range(HALO):
        acc = acc + cache_ref[j] * wdw_ref[j]
    conv_ref[...] = acc
    nc_ref[0:HALO - 1] = cache_ref[1:HALO]
    nc_ref[HALO - 1] = u

    shift = (n_seq - blk * SAMPLE_BLOCK) % n_seq
    qt = pltpu.roll(qt_ref[...], shift, 1)
    kt = pltpu.roll(kt_ref[...], shift, 1)
    for s in range(SAMPLE_BLOCK):
        for h in range(HEADS):
            k_col = kt[h * DK:(h + 1) * DK, s:s + 1]
            q_col = qt[h * DK:(h + 1) * DK, s:s + 1]
            v_row = v_ref[s:s + 1, h * DV:(h + 1) * DV]
            new_state = GAMMA[h] * st_ref[s, h] + k_col * v_row
            so_ref[s, h] = new_state
            o_ref[s:s + 1, h * DV:(h + 1) * DV] = jnp.sum(new_state * q_col, axis=0,
                                                          keepdims=True)


def _ffn_state_kernel(h_ref, npre_ref, npost_ref, wup_ref, wdn_ref,
                      cache_ref, u_ref, wdw_ref, bdw_ref, qt_ref, kt_ref, v_ref, st_ref,
                      y_ref, conv_ref, nc_ref, o_ref, so_ref, *, steps_per_block):
    y_ref[...] = _ffn_body(h_ref[...], npre_ref[...], npost_ref[...], wup_ref, wdn_ref)
    _sample_state_body(pl.program_id(0) // steps_per_block, cache_ref, u_ref, wdw_ref, bdw_ref,
                       qt_ref, kt_ref, v_ref, st_ref, conv_ref, nc_ref, o_ref, so_ref)


def _ffn_state(h, npre, npost, wup, wdn, cache, u, wdw, bdw, qt, kt, v, state):
    n = h.shape[0]
    rows = FFN_ROWS
    steps = n // rows
    n_seq = u.shape[0]
    sb = SAMPLE_BLOCK
    spb = steps // (n_seq // sb)
    assert spb * (n_seq // sb) == steps
    in_specs = [
        pl.BlockSpec((rows, D_MODEL), lambda i: (i, 0)),
        _const_spec((1, D_MODEL)),
        _const_spec((1, D_MODEL)),
        _const_spec((D_MODEL, FFN_DIM)),
        _const_spec((FFN_DIM, D_MODEL)),
        pl.BlockSpec((HALO, sb, CONV_DIM), lambda i: (0, i // spb, 0)),
        pl.BlockSpec((sb, CONV_DIM), lambda i: (i // spb, 0)),
        _const_spec((CONV_WIDTH, 1, CONV_DIM)),
        _const_spec((1, CONV_DIM)),
        _const_spec((QK_W, n_seq)),
        _const_spec((QK_W, n_seq)),
        pl.BlockSpec((sb, V_W), lambda i: (i // spb, 0)),
        pl.BlockSpec((sb, HEADS, DK, DV), lambda i: (i // spb, 0, 0, 0)),
    ]
    out_specs = [
        pl.BlockSpec((rows, D_MODEL), lambda i: (i, 0)),
        pl.BlockSpec((sb, CONV_DIM), lambda i: (i // spb, 0)),
        pl.BlockSpec((HALO, sb, CONV_DIM), lambda i: (0, i // spb, 0)),
        pl.BlockSpec((sb, V_W), lambda i: (i // spb, 0)),
        pl.BlockSpec((sb, HEADS, DK, DV), lambda i: (i // spb, 0, 0, 0)),
    ]
    out_shape = [
        jax.ShapeDtypeStruct((n, D_MODEL), F32),
        jax.ShapeDtypeStruct((n_seq, CONV_DIM), F32),
        jax.ShapeDtypeStruct((HALO, n_seq, CONV_DIM), F32),
        jax.ShapeDtypeStruct((n_seq, V_W), F32),
        jax.ShapeDtypeStruct((n_seq, HEADS, DK, DV), F32),
    ]
    return pl.pallas_call(
        functools.partial(_ffn_state_kernel, steps_per_block=spb),
        grid=(steps,), in_specs=in_specs, out_specs=out_specs, out_shape=out_shape,
        compiler_params=pltpu.CompilerParams(
            dimension_semantics=("arbitrary",),
            vmem_limit_bytes=VMEM_LIMIT_BYTES),
        name="ffn_prompt_sample_state",
    )(h, npre, npost, wup, wdn, cache, u, wdw, bdw, qt, kt, v, state)


def _sample_pre_kernel(x_ref, cos_ref, sin_ref, npre_ref, win_ref,
                       u_ref, qt_ref, kt_ref, v_ref, g_ref, ga_ref, gb_ref):
    xn = _rms(x_ref[:, 0, :], npre_ref[...]).astype(BF16)
    pc = _dot(xn, win_ref[:, C_CONV:C_CONV + 2 * CONV_DIM])
    u_ref[...] = pc[:, :CONV_DIM] * jax.nn.sigmoid(pc[:, CONV_DIM:])
    v_ref[...] = _dot(xn, win_ref[:, C_V:C_V + V_W])
    g_ref[...] = _dot(xn, win_ref[:, C_G:C_G + V_W])
    pgate = _dot(xn, win_ref[:, C_GATES:C_GATES + 2 * D_MODEL])
    ga_ref[...] = pgate[:, :D_MODEL]
    gb_ref[...] = pgate[:, D_MODEL:]
    qkt = _dot(xn, win_ref[:, C_Q:C_Q + 2 * QK_W]).T
    cos_t = cos_ref[...]
    sin_t = sin_ref[...]
    half = DK // 2
    for g in range(2 * HEADS):
        x1 = qkt[g * DK:g * DK + half]
        x2 = qkt[g * DK + half:(g + 1) * DK]
        o1 = x1 * cos_t - x2 * sin_t
        o2 = x2 * cos_t + x1 * sin_t
        if g < HEADS:
            qt_ref[g * DK:g * DK + half, :] = o1
            qt_ref[g * DK + half:(g + 1) * DK, :] = o2
        else:
            k0 = (g - HEADS) * DK
            kt_ref[k0:k0 + half, :] = o1 * K_SCALE
            kt_ref[k0 + half:k0 + DK, :] = o2 * K_SCALE


def _sample_pre(x, cos_t, sin_t, npre, win):
    n = x.shape[0]
    out_shape = [
        jax.ShapeDtypeStruct((n, CONV_DIM), F32),
        jax.ShapeDtypeStruct((QK_W, n), F32),
        jax.ShapeDtypeStruct((QK_W, n), F32),
        jax.ShapeDtypeStruct((n, V_W), F32),
        jax.ShapeDtypeStruct((n, V_W), F32),
        jax.ShapeDtypeStruct((n, D_MODEL), F32),
        jax.ShapeDtypeStruct((n, D_MODEL), F32),
    ]
    return pl.pallas_call(
        _sample_pre_kernel, out_shape=out_shape,
        compiler_params=pltpu.CompilerParams(vmem_limit_bytes=VMEM_LIMIT_BYTES),
        name="sample_pre",
    )(x, cos_t, sin_t, npre, win)


def _sample_post_kernel(x_ref, conv_ref, o_ref, g_ref, ga_ref, gb_ref, lnw_ref, lnb_ref,
                        wco_ref, wro_ref, wo_ref, npost_ref, nfpre_ref, nfpost_ref,
                        wup_ref, wdn_ref, y_ref):
    a_act = _silu(_layer_norm(conv_ref[...], lnw_ref[...], lnb_ref[...])).astype(BF16)
    a_out = _dot(a_act, wco_ref[...])
    b_out = jnp.zeros_like(a_out)
    for h in range(HEADS):
        cols = slice(h * DV, (h + 1) * DV)
        yh = (_silu(g_ref[:, cols]) * _group_norm(o_ref[:, cols])).astype(BF16)
        b_out = b_out + _dot(yh, wro_ref[cols, :])
    merged = jax.nn.sigmoid(ga_ref[...]) * a_out + jax.nn.sigmoid(gb_ref[...]) * b_out
    m = _dot(merged.astype(BF16), wo_ref[...])
    hres = x_ref[:, 0, :] + _rms(m, npost_ref[...])
    y_ref[:, 0, :] = _ffn_body(hres, nfpre_ref[...], nfpost_ref[...], wup_ref, wdn_ref)


def _sample_post(x, conv, o, g, ga, gb, lnw, lnb, wco, wro, wo, npost, nfpre, nfpost, wup, wdn):
    return pl.pallas_call(
        _sample_post_kernel,
        out_shape=jax.ShapeDtypeStruct(x.shape, F32),
        compiler_params=pltpu.CompilerParams(vmem_limit_bytes=VMEM_LIMIT_BYTES),
        name="sample_post",
    )(x, conv, o, g, ga, gb, lnw, lnb, wco, wro, wo, npost, nfpre, nfpost, wup, wdn)


def _rope_angles(pos):
    half = DK // 2
    freqs = 1.0 / (ROPE_BASE ** jnp.linspace(0.0, 1.0, half, dtype=F32))
    return pos[:, None] * freqs[None, :]


def kernel(x_prompt, x_sample, cache_conv, state_ret, norm_mix_pre, norm_mix_post, w_in, w_dw, b_dw, conv_ln_w, conv_ln_b, w_conv_out, w_ret_out, w_o, norm_ffn_pre, norm_ffn_post, w_ffn_up, w_ffn_down):
    batch, seq, _ = x_prompt.shape
    n_seq, dec_seq, _ = x_sample.shape
    depth = w_in.shape[0]
    assert dec_seq == 1 and seq % MIX_ROWS == 0 and (batch * seq) % FFN_ROWS == 0
    assert n_seq % SAMPLE_BLOCK == 0

    ang_p = _rope_angles(jnp.arange(seq, dtype=F32))
    cos_p = jnp.concatenate([jnp.cos(ang_p), jnp.cos(ang_p)], axis=1)
    sin_p = jnp.concatenate([-jnp.sin(ang_p), jnp.sin(ang_p)], axis=1)
    ang_s = _rope_angles(PAST_LEN + jnp.arange(dec_seq, dtype=F32))
    cos_s = jnp.broadcast_to(jnp.cos(ang_s).T, (DK // 2, n_seq))
    sin_s = jnp.broadcast_to(jnp.sin(ang_s).T, (DK // 2, n_seq))

    xp = x_prompt
    xs = x_sample
    conv_p, ret_p, conv_s, ret_s = [], [], [], []
    for l in range(depth):
        npre = norm_mix_pre[l][None]
        npost = norm_mix_post[l][None]
        nfpre = norm_ffn_pre[l][None]
        nfpost = norm_ffn_post[l][None]
        win = w_in[l].astype(BF16)
        wdw = jnp.transpose(w_dw, (1, 0, 2))[:, l:l + 1, :]
        bdw = b_dw[l][None]
        lnw = conv_ln_w[l][None]
        lnb = conv_ln_b[l][None]
        wco = w_conv_out[l].astype(BF16)
        wro = w_ret_out[l].astype(BF16)
        wo = w_o[l].astype(BF16)
        wup = w_ffn_up[l].astype(BF16)
        wdn = w_ffn_down[l].astype(BF16)

        u, qt, kt, v, g, ga, gb = _sample_pre(xs, cos_s, sin_s, npre, win)
        cache_t = jnp.transpose(cache_conv[l], (1, 0, 2))
        h_p, nc_p, s_p = _mixer_prompt(xp, cos_p, sin_p, npre, npost, win, wdw, bdw, lnw, lnb,
                                       wco, wro, wo)
        y_p, conv, nc_t, o, s_s = _ffn_state(h_p.reshape(batch * seq, D_MODEL), nfpre, nfpost,
                                             wup, wdn, cache_t, u, wdw, bdw, qt, kt, v,
                                             state_ret[l])
        xp = y_p.reshape(batch, seq, D_MODEL)
        nc_s = jnp.transpose(nc_t, (1, 0, 2))
        xs = _sample_post(xs, conv, o, g, ga, gb, lnw, lnb, wco, wro, wo, npost, nfpre, nfpost,
                          wup, wdn)
        conv_p.append(nc_p)
        ret_p.append(s_p)
        conv_s.append(nc_s)
        ret_s.append(s_s)

    return (xp, xs, jnp.stack(conv_p), jnp.stack(ret_p),
            jnp.stack(conv_s), jnp.stack(ret_s))
```

```python
import functools
import math

import jax
import jax.numpy as jnp
from jax import lax
from jax.experimental import pallas as pl
from jax.experimental.pallas import tpu as pltpu

F32 = jnp.float32
BF16 = jnp.bfloat16

D_MODEL = 1024
CONV_DIM = 512
CONV_WIDTH = 31
HALO = CONV_WIDTH - 1
HEADS = 4
DK = 128
DV = 256
QK_W = HEADS * DK
V_W = HEADS * DV
CHUNK = 128
FFN_DIM = 4 * D_MODEL
EPS = 1e-6
ROPE_BASE = 10000.0
PAST_LEN = 16384
K_SCALE = DK ** -0.5

C_CONV = 0
C_Q = 2 * CONV_DIM
C_K = C_Q + QK_W
C_V = C_K + QK_W
C_G = C_V + V_W
C_GATES = C_G + V_W
IN_COLS = C_GATES + 2 * D_MODEL

LOG_GAMMA = tuple(math.log1p(-(2.0 ** (-5 - h))) for h in range(HEADS))
GAMMA = tuple(math.exp(lg) for lg in LOG_GAMMA)
GAMMA_CHUNK = tuple(math.exp(CHUNK * lg) for lg in LOG_GAMMA)

VMEM_LIMIT_BYTES = 60 * 1024 * 1024
SUBLANES = 8
HALO_PAD = 32
MIX_ROWS = 512
FFN_ROWS = 1024
FFN_CHUNK = 1024
CONV_TILE = 256
CONV_ROWS = 32
SAMPLE_BLOCK = 8
PRE_COLS = 1024

NT_DIMS = (((1,), (1,)), ((), ()))
TN_DIMS = (((0,), (0,)), ((), ()))


def _dot(a, b):
    return jnp.dot(a, b, preferred_element_type=F32)


def _rms(x, w):
    return x * lax.rsqrt(jnp.mean(x * x, axis=-1, keepdims=True) + EPS) * w


def _layer_norm(x, w, b):
    mu = jnp.mean(x, axis=-1, keepdims=True)
    xc = x - mu
    return xc * lax.rsqrt(jnp.mean(xc * xc, axis=-1, keepdims=True) + EPS) * w + b


def _silu(x):
    return x * jax.nn.sigmoid(x)


def _group_norm(o):
    return o * lax.rsqrt(jnp.mean(o * o, axis=-1, keepdims=True) + EPS)


def _const_spec(shape):
    zeros = (0,) * len(shape)
    return pl.BlockSpec(shape, lambda *_: zeros, pipeline_mode=pl.Buffered(1))


def _mixer_prompt_kernel(x_ref, cos_ref, sin_ref, npre_ref, npost_ref, win_ref, wdw_ref,
                         bdw_ref, lnw_ref, lnb_ref, wco_ref, wro_ref, wo_ref,
                         h_ref, nc_ref, sout_ref,
                         fb_ref, u_ref, y_ref, dec_ref, qd_ref, kd_ref, wb_ref, s_ref,
                         aact_ref, gate_ref, bout_ref, xprev_ref, *, tiles_per_seq, n_tiles):
    g = pl.program_id(0)
    i = jnp.minimum(g, n_tiles - 1) % tiles_per_seq
    rows = x_ref.shape[1]

    @pl.when(g == 0)
    def _init_tables():
        aact_ref[...] = jnp.zeros(aact_ref.shape, BF16)
        gate_ref[...] = jnp.zeros(gate_ref.shape, F32)
        bout_ref[...] = jnp.zeros(bout_ref.shape, F32)
        xprev_ref[...] = jnp.zeros(xprev_ref.shape, F32)
        ii = lax.broadcasted_iota(jnp.int32, (CHUNK, CHUNK), 0)
        jj = lax.broadcasted_iota(jnp.int32, (CHUNK, CHUNK), 1)
        diff = (ii - jj).astype(F32)
        row_k = lax.broadcasted_iota(jnp.int32, (CHUNK, DK), 0).astype(F32)
        for h in range(HEADS):
            lg = LOG_GAMMA[h]
            dec_ref[h] = jnp.where(diff >= 0.0, jnp.exp(jnp.maximum(diff, 0.0) * lg), 0.0)
            qd_ref[h] = jnp.exp((row_k + 1.0) * lg)
            kd_ref[h] = jnp.exp((CHUNK - 1.0 - row_k) * lg)
        for j in range(CONV_WIDTH):
            wb_ref[j] = jnp.broadcast_to(wdw_ref[j], (SUBLANES, CONV_DIM))

    @pl.when(i == 0)
    def _start_sequence():
        fb_ref[0, 0:HALO_PAD, :] = jnp.zeros((HALO_PAD, CONV_DIM), F32)
        s_ref[...] = jnp.zeros(s_ref.shape, F32)

    a_out = _dot(aact_ref[...], wco_ref[...])
    x = x_ref[0]
    xn = _rms(x, npre_ref[...]).astype(BF16)
    pc = _dot(xn, win_ref[:, C_CONV:C_CONV + 2 * CONV_DIM])
    merged = (jax.nn.sigmoid(gate_ref[:, :D_MODEL]) * a_out
              + jax.nn.sigmoid(gate_ref[:, D_MODEL:]) * bout_ref[...])
    m = _dot(merged.astype(BF16), wo_ref[...])
    h_ref[0] = xprev_ref[...] + _rms(m, npost_ref[...])

    xprev_ref[...] = x

    u_ref[...] = pc[:, :CONV_DIM] * jax.nn.sigmoid(pc[:, CONV_DIM:])

    pqk = _dot(xn, win_ref[:, C_Q:C_Q + 2 * QK_W])
    pv = _dot(xn, win_ref[:, C_V:C_V + V_W]).astype(BF16)
    pg = _dot(xn, win_ref[:, C_G:C_G + V_W])

    first = HALO_PAD - HALO
    span = CONV_TILE + HALO_PAD - SUBLANES
    for t in range(rows // CONV_TILE):
        fb_ref[0, HALO_PAD:HALO_PAD + CONV_TILE, :] = u_ref[t * CONV_TILE:(t + 1) * CONV_TILE, :]
        for r in range(1, SUBLANES):
            fb_ref[r, 0:span, :] = fb_ref[0, r:r + span, :]
        for rb in range(CONV_TILE // CONV_ROWS):
            acc = jnp.broadcast_to(bdw_ref[...], (CONV_ROWS, CONV_DIM))
            for j in range(CONV_WIDTH):
                off = first + j
                base = rb * CONV_ROWS + off - off % SUBLANES
                slab = fb_ref[off % SUBLANES, base:base + CONV_ROWS, :]
                acc = acc + (slab.reshape(CONV_ROWS // SUBLANES, SUBLANES, CONV_DIM)
                             * wb_ref[j][None]).reshape(CONV_ROWS, CONV_DIM)
            r0 = t * CONV_TILE + rb * CONV_ROWS
            aact_ref[r0:r0 + CONV_ROWS, :] = _silu(
                _layer_norm(acc, lnw_ref[...], lnb_ref[...])).astype(BF16)
        fb_ref[0, 0:HALO_PAD, :] = fb_ref[0, CONV_TILE:CONV_TILE + HALO_PAD, :]

    cos2 = cos_ref[...]
    sin2 = sin_ref[...]

    def rot(t):
        return t * cos2 + pltpu.roll(t, DK // 2, 1) * sin2

    heads = range(HEADS)
    q_rot = [rot(pqk[:, h * DK:(h + 1) * DK]) for h in heads]
    q_bf = [q.astype(BF16) for q in q_rot]
    k_rot = [rot(pqk[:, QK_W + h * DK:QK_W + (h + 1) * DK]) * K_SCALE for h in heads]
    k_bf = [k.astype(BF16) for k in k_rot]
    g_act = [_silu(pg[:, h * DV:(h + 1) * DV]) for h in heads]

    n_chunks = rows // CHUNK
    n_fill = 2 * n_chunks
    gate_cols = 2 * D_MODEL // n_fill

    def gate_slice(f):
        gate_ref[:, f * gate_cols:(f + 1) * gate_cols] = _dot(
            xn, win_ref[:, C_GATES + f * gate_cols:C_GATES + (f + 1) * gate_cols])

    for c in range(n_chunks):
        rs = slice(c * CHUNK, (c + 1) * CHUNK)
        v_c = [pv[rs, h * DV:(h + 1) * DV] for h in heads]
        state = [s_ref[h] for h in heads]
        scores = []
        zeros = jnp.zeros((CHUNK, DK), BF16)
        for a in range(0, HEADS, 2):
            q_pair = jnp.concatenate([q_bf[a][rs], q_bf[a + 1][rs]], axis=1)
            k_pair = jnp.concatenate(
                [jnp.concatenate([k_bf[a][rs], zeros], axis=1),
                 jnp.concatenate([zeros, k_bf[a + 1][rs]], axis=1)], axis=0)
            s_pair = lax.dot_general(q_pair, k_pair, NT_DIMS, preferred_element_type=F32)
            scores += [s_pair[:, :CHUNK], s_pair[:, CHUNK:]]
        kv = [lax.dot_general((k_rot[h][rs] * kd_ref[h]).astype(BF16), v_c[h], TN_DIMS,
                              preferred_element_type=F32) for h in heads]
        gate_slice(2 * c)
        lhs = [jnp.concatenate([(scores[h] * dec_ref[h]).astype(BF16),
                                (q_rot[h][rs] * qd_ref[h]).astype(BF16)], axis=1)
               for h in heads]
        rhs = [jnp.concatenate([v_c[h], state[h].astype(BF16)], axis=0) for h in heads]
        out = [_dot(lhs[h], rhs[h]) for h in heads]
        gate_slice(2 * c + 1)
        for h in heads:
            s_ref[h] = GAMMA_CHUNK[h] * state[h] + kv[h]
            y_ref[rs, h * DV:(h + 1) * DV] = (g_act[h][rs] * _group_norm(out[h])).astype(BF16)
    bout_ref[...] = _dot(y_ref[...], wro_ref[...])

    @pl.when(jnp.logical_and(i == tiles_per_seq - 1, g < n_tiles))
    def _emit_sequence_state():
        nc_ref[0] = fb_ref[0, HALO_PAD - HALO:HALO_PAD, :]
        sout_ref[0] = s_ref[...]


def _mixer_prompt(x, cos2, sin2, npre, npost, win, wdw, bdw, lnw, lnb, wco, wro, wo):
    batch, seq, _ = x.shape
    rows = MIX_ROWS
    nt = seq // rows
    n_tiles = batch * nt

    def front(g):
        return jnp.minimum(g, n_tiles - 1)

    def back(g):
        return jnp.maximum(g - 1, 0)

    in_specs = [
        pl.BlockSpec((1, rows, D_MODEL), lambda g: (front(g) // nt, front(g) % nt, 0)),
        pl.BlockSpec((rows, DK), lambda g: (front(g) % nt, 0)),
        pl.BlockSpec((rows, DK), lambda g: (front(g) % nt, 0)),
        _const_spec((1, D_MODEL)),
        _const_spec((1, D_MODEL)),
        _const_spec((D_MODEL, IN_COLS)),
        _const_spec((CONV_WIDTH, 1, CONV_DIM)),
        _const_spec((1, CONV_DIM)),
        _const_spec((1, CONV_DIM)),
        _const_spec((1, CONV_DIM)),
        _const_spec((CONV_DIM, D_MODEL)),
        _const_spec((V_W, D_MODEL)),
        _const_spec((D_MODEL, D_MODEL)),
    ]
    out_specs = [
        pl.BlockSpec((1, rows, D_MODEL), lambda g: (back(g) // nt, back(g) % nt, 0)),
        pl.BlockSpec((1, HALO, CONV_DIM), lambda g: (front(g) // nt, 0, 0)),
        pl.BlockSpec((1, HEADS, DK, DV), lambda g: (front(g) // nt, 0, 0, 0)),
    ]
    out_shape = [
        jax.ShapeDtypeStruct((batch, seq, D_MODEL), F32),
        jax.ShapeDtypeStruct((batch, HALO, CONV_DIM), F32),
        jax.ShapeDtypeStruct((batch, HEADS, DK, DV), F32),
    ]
    scratch = [
        pltpu.VMEM((SUBLANES, HALO_PAD + CONV_TILE, CONV_DIM), F32),
        pltpu.VMEM((rows, CONV_DIM), F32),
        pltpu.VMEM((rows, V_W), BF16),
        pltpu.VMEM((HEADS, CHUNK, CHUNK), F32),
        pltpu.VMEM((HEADS, CHUNK, DK), F32),
        pltpu.VMEM((HEADS, CHUNK, DK), F32),
        pltpu.VMEM((CONV_WIDTH, SUBLANES, CONV_DIM), F32),
        pltpu.VMEM((HEADS, DK, DV), F32),
        pltpu.VMEM((rows, CONV_DIM), BF16),
        pltpu.VMEM((rows, 2 * D_MODEL), F32),
        pltpu.VMEM((rows, D_MODEL), F32),
        pltpu.VMEM((rows, D_MODEL), F32),
    ]
    return pl.pallas_call(
        functools.partial(_mixer_prompt_kernel, tiles_per_seq=nt, n_tiles=n_tiles),
        grid=(n_tiles + 1,), in_specs=in_specs, out_specs=out_specs, out_shape=out_shape,
        scratch_shapes=scratch,
        compiler_params=pltpu.CompilerParams(
            dimension_semantics=("arbitrary",),
            vmem_limit_bytes=VMEM_LIMIT_BYTES),
        name="mixer_prompt",
    )(x, cos2, sin2, npre, npost, win, wdw, bdw, lnw, lnb, wco, wro, wo)


def _ffn_body(h, npre, npost, wup_ref, wdn_ref):
    hn = _rms(h, npre).astype(BF16)
    f = None
    for c in range(FFN_DIM // FFN_CHUNK):
        cols = slice(c * FFN_CHUNK, (c + 1) * FFN_CHUNK)
        up = _dot(hn, wup_ref[:, cols])
        act = jnp.square(jnp.maximum(up, 0.0)).astype(BF16)
        part = _dot(act, wdn_ref[cols, :])
        f = part if f is None else f + part
    return h + _rms(f, npost)


def _sample_state_body(blk, cache_ref, u_ref, wdw_ref, bdw_ref, qt_ref, kt_ref, v_ref, st_ref,
                       conv_ref, nc_ref, o_ref, so_ref):
    n_seq = qt_ref.shape[1]
    u = u_ref[...]
    acc = u * wdw_ref[HALO] + bdw_ref[...]
    for j in range(HALO):
        acc = acc + cache_ref[j] * wdw_ref[j]
    conv_ref[...] = acc
    nc_ref[0:HALO - 1] = cache_ref[1:HALO]
    nc_ref[HALO - 1] = u

    shift = (n_seq - blk * SAMPLE_BLOCK) % n_seq
    qt = pltpu.roll(qt_ref[...], shift, 1)
    kt = pltpu.roll(kt_ref[...], shift, 1)
    for s in range(SAMPLE_BLOCK):
        for h in range(HEADS):
            k_col = kt[h * DK:(h + 1) * DK, s:s + 1]
            q_col = qt[h * DK:(h + 1) * DK, s:s + 1]
            v_row = v_ref[s:s + 1, h * DV:(h + 1) * DV]
            new_state = GAMMA[h] * st_ref[s, h] + k_col * v_row
            so_ref[s, h] = new_state
            o_ref[s:s + 1, h * DV:(h + 1) * DV] = jnp.sum(new_state * q_col, axis=0,
                                                          keepdims=True)


def _ffn_state_kernel(h_ref, npre_ref, npost_ref, wup_ref, wdn_ref,
                      cache_ref, u_ref, wdw_ref, bdw_ref, qt_ref, kt_ref, v_ref, st_ref,
                      y_ref, conv_ref, nc_ref, o_ref, so_ref, *, steps_per_block):
    y_ref[...] = _ffn_body(h_ref[...], npre_ref[...], npost_ref[...], wup_ref, wdn_ref)
    _sample_state_body(pl.program_id(0) // steps_per_block, cache_ref, u_ref, wdw_ref, bdw_ref,
                       qt_ref, kt_ref, v_ref, st_ref, conv_ref, nc_ref, o_ref, so_ref)


def _ffn_state(h, npre, npost, wup, wdn, cache, u, wdw, bdw, qt, kt, v, state):
    n = h.shape[0]
    rows = FFN_ROWS
    steps = n // rows
    n_seq = u.shape[0]
    sb = SAMPLE_BLOCK
    spb = steps // (n_seq // sb)
    assert spb * (n_seq // sb) == steps
    in_specs = [
        pl.BlockSpec((rows, D_MODEL), lambda i: (i, 0)),
        _const_spec((1, D_MODEL)),
        _const_spec((1, D_MODEL)),
        _const_spec((D_MODEL, FFN_DIM)),
        _const_spec((FFN_DIM, D_MODEL)),
        pl.BlockSpec((HALO, sb, CONV_DIM), lambda i: (0, i // spb, 0)),
        pl.BlockSpec((sb, CONV_DIM), lambda i: (i // spb, 0)),
        _const_spec((CONV_WIDTH, 1, CONV_DIM)),
        _const_spec((1, CONV_DIM)),
        _const_spec((QK_W, n_seq)),
        _const_spec((QK_W, n_seq)),
        pl.BlockSpec((sb, V_W), lambda i: (i // spb, 0)),
        pl.BlockSpec((sb, HEADS, DK, DV), lambda i: (i // spb, 0, 0, 0)),
    ]
    out_specs = [
        pl.BlockSpec((rows, D_MODEL), lambda i: (i, 0)),
        pl.BlockSpec((sb, CONV_DIM), lambda i: (i // spb, 0)),
        pl.BlockSpec((HALO, sb, CONV_DIM), lambda i: (0, i // spb, 0)),
        pl.BlockSpec((sb, V_W), lambda i: (i // spb, 0)),
        pl.BlockSpec((sb, HEADS, DK, DV), lambda i: (i // spb, 0, 0, 0)),
    ]
    out_shape = [
        jax.ShapeDtypeStruct((n, D_MODEL), F32),
        jax.ShapeDtypeStruct((n_seq, CONV_DIM), F32),
        jax.ShapeDtypeStruct((HALO, n_seq, CONV_DIM), F32),
        jax.ShapeDtypeStruct((n_seq, V_W), F32),
        jax.ShapeDtypeStruct((n_seq, HEADS, DK, DV), F32),
    ]
    return pl.pallas_call(
        functools.partial(_ffn_state_kernel, steps_per_block=spb),
        grid=(steps,), in_specs=in_specs, out_specs=out_specs, out_shape=out_shape,
        compiler_params=pltpu.CompilerParams(
            dimension_semantics=("arbitrary",),
            vmem_limit_bytes=VMEM_LIMIT_BYTES),
        name="ffn_prompt_sample_state",
    )(h, npre, npost, wup, wdn, cache, u, wdw, bdw, qt, kt, v, state)


def _sample_pre_kernel(x_ref, cos_ref, sin_ref, npre_ref, win_ref,
                       wbf_ref, u_ref, qt_ref, kt_ref, v_ref, g_ref, ga_ref, gb_ref, xn_ref):
    j = pl.program_id(0)

    @pl.when(j == 0)
    def _normalise():
        xn_ref[...] = _rms(x_ref[:, 0, :], npre_ref[...]).astype(BF16)

    w_slab = win_ref[...].astype(BF16)
    wbf_ref[...] = w_slab
    p = _dot(xn_ref[...], w_slab)

    @pl.when(j == C_CONV // PRE_COLS)
    def _glu():
        u_ref[...] = p[:, :CONV_DIM] * jax.nn.sigmoid(p[:, CONV_DIM:])

    @pl.when(j == C_Q // PRE_COLS)
    def _rotary():
        qkt = p.T
        cos_t = cos_ref[...]
        sin_t = sin_ref[...]
        half = DK // 2
        for g in range(2 * HEADS):
            x1 = qkt[g * DK:g * DK + half]
            x2 = qkt[g * DK + half:(g + 1) * DK]
            o1 = x1 * cos_t - x2 * sin_t
            o2 = x2 * cos_t + x1 * sin_t
            if g < HEADS:
                qt_ref[g * DK:g * DK + half, :] = o1
                qt_ref[g * DK + half:(g + 1) * DK, :] = o2
            else:
                k0 = (g - HEADS) * DK
                kt_ref[k0:k0 + half, :] = o1 * K_SCALE
                kt_ref[k0 + half:k0 + DK, :] = o2 * K_SCALE

    for out_ref, col in ((v_ref, C_V), (g_ref, C_G), (ga_ref, C_GATES),
                         (gb_ref, C_GATES + D_MODEL)):
        @pl.when(j == col // PRE_COLS)
        def _store(out_ref=out_ref):
            out_ref[...] = p


def _sample_pre(x, cos_t, sin_t, npre, win_f32):
    n = x.shape[0]
    assert all(c % PRE_COLS == 0 for c in (C_CONV, C_Q, C_V, C_G, C_GATES, IN_COLS))
    assert 2 * CONV_DIM == 2 * QK_W == V_W == D_MODEL == PRE_COLS

    def whole(shape):
        zeros = (0,) * len(shape)
        return pl.BlockSpec(shape, lambda j: zeros)

    in_specs = [
        whole((n, 1, D_MODEL)),
        whole((DK // 2, n)),
        whole((DK // 2, n)),
        whole((1, D_MODEL)),
        pl.BlockSpec((D_MODEL, PRE_COLS), lambda j: (0, j)),
    ]
    out_specs = [
        pl.BlockSpec((D_MODEL, PRE_COLS), lambda j: (0, j)),
        whole((n, CONV_DIM)),
        whole((QK_W, n)),
        whole((QK_W, n)),
        whole((n, V_W)),
        whole((n, V_W)),
        whole((n, D_MODEL)),
        whole((n, D_MODEL)),
    ]
    out_shape = [
        jax.ShapeDtypeStruct((D_MODEL, IN_COLS), BF16),
        jax.ShapeDtypeStruct((n, CONV_DIM), F32),
        jax.ShapeDtypeStruct((QK_W, n), F32),
        jax.ShapeDtypeStruct((QK_W, n), F32),
        jax.ShapeDtypeStruct((n, V_W), F32),
        jax.ShapeDtypeStruct((n, V_W), F32),
        jax.ShapeDtypeStruct((n, D_MODEL), F32),
        jax.ShapeDtypeStruct((n, D_MODEL), F32),
    ]
    return pl.pallas_call(
        _sample_pre_kernel,
        grid=(IN_COLS // PRE_COLS,), in_specs=in_specs, out_specs=out_specs, out_shape=out_shape,
        scratch_shapes=[pltpu.VMEM((n, D_MODEL), BF16)],
        compiler_params=pltpu.CompilerParams(
            dimension_semantics=("arbitrary",),
            vmem_limit_bytes=VMEM_LIMIT_BYTES),
        name="sample_pre_cast_w_in",
    )(x, cos_t, sin_t, npre, win_f32)


def _sample_post_kernel(x_ref, conv_ref, o_ref, g_ref, ga_ref, gb_ref, lnw_ref, lnb_ref,
                        wco_ref, wro_ref, wo_ref, npost_ref, nfpre_ref, nfpost_ref,
                        wup_ref, wdn_ref, y_ref):
    a_act = _silu(_layer_norm(conv_ref[...], lnw_ref[...], lnb_ref[...])).astype(BF16)
    a_out = _dot(a_act, wco_ref[...])
    b_out = jnp.zeros_like(a_out)
    for h in range(HEADS):
        cols = slice(h * DV, (h + 1) * DV)
        yh = (_silu(g_ref[:, cols]) * _group_norm(o_ref[:, cols])).astype(BF16)
        b_out = b_out + _dot(yh, wro_ref[cols, :])
    merged = jax.nn.sigmoid(ga_ref[...]) * a_out + jax.nn.sigmoid(gb_ref[...]) * b_out
    m = _dot(merged.astype(BF16), wo_ref[...])
    hres = x_ref[:, 0, :] + _rms(m, npost_ref[...])
    y_ref[:, 0, :] = _ffn_body(hres, nfpre_ref[...], nfpost_ref[...], wup_ref, wdn_ref)


def _sample_post(x, conv, o, g, ga, gb, lnw, lnb, wco, wro, wo, npost, nfpre, nfpost, wup, wdn):
    return pl.pallas_call(
        _sample_post_kernel,
        out_shape=jax.ShapeDtypeStruct(x.shape, F32),
        compiler_params=pltpu.CompilerParams(vmem_limit_bytes=VMEM_LIMIT_BYTES),
        name="sample_post",
    )(x, conv, o, g, ga, gb, lnw, lnb, wco, wro, wo, npost, nfpre, nfpost, wup, wdn)


def _rope_angles(pos):
    half = DK // 2
    freqs = 1.0 / (ROPE_BASE ** jnp.linspace(0.0, 1.0, half, dtype=F32))
    return pos[:, None] * freqs[None, :]


def kernel(x_prompt, x_sample, cache_conv, state_ret, norm_mix_pre, norm_mix_post, w_in, w_dw, b_dw, conv_ln_w, conv_ln_b, w_conv_out, w_ret_out, w_o, norm_ffn_pre, norm_ffn_post, w_ffn_up, w_ffn_down):
    batch, seq, _ = x_prompt.shape
    n_seq, dec_seq, _ = x_sample.shape
    depth = w_in.shape[0]
    assert dec_seq == 1 and seq % MIX_ROWS == 0 and (batch * seq) % FFN_ROWS == 0
    assert n_seq % SAMPLE_BLOCK == 0

    ang_p = _rope_angles(jnp.arange(seq, dtype=F32))
    cos_p = jnp.concatenate([jnp.cos(ang_p), jnp.cos(ang_p)], axis=1)
    sin_p = jnp.concatenate([-jnp.sin(ang_p), jnp.sin(ang_p)], axis=1)
    ang_s = _rope_angles(PAST_LEN + jnp.arange(dec_seq, dtype=F32))
    cos_s = jnp.broadcast_to(jnp.cos(ang_s).T, (DK // 2, n_seq))
    sin_s = jnp.broadcast_to(jnp.sin(ang_s).T, (DK // 2, n_seq))

    xp = x_prompt
    xs = x_sample
    conv_p, ret_p, conv_s, ret_s = [], [], [], []
    for l in range(depth):
        npre = norm_mix_pre[l][None]
        npost = norm_mix_post[l][None]
        nfpre = norm_ffn_pre[l][None]
        nfpost = norm_ffn_post[l][None]
        wdw = jnp.transpose(w_dw, (1, 0, 2))[:, l:l + 1, :]
        bdw = b_dw[l][None]
        lnw = conv_ln_w[l][None]
        lnb = conv_ln_b[l][None]
        wco = w_conv_out[l].astype(BF16)
        wro = w_ret_out[l].astype(BF16)
        wo = w_o[l].astype(BF16)
        wup = w_ffn_up[l].astype(BF16)
        wdn = w_ffn_down[l].astype(BF16)

        win, u, qt, kt, v, g, ga, gb = _sample_pre(xs, cos_s, sin_s, npre, w_in[l])
        cache_t = jnp.transpose(cache_conv[l], (1, 0, 2))
        h_p, nc_p, s_p = _mixer_prompt(xp, cos_p, sin_p, npre, npost, win, wdw, bdw, lnw, lnb,
                                       wco, wro, wo)
        y_p, conv, nc_t, o, s_s = _ffn_state(h_p.reshape(batch * seq, D_MODEL), nfpre, nfpost,
                                             wup, wdn, cache_t, u, wdw, bdw, qt, kt, v,
                                             state_ret[l])
        xp = y_p.reshape(batch, seq, D_MODEL)
        nc_s = jnp.transpose(nc_t, (1, 0, 2))
        xs = _sample_post(xs, conv, o, g, ga, gb, lnw, lnb, wco, wro, wo, npost, nfpre, nfpost,
                          wup, wdn)
        conv_p.append(nc_p)
        ret_p.append(s_p)
        conv_s.append(nc_s)
        ret_s.append(s_s)

    return (xp, xs, jnp.stack(conv_p), jnp.stack(ret_p),
            jnp.stack(conv_s), jnp.stack(ret_s))
```

```python
import functools
import math

import jax
import jax.numpy as jnp
from jax import lax
from jax.experimental import pallas as pl
from jax.experimental.pallas import tpu as pltpu

F32 = jnp.float32
BF16 = jnp.bfloat16

D_MODEL = 1024
CONV_DIM = 512
CONV_WIDTH = 31
HALO = CONV_WIDTH - 1
HEADS = 4
DK = 128
DV = 256
QK_W = HEADS * DK
V_W = HEADS * DV
CHUNK = 128
FFN_DIM = 4 * D_MODEL
EPS = 1e-6
ROPE_BASE = 10000.0
PAST_LEN = 16384
K_SCALE = DK ** -0.5

C_CONV = 0
C_Q = 2 * CONV_DIM
C_K = C_Q + QK_W
C_V = C_K + QK_W
C_G = C_V + V_W
C_GATES = C_G + V_W
IN_COLS = C_GATES + 2 * D_MODEL

LOG_GAMMA = tuple(math.log1p(-(2.0 ** (-5 - h))) for h in range(HEADS))
GAMMA = tuple(math.exp(lg) for lg in LOG_GAMMA)
GAMMA_CHUNK = tuple(math.exp(CHUNK * lg) for lg in LOG_GAMMA)

VMEM_LIMIT_BYTES = 60 * 1024 * 1024
SUBLANES = 8
HALO_PAD = 32
MIX_ROWS = 512
FFN_ROWS = 1024
FFN_CHUNK = 1024
CONV_TILE = 256
CONV_ROWS = 32
SAMPLE_BLOCK = 8
PRE_COLS = 1024

NT_DIMS = (((1,), (1,)), ((), ()))
TN_DIMS = (((0,), (0,)), ((), ()))


def _dot(a, b):
    return jnp.dot(a, b, preferred_element_type=F32)


def _rms(x, w):
    return x * lax.rsqrt(jnp.mean(x * x, axis=-1, keepdims=True) + EPS) * w


def _layer_norm(x, w, b):
    mu = jnp.mean(x, axis=-1, keepdims=True)
    xc = x - mu
    return xc * lax.rsqrt(jnp.mean(xc * xc, axis=-1, keepdims=True) + EPS) * w + b


def _silu(x):
    return x * jax.nn.sigmoid(x)


def _group_norm(o):
    return o * lax.rsqrt(jnp.mean(o * o, axis=-1, keepdims=True) + EPS)


def _const_spec(shape):
    zeros = (0,) * len(shape)
    return pl.BlockSpec(shape, lambda *_: zeros, pipeline_mode=pl.Buffered(1))


def _mixer_prompt_kernel(x_ref, cos_ref, sin_ref, npre_ref, npost_ref, win_ref, wdw_ref,
                         bdw_ref, lnw_ref, lnb_ref, wco_ref, wro_ref, wo_ref,
                         h_ref, nc_ref, sout_ref,
                         fb_ref, u_ref, y_ref, dec_ref, qd_ref, kd_ref, wb_ref, s_ref,
                         aact_ref, gate_ref, bout_ref, xprev_ref, *, tiles_per_seq, n_tiles):
    g = pl.program_id(0)
    i = jnp.minimum(g, n_tiles - 1) % tiles_per_seq
    rows = x_ref.shape[1]

    @pl.when(g == 0)
    def _init_tables():
        aact_ref[...] = jnp.zeros(aact_ref.shape, BF16)
        gate_ref[...] = jnp.zeros(gate_ref.shape, F32)
        bout_ref[...] = jnp.zeros(bout_ref.shape, F32)
        xprev_ref[...] = jnp.zeros(xprev_ref.shape, F32)
        ii = lax.broadcasted_iota(jnp.int32, (CHUNK, CHUNK), 0)
        jj = lax.broadcasted_iota(jnp.int32, (CHUNK, CHUNK), 1)
        diff = (ii - jj).astype(F32)
        row_k = lax.broadcasted_iota(jnp.int32, (CHUNK, DK), 0).astype(F32)
        for h in range(HEADS):
            lg = LOG_GAMMA[h]
            dec_ref[h] = jnp.where(diff >= 0.0, jnp.exp(jnp.maximum(diff, 0.0) * lg), 0.0)
            qd_ref[h] = jnp.exp((row_k + 1.0) * lg)
            kd_ref[h] = jnp.exp((CHUNK - 1.0 - row_k) * lg)
        for j in range(CONV_WIDTH):
            wb_ref[j] = jnp.broadcast_to(wdw_ref[j], (SUBLANES, CONV_DIM))

    @pl.when(i == 0)
    def _start_sequence():
        fb_ref[0, 0:HALO_PAD, :] = jnp.zeros((HALO_PAD, CONV_DIM), F32)
        s_ref[...] = jnp.zeros(s_ref.shape, F32)

    a_out = _dot(aact_ref[...], wco_ref[...])
    x = x_ref[0]
    xn = _rms(x, npre_ref[...]).astype(BF16)
    pc = _dot(xn, win_ref[:, C_CONV:C_CONV + 2 * CONV_DIM])
    merged = (jax.nn.sigmoid(gate_ref[:, :D_MODEL]) * a_out
              + jax.nn.sigmoid(gate_ref[:, D_MODEL:]) * bout_ref[...])
    m = _dot(merged.astype(BF16), wo_ref[...])
    h_ref[0] = xprev_ref[...] + _rms(m, npost_ref[...])

    xprev_ref[...] = x

    u_ref[...] = pc[:, :CONV_DIM] * jax.nn.sigmoid(pc[:, CONV_DIM:])

    pqk = _dot(xn, win_ref[:, C_Q:C_Q + 2 * QK_W])
    pv = _dot(xn, win_ref[:, C_V:C_V + V_W]).astype(BF16)
    pg = _dot(xn, win_ref[:, C_G:C_G + V_W])

    first = HALO_PAD - HALO
    span = CONV_TILE + HALO_PAD - SUBLANES
    for t in range(rows // CONV_TILE):
        fb_ref[0, HALO_PAD:HALO_PAD + CONV_TILE, :] = u_ref[t * CONV_TILE:(t + 1) * CONV_TILE, :]
        for r in range(1, SUBLANES):
            fb_ref[r, 0:span, :] = fb_ref[0, r:r + span, :]
        for rb in range(CONV_TILE // CONV_ROWS):
            acc = jnp.broadcast_to(bdw_ref[...], (CONV_ROWS, CONV_DIM))
            for j in range(CONV_WIDTH):
                off = first + j
                base = rb * CONV_ROWS + off - off % SUBLANES
                slab = fb_ref[off % SUBLANES, base:base + CONV_ROWS, :]
                acc = acc + (slab.reshape(CONV_ROWS // SUBLANES, SUBLANES, CONV_DIM)
                             * wb_ref[j][None]).reshape(CONV_ROWS, CONV_DIM)
            r0 = t * CONV_TILE + rb * CONV_ROWS
            aact_ref[r0:r0 + CONV_ROWS, :] = _silu(
                _layer_norm(acc, lnw_ref[...], lnb_ref[...])).astype(BF16)
        fb_ref[0, 0:HALO_PAD, :] = fb_ref[0, CONV_TILE:CONV_TILE + HALO_PAD, :]

    cos2 = cos_ref[...]
    sin2 = sin_ref[...]

    def rot(t):
        return t * cos2 + pltpu.roll(t, DK // 2, 1) * sin2

    heads = range(HEADS)
    q_rot = [rot(pqk[:, h * DK:(h + 1) * DK]) for h in heads]
    q_bf = [q.astype(BF16) for q in q_rot]
    k_rot = [rot(pqk[:, QK_W + h * DK:QK_W + (h + 1) * DK]) * K_SCALE for h in heads]
    k_bf = [k.astype(BF16) for k in k_rot]
    g_act = [_silu(pg[:, h * DV:(h + 1) * DV]) for h in heads]

    n_chunks = rows // CHUNK
    n_fill = 2 * n_chunks
    gate_cols = 2 * D_MODEL // n_fill

    def gate_slice(f):
        gate_ref[:, f * gate_cols:(f + 1) * gate_cols] = _dot(
            xn, win_ref[:, C_GATES + f * gate_cols:C_GATES + (f + 1) * gate_cols])

    for c in range(n_chunks):
        rs = slice(c * CHUNK, (c + 1) * CHUNK)
        v_c = [pv[rs, h * DV:(h + 1) * DV] for h in heads]
        state = [s_ref[h] for h in heads]
        scores = []
        zeros = jnp.zeros((CHUNK, DK), BF16)
        for a in range(0, HEADS, 2):
            q_pair = jnp.concatenate([q_bf[a][rs], q_bf[a + 1][rs]], axis=1)
            k_pair = jnp.concatenate(
                [jnp.concatenate([k_bf[a][rs], zeros], axis=1),
                 jnp.concatenate([zeros, k_bf[a + 1][rs]], axis=1)], axis=0)
            s_pair = lax.dot_general(q_pair, k_pair, NT_DIMS, preferred_element_type=F32)
            scores += [s_pair[:, :CHUNK], s_pair[:, CHUNK:]]
        kv = [lax.dot_general((k_rot[h][rs] * kd_ref[h]).astype(BF16), v_c[h], TN_DIMS,
                              preferred_element_type=F32) for h in heads]
        gate_slice(2 * c)
        lhs = [jnp.concatenate([(scores[h] * dec_ref[h]).astype(BF16),
                                (q_rot[h][rs] * qd_ref[h]).astype(BF16)], axis=1)
               for h in heads]
        rhs = [jnp.concatenate([v_c[h], state[h].astype(BF16)], axis=0) for h in heads]
        out = [_dot(lhs[h], rhs[h]) for h in heads]
        gate_slice(2 * c + 1)
        for h in heads:
            s_ref[h] = GAMMA_CHUNK[h] * state[h] + kv[h]
            y_ref[rs, h * DV:(h + 1) * DV] = (g_act[h][rs] * _group_norm(out[h])).astype(BF16)
    bout_ref[...] = _dot(y_ref[...], wro_ref[...])

    @pl.when(jnp.logical_and(i == tiles_per_seq - 1, g < n_tiles))
    def _emit_sequence_state():
        seq = jnp.minimum(g, n_tiles - 1) // tiles_per_seq
        for j in range(HALO):
            nc_ref[j, pl.ds(seq, 1), :] = fb_ref[0, HALO_PAD - HALO + j:HALO_PAD - HALO + j + 1, :]
        sout_ref[0] = s_ref[...]


def _mixer_prompt(x, cos2, sin2, npre, npost, win, wdw, bdw, lnw, lnb, wco, wro, wo):
    batch, seq, _ = x.shape
    rows = MIX_ROWS
    nt = seq // rows
    n_tiles = batch * nt

    def front(g):
        return jnp.minimum(g, n_tiles - 1)

    def back(g):
        return jnp.maximum(g - 1, 0)

    in_specs = [
        pl.BlockSpec((1, rows, D_MODEL), lambda g: (front(g) // nt, front(g) % nt, 0)),
        pl.BlockSpec((rows, DK), lambda g: (front(g) % nt, 0)),
        pl.BlockSpec((rows, DK), lambda g: (front(g) % nt, 0)),
        _const_spec((1, D_MODEL)),
        _const_spec((1, D_MODEL)),
        _const_spec((D_MODEL, IN_COLS)),
        _const_spec((CONV_WIDTH, 1, CONV_DIM)),
        _const_spec((1, CONV_DIM)),
        _const_spec((1, CONV_DIM)),
        _const_spec((1, CONV_DIM)),
        _const_spec((CONV_DIM, D_MODEL)),
        _const_spec((V_W, D_MODEL)),
        _const_spec((D_MODEL, D_MODEL)),
    ]
    out_specs = [
        pl.BlockSpec((1, rows, D_MODEL), lambda g: (back(g) // nt, back(g) % nt, 0)),
        pl.BlockSpec((HALO, batch, CONV_DIM), lambda g: (0, 0, 0)),
        pl.BlockSpec((1, HEADS, DK, DV), lambda g: (front(g) // nt, 0, 0, 0)),
    ]
    out_shape = [
        jax.ShapeDtypeStruct((batch, seq, D_MODEL), F32),
        jax.ShapeDtypeStruct((HALO, batch, CONV_DIM), F32),
        jax.ShapeDtypeStruct((batch, HEADS, DK, DV), F32),
    ]
    scratch = [
        pltpu.VMEM((SUBLANES, HALO_PAD + CONV_TILE, CONV_DIM), F32),
        pltpu.VMEM((rows, CONV_DIM), F32),
        pltpu.VMEM((rows, V_W), BF16),
        pltpu.VMEM((HEADS, CHUNK, CHUNK), F32),
        pltpu.VMEM((HEADS, CHUNK, DK), F32),
        pltpu.VMEM((HEADS, CHUNK, DK), F32),
        pltpu.VMEM((CONV_WIDTH, SUBLANES, CONV_DIM), F32),
        pltpu.VMEM((HEADS, DK, DV), F32),
        pltpu.VMEM((rows, CONV_DIM), BF16),
        pltpu.VMEM((rows, 2 * D_MODEL), F32),
        pltpu.VMEM((rows, D_MODEL), F32),
        pltpu.VMEM((rows, D_MODEL), F32),
    ]
    return pl.pallas_call(
        functools.partial(_mixer_prompt_kernel, tiles_per_seq=nt, n_tiles=n_tiles),
        grid=(n_tiles + 1,), in_specs=in_specs, out_specs=out_specs, out_shape=out_shape,
        scratch_shapes=scratch,
        compiler_params=pltpu.CompilerParams(
            dimension_semantics=("arbitrary",),
            vmem_limit_bytes=VMEM_LIMIT_BYTES),
        name="mixer_prompt",
    )(x, cos2, sin2, npre, npost, win, wdw, bdw, lnw, lnb, wco, wro, wo)


def _ffn_body(h, npre, npost, wup_ref, wdn_ref):
    hn = _rms(h, npre).astype(BF16)
    f = None
    for c in range(FFN_DIM // FFN_CHUNK):
        cols = slice(c * FFN_CHUNK, (c + 1) * FFN_CHUNK)
        up = _dot(hn, wup_ref[:, cols])
        act = jnp.square(jnp.maximum(up, 0.0)).astype(BF16)
        part = _dot(act, wdn_ref[cols, :])
        f = part if f is None else f + part
    return h + _rms(f, npost)


def _sample_state_body(blk, cache_ref, u_ref, wdw_ref, bdw_ref, qt_ref, kt_ref, v_ref, st_ref,
                       conv_ref, nc_ref, o_ref, so_ref):
    n_seq = qt_ref.shape[1]
    u = u_ref[...]
    acc = u * wdw_ref[HALO] + bdw_ref[...]
    for j in range(HALO):
        acc = acc + cache_ref[j] * wdw_ref[j]
    conv_ref[...] = acc
    nc_ref[0:HALO - 1] = cache_ref[1:HALO]
    nc_ref[HALO - 1] = u

    shift = (n_seq - blk * SAMPLE_BLOCK) % n_seq
    qt = pltpu.roll(qt_ref[...], shift, 1)
    kt = pltpu.roll(kt_ref[...], shift, 1)
    for s in range(SAMPLE_BLOCK):
        for h in range(HEADS):
            k_col = kt[h * DK:(h + 1) * DK, s:s + 1]
            q_col = qt[h * DK:(h + 1) * DK, s:s + 1]
            v_row = v_ref[s:s + 1, h * DV:(h + 1) * DV]
            new_state = GAMMA[h] * st_ref[s, h] + k_col * v_row
            so_ref[s, h] = new_state
            o_ref[s:s + 1, h * DV:(h + 1) * DV] = jnp.sum(new_state * q_col, axis=0,
                                                          keepdims=True)


def _ffn_state_kernel(h_ref, npre_ref, npost_ref, wup_ref, wdn_ref,
                      cache_ref, u_ref, wdw_ref, bdw_ref, qt_ref, kt_ref, v_ref, st_ref,
                      y_ref, conv_ref, nc_ref, o_ref, so_ref, *, steps_per_block):
    y_ref[...] = _ffn_body(h_ref[...], npre_ref[...], npost_ref[...], wup_ref, wdn_ref)
    _sample_state_body(pl.program_id(0) // steps_per_block, cache_ref, u_ref, wdw_ref, bdw_ref,
                       qt_ref, kt_ref, v_ref, st_ref, conv_ref, nc_ref, o_ref, so_ref)


def _ffn_state(h, npre, npost, wup, wdn, cache, u, wdw, bdw, qt, kt, v, state):
    n = h.shape[0]
    rows = FFN_ROWS
    steps = n // rows
    n_seq = u.shape[0]
    sb = SAMPLE_BLOCK
    spb = steps // (n_seq // sb)
    assert spb * (n_seq // sb) == steps
    in_specs = [
        pl.BlockSpec((rows, D_MODEL), lambda i: (i, 0)),
        _const_spec((1, D_MODEL)),
        _const_spec((1, D_MODEL)),
        _const_spec((D_MODEL, FFN_DIM)),
        _const_spec((FFN_DIM, D_MODEL)),
        pl.BlockSpec((HALO, sb, CONV_DIM), lambda i: (0, i // spb, 0)),
        pl.BlockSpec((sb, CONV_DIM), lambda i: (i // spb, 0)),
        _const_spec((CONV_WIDTH, 1, CONV_DIM)),
        _const_spec((1, CONV_DIM)),
        _const_spec((QK_W, n_seq)),
        _const_spec((QK_W, n_seq)),
        pl.BlockSpec((sb, V_W), lambda i: (i // spb, 0)),
        pl.BlockSpec((sb, HEADS, DK, DV), lambda i: (i // spb, 0, 0, 0)),
    ]
    out_specs = [
        pl.BlockSpec((rows, D_MODEL), lambda i: (i, 0)),
        pl.BlockSpec((sb, CONV_DIM), lambda i: (i // spb, 0)),
        pl.BlockSpec((HALO, sb, CONV_DIM), lambda i: (0, i // spb, 0)),
        pl.BlockSpec((sb, V_W), lambda i: (i // spb, 0)),
        pl.BlockSpec((sb, HEADS, DK, DV), lambda i: (i // spb, 0, 0, 0)),
    ]
    out_shape = [
        jax.ShapeDtypeStruct((n, D_MODEL), F32),
        jax.ShapeDtypeStruct((n_seq, CONV_DIM), F32),
        jax.ShapeDtypeStruct((HALO, n_seq, CONV_DIM), F32),
        jax.ShapeDtypeStruct((n_seq, V_W), F32),
        jax.ShapeDtypeStruct((n_seq, HEADS, DK, DV), F32),
    ]
    return pl.pallas_call(
        functools.partial(_ffn_state_kernel, steps_per_block=spb),
        grid=(steps,), in_specs=in_specs, out_specs=out_specs, out_shape=out_shape,
        compiler_params=pltpu.CompilerParams(
            dimension_semantics=("arbitrary",),
            vmem_limit_bytes=VMEM_LIMIT_BYTES),
        name="ffn_prompt_sample_state",
    )(h, npre, npost, wup, wdn, cache, u, wdw, bdw, qt, kt, v, state)


def _sample_pre_kernel(x_ref, cos_ref, sin_ref, npre_ref, win_ref,
                       wbf_ref, u_ref, qt_ref, kt_ref, v_ref, g_ref, ga_ref, gb_ref, xn_ref):
    j = pl.program_id(0)

    @pl.when(j == 0)
    def _normalise():
        xn_ref[...] = _rms(x_ref[:, 0, :], npre_ref[...]).astype(BF16)

    w_slab = win_ref[...].astype(BF16)
    wbf_ref[...] = w_slab
    p = _dot(xn_ref[...], w_slab)

    @pl.when(j == C_CONV // PRE_COLS)
    def _glu():
        u_ref[...] = p[:, :CONV_DIM] * jax.nn.sigmoid(p[:, CONV_DIM:])

    @pl.when(j == C_Q // PRE_COLS)
    def _rotary():
        qkt = p.T
        cos_t = cos_ref[...]
        sin_t = sin_ref[...]
        half = DK // 2
        for g in range(2 * HEADS):
            x1 = qkt[g * DK:g * DK + half]
            x2 = qkt[g * DK + half:(g + 1) * DK]
            o1 = x1 * cos_t - x2 * sin_t
            o2 = x2 * cos_t + x1 * sin_t
            if g < HEADS:
                qt_ref[g * DK:g * DK + half, :] = o1
                qt_ref[g * DK + half:(g + 1) * DK, :] = o2
            else:
                k0 = (g - HEADS) * DK
                kt_ref[k0:k0 + half, :] = o1 * K_SCALE
                kt_ref[k0 + half:k0 + DK, :] = o2 * K_SCALE

    for out_ref, col in ((v_ref, C_V), (g_ref, C_G), (ga_ref, C_GATES),
                         (gb_ref, C_GATES + D_MODEL)):
        @pl.when(j == col // PRE_COLS)
        def _store(out_ref=out_ref):
            out_ref[...] = p


def _sample_pre(x, cos_t, sin_t, npre, win_f32):
    n = x.shape[0]
    assert all(c % PRE_COLS == 0 for c in (C_CONV, C_Q, C_V, C_G, C_GATES, IN_COLS))
    assert 2 * CONV_DIM == 2 * QK_W == V_W == D_MODEL == PRE_COLS

    def whole(shape):
        zeros = (0,) * len(shape)
        return pl.BlockSpec(shape, lambda j: zeros)

    in_specs = [
        whole((n, 1, D_MODEL)),
        whole((DK // 2, n)),
        whole((DK // 2, n)),
        whole((1, D_MODEL)),
        pl.BlockSpec((D_MODEL, PRE_COLS), lambda j: (0, j)),
    ]
    out_specs = [
        pl.BlockSpec((D_MODEL, PRE_COLS), lambda j: (0, j)),
        whole((n, CONV_DIM)),
        whole((QK_W, n)),
        whole((QK_W, n)),
        whole((n, V_W)),
        whole((n, V_W)),
        whole((n, D_MODEL)),
        whole((n, D_MODEL)),
    ]
    out_shape = [
        jax.ShapeDtypeStruct((D_MODEL, IN_COLS), BF16),
        jax.ShapeDtypeStruct((n, CONV_DIM), F32),
        jax.ShapeDtypeStruct((QK_W, n), F32),
        jax.ShapeDtypeStruct((QK_W, n), F32),
        jax.ShapeDtypeStruct((n, V_W), F32),
        jax.ShapeDtypeStruct((n, V_W), F32),
        jax.ShapeDtypeStruct((n, D_MODEL), F32),
        jax.ShapeDtypeStruct((n, D_MODEL), F32),
    ]
    return pl.pallas_call(
        _sample_pre_kernel,
        grid=(IN_COLS // PRE_COLS,), in_specs=in_specs, out_specs=out_specs, out_shape=out_shape,
        scratch_shapes=[pltpu.VMEM((n, D_MODEL), BF16)],
        compiler_params=pltpu.CompilerParams(
            dimension_semantics=("arbitrary",),
            vmem_limit_bytes=VMEM_LIMIT_BYTES),
        name="sample_pre_cast_w_in",
    )(x, cos_t, sin_t, npre, win_f32)


def _sample_post_kernel(x_ref, conv_ref, o_ref, g_ref, ga_ref, gb_ref, lnw_ref, lnb_ref,
                        wco_ref, wro_ref, wo_ref, npost_ref, nfpre_ref, nfpost_ref,
                        wup_ref, wdn_ref, y_ref, hres_ref, hn_ref, f_ref):
    c = pl.program_id(0)

    @pl.when(c == 0)
    def _merge():
        a_act = _silu(_layer_norm(conv_ref[...], lnw_ref[...], lnb_ref[...])).astype(BF16)
        a_out = _dot(a_act, wco_ref[...])
        b_out = jnp.zeros_like(a_out)
        for h in range(HEADS):
            cols = slice(h * DV, (h + 1) * DV)
            yh = (_silu(g_ref[:, cols]) * _group_norm(o_ref[:, cols])).astype(BF16)
            b_out = b_out + _dot(yh, wro_ref[cols, :])
        merged = jax.nn.sigmoid(ga_ref[...]) * a_out + jax.nn.sigmoid(gb_ref[...]) * b_out
        m = _dot(merged.astype(BF16), wo_ref[...])
        hres = x_ref[:, 0, :] + _rms(m, npost_ref[...])
        hres_ref[...] = hres
        hn_ref[...] = _rms(hres, nfpre_ref[...]).astype(BF16)
        f_ref[...] = jnp.zeros(f_ref.shape, F32)

    up = _dot(hn_ref[...], wup_ref[...])
    act = jnp.square(jnp.maximum(up, 0.0)).astype(BF16)
    f_ref[...] += _dot(act, wdn_ref[...])

    @pl.when(c == pl.num_programs(0) - 1)
    def _finish():
        y_ref[:, 0, :] = hres_ref[...] + _rms(f_ref[...], nfpost_ref[...])


def _sample_post(x, conv, o, g, ga, gb, lnw, lnb, wco, wro, wo, npost, nfpre, nfpost, wup, wdn):
    n = x.shape[0]

    def whole(a):
        zeros = (0,) * a.ndim
        return pl.BlockSpec(a.shape, lambda c: zeros)

    resident = (x, conv, o, g, ga, gb, lnw, lnb, wco, wro, wo, npost, nfpre, nfpost)
    in_specs = [whole(a) for a in resident] + [
        pl.BlockSpec((D_MODEL, FFN_CHUNK), lambda c: (0, c)),
        pl.BlockSpec((FFN_CHUNK, D_MODEL), lambda c: (c, 0)),
    ]
    return pl.pallas_call(
        _sample_post_kernel,
        grid=(FFN_DIM // FFN_CHUNK,), in_specs=in_specs,
        out_specs=pl.BlockSpec(x.shape, lambda c: (0, 0, 0)),
        out_shape=jax.ShapeDtypeStruct(x.shape, F32),
        scratch_shapes=[
            pltpu.VMEM((n, D_MODEL), F32),
            pltpu.VMEM((n, D_MODEL), BF16),
            pltpu.VMEM((n, D_MODEL), F32),
        ],
        compiler_params=pltpu.CompilerParams(
            dimension_semantics=("arbitrary",),
            vmem_limit_bytes=VMEM_LIMIT_BYTES),
        name="sample_post",
    )(x, conv, o, g, ga, gb, lnw, lnb, wco, wro, wo, npost, nfpre, nfpost, wup, wdn)


def _rope_angles(pos):
    half = DK // 2
    freqs = 1.0 / (ROPE_BASE ** jnp.linspace(0.0, 1.0, half, dtype=F32))
    return pos[:, None] * freqs[None, :]


def kernel(x_prompt, x_sample, cache_conv, state_ret, norm_mix_pre, norm_mix_post, w_in, w_dw, b_dw, conv_ln_w, conv_ln_b, w_conv_out, w_ret_out, w_o, norm_ffn_pre, norm_ffn_post, w_ffn_up, w_ffn_down):
    batch, seq, _ = x_prompt.shape
    n_seq, dec_seq, _ = x_sample.shape
    depth = w_in.shape[0]
    assert dec_seq == 1 and seq % MIX_ROWS == 0 and (batch * seq) % FFN_ROWS == 0
    assert n_seq % SAMPLE_BLOCK == 0

    ang_p = _rope_angles(jnp.arange(seq, dtype=F32))
    cos_p = jnp.concatenate([jnp.cos(ang_p), jnp.cos(ang_p)], axis=1)
    sin_p = jnp.concatenate([-jnp.sin(ang_p), jnp.sin(ang_p)], axis=1)
    ang_s = _rope_angles(PAST_LEN + jnp.arange(dec_seq, dtype=F32))
    cos_s = jnp.broadcast_to(jnp.cos(ang_s).T, (DK // 2, n_seq))
    sin_s = jnp.broadcast_to(jnp.sin(ang_s).T, (DK // 2, n_seq))

    xp = x_prompt
    xs = x_sample
    conv_p, ret_p, conv_s, ret_s = [], [], [], []
    for l in range(depth):
        npre = norm_mix_pre[l][None]
        npost = norm_mix_post[l][None]
        nfpre = norm_ffn_pre[l][None]
        nfpost = norm_ffn_post[l][None]
        wdw = jnp.transpose(w_dw, (1, 0, 2))[:, l:l + 1, :]
        bdw = b_dw[l][None]
        lnw = conv_ln_w[l][None]
        lnb = conv_ln_b[l][None]
        wco = w_conv_out[l].astype(BF16)
        wro = w_ret_out[l].astype(BF16)
        wo = w_o[l].astype(BF16)
        wup = w_ffn_up[l].astype(BF16)
        wdn = w_ffn_down[l].astype(BF16)

        win, u, qt, kt, v, g, ga, gb = _sample_pre(xs, cos_s, sin_s, npre, w_in[l])
        cache_t = jnp.transpose(cache_conv[l], (1, 0, 2))
        h_p, nc_p, s_p = _mixer_prompt(xp, cos_p, sin_p, npre, npost, win, wdw, bdw, lnw, lnb,
                                       wco, wro, wo)
        y_p, conv, nc_t, o, s_s = _ffn_state(h_p.reshape(batch * seq, D_MODEL), nfpre, nfpost,
                                             wup, wdn, cache_t, u, wdw, bdw, qt, kt, v,
                                             state_ret[l])
        xp = y_p.reshape(batch, seq, D_MODEL)
        nc_s = jnp.transpose(nc_t, (1, 0, 2))
        xs = _sample_post(xs, conv, o, g, ga, gb, lnw, lnb, wco, wro, wo, npost, nfpre, nfpost,
                          wup, wdn)
        conv_p.append(jnp.transpose(nc_p, (1, 0, 2)))
        ret_p.append(s_p)
        conv_s.append(nc_s)
        ret_s.append(s_s)

    return (xp, xs, jnp.stack(conv_p), jnp.stack(ret_p),
            jnp.stack(conv_s), jnp.stack(ret_s))
```

```python
import functools
import math

import jax
import jax.numpy as jnp
from jax import lax
from jax.experimental import pallas as pl
from jax.experimental.pallas import tpu as pltpu

F32 = jnp.float32
BF16 = jnp.bfloat16

D_MODEL = 1024
CONV_DIM = 512
CONV_WIDTH = 31
HALO = CONV_WIDTH - 1
HEADS = 4
DK = 128
DV = 256
QK_W = HEADS * DK
V_W = HEADS * DV
CHUNK = 128
FFN_DIM = 4 * D_MODEL
EPS = 1e-6
ROPE_BASE = 10000.0
PAST_LEN = 16384
K_SCALE = DK ** -0.5

C_CONV = 0
C_Q = 2 * CONV_DIM
C_K = C_Q + QK_W
C_V = C_K + QK_W
C_G = C_V + V_W
C_GATES = C_G + V_W
IN_COLS = C_GATES + 2 * D_MODEL

LOG_GAMMA = tuple(math.log1p(-(2.0 ** (-5 - h))) for h in range(HEADS))
GAMMA = tuple(math.exp(lg) for lg in LOG_GAMMA)
GAMMA_CHUNK = tuple(math.exp(CHUNK * lg) for lg in LOG_GAMMA)

VMEM_LIMIT_BYTES = 60 * 1024 * 1024
SUBLANES = 8
HALO_PAD = 32
MIX_ROWS = 512
FFN_ROWS = 1024
FFN_CHUNK = 1024
CONV_TILE = 256
CONV_ROWS = 32
SAMPLE_BLOCK = 8
PRE_COLS = 1024

NT_DIMS = (((1,), (1,)), ((), ()))
TN_DIMS = (((0,), (0,)), ((), ()))


def _dot(a, b):
    return jnp.dot(a, b, preferred_element_type=F32)


def _rms(x, w):
    return x * lax.rsqrt(jnp.mean(x * x, axis=-1, keepdims=True) + EPS) * w


def _layer_norm(x, w, b):
    mu = jnp.mean(x, axis=-1, keepdims=True)
    xc = x - mu
    return xc * lax.rsqrt(jnp.mean(xc * xc, axis=-1, keepdims=True) + EPS) * w + b


def _sigmoid(x):
    return 0.5 * jnp.tanh(0.5 * x) + 0.5


def _silu(x):
    return x * _sigmoid(x)


def _group_norm(o):
    return o * lax.rsqrt(jnp.mean(o * o, axis=-1, keepdims=True) + EPS)


def _const_spec(shape):
    zeros = (0,) * len(shape)
    return pl.BlockSpec(shape, lambda *_: zeros, pipeline_mode=pl.Buffered(1))


def _mixer_prompt_kernel(x_ref, cos_ref, sin_ref, npre_ref, npost_ref, win_ref, wdw_ref,
                         bdw_ref, lnw_ref, lnb_ref, wco_ref, wro_ref, wo_ref,
                         h_ref, nc_ref, sout_ref,
                         fb_ref, u_ref, y_ref, dec_ref, qd_ref, kd_ref, wb_ref, s_ref,
                         aact_ref, gate_ref, bout_ref, xprev_ref, *, tiles_per_seq, n_tiles):
    g = pl.program_id(0)
    i = jnp.minimum(g, n_tiles - 1) % tiles_per_seq
    rows = x_ref.shape[1]

    @pl.when(g == 0)
    def _init_tables():
        aact_ref[...] = jnp.zeros(aact_ref.shape, BF16)
        gate_ref[...] = jnp.zeros(gate_ref.shape, F32)
        bout_ref[...] = jnp.zeros(bout_ref.shape, F32)
        xprev_ref[...] = jnp.zeros(xprev_ref.shape, F32)
        ii = lax.broadcasted_iota(jnp.int32, (CHUNK, CHUNK), 0)
        jj = lax.broadcasted_iota(jnp.int32, (CHUNK, CHUNK), 1)
        diff = (ii - jj).astype(F32)
        row_k = lax.broadcasted_iota(jnp.int32, (CHUNK, DK), 0).astype(F32)
        for h in range(HEADS):
            lg = LOG_GAMMA[h]
            dec_ref[h] = jnp.where(diff >= 0.0, jnp.exp(jnp.maximum(diff, 0.0) * lg), 0.0)
            qd_ref[h] = jnp.exp((row_k + 1.0) * lg)
            kd_ref[h] = jnp.exp((CHUNK - 1.0 - row_k) * lg)
        for j in range(CONV_WIDTH):
            wb_ref[j] = jnp.broadcast_to(wdw_ref[j], (SUBLANES, CONV_DIM))

    @pl.when(i == 0)
    def _start_sequence():
        fb_ref[0, 0:HALO_PAD, :] = jnp.zeros((HALO_PAD, CONV_DIM), F32)
        s_ref[...] = jnp.zeros(s_ref.shape, F32)

    a_out = _dot(aact_ref[...], wco_ref[...])
    x = x_ref[0]
    xn = _rms(x, npre_ref[...]).astype(BF16)
    pc = _dot(xn, win_ref[:, C_CONV:C_CONV + 2 * CONV_DIM])
    merged = (_sigmoid(gate_ref[:, :D_MODEL]) * a_out
              + _sigmoid(gate_ref[:, D_MODEL:]) * bout_ref[...])
    m = _dot(merged.astype(BF16), wo_ref[...])
    h_ref[0] = xprev_ref[...] + _rms(m, npost_ref[...])

    xprev_ref[...] = x

    u_ref[...] = pc[:, :CONV_DIM] * _sigmoid(pc[:, CONV_DIM:])

    pqk = _dot(xn, win_ref[:, C_Q:C_Q + 2 * QK_W])
    pv = _dot(xn, win_ref[:, C_V:C_V + V_W]).astype(BF16)
    pg = _dot(xn, win_ref[:, C_G:C_G + V_W])

    first = HALO_PAD - HALO
    span = CONV_TILE + HALO_PAD - SUBLANES
    for t in range(rows // CONV_TILE):
        fb_ref[0, HALO_PAD:HALO_PAD + CONV_TILE, :] = u_ref[t * CONV_TILE:(t + 1) * CONV_TILE, :]
        for r in range(1, SUBLANES):
            fb_ref[r, 0:span, :] = fb_ref[0, r:r + span, :]
        for rb in range(CONV_TILE // CONV_ROWS):
            acc = jnp.broadcast_to(bdw_ref[...], (CONV_ROWS, CONV_DIM))
            for j in range(CONV_WIDTH):
                off = first + j
                base = rb * CONV_ROWS + off - off % SUBLANES
                slab = fb_ref[off % SUBLANES, base:base + CONV_ROWS, :]
                acc = acc + (slab.reshape(CONV_ROWS // SUBLANES, SUBLANES, CONV_DIM)
                             * wb_ref[j][None]).reshape(CONV_ROWS, CONV_DIM)
            r0 = t * CONV_TILE + rb * CONV_ROWS
            aact_ref[r0:r0 + CONV_ROWS, :] = _silu(
                _layer_norm(acc, lnw_ref[...], lnb_ref[...])).astype(BF16)
        fb_ref[0, 0:HALO_PAD, :] = fb_ref[0, CONV_TILE:CONV_TILE + HALO_PAD, :]

    cos2 = cos_ref[...]
    sin2 = sin_ref[...]

    def rot(t):
        return t * cos2 + pltpu.roll(t, DK // 2, 1) * sin2

    heads = range(HEADS)
    q_rot = [rot(pqk[:, h * DK:(h + 1) * DK]) for h in heads]
    q_bf = [q.astype(BF16) for q in q_rot]
    k_rot = [rot(pqk[:, QK_W + h * DK:QK_W + (h + 1) * DK]) * K_SCALE for h in heads]
    k_bf = [k.astype(BF16) for k in k_rot]
    g_act = [_silu(pg[:, h * DV:(h + 1) * DV]) for h in heads]

    n_chunks = rows // CHUNK
    n_fill = 2 * n_chunks
    gate_cols = 2 * D_MODEL // n_fill

    def gate_slice(f):
        gate_ref[:, f * gate_cols:(f + 1) * gate_cols] = _dot(
            xn, win_ref[:, C_GATES + f * gate_cols:C_GATES + (f + 1) * gate_cols])

    for c in range(n_chunks):
        rs = slice(c * CHUNK, (c + 1) * CHUNK)
        v_c = [pv[rs, h * DV:(h + 1) * DV] for h in heads]
        state = [s_ref[h] for h in heads]
        scores = []
        zeros = jnp.zeros((CHUNK, DK), BF16)
        for a in range(0, HEADS, 2):
            q_pair = jnp.concatenate([q_bf[a][rs], q_bf[a + 1][rs]], axis=1)
            k_pair = jnp.concatenate(
                [jnp.concatenate([k_bf[a][rs], zeros], axis=1),
                 jnp.concatenate([zeros, k_bf[a + 1][rs]], axis=1)], axis=0)
            s_pair = lax.dot_general(q_pair, k_pair, NT_DIMS, preferred_element_type=F32)
            scores += [s_pair[:, :CHUNK], s_pair[:, CHUNK:]]
        kv = [lax.dot_general((k_rot[h][rs] * kd_ref[h]).astype(BF16), v_c[h], TN_DIMS,
                              preferred_element_type=F32) for h in heads]
        gate_slice(2 * c)
        lhs = [jnp.concatenate([(scores[h] * dec_ref[h]).astype(BF16),
                                (q_rot[h][rs] * qd_ref[h]).astype(BF16)], axis=1)
               for h in heads]
        rhs = [jnp.concatenate([v_c[h], state[h].astype(BF16)], axis=0) for h in heads]
        out = [_dot(lhs[h], rhs[h]) for h in heads]
        gate_slice(2 * c + 1)
        for h in heads:
            s_ref[h] = GAMMA_CHUNK[h] * state[h] + kv[h]
            y_ref[rs, h * DV:(h + 1) * DV] = (g_act[h][rs] * _group_norm(out[h])).astype(BF16)
    bout_ref[...] = _dot(y_ref[...], wro_ref[...])

    @pl.when(jnp.logical_and(i == tiles_per_seq - 1, g < n_tiles))
    def _emit_sequence_state():
        seq = jnp.minimum(g, n_tiles - 1) // tiles_per_seq
        for j in range(HALO):
            nc_ref[j, pl.ds(seq, 1), :] = fb_ref[0, HALO_PAD - HALO + j:HALO_PAD - HALO + j + 1, :]
        sout_ref[0] = s_ref[...]


def _mixer_prompt(x, cos2, sin2, npre, npost, win, wdw, bdw, lnw, lnb, wco, wro, wo):
    batch, seq, _ = x.shape
    rows = MIX_ROWS
    nt = seq // rows
    n_tiles = batch * nt

    def front(g):
        return jnp.minimum(g, n_tiles - 1)

    def back(g):
        return jnp.maximum(g - 1, 0)

    in_specs = [
        pl.BlockSpec((1, rows, D_MODEL), lambda g: (front(g) // nt, front(g) % nt, 0)),
        pl.BlockSpec((rows, DK), lambda g: (front(g) % nt, 0)),
        pl.BlockSpec((rows, DK), lambda g: (front(g) % nt, 0)),
        _const_spec((1, D_MODEL)),
        _const_spec((1, D_MODEL)),
        _const_spec((D_MODEL, IN_COLS)),
        _const_spec((CONV_WIDTH, 1, CONV_DIM)),
        _const_spec((1, CONV_DIM)),
        _const_spec((1, CONV_DIM)),
        _const_spec((1, CONV_DIM)),
        _const_spec((CONV_DIM, D_MODEL)),
        _const_spec((V_W, D_MODEL)),
        _const_spec((D_MODEL, D_MODEL)),
    ]
    out_specs = [
        pl.BlockSpec((1, rows, D_MODEL), lambda g: (back(g) // nt, back(g) % nt, 0)),
        pl.BlockSpec((HALO, batch, CONV_DIM), lambda g: (0, 0, 0)),
        pl.BlockSpec((1, HEADS, DK, DV), lambda g: (front(g) // nt, 0, 0, 0)),
    ]
    out_shape = [
        jax.ShapeDtypeStruct((batch, seq, D_MODEL), F32),
        jax.ShapeDtypeStruct((HALO, batch, CONV_DIM), F32),
        jax.ShapeDtypeStruct((batch, HEADS, DK, DV), F32),
    ]
    scratch = [
        pltpu.VMEM((SUBLANES, HALO_PAD + CONV_TILE, CONV_DIM), F32),
        pltpu.VMEM((rows, CONV_DIM), F32),
        pltpu.VMEM((rows, V_W), BF16),
        pltpu.VMEM((HEADS, CHUNK, CHUNK), F32),
        pltpu.VMEM((HEADS, CHUNK, DK), F32),
        pltpu.VMEM((HEADS, CHUNK, DK), F32),
        pltpu.VMEM((CONV_WIDTH, SUBLANES, CONV_DIM), F32),
        pltpu.VMEM((HEADS, DK, DV), F32),
        pltpu.VMEM((rows, CONV_DIM), BF16),
        pltpu.VMEM((rows, 2 * D_MODEL), F32),
        pltpu.VMEM((rows, D_MODEL), F32),
        pltpu.VMEM((rows, D_MODEL), F32),
    ]
    return pl.pallas_call(
        functools.partial(_mixer_prompt_kernel, tiles_per_seq=nt, n_tiles=n_tiles),
        grid=(n_tiles + 1,), in_specs=in_specs, out_specs=out_specs, out_shape=out_shape,
        scratch_shapes=scratch,
        compiler_params=pltpu.CompilerParams(
            dimension_semantics=("arbitrary",),
            vmem_limit_bytes=VMEM_LIMIT_BYTES),
        name="mixer_prompt",
    )(x, cos2, sin2, npre, npost, win, wdw, bdw, lnw, lnb, wco, wro, wo)


def _ffn_body(h, npre, npost, wup_ref, wdn_ref):
    hn = _rms(h, npre).astype(BF16)
    f = None
    for c in range(FFN_DIM // FFN_CHUNK):
        cols = slice(c * FFN_CHUNK, (c + 1) * FFN_CHUNK)
        up = _dot(hn, wup_ref[:, cols])
        act = jnp.square(jnp.maximum(up, 0.0)).astype(BF16)
        part = _dot(act, wdn_ref[cols, :])
        f = part if f is None else f + part
    return h + _rms(f, npost)


def _sample_state_body(blk, cache_ref, u_ref, wdw_ref, bdw_ref, qt_ref, kt_ref, v_ref, st_ref,
                       conv_ref, nc_ref, o_ref, so_ref):
    n_seq = qt_ref.shape[1]
    u = u_ref[...]
    acc = u * wdw_ref[HALO] + bdw_ref[...]
    for j in range(HALO):
        acc = acc + cache_ref[j] * wdw_ref[j]
    conv_ref[...] = acc
    nc_ref[0:HALO - 1] = cache_ref[1:HALO]
    nc_ref[HALO - 1] = u

    shift = (n_seq - blk * SAMPLE_BLOCK) % n_seq
    qt = pltpu.roll(qt_ref[...], shift, 1)
    kt = pltpu.roll(kt_ref[...], shift, 1)
    for s in range(SAMPLE_BLOCK):
        for h in range(HEADS):
            k_col = kt[h * DK:(h + 1) * DK, s:s + 1]
            q_col = qt[h * DK:(h + 1) * DK, s:s + 1]
            v_row = v_ref[s:s + 1, h * DV:(h + 1) * DV]
            new_state = GAMMA[h] * st_ref[s, h] + k_col * v_row
            so_ref[s, h] = new_state
            o_ref[s:s + 1, h * DV:(h + 1) * DV] = jnp.sum(new_state * q_col, axis=0,
                                                          keepdims=True)


def _ffn_state_kernel(h_ref, npre_ref, npost_ref, wup_ref, wdn_ref,
                      cache_ref, u_ref, wdw_ref, bdw_ref, qt_ref, kt_ref, v_ref, st_ref,
                      y_ref, conv_ref, nc_ref, o_ref, so_ref, *, steps_per_block):
    y_ref[...] = _ffn_body(h_ref[...], npre_ref[...], npost_ref[...], wup_ref, wdn_ref)
    _sample_state_body(pl.program_id(0) // steps_per_block, cache_ref, u_ref, wdw_ref, bdw_ref,
                       qt_ref, kt_ref, v_ref, st_ref, conv_ref, nc_ref, o_ref, so_ref)


def _ffn_state(h, npre, npost, wup, wdn, cache, u, wdw, bdw, qt, kt, v, state):
    n = h.shape[0]
    rows = FFN_ROWS
    steps = n // rows
    n_seq = u.shape[0]
    sb = SAMPLE_BLOCK
    spb = steps // (n_seq // sb)
    assert spb * (n_seq // sb) == steps
    in_specs = [
        pl.BlockSpec((rows, D_MODEL), lambda i: (i, 0)),
        _const_spec((1, D_MODEL)),
        _const_spec((1, D_MODEL)),
        _const_spec((D_MODEL, FFN_DIM)),
        _const_spec((FFN_DIM, D_MODEL)),
        pl.BlockSpec((HALO, sb, CONV_DIM), lambda i: (0, i // spb, 0)),
        pl.BlockSpec((sb, CONV_DIM), lambda i: (i // spb, 0)),
        _const_spec((CONV_WIDTH, 1, CONV_DIM)),
        _const_spec((1, CONV_DIM)),
        _const_spec((QK_W, n_seq)),
        _const_spec((QK_W, n_seq)),
        pl.BlockSpec((sb, V_W), lambda i: (i // spb, 0)),
        pl.BlockSpec((sb, HEADS, DK, DV), lambda i: (i // spb, 0, 0, 0)),
    ]
    out_specs = [
        pl.BlockSpec((rows, D_MODEL), lambda i: (i, 0)),
        pl.BlockSpec((sb, CONV_DIM), lambda i: (i // spb, 0)),
        pl.BlockSpec((HALO, sb, CONV_DIM), lambda i: (0, i // spb, 0)),
        pl.BlockSpec((sb, V_W), lambda i: (i // spb, 0)),
        pl.BlockSpec((sb, HEADS, DK, DV), lambda i: (i // spb, 0, 0, 0)),
    ]
    out_shape = [
        jax.ShapeDtypeStruct((n, D_MODEL), F32),
        jax.ShapeDtypeStruct((n_seq, CONV_DIM), F32),
        jax.ShapeDtypeStruct((HALO, n_seq, CONV_DIM), F32),
        jax.ShapeDtypeStruct((n_seq, V_W), F32),
        jax.ShapeDtypeStruct((n_seq, HEADS, DK, DV), F32),
    ]
    return pl.pallas_call(
        functools.partial(_ffn_state_kernel, steps_per_block=spb),
        grid=(steps,), in_specs=in_specs, out_specs=out_specs, out_shape=out_shape,
        compiler_params=pltpu.CompilerParams(
            dimension_semantics=("arbitrary",),
            vmem_limit_bytes=VMEM_LIMIT_BYTES),
        name="ffn_prompt_sample_state",
    )(h, npre, npost, wup, wdn, cache, u, wdw, bdw, qt, kt, v, state)


def _sample_pre_kernel(x_ref, cos_ref, sin_ref, npre_ref, win_ref,
                       wbf_ref, u_ref, qt_ref, kt_ref, v_ref, g_ref, ga_ref, gb_ref, xn_ref):
    j = pl.program_id(0)

    @pl.when(j == 0)
    def _normalise():
        xn_ref[...] = _rms(x_ref[:, 0, :], npre_ref[...]).astype(BF16)

    w_slab = win_ref[...].astype(BF16)
    wbf_ref[...] = w_slab
    p = _dot(xn_ref[...], w_slab)

    @pl.when(j == C_CONV // PRE_COLS)
    def _glu():
        u_ref[...] = p[:, :CONV_DIM] * _sigmoid(p[:, CONV_DIM:])

    @pl.when(j == C_Q // PRE_COLS)
    def _rotary():
        qkt = p.T
        cos_t = cos_ref[...]
        sin_t = sin_ref[...]
        half = DK // 2
        for g in range(2 * HEADS):
            x1 = qkt[g * DK:g * DK + half]
            x2 = qkt[g * DK + half:(g + 1) * DK]
            o1 = x1 * cos_t - x2 * sin_t
            o2 = x2 * cos_t + x1 * sin_t
            if g < HEADS:
                qt_ref[g * DK:g * DK + half, :] = o1
                qt_ref[g * DK + half:(g + 1) * DK, :] = o2
            else:
                k0 = (g - HEADS) * DK
                kt_ref[k0:k0 + half, :] = o1 * K_SCALE
                kt_ref[k0 + half:k0 + DK, :] = o2 * K_SCALE

    for out_ref, col in ((v_ref, C_V), (g_ref, C_G), (ga_ref, C_GATES),
                         (gb_ref, C_GATES + D_MODEL)):
        @pl.when(j == col // PRE_COLS)
        def _store(out_ref=out_ref):
            out_ref[...] = p


def _sample_pre(x, cos_t, sin_t, npre, win_f32):
    n = x.shape[0]
    assert all(c % PRE_COLS == 0 for c in (C_CONV, C_Q, C_V, C_G, C_GATES, IN_COLS))
    assert 2 * CONV_DIM == 2 * QK_W == V_W == D_MODEL == PRE_COLS

    def whole(shape):
        zeros = (0,) * len(shape)
        return pl.BlockSpec(shape, lambda j: zeros)

    in_specs = [
        whole((n, 1, D_MODEL)),
        whole((DK // 2, n)),
        whole((DK // 2, n)),
        whole((1, D_MODEL)),
        pl.BlockSpec((D_MODEL, PRE_COLS), lambda j: (0, j)),
    ]
    out_specs = [
        pl.BlockSpec((D_MODEL, PRE_COLS), lambda j: (0, j)),
        whole((n, CONV_DIM)),
        whole((QK_W, n)),
        whole((QK_W, n)),
        whole((n, V_W)),
        whole((n, V_W)),
        whole((n, D_MODEL)),
        whole((n, D_MODEL)),
    ]
    out_shape = [
        jax.ShapeDtypeStruct((D_MODEL, IN_COLS), BF16),
        jax.ShapeDtypeStruct((n, CONV_DIM), F32),
        jax.ShapeDtypeStruct((QK_W, n), F32),
        jax.ShapeDtypeStruct((QK_W, n), F32),
        jax.ShapeDtypeStruct((n, V_W), F32),
        jax.ShapeDtypeStruct((n, V_W), F32),
        jax.ShapeDtypeStruct((n, D_MODEL), F32),
        jax.ShapeDtypeStruct((n, D_MODEL), F32),
    ]
    return pl.pallas_call(
        _sample_pre_kernel,
        grid=(IN_COLS // PRE_COLS,), in_specs=in_specs, out_specs=out_specs, out_shape=out_shape,
        scratch_shapes=[pltpu.VMEM((n, D_MODEL), BF16)],
        compiler_params=pltpu.CompilerParams(
            dimension_semantics=("arbitrary",),
            vmem_limit_bytes=VMEM_LIMIT_BYTES),
        name="sample_pre_cast_w_in",
    )(x, cos_t, sin_t, npre, win_f32)


def _sample_post_kernel(x_ref, conv_ref, o_ref, g_ref, ga_ref, gb_ref, lnw_ref, lnb_ref,
                        wco_ref, wro_ref, wo_ref, npost_ref, nfpre_ref, nfpost_ref,
                        wup_ref, wdn_ref, y_ref, hres_ref, hn_ref, f_ref):
    c = pl.program_id(0)

    @pl.when(c == 0)
    def _merge():
        a_act = _silu(_layer_norm(conv_ref[...], lnw_ref[...], lnb_ref[...])).astype(BF16)
        a_out = _dot(a_act, wco_ref[...])
        b_out = jnp.zeros_like(a_out)
        for h in range(HEADS):
            cols = slice(h * DV, (h + 1) * DV)
            yh = (_silu(g_ref[:, cols]) * _group_norm(o_ref[:, cols])).astype(BF16)
            b_out = b_out + _dot(yh, wro_ref[cols, :])
        merged = _sigmoid(ga_ref[...]) * a_out + _sigmoid(gb_ref[...]) * b_out
        m = _dot(merged.astype(BF16), wo_ref[...])
        hres = x_ref[:, 0, :] + _rms(m, npost_ref[...])
        hres_ref[...] = hres
        hn_ref[...] = _rms(hres, nfpre_ref[...]).astype(BF16)
        f_ref[...] = jnp.zeros(f_ref.shape, F32)

    up = _dot(hn_ref[...], wup_ref[...])
    act = jnp.square(jnp.maximum(up, 0.0)).astype(BF16)
    f_ref[...] += _dot(act, wdn_ref[...])

    @pl.when(c == pl.num_programs(0) - 1)
    def _finish():
        y_ref[:, 0, :] = hres_ref[...] + _rms(f_ref[...], nfpost_ref[...])


def _sample_post(x, conv, o, g, ga, gb, lnw, lnb, wco, wro, wo, npost, nfpre, nfpost, wup, wdn):
    n = x.shape[0]

    def whole(a):
        zeros = (0,) * a.ndim
        return pl.BlockSpec(a.shape, lambda c: zeros)

    resident = (x, conv, o, g, ga, gb, lnw, lnb, wco, wro, wo, npost, nfpre, nfpost)
    in_specs = [whole(a) for a in resident] + [
        pl.BlockSpec((D_MODEL, FFN_CHUNK), lambda c: (0, c)),
        pl.BlockSpec((FFN_CHUNK, D_MODEL), lambda c: (c, 0)),
    ]
    return pl.pallas_call(
        _sample_post_kernel,
        grid=(FFN_DIM // FFN_CHUNK,), in_specs=in_specs,
        out_specs=pl.BlockSpec(x.shape, lambda c: (0, 0, 0)),
        out_shape=jax.ShapeDtypeStruct(x.shape, F32),
        scratch_shapes=[
            pltpu.VMEM((n, D_MODEL), F32),
            pltpu.VMEM((n, D_MODEL), BF16),
            pltpu.VMEM((n, D_MODEL), F32),
        ],
        compiler_params=pltpu.CompilerParams(
            dimension_semantics=("arbitrary",),
            vmem_limit_bytes=VMEM_LIMIT_BYTES),
        name="sample_post",
    )(x, conv, o, g, ga, gb, lnw, lnb, wco, wro, wo, npost, nfpre, nfpost, wup, wdn)


def _rope_angles(pos):
    half = DK // 2
    freqs = 1.0 / (ROPE_BASE ** jnp.linspace(0.0, 1.0, half, dtype=F32))
    return pos[:, None] * freqs[None, :]


def kernel(x_prompt, x_sample, cache_conv, state_ret, norm_mix_pre, norm_mix_post, w_in, w_dw, b_dw, conv_ln_w, conv_ln_b, w_conv_out, w_ret_out, w_o, norm_ffn_pre, norm_ffn_post, w_ffn_up, w_ffn_down):
    batch, seq, _ = x_prompt.shape
    n_seq, dec_seq, _ = x_sample.shape
    depth = w_in.shape[0]
    assert dec_seq == 1 and seq % MIX_ROWS == 0 and (batch * seq) % FFN_ROWS == 0
    assert n_seq % SAMPLE_BLOCK == 0

    ang_p = _rope_angles(jnp.arange(seq, dtype=F32))
    cos_p = jnp.concatenate([jnp.cos(ang_p), jnp.cos(ang_p)], axis=1)
    sin_p = jnp.concatenate([-jnp.sin(ang_p), jnp.sin(ang_p)], axis=1)
    ang_s = _rope_angles(PAST_LEN + jnp.arange(dec_seq, dtype=F32))
    cos_s = jnp.broadcast_to(jnp.cos(ang_s).T, (DK // 2, n_seq))
    sin_s = jnp.broadcast_to(jnp.sin(ang_s).T, (DK // 2, n_seq))

    xp = x_prompt
    xs = x_sample
    conv_p, ret_p, conv_s, ret_s = [], [], [], []
    for l in range(depth):
        npre = norm_mix_pre[l][None]
        npost = norm_mix_post[l][None]
        nfpre = norm_ffn_pre[l][None]
        nfpost = norm_ffn_post[l][None]
        wdw = jnp.transpose(w_dw, (1, 0, 2))[:, l:l + 1, :]
        bdw = b_dw[l][None]
        lnw = conv_ln_w[l][None]
        lnb = conv_ln_b[l][None]
        wco = w_conv_out[l].astype(BF16)
        wro = w_ret_out[l].astype(BF16)
        wo = w_o[l].astype(BF16)
        wup = w_ffn_up[l].astype(BF16)
        wdn = w_ffn_down[l].astype(BF16)

        win, u, qt, kt, v, g, ga, gb = _sample_pre(xs, cos_s, sin_s, npre, w_in[l])
        cache_t = jnp.transpose(cache_conv[l], (1, 0, 2))
        h_p, nc_p, s_p = _mixer_prompt(xp, cos_p, sin_p, npre, npost, win, wdw, bdw, lnw, lnb,
                                       wco, wro, wo)
        y_p, conv, nc_t, o, s_s = _ffn_state(h_p.reshape(batch * seq, D_MODEL), nfpre, nfpost,
                                             wup, wdn, cache_t, u, wdw, bdw, qt, kt, v,
                                             state_ret[l])
        xp = y_p.reshape(batch, seq, D_MODEL)
        nc_s = jnp.transpose(nc_t, (1, 0, 2))
        xs = _sample_post(xs, conv, o, g, ga, gb, lnw, lnb, wco, wro, wo, npost, nfpre, nfpost,
                          wup, wdn)
        conv_p.append(jnp.transpose(nc_p, (1, 0, 2)))
        ret_p.append(s_p)
        conv_s.append(nc_s)
        ret_s.append(s_s)

    return (xp, xs, jnp.stack(conv_p), jnp.stack(ret_p),
            jnp.stack(conv_s), jnp.stack(ret_s))
```

```python
import functools
import math

import jax
import jax.numpy as jnp
from jax import lax
from jax.experimental import pallas as pl
from jax.experimental.pallas import tpu as pltpu

F32 = jnp.float32
BF16 = jnp.bfloat16

D_MODEL = 1024
CONV_DIM = 512
CONV_WIDTH = 31
HALO = CONV_WIDTH - 1
HEADS = 4
DK = 128
DV = 256
QK_W = HEADS * DK
V_W = HEADS * DV
CHUNK = 128
FFN_DIM = 4 * D_MODEL
EPS = 1e-6
ROPE_BASE = 10000.0
PAST_LEN = 16384
K_SCALE = DK ** -0.5

C_CONV = 0
C_Q = 2 * CONV_DIM
C_K = C_Q + QK_W
C_V = C_K + QK_W
C_G = C_V + V_W
C_GATES = C_G + V_W
IN_COLS = C_GATES + 2 * D_MODEL

LOG_GAMMA = tuple(math.log1p(-(2.0 ** (-5 - h))) for h in range(HEADS))
GAMMA = tuple(math.exp(lg) for lg in LOG_GAMMA)
GAMMA_CHUNK = tuple(math.exp(CHUNK * lg) for lg in LOG_GAMMA)

VMEM_LIMIT_BYTES = 60 * 1024 * 1024
SUBLANES = 8
HALO_PAD = 32
MIX_ROWS = 512
FFN_ROWS = 1024
FFN_CHUNK = 1024
CONV_TILE = 256
CONV_ROWS = 32
SAMPLE_BLOCK = 8
PRE_COLS = 1024

NT_DIMS = (((1,), (1,)), ((), ()))
TN_DIMS = (((0,), (0,)), ((), ()))


def _dot(a, b):
    return jnp.dot(a, b, preferred_element_type=F32)


def _rms(x, w):
    return x * lax.rsqrt(jnp.mean(x * x, axis=-1, keepdims=True) + EPS) * w


def _layer_norm(x, w, b):
    mu = jnp.mean(x, axis=-1, keepdims=True)
    xc = x - mu
    return xc * lax.rsqrt(jnp.mean(xc * xc, axis=-1, keepdims=True) + EPS) * w + b


def _silu(x):
    return x * jax.nn.sigmoid(x)


def _group_norm(o):
    return o * lax.rsqrt(jnp.mean(o * o, axis=-1, keepdims=True) + EPS)


def _const_spec(shape):
    zeros = (0,) * len(shape)
    return pl.BlockSpec(shape, lambda *_: zeros, pipeline_mode=pl.Buffered(1))


def _merge_out(aact_ref, gate_ref, bout_ref, xprev_ref, wco_ref, wo_ref, npost_ref, mid=None):
    a_out = _dot(aact_ref[...], wco_ref[...])
    mid_out = mid() if mid is not None else None
    merged = (jax.nn.sigmoid(gate_ref[:, :D_MODEL]) * a_out
              + jax.nn.sigmoid(gate_ref[:, D_MODEL:]) * bout_ref[...])
    m = _dot(merged.astype(BF16), wo_ref[...])
    return xprev_ref[...] + _rms(m, npost_ref[...]), mid_out


def _mixer_prompt_kernel(x_ref, cos_ref, sin_ref, npre_ref, npost_ref, win_ref, wdw_ref,
                         bdw_ref, lnw_ref, lnb_ref, wco_ref, wro_ref, wo_ref,
                         h_ref, nc_ref, sout_ref,
                         fb_ref, u_ref, y_ref, dec_ref, qd_ref, kd_ref, wb_ref, s_ref,
                         aact_ref, gate_ref, bout_ref, xprev_ref, *, tiles_per_seq, n_tiles):
    g = pl.program_id(0)

    @pl.when(g < n_tiles)
    def _pipelined_step():
        _mixer_step(x_ref, cos_ref, sin_ref, npre_ref, npost_ref, win_ref, wdw_ref,
                    bdw_ref, lnw_ref, lnb_ref, wco_ref, wro_ref, wo_ref,
                    h_ref, nc_ref, sout_ref,
                    fb_ref, u_ref, y_ref, dec_ref, qd_ref, kd_ref, wb_ref, s_ref,
                    aact_ref, gate_ref, bout_ref, xprev_ref, tiles_per_seq=tiles_per_seq)

    @pl.when(g == n_tiles)
    def _drain():
        h_ref[0], _ = _merge_out(aact_ref, gate_ref, bout_ref, xprev_ref, wco_ref, wo_ref,
                                 npost_ref)


def _mixer_step(x_ref, cos_ref, sin_ref, npre_ref, npost_ref, win_ref, wdw_ref,
                bdw_ref, lnw_ref, lnb_ref, wco_ref, wro_ref, wo_ref,
                h_ref, nc_ref, sout_ref,
                fb_ref, u_ref, y_ref, dec_ref, qd_ref, kd_ref, wb_ref, s_ref,
                aact_ref, gate_ref, bout_ref, xprev_ref, *, tiles_per_seq):
    g = pl.program_id(0)
    i = g % tiles_per_seq
    rows = x_ref.shape[1]

    @pl.when(g == 0)
    def _init_tables():
        aact_ref[...] = jnp.zeros(aact_ref.shape, BF16)
        gate_ref[...] = jnp.zeros(gate_ref.shape, F32)
        bout_ref[...] = jnp.zeros(bout_ref.shape, F32)
        xprev_ref[...] = jnp.zeros(xprev_ref.shape, F32)
        ii = lax.broadcasted_iota(jnp.int32, (CHUNK, CHUNK), 0)
        jj = lax.broadcasted_iota(jnp.int32, (CHUNK, CHUNK), 1)
        diff = (ii - jj).astype(F32)
        row_k = lax.broadcasted_iota(jnp.int32, (CHUNK, DK), 0).astype(F32)
        for h in range(HEADS):
            lg = LOG_GAMMA[h]
            dec_ref[h] = jnp.where(diff >= 0.0, jnp.exp(jnp.maximum(diff, 0.0) * lg), 0.0)
            qd_ref[h] = jnp.exp((row_k + 1.0) * lg)
            kd_ref[h] = jnp.exp((CHUNK - 1.0 - row_k) * lg)
        for j in range(CONV_WIDTH):
            wb_ref[j] = jnp.broadcast_to(wdw_ref[j], (SUBLANES, CONV_DIM))

    @pl.when(i == 0)
    def _start_sequence():
        fb_ref[0, 0:HALO_PAD, :] = jnp.zeros((HALO_PAD, CONV_DIM), F32)
        s_ref[...] = jnp.zeros(s_ref.shape, F32)

    x = x_ref[0]
    xn = _rms(x, npre_ref[...]).astype(BF16)
    h_ref[0], pc = _merge_out(
        aact_ref, gate_ref, bout_ref, xprev_ref, wco_ref, wo_ref, npost_ref,
        mid=lambda: _dot(xn, win_ref[:, C_CONV:C_CONV + 2 * CONV_DIM]))

    xprev_ref[...] = x

    u_ref[...] = pc[:, :CONV_DIM] * jax.nn.sigmoid(pc[:, CONV_DIM:])

    pqk = _dot(xn, win_ref[:, C_Q:C_Q + 2 * QK_W])
    pv = _dot(xn, win_ref[:, C_V:C_V + V_W]).astype(BF16)
    pg = _dot(xn, win_ref[:, C_G:C_G + V_W])

    first = HALO_PAD - HALO
    span = CONV_TILE + HALO_PAD - SUBLANES
    for t in range(rows // CONV_TILE):
        fb_ref[0, HALO_PAD:HALO_PAD + CONV_TILE, :] = u_ref[t * CONV_TILE:(t + 1) * CONV_TILE, :]
        for r in range(1, SUBLANES):
            fb_ref[r, 0:span, :] = fb_ref[0, r:r + span, :]
        for rb in range(CONV_TILE // CONV_ROWS):
            acc = jnp.broadcast_to(bdw_ref[...], (CONV_ROWS, CONV_DIM))
            for j in range(CONV_WIDTH):
                off = first + j
                base = rb * CONV_ROWS + off - off % SUBLANES
                slab = fb_ref[off % SUBLANES, base:base + CONV_ROWS, :]
                acc = acc + (slab.reshape(CONV_ROWS // SUBLANES, SUBLANES, CONV_DIM)
                             * wb_ref[j][None]).reshape(CONV_ROWS, CONV_DIM)
            r0 = t * CONV_TILE + rb * CONV_ROWS
            aact_ref[r0:r0 + CONV_ROWS, :] = _silu(
                _layer_norm(acc, lnw_ref[...], lnb_ref[...])).astype(BF16)
        fb_ref[0, 0:HALO_PAD, :] = fb_ref[0, CONV_TILE:CONV_TILE + HALO_PAD, :]

    cos2 = cos_ref[...]
    sin2 = sin_ref[...]

    def rot(t):
        return t * cos2 + pltpu.roll(t, DK // 2, 1) * sin2

    heads = range(HEADS)
    q_rot = [rot(pqk[:, h * DK:(h + 1) * DK]) for h in heads]
    q_bf = [q.astype(BF16) for q in q_rot]
    k_rot = [rot(pqk[:, QK_W + h * DK:QK_W + (h + 1) * DK]) * K_SCALE for h in heads]
    k_bf = [k.astype(BF16) for k in k_rot]
    g_act = [_silu(pg[:, h * DV:(h + 1) * DV]) for h in heads]

    n_chunks = rows // CHUNK
    n_fill = 2 * n_chunks
    gate_cols = 2 * D_MODEL // n_fill

    def gate_slice(f):
        gate_ref[:, f * gate_cols:(f + 1) * gate_cols] = _dot(
            xn, win_ref[:, C_GATES + f * gate_cols:C_GATES + (f + 1) * gate_cols])

    for c in range(n_chunks):
        rs = slice(c * CHUNK, (c + 1) * CHUNK)
        v_c = [pv[rs, h * DV:(h + 1) * DV] for h in heads]
        state = [s_ref[h] for h in heads]
        scores = []
        zeros = jnp.zeros((CHUNK, DK), BF16)
        for a in range(0, HEADS, 2):
            q_pair = jnp.concatenate([q_bf[a][rs], q_bf[a + 1][rs]], axis=1)
            k_pair = jnp.concatenate(
                [jnp.concatenate([k_bf[a][rs], zeros], axis=1),
                 jnp.concatenate([zeros, k_bf[a + 1][rs]], axis=1)], axis=0)
            s_pair = lax.dot_general(q_pair, k_pair, NT_DIMS, preferred_element_type=F32)
            scores += [s_pair[:, :CHUNK], s_pair[:, CHUNK:]]
        kv = [lax.dot_general((k_rot[h][rs] * kd_ref[h]).astype(BF16), v_c[h], TN_DIMS,
                              preferred_element_type=F32) for h in heads]
        gate_slice(2 * c)
        lhs = [jnp.concatenate([(scores[h] * dec_ref[h]).astype(BF16),
                                (q_rot[h][rs] * qd_ref[h]).astype(BF16)], axis=1)
               for h in heads]
        rhs = [jnp.concatenate([v_c[h], state[h].astype(BF16)], axis=0) for h in heads]
        out = [_dot(lhs[h], rhs[h]) for h in heads]
        gate_slice(2 * c + 1)
        for h in heads:
            s_ref[h] = GAMMA_CHUNK[h] * state[h] + kv[h]
            y_ref[rs, h * DV:(h + 1) * DV] = (g_act[h][rs] * _group_norm(out[h])).astype(BF16)
    bout_ref[...] = _dot(y_ref[...], wro_ref[...])

    @pl.when(i == tiles_per_seq - 1)
    def _emit_sequence_state():
        seq = g // tiles_per_seq
        for j in range(HALO):
            nc_ref[j, pl.ds(seq, 1), :] = fb_ref[0, HALO_PAD - HALO + j:HALO_PAD - HALO + j + 1, :]
        sout_ref[0] = s_ref[...]


def _mixer_prompt(x, cos2, sin2, npre, npost, win, wdw, bdw, lnw, lnb, wco, wro, wo):
    batch, seq, _ = x.shape
    rows = MIX_ROWS
    nt = seq // rows
    n_tiles = batch * nt

    def front(g):
        return jnp.minimum(g, n_tiles - 1)

    def back(g):
        return jnp.maximum(g - 1, 0)

    in_specs = [
        pl.BlockSpec((1, rows, D_MODEL), lambda g: (front(g) // nt, front(g) % nt, 0)),
        pl.BlockSpec((rows, DK), lambda g: (front(g) % nt, 0)),
        pl.BlockSpec((rows, DK), lambda g: (front(g) % nt, 0)),
        _const_spec((1, D_MODEL)),
        _const_spec((1, D_MODEL)),
        _const_spec((D_MODEL, IN_COLS)),
        _const_spec((CONV_WIDTH, 1, CONV_DIM)),
        _const_spec((1, CONV_DIM)),
        _const_spec((1, CONV_DIM)),
        _const_spec((1, CONV_DIM)),
        _const_spec((CONV_DIM, D_MODEL)),
        _const_spec((V_W, D_MODEL)),
        _const_spec((D_MODEL, D_MODEL)),
    ]
    out_specs = [
        pl.BlockSpec((1, rows, D_MODEL), lambda g: (back(g) // nt, back(g) % nt, 0)),
        pl.BlockSpec((HALO, batch, CONV_DIM), lambda g: (0, 0, 0)),
        pl.BlockSpec((1, HEADS, DK, DV), lambda g: (front(g) // nt, 0, 0, 0)),
    ]
    out_shape = [
        jax.ShapeDtypeStruct((batch, seq, D_MODEL), F32),
        jax.ShapeDtypeStruct((HALO, batch, CONV_DIM), F32),
        jax.ShapeDtypeStruct((batch, HEADS, DK, DV), F32),
    ]
    scratch = [
        pltpu.VMEM((SUBLANES, HALO_PAD + CONV_TILE, CONV_DIM), F32),
        pltpu.VMEM((rows, CONV_DIM), F32),
        pltpu.VMEM((rows, V_W), BF16),
        pltpu.VMEM((HEADS, CHUNK, CHUNK), F32),
        pltpu.VMEM((HEADS, CHUNK, DK), F32),
        pltpu.VMEM((HEADS, CHUNK, DK), F32),
        pltpu.VMEM((CONV_WIDTH, SUBLANES, CONV_DIM), F32),
        pltpu.VMEM((HEADS, DK, DV), F32),
        pltpu.VMEM((rows, CONV_DIM), BF16),
        pltpu.VMEM((rows, 2 * D_MODEL), F32),
        pltpu.VMEM((rows, D_MODEL), F32),
        pltpu.VMEM((rows, D_MODEL), F32),
    ]
    return pl.pallas_call(
        functools.partial(_mixer_prompt_kernel, tiles_per_seq=nt, n_tiles=n_tiles),
        grid=(n_tiles + 1,), in_specs=in_specs, out_specs=out_specs, out_shape=out_shape,
        scratch_shapes=scratch,
        compiler_params=pltpu.CompilerParams(
            dimension_semantics=("arbitrary",),
            vmem_limit_bytes=VMEM_LIMIT_BYTES),
        name="mixer_prompt",
    )(x, cos2, sin2, npre, npost, win, wdw, bdw, lnw, lnb, wco, wro, wo)


def _ffn_body(h, npre, npost, wup_ref, wdn_ref):
    hn = _rms(h, npre).astype(BF16)
    f = None
    for c in range(FFN_DIM // FFN_CHUNK):
        cols = slice(c * FFN_CHUNK, (c + 1) * FFN_CHUNK)
        up = _dot(hn, wup_ref[:, cols])
        act = jnp.square(jnp.maximum(up, 0.0)).astype(BF16)
        part = _dot(act, wdn_ref[cols, :])
        f = part if f is None else f + part
    return h + _rms(f, npost)


def _sample_state_body(blk, cache_ref, u_ref, wdw_ref, bdw_ref, qt_ref, kt_ref, v_ref, st_ref,
                       conv_ref, nc_ref, o_ref, so_ref):
    n_seq = qt_ref.shape[1]
    u = u_ref[...]
    acc = u * wdw_ref[HALO] + bdw_ref[...]
    for j in range(HALO):
        acc = acc + cache_ref[j] * wdw_ref[j]
    conv_ref[...] = acc
    nc_ref[0:HALO - 1] = cache_ref[1:HALO]
    nc_ref[HALO - 1] = u

    shift = (n_seq - blk * SAMPLE_BLOCK) % n_seq
    qt = pltpu.roll(qt_ref[...], shift, 1)
    kt = pltpu.roll(kt_ref[...], shift, 1)
    for s in range(SAMPLE_BLOCK):
        for h in range(HEADS):
            k_col = kt[h * DK:(h + 1) * DK, s:s + 1]
            q_col = qt[h * DK:(h + 1) * DK, s:s + 1]
            v_row = v_ref[s:s + 1, h * DV:(h + 1) * DV]
            new_state = GAMMA[h] * st_ref[s, h] + k_col * v_row
            so_ref[s, h] = new_state
            o_ref[s:s + 1, h * DV:(h + 1) * DV] = jnp.sum(new_state * q_col, axis=0,
                                                          keepdims=True)


def _ffn_state_kernel(h_ref, npre_ref, npost_ref, wup_ref, wdn_ref,
                      cache_ref, u_ref, wdw_ref, bdw_ref, qt_ref, kt_ref, v_ref, st_ref,
                      y_ref, conv_ref, nc_ref, o_ref, so_ref, *, steps_per_block):
    y_ref[...] = _ffn_body(h_ref[...], npre_ref[...], npost_ref[...], wup_ref, wdn_ref)
    _sample_state_body(pl.program_id(0) // steps_per_block, cache_ref, u_ref, wdw_ref, bdw_ref,
                       qt_ref, kt_ref, v_ref, st_ref, conv_ref, nc_ref, o_ref, so_ref)


def _ffn_state(h, npre, npost, wup, wdn, cache, u, wdw, bdw, qt, kt, v, state):
    n = h.shape[0]
    rows = FFN_ROWS
    steps = n // rows
    n_seq = u.shape[0]
    sb = SAMPLE_BLOCK
    spb = steps // (n_seq // sb)
    assert spb * (n_seq // sb) == steps
    in_specs = [
        pl.BlockSpec((rows, D_MODEL), lambda i: (i, 0)),
        _const_spec((1, D_MODEL)),
        _const_spec((1, D_MODEL)),
        _const_spec((D_MODEL, FFN_DIM)),
        _const_spec((FFN_DIM, D_MODEL)),
        pl.BlockSpec((HALO, sb, CONV_DIM), lambda i: (0, i // spb, 0)),
        pl.BlockSpec((sb, CONV_DIM), lambda i: (i // spb, 0)),
        _const_spec((CONV_WIDTH, 1, CONV_DIM)),
        _const_spec((1, CONV_DIM)),
        _const_spec((QK_W, n_seq)),
        _const_spec((QK_W, n_seq)),
        pl.BlockSpec((sb, V_W), lambda i: (i // spb, 0)),
        pl.BlockSpec((sb, HEADS, DK, DV), lambda i: (i // spb, 0, 0, 0)),
    ]
    out_specs = [
        pl.BlockSpec((rows, D_MODEL), lambda i: (i, 0)),
        pl.BlockSpec((sb, CONV_DIM), lambda i: (i // spb, 0)),
        pl.BlockSpec((HALO, sb, CONV_DIM), lambda i: (0, i // spb, 0)),
        pl.BlockSpec((sb, V_W), lambda i: (i // spb, 0)),
        pl.BlockSpec((sb, HEADS, DK, DV), lambda i: (i // spb, 0, 0, 0)),
    ]
    out_shape = [
        jax.ShapeDtypeStruct((n, D_MODEL), F32),
        jax.ShapeDtypeStruct((n_seq, CONV_DIM), F32),
        jax.ShapeDtypeStruct((HALO, n_seq, CONV_DIM), F32),
        jax.ShapeDtypeStruct((n_seq, V_W), F32),
        jax.ShapeDtypeStruct((n_seq, HEADS, DK, DV), F32),
    ]
    return pl.pallas_call(
        functools.partial(_ffn_state_kernel, steps_per_block=spb),
        grid=(steps,), in_specs=in_specs, out_specs=out_specs, out_shape=out_shape,
        compiler_params=pltpu.CompilerParams(
            dimension_semantics=("arbitrary",),
            vmem_limit_bytes=VMEM_LIMIT_BYTES),
        name="ffn_prompt_sample_state",
    )(h, npre, npost, wup, wdn, cache, u, wdw, bdw, qt, kt, v, state)


def _sample_pre_kernel(x_ref, cos_ref, sin_ref, npre_ref, win_ref,
                       wbf_ref, u_ref, qt_ref, kt_ref, v_ref, g_ref, ga_ref, gb_ref, xn_ref):
    j = pl.program_id(0)

    @pl.when(j == 0)
    def _normalise():
        xn_ref[...] = _rms(x_ref[:, 0, :], npre_ref[...]).astype(BF16)

    w_slab = win_ref[...].astype(BF16)
    wbf_ref[...] = w_slab
    p = _dot(xn_ref[...], w_slab)

    @pl.when(j == C_CONV // PRE_COLS)
    def _glu():
        u_ref[...] = p[:, :CONV_DIM] * jax.nn.sigmoid(p[:, CONV_DIM:])

    @pl.when(j == C_Q // PRE_COLS)
    def _rotary():
        qkt = p.T
        cos_t = cos_ref[...]
        sin_t = sin_ref[...]
        half = DK // 2
        for g in range(2 * HEADS):
            x1 = qkt[g * DK:g * DK + half]
            x2 = qkt[g * DK + half:(g + 1) * DK]
            o1 = x1 * cos_t - x2 * sin_t
            o2 = x2 * cos_t + x1 * sin_t
            if g < HEADS:
                qt_ref[g * DK:g * DK + half, :] = o1
                qt_ref[g * DK + half:(g + 1) * DK, :] = o2
            else:
                k0 = (g - HEADS) * DK
                kt_ref[k0:k0 + half, :] = o1 * K_SCALE
                kt_ref[k0 + half:k0 + DK, :] = o2 * K_SCALE

    for out_ref, col in ((v_ref, C_V), (g_ref, C_G), (ga_ref, C_GATES),
                         (gb_ref, C_GATES + D_MODEL)):
        @pl.when(j == col // PRE_COLS)
        def _store(out_ref=out_ref):
            out_ref[...] = p


def _sample_pre(x, cos_t, sin_t, npre, win_f32):
    n = x.shape[0]
    assert all(c % PRE_COLS == 0 for c in (C_CONV, C_Q, C_V, C_G, C_GATES, IN_COLS))
    assert 2 * CONV_DIM == 2 * QK_W == V_W == D_MODEL == PRE_COLS

    def whole(shape):
        zeros = (0,) * len(shape)
        return pl.BlockSpec(shape, lambda j: zeros)

    in_specs = [
        whole((n, 1, D_MODEL)),
        whole((DK // 2, n)),
        whole((DK // 2, n)),
        whole((1, D_MODEL)),
        pl.BlockSpec((D_MODEL, PRE_COLS), lambda j: (0, j)),
    ]
    out_specs = [
        pl.BlockSpec((D_MODEL, PRE_COLS), lambda j: (0, j)),
        whole((n, CONV_DIM)),
        whole((QK_W, n)),
        whole((QK_W, n)),
        whole((n, V_W)),
        whole((n, V_W)),
        whole((n, D_MODEL)),
        whole((n, D_MODEL)),
    ]
    out_shape = [
        jax.ShapeDtypeStruct((D_MODEL, IN_COLS), BF16),
        jax.ShapeDtypeStruct((n, CONV_DIM), F32),
        jax.ShapeDtypeStruct((QK_W, n), F32),
        jax.ShapeDtypeStruct((QK_W, n), F32),
        jax.ShapeDtypeStruct((n, V_W), F32),
        jax.ShapeDtypeStruct((n, V_W), F32),
        jax.ShapeDtypeStruct((n, D_MODEL), F32),
        jax.ShapeDtypeStruct((n, D_MODEL), F32),
    ]
    return pl.pallas_call(
        _sample_pre_kernel,
        grid=(IN_COLS // PRE_COLS,), in_specs=in_specs, out_specs=out_specs, out_shape=out_shape,
        scratch_shapes=[pltpu.VMEM((n, D_MODEL), BF16)],
        compiler_params=pltpu.CompilerParams(
            dimension_semantics=("arbitrary",),
            vmem_limit_bytes=VMEM_LIMIT_BYTES),
        name="sample_pre_cast_w_in",
    )(x, cos_t, sin_t, npre, win_f32)


def _sample_post_kernel(x_ref, conv_ref, o_ref, g_ref, ga_ref, gb_ref, lnw_ref, lnb_ref,
                        wco_ref, wro_ref, wo_ref, npost_ref, nfpre_ref, nfpost_ref,
                        wup_ref, wdn_ref, y_ref, hres_ref, hn_ref, f_ref):
    c = pl.program_id(0)

    @pl.when(c == 0)
    def _merge():
        a_act = _silu(_layer_norm(conv_ref[...], lnw_ref[...], lnb_ref[...])).astype(BF16)
        a_out = _dot(a_act, wco_ref[...])
        b_out = jnp.zeros_like(a_out)
        for h in range(HEADS):
            cols = slice(h * DV, (h + 1) * DV)
            yh = (_silu(g_ref[:, cols]) * _group_norm(o_ref[:, cols])).astype(BF16)
            b_out = b_out + _dot(yh, wro_ref[cols, :])
        merged = jax.nn.sigmoid(ga_ref[...]) * a_out + jax.nn.sigmoid(gb_ref[...]) * b_out
        m = _dot(merged.astype(BF16), wo_ref[...])
        hres = x_ref[:, 0, :] + _rms(m, npost_ref[...])
        hres_ref[...] = hres
        hn_ref[...] = _rms(hres, nfpre_ref[...]).astype(BF16)
        f_ref[...] = jnp.zeros(f_ref.shape, F32)

    up = _dot(hn_ref[...], wup_ref[...])
    act = jnp.square(jnp.maximum(up, 0.0)).astype(BF16)
    f_ref[...] += _dot(act, wdn_ref[...])

    @pl.when(c == pl.num_programs(0) - 1)
    def _finish():
        y_ref[:, 0, :] = hres_ref[...] + _rms(f_ref[...], nfpost_ref[...])


def _sample_post(x, conv, o, g, ga, gb, lnw, lnb, wco, wro, wo, npost, nfpre, nfpost, wup, wdn):
    n = x.shape[0]

    def whole(a):
        zeros = (0,) * a.ndim
        return pl.BlockSpec(a.shape, lambda c: zeros)

    resident = (x, conv, o, g, ga, gb, lnw, lnb, wco, wro, wo, npost, nfpre, nfpost)
    in_specs = [whole(a) for a in resident] + [
        pl.BlockSpec((D_MODEL, FFN_CHUNK), lambda c: (0, c)),
        pl.BlockSpec((FFN_CHUNK, D_MODEL), lambda c: (c, 0)),
    ]
    return pl.pallas_call(
        _sample_post_kernel,
        grid=(FFN_DIM // FFN_CHUNK,), in_specs=in_specs,
        out_specs=pl.BlockSpec(x.shape, lambda c: (0, 0, 0)),
        out_shape=jax.ShapeDtypeStruct(x.shape, F32),
        scratch_shapes=[
            pltpu.VMEM((n, D_MODEL), F32),
            pltpu.VMEM((n, D_MODEL), BF16),
            pltpu.VMEM((n, D_MODEL), F32),
        ],
        compiler_params=pltpu.CompilerParams(
            dimension_semantics=("arbitrary",),
            vmem_limit_bytes=VMEM_LIMIT_BYTES),
        name="sample_post",
    )(x, conv, o, g, ga, gb, lnw, lnb, wco, wro, wo, npost, nfpre, nfpost, wup, wdn)


def _rope_angles(pos):
    half = DK // 2
    freqs = 1.0 / (ROPE_BASE ** jnp.linspace(0.0, 1.0, half, dtype=F32))
    return pos[:, None] * freqs[None, :]


def kernel(x_prompt, x_sample, cache_conv, state_ret, norm_mix_pre, norm_mix_post, w_in, w_dw, b_dw, conv_ln_w, conv_ln_b, w_conv_out, w_ret_out, w_o, norm_ffn_pre, norm_ffn_post, w_ffn_up, w_ffn_down):
    batch, seq, _ = x_prompt.shape
    n_seq, dec_seq, _ = x_sample.shape
    depth = w_in.shape[0]
    assert dec_seq == 1 and seq % MIX_ROWS == 0 and (batch * seq) % FFN_ROWS == 0
    assert n_seq % SAMPLE_BLOCK == 0

    ang_p = _rope_angles(jnp.arange(seq, dtype=F32))
    cos_p = jnp.concatenate([jnp.cos(ang_p), jnp.cos(ang_p)], axis=1)
    sin_p = jnp.concatenate([-jnp.sin(ang_p), jnp.sin(ang_p)], axis=1)
    ang_s = _rope_angles(PAST_LEN + jnp.arange(dec_seq, dtype=F32))
    cos_s = jnp.broadcast_to(jnp.cos(ang_s).T, (DK // 2, n_seq))
    sin_s = jnp.broadcast_to(jnp.sin(ang_s).T, (DK // 2, n_seq))

    xp = x_prompt
    xs = x_sample
    conv_p, ret_p, conv_s, ret_s = [], [], [], []
    for l in range(depth):
        npre = norm_mix_pre[l][None]
        npost = norm_mix_post[l][None]
        nfpre = norm_ffn_pre[l][None]
        nfpost = norm_ffn_post[l][None]
        wdw = jnp.transpose(w_dw, (1, 0, 2))[:, l:l + 1, :]
        bdw = b_dw[l][None]
        lnw = conv_ln_w[l][None]
        lnb = conv_ln_b[l][None]
        wco = w_conv_out[l].astype(BF16)
        wro = w_ret_out[l].astype(BF16)
        wo = w_o[l].astype(BF16)
        wup = w_ffn_up[l].astype(BF16)
        wdn = w_ffn_down[l].astype(BF16)

        win, u, qt, kt, v, g, ga, gb = _sample_pre(xs, cos_s, sin_s, npre, w_in[l])
        cache_t = jnp.transpose(cache_conv[l], (1, 0, 2))
        h_p, nc_p, s_p = _mixer_prompt(xp, cos_p, sin_p, npre, npost, win, wdw, bdw, lnw, lnb,
                                       wco, wro, wo)
        y_p, conv, nc_t, o, s_s = _ffn_state(h_p.reshape(batch * seq, D_MODEL), nfpre, nfpost,
                                             wup, wdn, cache_t, u, wdw, bdw, qt, kt, v,
                                             state_ret[l])
        xp = y_p.reshape(batch, seq, D_MODEL)
        nc_s = jnp.transpose(nc_t, (1, 0, 2))
        xs = _sample_post(xs, conv, o, g, ga, gb, lnw, lnb, wco, wro, wo, npost, nfpre, nfpost,
                          wup, wdn)
        conv_p.append(jnp.transpose(nc_p, (1, 0, 2)))
        ret_p.append(s_p)
        conv_s.append(nc_s)
        ret_s.append(s_s)

    return (xp, xs, jnp.stack(conv_p), jnp.stack(ret_p),
            jnp.stack(conv_s), jnp.stack(ret_s))
```

```python
import functools
import math

import jax
import jax.numpy as jnp
from jax import lax
from jax.experimental import pallas as pl
from jax.experimental.pallas import tpu as pltpu

F32 = jnp.float32
BF16 = jnp.bfloat16

D_MODEL = 1024
CONV_DIM = 512
CONV_WIDTH = 31
HALO = CONV_WIDTH - 1
HEADS = 4
DK = 128
DV = 256
QK_W = HEADS * DK
V_W = HEADS * DV
CHUNK = 128
FFN_DIM = 4 * D_MODEL
EPS = 1e-6
ROPE_BASE = 10000.0
PAST_LEN = 16384
K_SCALE = DK ** -0.5

C_CONV = 0
C_Q = 2 * CONV_DIM
C_K = C_Q + QK_W
C_V = C_K + QK_W
C_G = C_V + V_W
C_GATES = C_G + V_W
IN_COLS = C_GATES + 2 * D_MODEL

LOG_GAMMA = tuple(math.log1p(-(2.0 ** (-5 - h))) for h in range(HEADS))
GAMMA = tuple(math.exp(lg) for lg in LOG_GAMMA)
GAMMA_CHUNK = tuple(math.exp(CHUNK * lg) for lg in LOG_GAMMA)

VMEM_LIMIT_BYTES = 60 * 1024 * 1024
SUBLANES = 8
HALO_PAD = 32
MIX_ROWS = 512
FFN_ROWS = 1024
FFN_CHUNK = 1024
CONV_TILE = 256
CONV_ROWS = 32
SAMPLE_BLOCK = 8
PRE_COLS = 1024
ROPE_FINE = 128

NT_DIMS = (((1,), (1,)), ((), ()))
TN_DIMS = (((0,), (0,)), ((), ()))


def _dot(a, b):
    return jnp.dot(a, b, preferred_element_type=F32)


def _rms(x, w):
    return x * lax.rsqrt(jnp.mean(x * x, axis=-1, keepdims=True) + EPS) * w


def _layer_norm(x, w, b):
    mu = jnp.mean(x, axis=-1, keepdims=True)
    xc = x - mu
    return xc * lax.rsqrt(jnp.mean(xc * xc, axis=-1, keepdims=True) + EPS) * w + b


def _silu(x):
    return x * jax.nn.sigmoid(x)


def _group_norm(o):
    return o * lax.rsqrt(jnp.mean(o * o, axis=-1, keepdims=True) + EPS)


def _const_spec(shape):
    zeros = (0,) * len(shape)
    return pl.BlockSpec(shape, lambda *_: zeros, pipeline_mode=pl.Buffered(1))


def _merge_out(aact_ref, gate_ref, bout_ref, xprev_ref, wco_ref, wo_ref, npost_ref, mid=None):
    a_out = _dot(aact_ref[...], wco_ref[...])
    mid_out = mid() if mid is not None else None
    merged = (jax.nn.sigmoid(gate_ref[:, :D_MODEL]) * a_out
              + jax.nn.sigmoid(gate_ref[:, D_MODEL:]) * bout_ref[...])
    m = _dot(merged.astype(BF16), wo_ref[...])
    return xprev_ref[...] + _rms(m, npost_ref[...]), mid_out


def _mixer_prompt_kernel(x_ref, cos_ref, sin_ref, npre_ref, npost_ref, win_ref, wdw_ref,
                         bdw_ref, lnw_ref, lnb_ref, wco_ref, wro_ref, wo_ref,
                         h_ref, nc_ref, sout_ref,
                         fb_ref, u_ref, y_ref, dec_ref, qd_ref, kd_ref, wb_ref, s_ref,
                         aact_ref, gate_ref, bout_ref, xprev_ref, *, tiles_per_seq, n_tiles):
    g = pl.program_id(0)

    @pl.when(g < n_tiles)
    def _pipelined_step():
        _mixer_step(x_ref, cos_ref, sin_ref, npre_ref, npost_ref, win_ref, wdw_ref,
                    bdw_ref, lnw_ref, lnb_ref, wco_ref, wro_ref, wo_ref,
                    h_ref, nc_ref, sout_ref,
                    fb_ref, u_ref, y_ref, dec_ref, qd_ref, kd_ref, wb_ref, s_ref,
                    aact_ref, gate_ref, bout_ref, xprev_ref, tiles_per_seq=tiles_per_seq)

    @pl.when(g == n_tiles)
    def _drain():
        h_ref[0], _ = _merge_out(aact_ref, gate_ref, bout_ref, xprev_ref, wco_ref, wo_ref,
                                 npost_ref)


def _mixer_step(x_ref, cos_ref, sin_ref, npre_ref, npost_ref, win_ref, wdw_ref,
                bdw_ref, lnw_ref, lnb_ref, wco_ref, wro_ref, wo_ref,
                h_ref, nc_ref, sout_ref,
                fb_ref, u_ref, y_ref, dec_ref, qd_ref, kd_ref, wb_ref, s_ref,
                aact_ref, gate_ref, bout_ref, xprev_ref, *, tiles_per_seq):
    g = pl.program_id(0)
    i = g % tiles_per_seq
    rows = x_ref.shape[1]

    @pl.when(g == 0)
    def _init_tables():
        aact_ref[...] = jnp.zeros(aact_ref.shape, BF16)
        gate_ref[...] = jnp.zeros(gate_ref.shape, F32)
        bout_ref[...] = jnp.zeros(bout_ref.shape, F32)
        xprev_ref[...] = jnp.zeros(xprev_ref.shape, F32)
        ii = lax.broadcasted_iota(jnp.int32, (CHUNK, CHUNK), 0)
        jj = lax.broadcasted_iota(jnp.int32, (CHUNK, CHUNK), 1)
        diff = (ii - jj).astype(F32)
        row_k = lax.broadcasted_iota(jnp.int32, (CHUNK, DK), 0).astype(F32)
        for h in range(HEADS):
            lg = LOG_GAMMA[h]
            dec_ref[h] = jnp.where(diff >= 0.0, jnp.exp(jnp.maximum(diff, 0.0) * lg), 0.0)
            qd_ref[h] = jnp.exp((row_k + 1.0) * lg)
            kd_ref[h] = jnp.exp((CHUNK - 1.0 - row_k) * lg)
        for j in range(CONV_WIDTH):
            wb_ref[j] = jnp.broadcast_to(wdw_ref[j], (SUBLANES, CONV_DIM))

    @pl.when(i == 0)
    def _start_sequence():
        fb_ref[0, 0:HALO_PAD, :] = jnp.zeros((HALO_PAD, CONV_DIM), F32)
        s_ref[...] = jnp.zeros(s_ref.shape, F32)

    x = x_ref[0]
    xn = _rms(x, npre_ref[...]).astype(BF16)
    h_ref[0], pc = _merge_out(
        aact_ref, gate_ref, bout_ref, xprev_ref, wco_ref, wo_ref, npost_ref,
        mid=lambda: _dot(xn, win_ref[:, C_CONV:C_CONV + 2 * CONV_DIM]))

    xprev_ref[...] = x

    u_ref[...] = pc[:, :CONV_DIM] * jax.nn.sigmoid(pc[:, CONV_DIM:])

    pqk = _dot(xn, win_ref[:, C_Q:C_Q + 2 * QK_W])
    pv = _dot(xn, win_ref[:, C_V:C_V + V_W]).astype(BF16)
    pg = _dot(xn, win_ref[:, C_G:C_G + V_W])

    first = HALO_PAD - HALO
    span = CONV_TILE + HALO_PAD - SUBLANES
    for t in range(rows // CONV_TILE):
        fb_ref[0, HALO_PAD:HALO_PAD + CONV_TILE, :] = u_ref[t * CONV_TILE:(t + 1) * CONV_TILE, :]
        for r in range(1, SUBLANES):
            fb_ref[r, 0:span, :] = fb_ref[0, r:r + span, :]
        for rb in range(CONV_TILE // CONV_ROWS):
            acc = jnp.broadcast_to(bdw_ref[...], (CONV_ROWS, CONV_DIM))
            for j in range(CONV_WIDTH):
                off = first + j
                base = rb * CONV_ROWS + off - off % SUBLANES
                slab = fb_ref[off % SUBLANES, base:base + CONV_ROWS, :]
                acc = acc + (slab.reshape(CONV_ROWS // SUBLANES, SUBLANES, CONV_DIM)
                             * wb_ref[j][None]).reshape(CONV_ROWS, CONV_DIM)
            r0 = t * CONV_TILE + rb * CONV_ROWS
            aact_ref[r0:r0 + CONV_ROWS, :] = _silu(
                _layer_norm(acc, lnw_ref[...], lnb_ref[...])).astype(BF16)
        fb_ref[0, 0:HALO_PAD, :] = fb_ref[0, CONV_TILE:CONV_TILE + HALO_PAD, :]

    cos2 = cos_ref[...]
    sin2 = sin_ref[...]

    def rot(t):
        return t * cos2 + pltpu.roll(t, DK // 2, 1) * sin2

    heads = range(HEADS)
    q_rot = [rot(pqk[:, h * DK:(h + 1) * DK]) for h in heads]
    q_bf = [q.astype(BF16) for q in q_rot]
    k_rot = [rot(pqk[:, QK_W + h * DK:QK_W + (h + 1) * DK]) * K_SCALE for h in heads]
    k_bf = [k.astype(BF16) for k in k_rot]
    g_act = [_silu(pg[:, h * DV:(h + 1) * DV]) for h in heads]

    n_chunks = rows // CHUNK
    n_fill = 2 * n_chunks
    gate_cols = 2 * D_MODEL // n_fill

    def gate_slice(f):
        gate_ref[:, f * gate_cols:(f + 1) * gate_cols] = _dot(
            xn, win_ref[:, C_GATES + f * gate_cols:C_GATES + (f + 1) * gate_cols])

    for c in range(n_chunks):
        rs = slice(c * CHUNK, (c + 1) * CHUNK)
        v_c = [pv[rs, h * DV:(h + 1) * DV] for h in heads]
        state = [s_ref[h] for h in heads]
        scores = []
        zeros = jnp.zeros((CHUNK, DK), BF16)
        for a in range(0, HEADS, 2):
            q_pair = jnp.concatenate([q_bf[a][rs], q_bf[a + 1][rs]], axis=1)
            k_pair = jnp.concatenate(
                [jnp.concatenate([k_bf[a][rs], zeros], axis=1),
                 jnp.concatenate([zeros, k_bf[a + 1][rs]], axis=1)], axis=0)
            s_pair = lax.dot_general(q_pair, k_pair, NT_DIMS, preferred_element_type=F32)
            scores += [s_pair[:, :CHUNK], s_pair[:, CHUNK:]]
        kv = [lax.dot_general((k_rot[h][rs] * kd_ref[h]).astype(BF16), v_c[h], TN_DIMS,
                              preferred_element_type=F32) for h in heads]
        gate_slice(2 * c)
        lhs = [jnp.concatenate([(scores[h] * dec_ref[h]).astype(BF16),
                                (q_rot[h][rs] * qd_ref[h]).astype(BF16)], axis=1)
               for h in heads]
        rhs = [jnp.concatenate([v_c[h], state[h].astype(BF16)], axis=0) for h in heads]
        out = [_dot(lhs[h], rhs[h]) for h in heads]
        gate_slice(2 * c + 1)
        for h in heads:
            s_ref[h] = GAMMA_CHUNK[h] * state[h] + kv[h]
            y_ref[rs, h * DV:(h + 1) * DV] = (g_act[h][rs] * _group_norm(out[h])).astype(BF16)
    bout_ref[...] = _dot(y_ref[...], wro_ref[...])

    @pl.when(i == tiles_per_seq - 1)
    def _emit_sequence_state():
        seq = g // tiles_per_seq
        for j in range(HALO):
            nc_ref[j, pl.ds(seq, 1), :] = fb_ref[0, HALO_PAD - HALO + j:HALO_PAD - HALO + j + 1, :]
        sout_ref[0] = s_ref[...]


def _mixer_prompt(x, cos2, sin2, npre, npost, win, wdw, bdw, lnw, lnb, wco, wro, wo):
    batch, seq, _ = x.shape
    rows = MIX_ROWS
    nt = seq // rows
    n_tiles = batch * nt

    def front(g):
        return jnp.minimum(g, n_tiles - 1)

    def back(g):
        return jnp.maximum(g - 1, 0)

    in_specs = [
        pl.BlockSpec((1, rows, D_MODEL), lambda g: (front(g) // nt, front(g) % nt, 0)),
        pl.BlockSpec((rows, DK), lambda g: (front(g) % nt, 0)),
        pl.BlockSpec((rows, DK), lambda g: (front(g) % nt, 0)),
        _const_spec((1, D_MODEL)),
        _const_spec((1, D_MODEL)),
        _const_spec((D_MODEL, IN_COLS)),
        _const_spec((CONV_WIDTH, 1, CONV_DIM)),
        _const_spec((1, CONV_DIM)),
        _const_spec((1, CONV_DIM)),
        _const_spec((1, CONV_DIM)),
        _const_spec((CONV_DIM, D_MODEL)),
        _const_spec((V_W, D_MODEL)),
        _const_spec((D_MODEL, D_MODEL)),
    ]
    out_specs = [
        pl.BlockSpec((1, rows, D_MODEL), lambda g: (back(g) // nt, back(g) % nt, 0)),
        pl.BlockSpec((HALO, batch, CONV_DIM), lambda g: (0, 0, 0)),
        pl.BlockSpec((1, HEADS, DK, DV), lambda g: (front(g) // nt, 0, 0, 0)),
    ]
    out_shape = [
        jax.ShapeDtypeStruct((batch, seq, D_MODEL), F32),
        jax.ShapeDtypeStruct((HALO, batch, CONV_DIM), F32),
        jax.ShapeDtypeStruct((batch, HEADS, DK, DV), F32),
    ]
    scratch = [
        pltpu.VMEM((SUBLANES, HALO_PAD + CONV_TILE, CONV_DIM), F32),
        pltpu.VMEM((rows, CONV_DIM), F32),
        pltpu.VMEM((rows, V_W), BF16),
        pltpu.VMEM((HEADS, CHUNK, CHUNK), F32),
        pltpu.VMEM((HEADS, CHUNK, DK), F32),
        pltpu.VMEM((HEADS, CHUNK, DK), F32),
        pltpu.VMEM((CONV_WIDTH, SUBLANES, CONV_DIM), F32),
        pltpu.VMEM((HEADS, DK, DV), F32),
        pltpu.VMEM((rows, CONV_DIM), BF16),
        pltpu.VMEM((rows, 2 * D_MODEL), F32),
        pltpu.VMEM((rows, D_MODEL), F32),
        pltpu.VMEM((rows, D_MODEL), F32),
    ]
    return pl.pallas_call(
        functools.partial(_mixer_prompt_kernel, tiles_per_seq=nt, n_tiles=n_tiles),
        grid=(n_tiles + 1,), in_specs=in_specs, out_specs=out_specs, out_shape=out_shape,
        scratch_shapes=scratch,
        compiler_params=pltpu.CompilerParams(
            dimension_semantics=("arbitrary",),
            vmem_limit_bytes=VMEM_LIMIT_BYTES),
        name="mixer_prompt",
    )(x, cos2, sin2, npre, npost, win, wdw, bdw, lnw, lnb, wco, wro, wo)


def _ffn_body(h, npre, npost, wup_ref, wdn_ref):
    hn = _rms(h, npre).astype(BF16)
    f = None
    for c in range(FFN_DIM // FFN_CHUNK):
        cols = slice(c * FFN_CHUNK, (c + 1) * FFN_CHUNK)
        up = _dot(hn, wup_ref[:, cols])
        act = jnp.square(jnp.maximum(up, 0.0)).astype(BF16)
        part = _dot(act, wdn_ref[cols, :])
        f = part if f is None else f + part
    return h + _rms(f, npost)


def _sample_state_body(blk, cache_ref, u_ref, wdw_ref, bdw_ref, qt_ref, kt_ref, v_ref, st_ref,
                       conv_ref, nc_ref, o_ref, so_ref):
    n_seq = qt_ref.shape[1]
    u = u_ref[...]
    acc = u * wdw_ref[HALO] + bdw_ref[...]
    for j in range(HALO):
        acc = acc + cache_ref[j] * wdw_ref[j]
    conv_ref[...] = acc
    nc_ref[0:HALO - 1] = cache_ref[1:HALO]
    nc_ref[HALO - 1] = u

    shift = (n_seq - blk * SAMPLE_BLOCK) % n_seq
    qt = pltpu.roll(qt_ref[...], shift, 1)
    kt = pltpu.roll(kt_ref[...], shift, 1)
    for s in range(SAMPLE_BLOCK):
        for h in range(HEADS):
            k_col = kt[h * DK:(h + 1) * DK, s:s + 1]
            q_col = qt[h * DK:(h + 1) * DK, s:s + 1]
            v_row = v_ref[s:s + 1, h * DV:(h + 1) * DV]
            new_state = GAMMA[h] * st_ref[s, h] + k_col * v_row
            so_ref[s, h] = new_state
            o_ref[s:s + 1, h * DV:(h + 1) * DV] = jnp.sum(new_state * q_col, axis=0,
                                                          keepdims=True)


def _ffn_state_kernel(h_ref, npre_ref, npost_ref, wup_ref, wdn_ref,
                      cache_ref, u_ref, wdw_ref, bdw_ref, qt_ref, kt_ref, v_ref, st_ref,
                      y_ref, conv_ref, nc_ref, o_ref, so_ref, *, steps_per_block):
    y_ref[...] = _ffn_body(h_ref[...], npre_ref[...], npost_ref[...], wup_ref, wdn_ref)
    _sample_state_body(pl.program_id(0) // steps_per_block, cache_ref, u_ref, wdw_ref, bdw_ref,
                       qt_ref, kt_ref, v_ref, st_ref, conv_ref, nc_ref, o_ref, so_ref)


def _ffn_state(h, npre, npost, wup, wdn, cache, u, wdw, bdw, qt, kt, v, state):
    n = h.shape[0]
    rows = FFN_ROWS
    steps = n // rows
    n_seq = u.shape[0]
    sb = SAMPLE_BLOCK
    spb = steps // (n_seq // sb)
    assert spb * (n_seq // sb) == steps
    in_specs = [
        pl.BlockSpec((rows, D_MODEL), lambda i: (i, 0)),
        _const_spec((1, D_MODEL)),
        _const_spec((1, D_MODEL)),
        _const_spec((D_MODEL, FFN_DIM)),
        _const_spec((FFN_DIM, D_MODEL)),
        pl.BlockSpec((HALO, sb, CONV_DIM), lambda i: (0, i // spb, 0)),
        pl.BlockSpec((sb, CONV_DIM), lambda i: (i // spb, 0)),
        _const_spec((CONV_WIDTH, 1, CONV_DIM)),
        _const_spec((1, CONV_DIM)),
        _const_spec((QK_W, n_seq)),
        _const_spec((QK_W, n_seq)),
        pl.BlockSpec((sb, V_W), lambda i: (i // spb, 0)),
        pl.BlockSpec((sb, HEADS, DK, DV), lambda i: (i // spb, 0, 0, 0)),
    ]
    out_specs = [
        pl.BlockSpec((rows, D_MODEL), lambda i: (i, 0)),
        pl.BlockSpec((sb, CONV_DIM), lambda i: (i // spb, 0)),
        pl.BlockSpec((HALO, sb, CONV_DIM), lambda i: (0, i // spb, 0)),
        pl.BlockSpec((sb, V_W), lambda i: (i // spb, 0)),
        pl.BlockSpec((sb, HEADS, DK, DV), lambda i: (i // spb, 0, 0, 0)),
    ]
    out_shape = [
        jax.ShapeDtypeStruct((n, D_MODEL), F32),
        jax.ShapeDtypeStruct((n_seq, CONV_DIM), F32),
        jax.ShapeDtypeStruct((HALO, n_seq, CONV_DIM), F32),
        jax.ShapeDtypeStruct((n_seq, V_W), F32),
        jax.ShapeDtypeStruct((n_seq, HEADS, DK, DV), F32),
    ]
    return pl.pallas_call(
        functools.partial(_ffn_state_kernel, steps_per_block=spb),
        grid=(steps,), in_specs=in_specs, out_specs=out_specs, out_shape=out_shape,
        compiler_params=pltpu.CompilerParams(
            dimension_semantics=("arbitrary",),
            vmem_limit_bytes=VMEM_LIMIT_BYTES),
        name="ffn_prompt_sample_state",
    )(h, npre, npost, wup, wdn, cache, u, wdw, bdw, qt, kt, v, state)


def _sample_pre_kernel(x_ref, cos_ref, sin_ref, npre_ref, win_ref,
                       wbf_ref, u_ref, qt_ref, kt_ref, v_ref, g_ref, ga_ref, gb_ref, xn_ref):
    j = pl.program_id(0)

    @pl.when(j == 0)
    def _normalise():
        xn_ref[...] = _rms(x_ref[:, 0, :], npre_ref[...]).astype(BF16)

    w_slab = win_ref[...].astype(BF16)
    wbf_ref[...] = w_slab
    p = _dot(xn_ref[...], w_slab)

    @pl.when(j == C_CONV // PRE_COLS)
    def _glu():
        u_ref[...] = p[:, :CONV_DIM] * jax.nn.sigmoid(p[:, CONV_DIM:])

    @pl.when(j == C_Q // PRE_COLS)
    def _rotary():
        qkt = p.T
        cos_t = cos_ref[...]
        sin_t = sin_ref[...]
        half = DK // 2
        for g in range(2 * HEADS):
            x1 = qkt[g * DK:g * DK + half]
            x2 = qkt[g * DK + half:(g + 1) * DK]
            o1 = x1 * cos_t - x2 * sin_t
            o2 = x2 * cos_t + x1 * sin_t
            if g < HEADS:
                qt_ref[g * DK:g * DK + half, :] = o1
                qt_ref[g * DK + half:(g + 1) * DK, :] = o2
            else:
                k0 = (g - HEADS) * DK
                kt_ref[k0:k0 + half, :] = o1 * K_SCALE
                kt_ref[k0 + half:k0 + DK, :] = o2 * K_SCALE

    for out_ref, col in ((v_ref, C_V), (g_ref, C_G), (ga_ref, C_GATES),
                         (gb_ref, C_GATES + D_MODEL)):
        @pl.when(j == col // PRE_COLS)
        def _store(out_ref=out_ref):
            out_ref[...] = p


def _sample_pre(x, cos_t, sin_t, npre, win_f32):
    n = x.shape[0]
    assert all(c % PRE_COLS == 0 for c in (C_CONV, C_Q, C_V, C_G, C_GATES, IN_COLS))
    assert 2 * CONV_DIM == 2 * QK_W == V_W == D_MODEL == PRE_COLS

    def whole(shape):
        zeros = (0,) * len(shape)
        return pl.BlockSpec(shape, lambda j: zeros)

    in_specs = [
        whole((n, 1, D_MODEL)),
        whole((DK // 2, n)),
        whole((DK // 2, n)),
        whole((1, D_MODEL)),
        pl.BlockSpec((D_MODEL, PRE_COLS), lambda j: (0, j)),
    ]
    out_specs = [
        pl.BlockSpec((D_MODEL, PRE_COLS), lambda j: (0, j)),
        whole((n, CONV_DIM)),
        whole((QK_W, n)),
        whole((QK_W, n)),
        whole((n, V_W)),
        whole((n, V_W)),
        whole((n, D_MODEL)),
        whole((n, D_MODEL)),
    ]
    out_shape = [
        jax.ShapeDtypeStruct((D_MODEL, IN_COLS), BF16),
        jax.ShapeDtypeStruct((n, CONV_DIM), F32),
        jax.ShapeDtypeStruct((QK_W, n), F32),
        jax.ShapeDtypeStruct((QK_W, n), F32),
        jax.ShapeDtypeStruct((n, V_W), F32),
        jax.ShapeDtypeStruct((n, V_W), F32),
        jax.ShapeDtypeStruct((n, D_MODEL), F32),
        jax.ShapeDtypeStruct((n, D_MODEL), F32),
    ]
    return pl.pallas_call(
        _sample_pre_kernel,
        grid=(IN_COLS // PRE_COLS,), in_specs=in_specs, out_specs=out_specs, out_shape=out_shape,
        scratch_shapes=[pltpu.VMEM((n, D_MODEL), BF16)],
        compiler_params=pltpu.CompilerParams(
            dimension_semantics=("arbitrary",),
            vmem_limit_bytes=VMEM_LIMIT_BYTES),
        name="sample_pre_cast_w_in",
    )(x, cos_t, sin_t, npre, win_f32)


def _sample_post_kernel(x_ref, conv_ref, o_ref, g_ref, ga_ref, gb_ref, lnw_ref, lnb_ref,
                        wco_ref, wro_ref, wo_ref, npost_ref, nfpre_ref, nfpost_ref,
                        wup_ref, wdn_ref, y_ref, hres_ref, hn_ref, f_ref):
    c = pl.program_id(0)

    @pl.when(c == 0)
    def _merge():
        a_act = _silu(_layer_norm(conv_ref[...], lnw_ref[...], lnb_ref[...])).astype(BF16)
        a_out = _dot(a_act, wco_ref[...])
        b_out = jnp.zeros_like(a_out)
        for h in range(HEADS):
            cols = slice(h * DV, (h + 1) * DV)
            yh = (_silu(g_ref[:, cols]) * _group_norm(o_ref[:, cols])).astype(BF16)
            b_out = b_out + _dot(yh, wro_ref[cols, :])
        merged = jax.nn.sigmoid(ga_ref[...]) * a_out + jax.nn.sigmoid(gb_ref[...]) * b_out
        m = _dot(merged.astype(BF16), wo_ref[...])
        hres = x_ref[:, 0, :] + _rms(m, npost_ref[...])
        hres_ref[...] = hres
        hn_ref[...] = _rms(hres, nfpre_ref[...]).astype(BF16)
        f_ref[...] = jnp.zeros(f_ref.shape, F32)

    up = _dot(hn_ref[...], wup_ref[...])
    act = jnp.square(jnp.maximum(up, 0.0)).astype(BF16)
    f_ref[...] += _dot(act, wdn_ref[...])

    @pl.when(c == pl.num_programs(0) - 1)
    def _finish():
        y_ref[:, 0, :] = hres_ref[...] + _rms(f_ref[...], nfpost_ref[...])


def _sample_post(x, conv, o, g, ga, gb, lnw, lnb, wco, wro, wo, npost, nfpre, nfpost, wup, wdn):
    n = x.shape[0]

    def whole(a):
        zeros = (0,) * a.ndim
        return pl.BlockSpec(a.shape, lambda c: zeros)

    resident = (x, conv, o, g, ga, gb, lnw, lnb, wco, wro, wo, npost, nfpre, nfpost)
    in_specs = [whole(a) for a in resident] + [
        pl.BlockSpec((D_MODEL, FFN_CHUNK), lambda c: (0, c)),
        pl.BlockSpec((FFN_CHUNK, D_MODEL), lambda c: (c, 0)),
    ]
    return pl.pallas_call(
        _sample_post_kernel,
        grid=(FFN_DIM // FFN_CHUNK,), in_specs=in_specs,
        out_specs=pl.BlockSpec(x.shape, lambda c: (0, 0, 0)),
        out_shape=jax.ShapeDtypeStruct(x.shape, F32),
        scratch_shapes=[
            pltpu.VMEM((n, D_MODEL), F32),
            pltpu.VMEM((n, D_MODEL), BF16),
            pltpu.VMEM((n, D_MODEL), F32),
        ],
        compiler_params=pltpu.CompilerParams(
            dimension_semantics=("arbitrary",),
            vmem_limit_bytes=VMEM_LIMIT_BYTES),
        name="sample_post",
    )(x, conv, o, g, ga, gb, lnw, lnb, wco, wro, wo, npost, nfpre, nfpost, wup, wdn)


def _rope_angles(pos):
    half = DK // 2
    freqs = 1.0 / (ROPE_BASE ** jnp.linspace(0.0, 1.0, half, dtype=F32))
    return pos[:, None] * freqs[None, :]


def kernel(x_prompt, x_sample, cache_conv, state_ret, norm_mix_pre, norm_mix_post, w_in, w_dw, b_dw, conv_ln_w, conv_ln_b, w_conv_out, w_ret_out, w_o, norm_ffn_pre, norm_ffn_post, w_ffn_up, w_ffn_down):
    batch, seq, _ = x_prompt.shape
    n_seq, dec_seq, _ = x_sample.shape
    depth = w_in.shape[0]
    assert dec_seq == 1 and seq % MIX_ROWS == 0 and (batch * seq) % FFN_ROWS == 0
    assert n_seq % SAMPLE_BLOCK == 0

    assert seq % ROPE_FINE == 0
    ang_a = _rope_angles(ROPE_FINE * jnp.arange(seq // ROPE_FINE, dtype=F32))[:, None, :]
    ang_b = _rope_angles(jnp.arange(ROPE_FINE, dtype=F32))[None, :, :]
    cos_h = (jnp.cos(ang_a) * jnp.cos(ang_b) - jnp.sin(ang_a) * jnp.sin(ang_b)).reshape(seq, -1)
    sin_h = (jnp.sin(ang_a) * jnp.cos(ang_b) + jnp.cos(ang_a) * jnp.sin(ang_b)).reshape(seq, -1)
    cos_p = jnp.concatenate([cos_h, cos_h], axis=1)
    sin_p = jnp.concatenate([-sin_h, sin_h], axis=1)
    ang_s = _rope_angles(PAST_LEN + jnp.arange(dec_seq, dtype=F32))
    cos_s = jnp.broadcast_to(jnp.cos(ang_s).T, (DK // 2, n_seq))
    sin_s = jnp.broadcast_to(jnp.sin(ang_s).T, (DK // 2, n_seq))

    xp = x_prompt
    xs = x_sample
    conv_p, ret_p, conv_s, ret_s = [], [], [], []
    for l in range(depth):
        npre = norm_mix_pre[l][None]
        npost = norm_mix_post[l][None]
        nfpre = norm_ffn_pre[l][None]
        nfpost = norm_ffn_post[l][None]
        wdw = jnp.transpose(w_dw, (1, 0, 2))[:, l:l + 1, :]
        bdw = b_dw[l][None]
        lnw = conv_ln_w[l][None]
        lnb = conv_ln_b[l][None]
        wco = w_conv_out[l].astype(BF16)
        wro = w_ret_out[l].astype(BF16)
        wo = w_o[l].astype(BF16)
        wup = w_ffn_up[l].astype(BF16)
        wdn = w_ffn_down[l].astype(BF16)

        win, u, qt, kt, v, g, ga, gb = _sample_pre(xs, cos_s, sin_s, npre, w_in[l])
        cache_t = jnp.transpose(cache_conv[l], (1, 0, 2))
        h_p, nc_p, s_p = _mixer_prompt(xp, cos_p, sin_p, npre, npost, win, wdw, bdw, lnw, lnb,
                                       wco, wro, wo)
        y_p, conv, nc_t, o, s_s = _ffn_state(h_p.reshape(batch * seq, D_MODEL), nfpre, nfpost,
                                             wup, wdn, cache_t, u, wdw, bdw, qt, kt, v,
                                             state_ret[l])
        xp = y_p.reshape(batch, seq, D_MODEL)
        nc_s = jnp.transpose(nc_t, (1, 0, 2))
        xs = _sample_post(xs, conv, o, g, ga, gb, lnw, lnb, wco, wro, wo, npost, nfpre, nfpost,
                          wup, wdn)
        conv_p.append(jnp.transpose(nc_p, (1, 0, 2)))
        ret_p.append(s_p)
        conv_s.append(nc_s)
        ret_s.append(s_s)

    return (xp, xs, jnp.stack(conv_p), jnp.stack(ret_p),
            jnp.stack(conv_s), jnp.stack(ret_s))
```

```python
import functools
import math

import jax
import jax.numpy as jnp
from jax import lax
from jax.experimental import pallas as pl
from jax.experimental.pallas import tpu as pltpu

F32 = jnp.float32
BF16 = jnp.bfloat16

D_MODEL = 1024
CONV_DIM = 512
CONV_WIDTH = 31
HALO = CONV_WIDTH - 1
HEADS = 4
DK = 128
DV = 256
QK_W = HEADS * DK
V_W = HEADS * DV
CHUNK = 128
FFN_DIM = 4 * D_MODEL
EPS = 1e-6
ROPE_BASE = 10000.0
PAST_LEN = 16384
K_SCALE = DK ** -0.5

C_CONV = 0
C_Q = 2 * CONV_DIM
C_K = C_Q + QK_W
C_V = C_K + QK_W
C_G = C_V + V_W
C_GATES = C_G + V_W
IN_COLS = C_GATES + 2 * D_MODEL

LOG_GAMMA = tuple(math.log1p(-(2.0 ** (-5 - h))) for h in range(HEADS))
GAMMA = tuple(math.exp(lg) for lg in LOG_GAMMA)
GAMMA_CHUNK = tuple(math.exp(CHUNK * lg) for lg in LOG_GAMMA)

VMEM_LIMIT_BYTES = 60 * 1024 * 1024
SUBLANES = 8
HALO_PAD = 32
MIX_ROWS = 512
FFN_ROWS = 1024
FFN_CHUNK = 1024
CONV_TILE = 256
CONV_ROWS = 32
SAMPLE_BLOCK = 8
PRE_COLS = 1024
CAST_SLABS = 4
ROPE_FINE = 128

NT_DIMS = (((1,), (1,)), ((), ()))
TN_DIMS = (((0,), (0,)), ((), ()))


def _dot(a, b):
    return jnp.dot(a, b, preferred_element_type=F32)


def _rms(x, w):
    return x * lax.rsqrt(jnp.mean(x * x, axis=-1, keepdims=True) + EPS) * w


def _layer_norm(x, w, b):
    mu = jnp.mean(x, axis=-1, keepdims=True)
    xc = x - mu
    return xc * lax.rsqrt(jnp.mean(xc * xc, axis=-1, keepdims=True) + EPS) * w + b


def _silu(x):
    return x * jax.nn.sigmoid(x)


def _group_norm(o):
    return o * lax.rsqrt(jnp.mean(o * o, axis=-1, keepdims=True) + EPS)


def _const_spec(shape):
    zeros = (0,) * len(shape)
    return pl.BlockSpec(shape, lambda *_: zeros, pipeline_mode=pl.Buffered(1))


def _merge_out(aact_ref, gate_ref, bout_ref, xprev_ref, wco_ref, wo_ref, npost_ref, mid=None):
    a_out = _dot(aact_ref[...], wco_ref[...])
    mid_out = mid() if mid is not None else None
    merged = (jax.nn.sigmoid(gate_ref[:, :D_MODEL]) * a_out
              + jax.nn.sigmoid(gate_ref[:, D_MODEL:]) * bout_ref[...])
    m = _dot(merged.astype(BF16), wo_ref[...])
    return xprev_ref[...] + _rms(m, npost_ref[...]), mid_out


def _mixer_prompt_kernel(x_ref, cos_ref, sin_ref, npre_ref, npost_ref, win_ref, wdw_ref,
                         bdw_ref, lnw_ref, lnb_ref, wco_ref, wro_ref, wo_ref,
                         h_ref, nc_ref, sout_ref,
                         fb_ref, u_ref, y_ref, dec_ref, qd_ref, kd_ref, wb_ref, s_ref,
                         aact_ref, gate_ref, bout_ref, xprev_ref, *, tiles_per_seq, n_tiles):
    g = pl.program_id(0)

    @pl.when(g < n_tiles)
    def _pipelined_step():
        _mixer_step(x_ref, cos_ref, sin_ref, npre_ref, npost_ref, win_ref, wdw_ref,
                    bdw_ref, lnw_ref, lnb_ref, wco_ref, wro_ref, wo_ref,
                    h_ref, nc_ref, sout_ref,
                    fb_ref, u_ref, y_ref, dec_ref, qd_ref, kd_ref, wb_ref, s_ref,
                    aact_ref, gate_ref, bout_ref, xprev_ref, tiles_per_seq=tiles_per_seq)

    @pl.when(g == n_tiles)
    def _drain():
        h_ref[0], _ = _merge_out(aact_ref, gate_ref, bout_ref, xprev_ref, wco_ref, wo_ref,
                                 npost_ref)


def _mixer_step(x_ref, cos_ref, sin_ref, npre_ref, npost_ref, win_ref, wdw_ref,
                bdw_ref, lnw_ref, lnb_ref, wco_ref, wro_ref, wo_ref,
                h_ref, nc_ref, sout_ref,
                fb_ref, u_ref, y_ref, dec_ref, qd_ref, kd_ref, wb_ref, s_ref,
                aact_ref, gate_ref, bout_ref, xprev_ref, *, tiles_per_seq):
    g = pl.program_id(0)
    i = g % tiles_per_seq
    rows = x_ref.shape[1]

    @pl.when(g == 0)
    def _init_tables():
        aact_ref[...] = jnp.zeros(aact_ref.shape, BF16)
        gate_ref[...] = jnp.zeros(gate_ref.shape, F32)
        bout_ref[...] = jnp.zeros(bout_ref.shape, F32)
        xprev_ref[...] = jnp.zeros(xprev_ref.shape, F32)
        ii = lax.broadcasted_iota(jnp.int32, (CHUNK, CHUNK), 0)
        jj = lax.broadcasted_iota(jnp.int32, (CHUNK, CHUNK), 1)
        diff = (ii - jj).astype(F32)
        row_k = lax.broadcasted_iota(jnp.int32, (CHUNK, DK), 0).astype(F32)
        for h in range(HEADS):
            lg = LOG_GAMMA[h]
            dec_ref[h] = jnp.where(diff >= 0.0, jnp.exp(jnp.maximum(diff, 0.0) * lg), 0.0)
            qd_ref[h] = jnp.exp((row_k + 1.0) * lg)
            kd_ref[h] = jnp.exp((CHUNK - 1.0 - row_k) * lg)
        for j in range(CONV_WIDTH):
            wb_ref[j] = jnp.broadcast_to(wdw_ref[j], (SUBLANES, CONV_DIM))

    @pl.when(i == 0)
    def _start_sequence():
        fb_ref[0, 0:HALO_PAD, :] = jnp.zeros((HALO_PAD, CONV_DIM), F32)
        s_ref[...] = jnp.zeros(s_ref.shape, F32)

    x = x_ref[0]
    xn = _rms(x, npre_ref[...]).astype(BF16)
    h_ref[0], pc = _merge_out(
        aact_ref, gate_ref, bout_ref, xprev_ref, wco_ref, wo_ref, npost_ref,
        mid=lambda: _dot(xn, win_ref[:, C_CONV:C_CONV + 2 * CONV_DIM]))

    xprev_ref[...] = x

    u_ref[...] = pc[:, :CONV_DIM] * jax.nn.sigmoid(pc[:, CONV_DIM:])

    pqk = _dot(xn, win_ref[:, C_Q:C_Q + 2 * QK_W])
    pv = _dot(xn, win_ref[:, C_V:C_V + V_W]).astype(BF16)
    pg = _dot(xn, win_ref[:, C_G:C_G + V_W])

    first = HALO_PAD - HALO
    span = CONV_TILE + HALO_PAD - SUBLANES
    for t in range(rows // CONV_TILE):
        fb_ref[0, HALO_PAD:HALO_PAD + CONV_TILE, :] = u_ref[t * CONV_TILE:(t + 1) * CONV_TILE, :]
        for r in range(1, SUBLANES):
            fb_ref[r, 0:span, :] = fb_ref[0, r:r + span, :]
        for rb in range(CONV_TILE // CONV_ROWS):
            acc = jnp.broadcast_to(bdw_ref[...], (CONV_ROWS, CONV_DIM))
            for j in range(CONV_WIDTH):
                off = first + j
                base = rb * CONV_ROWS + off - off % SUBLANES
                slab = fb_ref[off % SUBLANES, base:base + CONV_ROWS, :]
                acc = acc + (slab.reshape(CONV_ROWS // SUBLANES, SUBLANES, CONV_DIM)
                             * wb_ref[j][None]).reshape(CONV_ROWS, CONV_DIM)
            r0 = t * CONV_TILE + rb * CONV_ROWS
            aact_ref[r0:r0 + CONV_ROWS, :] = _silu(
                _layer_norm(acc, lnw_ref[...], lnb_ref[...])).astype(BF16)
        fb_ref[0, 0:HALO_PAD, :] = fb_ref[0, CONV_TILE:CONV_TILE + HALO_PAD, :]

    cos2 = cos_ref[...]
    sin2 = sin_ref[...]

    def rot(t):
        return t * cos2 + pltpu.roll(t, DK // 2, 1) * sin2

    heads = range(HEADS)
    q_rot = [rot(pqk[:, h * DK:(h + 1) * DK]) for h in heads]
    q_bf = [q.astype(BF16) for q in q_rot]
    k_rot = [rot(pqk[:, QK_W + h * DK:QK_W + (h + 1) * DK]) * K_SCALE for h in heads]
    k_bf = [k.astype(BF16) for k in k_rot]
    g_act = [_silu(pg[:, h * DV:(h + 1) * DV]) for h in heads]

    n_chunks = rows // CHUNK
    n_fill = 2 * n_chunks
    gate_cols = 2 * D_MODEL // n_fill

    def gate_slice(f):
        gate_ref[:, f * gate_cols:(f + 1) * gate_cols] = _dot(
            xn, win_ref[:, C_GATES + f * gate_cols:C_GATES + (f + 1) * gate_cols])

    for c in range(n_chunks):
        rs = slice(c * CHUNK, (c + 1) * CHUNK)
        v_c = [pv[rs, h * DV:(h + 1) * DV] for h in heads]
        state = [s_ref[h] for h in heads]
        scores = []
        zeros = jnp.zeros((CHUNK, DK), BF16)
        for a in range(0, HEADS, 2):
            q_pair = jnp.concatenate([q_bf[a][rs], q_bf[a + 1][rs]], axis=1)
            k_pair = jnp.concatenate(
                [jnp.concatenate([k_bf[a][rs], zeros], axis=1),
                 jnp.concatenate([zeros, k_bf[a + 1][rs]], axis=1)], axis=0)
            s_pair = lax.dot_general(q_pair, k_pair, NT_DIMS, preferred_element_type=F32)
            scores += [s_pair[:, :CHUNK], s_pair[:, CHUNK:]]
        kv = [lax.dot_general((k_rot[h][rs] * kd_ref[h]).astype(BF16), v_c[h], TN_DIMS,
                              preferred_element_type=F32) for h in heads]
        gate_slice(2 * c)
        lhs = [jnp.concatenate([(scores[h] * dec_ref[h]).astype(BF16),
                                (q_rot[h][rs] * qd_ref[h]).astype(BF16)], axis=1)
               for h in heads]
        rhs = [jnp.concatenate([v_c[h], state[h].astype(BF16)], axis=0) for h in heads]
        out = [_dot(lhs[h], rhs[h]) for h in heads]
        gate_slice(2 * c + 1)
        for h in heads:
            s_ref[h] = GAMMA_CHUNK[h] * state[h] + kv[h]
            y_ref[rs, h * DV:(h + 1) * DV] = (g_act[h][rs] * _group_norm(out[h])).astype(BF16)
    bout_ref[...] = _dot(y_ref[...], wro_ref[...])

    @pl.when(i == tiles_per_seq - 1)
    def _emit_sequence_state():
        seq = g // tiles_per_seq
        for j in range(HALO):
            nc_ref[j, pl.ds(seq, 1), :] = fb_ref[0, HALO_PAD - HALO + j:HALO_PAD - HALO + j + 1, :]
        sout_ref[0] = s_ref[...]


def _mixer_prompt(x, cos2, sin2, npre, npost, win, wdw, bdw, lnw, lnb, wco, wro, wo):
    batch, seq, _ = x.shape
    rows = MIX_ROWS
    nt = seq // rows
    n_tiles = batch * nt

    def front(g):
        return jnp.minimum(g, n_tiles - 1)

    def back(g):
        return jnp.maximum(g - 1, 0)

    in_specs = [
        pl.BlockSpec((1, rows, D_MODEL), lambda g: (front(g) // nt, front(g) % nt, 0)),
        pl.BlockSpec((rows, DK), lambda g: (front(g) % nt, 0)),
        pl.BlockSpec((rows, DK), lambda g: (front(g) % nt, 0)),
        _const_spec((1, D_MODEL)),
        _const_spec((1, D_MODEL)),
        _const_spec((D_MODEL, IN_COLS)),
        _const_spec((CONV_WIDTH, 1, CONV_DIM)),
        _const_spec((1, CONV_DIM)),
        _const_spec((1, CONV_DIM)),
        _const_spec((1, CONV_DIM)),
        _const_spec((CONV_DIM, D_MODEL)),
        _const_spec((V_W, D_MODEL)),
        _const_spec((D_MODEL, D_MODEL)),
    ]
    out_specs = [
        pl.BlockSpec((1, rows, D_MODEL), lambda g: (back(g) // nt, back(g) % nt, 0)),
        pl.BlockSpec((HALO, batch, CONV_DIM), lambda g: (0, 0, 0)),
        pl.BlockSpec((1, HEADS, DK, DV), lambda g: (front(g) // nt, 0, 0, 0)),
    ]
    out_shape = [
        jax.ShapeDtypeStruct((batch, seq, D_MODEL), F32),
        jax.ShapeDtypeStruct((HALO, batch, CONV_DIM), F32),
        jax.ShapeDtypeStruct((batch, HEADS, DK, DV), F32),
    ]
    scratch = [
        pltpu.VMEM((SUBLANES, HALO_PAD + CONV_TILE, CONV_DIM), F32),
        pltpu.VMEM((rows, CONV_DIM), F32),
        pltpu.VMEM((rows, V_W), BF16),
        pltpu.VMEM((HEADS, CHUNK, CHUNK), F32),
        pltpu.VMEM((HEADS, CHUNK, DK), F32),
        pltpu.VMEM((HEADS, CHUNK, DK), F32),
        pltpu.VMEM((CONV_WIDTH, SUBLANES, CONV_DIM), F32),
        pltpu.VMEM((HEADS, DK, DV), F32),
        pltpu.VMEM((rows, CONV_DIM), BF16),
        pltpu.VMEM((rows, 2 * D_MODEL), F32),
        pltpu.VMEM((rows, D_MODEL), F32),
        pltpu.VMEM((rows, D_MODEL), F32),
    ]
    return pl.pallas_call(
        functools.partial(_mixer_prompt_kernel, tiles_per_seq=nt, n_tiles=n_tiles),
        grid=(n_tiles + 1,), in_specs=in_specs, out_specs=out_specs, out_shape=out_shape,
        scratch_shapes=scratch,
        compiler_params=pltpu.CompilerParams(
            dimension_semantics=("arbitrary",),
            vmem_limit_bytes=VMEM_LIMIT_BYTES),
        name="mixer_prompt",
    )(x, cos2, sin2, npre, npost, win, wdw, bdw, lnw, lnb, wco, wro, wo)


def _ffn_body(h, npre, npost, wup_ref, wdn_ref):
    hn = _rms(h, npre).astype(BF16)
    f = None
    for c in range(FFN_DIM // FFN_CHUNK):
        cols = slice(c * FFN_CHUNK, (c + 1) * FFN_CHUNK)
        up = _dot(hn, wup_ref[:, cols])
        act = jnp.square(jnp.maximum(up, 0.0)).astype(BF16)
        part = _dot(act, wdn_ref[cols, :])
        f = part if f is None else f + part
    return h + _rms(f, npost)


def _sample_state_body(blk, cache_ref, u_ref, wdw_ref, bdw_ref, qt_ref, kt_ref, v_ref, st_ref,
                       conv_ref, nc_ref, o_ref, so_ref):
    n_seq = qt_ref.shape[1]
    u = u_ref[...]
    acc = u * wdw_ref[HALO] + bdw_ref[...]
    for j in range(HALO):
        acc = acc + cache_ref[j] * wdw_ref[j]
    conv_ref[...] = acc
    nc_ref[0:HALO - 1] = cache_ref[1:HALO]
    nc_ref[HALO - 1] = u

    shift = (n_seq - blk * SAMPLE_BLOCK) % n_seq
    qt = pltpu.roll(qt_ref[...], shift, 1)
    kt = pltpu.roll(kt_ref[...], shift, 1)
    for s in range(SAMPLE_BLOCK):
        for h in range(HEADS):
            k_col = kt[h * DK:(h + 1) * DK, s:s + 1]
            q_col = qt[h * DK:(h + 1) * DK, s:s + 1]
            v_row = v_ref[s:s + 1, h * DV:(h + 1) * DV]
            new_state = GAMMA[h] * st_ref[s, h] + k_col * v_row
            so_ref[s, h] = new_state
            o_ref[s:s + 1, h * DV:(h + 1) * DV] = jnp.sum(new_state * q_col, axis=0,
                                                          keepdims=True)


def _ffn_state_kernel(h_ref, npre_ref, npost_ref, wup_ref, wdn_ref,
                      cache_ref, u_ref, wdw_ref, bdw_ref, qt_ref, kt_ref, v_ref, st_ref,
                      y_ref, conv_ref, nc_ref, o_ref, so_ref, *, steps_per_block):
    y_ref[...] = _ffn_body(h_ref[...], npre_ref[...], npost_ref[...], wup_ref, wdn_ref)
    _sample_state_body(pl.program_id(0) // steps_per_block, cache_ref, u_ref, wdw_ref, bdw_ref,
                       qt_ref, kt_ref, v_ref, st_ref, conv_ref, nc_ref, o_ref, so_ref)


def _ffn_state(h, npre, npost, wup, wdn, cache, u, wdw, bdw, qt, kt, v, state):
    n = h.shape[0]
    rows = FFN_ROWS
    steps = n // rows
    n_seq = u.shape[0]
    sb = SAMPLE_BLOCK
    spb = steps // (n_seq // sb)
    assert spb * (n_seq // sb) == steps
    in_specs = [
        pl.BlockSpec((rows, D_MODEL), lambda i: (i, 0)),
        _const_spec((1, D_MODEL)),
        _const_spec((1, D_MODEL)),
        _const_spec((D_MODEL, FFN_DIM)),
        _const_spec((FFN_DIM, D_MODEL)),
        pl.BlockSpec((HALO, sb, CONV_DIM), lambda i: (0, i // spb, 0)),
        pl.BlockSpec((sb, CONV_DIM), lambda i: (i // spb, 0)),
        _const_spec((CONV_WIDTH, 1, CONV_DIM)),
        _const_spec((1, CONV_DIM)),
        _const_spec((QK_W, n_seq)),
        _const_spec((QK_W, n_seq)),
        pl.BlockSpec((sb, V_W), lambda i: (i // spb, 0)),
        pl.BlockSpec((sb, HEADS, DK, DV), lambda i: (i // spb, 0, 0, 0)),
    ]
    out_specs = [
        pl.BlockSpec((rows, D_MODEL), lambda i: (i, 0)),
        pl.BlockSpec((sb, CONV_DIM), lambda i: (i // spb, 0)),
        pl.BlockSpec((HALO, sb, CONV_DIM), lambda i: (0, i // spb, 0)),
        pl.BlockSpec((sb, V_W), lambda i: (i // spb, 0)),
        pl.BlockSpec((sb, HEADS, DK, DV), lambda i: (i // spb, 0, 0, 0)),
    ]
    out_shape = [
        jax.ShapeDtypeStruct((n, D_MODEL), F32),
        jax.ShapeDtypeStruct((n_seq, CONV_DIM), F32),
        jax.ShapeDtypeStruct((HALO, n_seq, CONV_DIM), F32),
        jax.ShapeDtypeStruct((n_seq, V_W), F32),
        jax.ShapeDtypeStruct((n_seq, HEADS, DK, DV), F32),
    ]
    return pl.pallas_call(
        functools.partial(_ffn_state_kernel, steps_per_block=spb),
        grid=(steps,), in_specs=in_specs, out_specs=out_specs, out_shape=out_shape,
        compiler_params=pltpu.CompilerParams(
            dimension_semantics=("arbitrary",),
            vmem_limit_bytes=VMEM_LIMIT_BYTES),
        name="ffn_prompt_sample_state",
    )(h, npre, npost, wup, wdn, cache, u, wdw, bdw, qt, kt, v, state)


def _sample_pre_kernel(x_ref, cos_ref, sin_ref, npre_ref, win_ref, wco_ref, wro_ref, wo_ref,
                       wbf_ref, wco_bf_ref, wro_bf_ref, wo_bf_ref,
                       u_ref, qt_ref, kt_ref, v_ref, g_ref, ga_ref, gb_ref, xn_ref):
    j = pl.program_id(0)
    wco_bf_ref[...] = wco_ref[...].astype(BF16)
    wro_bf_ref[...] = wro_ref[...].astype(BF16)
    wo_bf_ref[...] = wo_ref[...].astype(BF16)

    @pl.when(j == 0)
    def _normalise():
        xn_ref[...] = _rms(x_ref[:, 0, :], npre_ref[...]).astype(BF16)

    w_slab = win_ref[...].astype(BF16)
    wbf_ref[...] = w_slab
    p = _dot(xn_ref[...], w_slab)

    @pl.when(j == C_CONV // PRE_COLS)
    def _glu():
        u_ref[...] = p[:, :CONV_DIM] * jax.nn.sigmoid(p[:, CONV_DIM:])

    @pl.when(j == C_Q // PRE_COLS)
    def _rotary():
        qkt = p.T
        cos_t = cos_ref[...]
        sin_t = sin_ref[...]
        half = DK // 2
        for g in range(2 * HEADS):
            x1 = qkt[g * DK:g * DK + half]
            x2 = qkt[g * DK + half:(g + 1) * DK]
            o1 = x1 * cos_t - x2 * sin_t
            o2 = x2 * cos_t + x1 * sin_t
            if g < HEADS:
                qt_ref[g * DK:g * DK + half, :] = o1
                qt_ref[g * DK + half:(g + 1) * DK, :] = o2
            else:
                k0 = (g - HEADS) * DK
                kt_ref[k0:k0 + half, :] = o1 * K_SCALE
                kt_ref[k0 + half:k0 + DK, :] = o2 * K_SCALE

    for out_ref, col in ((v_ref, C_V), (g_ref, C_G), (ga_ref, C_GATES),
                         (gb_ref, C_GATES + D_MODEL)):
        @pl.when(j == col // PRE_COLS)
        def _store(out_ref=out_ref):
            out_ref[...] = p


def _sample_pre(x, cos_t, sin_t, npre, win_f32, wco_f32, wro_f32, wo_f32):
    n = x.shape[0]
    assert all(c % PRE_COLS == 0 for c in (C_CONV, C_Q, C_V, C_G, C_GATES, IN_COLS))
    assert 2 * CONV_DIM == 2 * QK_W == V_W == D_MODEL == PRE_COLS
    steps = IN_COLS // PRE_COLS
    assert steps >= CAST_SLABS

    def whole(shape):
        zeros = (0,) * len(shape)
        return pl.BlockSpec(shape, lambda j: zeros)

    def row_slab(rows):
        return pl.BlockSpec((rows // CAST_SLABS, D_MODEL),
                            lambda j: (jnp.minimum(j, CAST_SLABS - 1), 0))

    side = [row_slab(CONV_DIM), row_slab(V_W), row_slab(D_MODEL)]
    in_specs = [
        whole((n, 1, D_MODEL)),
        whole((DK // 2, n)),
        whole((DK // 2, n)),
        whole((1, D_MODEL)),
        pl.BlockSpec((D_MODEL, PRE_COLS), lambda j: (0, j)),
    ] + side
    out_specs = [
        pl.BlockSpec((D_MODEL, PRE_COLS), lambda j: (0, j)),
    ] + side + [
        whole((n, CONV_DIM)),
        whole((QK_W, n)),
        whole((QK_W, n)),
        whole((n, V_W)),
        whole((n, V_W)),
        whole((n, D_MODEL)),
        whole((n, D_MODEL)),
    ]
    out_shape = [
        jax.ShapeDtypeStruct((D_MODEL, IN_COLS), BF16),
        jax.ShapeDtypeStruct((CONV_DIM, D_MODEL), BF16),
        jax.ShapeDtypeStruct((V_W, D_MODEL), BF16),
        jax.ShapeDtypeStruct((D_MODEL, D_MODEL), BF16),
        jax.ShapeDtypeStruct((n, CONV_DIM), F32),
        jax.ShapeDtypeStruct((QK_W, n), F32),
        jax.ShapeDtypeStruct((QK_W, n), F32),
        jax.ShapeDtypeStruct((n, V_W), F32),
        jax.ShapeDtypeStruct((n, V_W), F32),
        jax.ShapeDtypeStruct((n, D_MODEL), F32),
        jax.ShapeDtypeStruct((n, D_MODEL), F32),
    ]
    return pl.pallas_call(
        _sample_pre_kernel,
        grid=(steps,), in_specs=in_specs, out_specs=out_specs, out_shape=out_shape,
        scratch_shapes=[pltpu.VMEM((n, D_MODEL), BF16)],
        compiler_params=pltpu.CompilerParams(
            dimension_semantics=("arbitrary",),
            vmem_limit_bytes=VMEM_LIMIT_BYTES),
        name="sample_pre_cast_weights",
    )(x, cos_t, sin_t, npre, win_f32, wco_f32, wro_f32, wo_f32)


def _sample_post_kernel(x_ref, conv_ref, o_ref, g_ref, ga_ref, gb_ref, lnw_ref, lnb_ref,
                        wco_ref, wro_ref, wo_ref, npost_ref, nfpre_ref, nfpost_ref,
                        wup_ref, wdn_ref, y_ref, hres_ref, hn_ref, f_ref):
    c = pl.program_id(0)

    @pl.when(c == 0)
    def _merge():
        a_act = _silu(_layer_norm(conv_ref[...], lnw_ref[...], lnb_ref[...])).astype(BF16)
        a_out = _dot(a_act, wco_ref[...])
        b_out = jnp.zeros_like(a_out)
        for h in range(HEADS):
            cols = slice(h * DV, (h + 1) * DV)
            yh = (_silu(g_ref[:, cols]) * _group_norm(o_ref[:, cols])).astype(BF16)
            b_out = b_out + _dot(yh, wro_ref[cols, :])
        merged = jax.nn.sigmoid(ga_ref[...]) * a_out + jax.nn.sigmoid(gb_ref[...]) * b_out
        m = _dot(merged.astype(BF16), wo_ref[...])
        hres = x_ref[:, 0, :] + _rms(m, npost_ref[...])
        hres_ref[...] = hres
        hn_ref[...] = _rms(hres, nfpre_ref[...]).astype(BF16)
        f_ref[...] = jnp.zeros(f_ref.shape, F32)

    up = _dot(hn_ref[...], wup_ref[...])
    act = jnp.square(jnp.maximum(up, 0.0)).astype(BF16)
    f_ref[...] += _dot(act, wdn_ref[...])

    @pl.when(c == pl.num_programs(0) - 1)
    def _finish():
        y_ref[:, 0, :] = hres_ref[...] + _rms(f_ref[...], nfpost_ref[...])


def _sample_post(x, conv, o, g, ga, gb, lnw, lnb, wco, wro, wo, npost, nfpre, nfpost, wup, wdn):
    n = x.shape[0]

    def whole(a):
        zeros = (0,) * a.ndim
        return pl.BlockSpec(a.shape, lambda c: zeros)

    resident = (x, conv, o, g, ga, gb, lnw, lnb, wco, wro, wo, npost, nfpre, nfpost)
    in_specs = [whole(a) for a in resident] + [
        pl.BlockSpec((D_MODEL, FFN_CHUNK), lambda c: (0, c)),
        pl.BlockSpec((FFN_CHUNK, D_MODEL), lambda c: (c, 0)),
    ]
    return pl.pallas_call(
        _sample_post_kernel,
        grid=(FFN_DIM // FFN_CHUNK,), in_specs=in_specs,
        out_specs=pl.BlockSpec(x.shape, lambda c: (0, 0, 0)),
        out_shape=jax.ShapeDtypeStruct(x.shape, F32),
        scratch_shapes=[
            pltpu.VMEM((n, D_MODEL), F32),
            pltpu.VMEM((n, D_MODEL), BF16),
            pltpu.VMEM((n, D_MODEL), F32),
        ],
        compiler_params=pltpu.CompilerParams(
            dimension_semantics=("arbitrary",),
            vmem_limit_bytes=VMEM_LIMIT_BYTES),
        name="sample_post",
    )(x, conv, o, g, ga, gb, lnw, lnb, wco, wro, wo, npost, nfpre, nfpost, wup, wdn)


def _rope_angles(pos):
    half = DK // 2
    freqs = 1.0 / (ROPE_BASE ** jnp.linspace(0.0, 1.0, half, dtype=F32))
    return pos[:, None] * freqs[None, :]


def kernel(x_prompt, x_sample, cache_conv, state_ret, norm_mix_pre, norm_mix_post, w_in, w_dw, b_dw, conv_ln_w, conv_ln_b, w_conv_out, w_ret_out, w_o, norm_ffn_pre, norm_ffn_post, w_ffn_up, w_ffn_down):
    batch, seq, _ = x_prompt.shape
    n_seq, dec_seq, _ = x_sample.shape
    depth = w_in.shape[0]
    assert dec_seq == 1 and seq % MIX_ROWS == 0 and (batch * seq) % FFN_ROWS == 0
    assert n_seq % SAMPLE_BLOCK == 0

    assert seq % ROPE_FINE == 0
    def both_halves(t):
        return jnp.concatenate([t, t], axis=-1)

    ang_a = both_halves(_rope_angles(ROPE_FINE * jnp.arange(seq // ROPE_FINE, dtype=F32)))
    ang_b = both_halves(_rope_angles(jnp.arange(ROPE_FINE, dtype=F32)))
    cos_a, sin_a = jnp.cos(ang_a)[:, None, :], jnp.sin(ang_a)[:, None, :]
    cos_b, sin_b = jnp.cos(ang_b)[None, :, :], jnp.sin(ang_b)[None, :, :]
    sign = jnp.where(jnp.arange(DK) < DK // 2, -1.0, 1.0).astype(F32)
    cos_p = (cos_a * cos_b - sin_a * sin_b).reshape(seq, DK)
    sin_p = ((sin_a * cos_b + cos_a * sin_b) * sign).reshape(seq, DK)
    ang_s = _rope_angles(PAST_LEN + jnp.arange(dec_seq, dtype=F32))
    cos_s = jnp.broadcast_to(jnp.cos(ang_s).T, (DK // 2, n_seq))
    sin_s = jnp.broadcast_to(jnp.sin(ang_s).T, (DK // 2, n_seq))

    xp = x_prompt
    xs = x_sample
    conv_p, ret_p, conv_s, ret_s = [], [], [], []
    for l in range(depth):
        npre = norm_mix_pre[l][None]
        npost = norm_mix_post[l][None]
        nfpre = norm_ffn_pre[l][None]
        nfpost = norm_ffn_post[l][None]
        wdw = jnp.transpose(w_dw, (1, 0, 2))[:, l:l + 1, :]
        bdw = b_dw[l][None]
        lnw = conv_ln_w[l][None]
        lnb = conv_ln_b[l][None]
        wup = w_ffn_up[l].astype(BF16)
        wdn = w_ffn_down[l].astype(BF16)

        win, wco, wro, wo, u, qt, kt, v, g, ga, gb = _sample_pre(
            xs, cos_s, sin_s, npre, w_in[l], w_conv_out[l], w_ret_out[l], w_o[l])
        cache_t = jnp.transpose(cache_conv[l], (1, 0, 2))
        h_p, nc_p, s_p = _mixer_prompt(xp, cos_p, sin_p, npre, npost, win, wdw, bdw, lnw, lnb,
                                       wco, wro, wo)
        y_p, conv, nc_t, o, s_s = _ffn_state(h_p.reshape(batch * seq, D_MODEL), nfpre, nfpost,
                                             wup, wdn, cache_t, u, wdw, bdw, qt, kt, v,
                                             state_ret[l])
        xp = y_p.reshape(batch, seq, D_MODEL)
        nc_s = jnp.transpose(nc_t, (1, 0, 2))
        xs = _sample_post(xs, conv, o, g, ga, gb, lnw, lnb, wco, wro, wo, npost, nfpre, nfpost,
                          wup, wdn)
        conv_p.append(jnp.transpose(nc_p, (1, 0, 2)))
        ret_p.append(s_p)
        conv_s.append(nc_s)
        ret_s.append(s_s)

    return (xp, xs, jnp.stack(conv_p), jnp.stack(ret_p),
            jnp.stack(conv_s), jnp.stack(ret_s))
```

```python
import functools
import math

import jax
import jax.numpy as jnp
from jax import lax
from jax.experimental import pallas as pl
from jax.experimental.pallas import tpu as pltpu

F32 = jnp.float32
BF16 = jnp.bfloat16

D_MODEL = 1024
CONV_DIM = 512
CONV_WIDTH = 31
HALO = CONV_WIDTH - 1
HEADS = 4
DK = 128
DV = 256
QK_W = HEADS * DK
V_W = HEADS * DV
CHUNK = 128
FFN_DIM = 4 * D_MODEL
EPS = 1e-6
ROPE_BASE = 10000.0
PAST_LEN = 16384
K_SCALE = DK ** -0.5

C_CONV = 0
C_Q = 2 * CONV_DIM
C_K = C_Q + QK_W
C_V = C_K + QK_W
C_G = C_V + V_W
C_GATES = C_G + V_W
IN_COLS = C_GATES + 2 * D_MODEL

LOG_GAMMA = tuple(math.log1p(-(2.0 ** (-5 - h))) for h in range(HEADS))
GAMMA = tuple(math.exp(lg) for lg in LOG_GAMMA)
GAMMA_CHUNK = tuple(math.exp(CHUNK * lg) for lg in LOG_GAMMA)

VMEM_LIMIT_BYTES = 61 * 1024 * 1024
SUBLANES = 8
HALO_PAD = 32
MIX_ROWS = 512
FFN_ROWS = 1024
FFN_CHUNK = 1024
CONV_TILE = 256
CONV_ROWS = 32
SAMPLE_BLOCK = 8
PRE_COLS = 1024
CAST_SLABS = 4
ROPE_FINE = 128

NT_DIMS = (((1,), (1,)), ((), ()))
TN_DIMS = (((0,), (0,)), ((), ()))


def _dot(a, b):
    return jnp.dot(a, b, preferred_element_type=F32)


def _rms(x, w):
    return x * lax.rsqrt(jnp.mean(x * x, axis=-1, keepdims=True) + EPS) * w


def _layer_norm(x, w, b):
    mu = jnp.mean(x, axis=-1, keepdims=True)
    xc = x - mu
    return xc * lax.rsqrt(jnp.mean(xc * xc, axis=-1, keepdims=True) + EPS) * w + b


def _silu(x):
    return x * jax.nn.sigmoid(x)


def _group_norm(o):
    return o * lax.rsqrt(jnp.mean(o * o, axis=-1, keepdims=True) + EPS)


def _const_spec(shape):
    zeros = (0,) * len(shape)
    return pl.BlockSpec(shape, lambda *_: zeros, pipeline_mode=pl.Buffered(1))


def _merge_out(aact_ref, gate_ref, bout_ref, xprev_ref, wco_ref, wo_ref, npost_ref, mid=None):
    a_out = _dot(aact_ref[...], wco_ref[...])
    mid_out = mid() if mid is not None else None
    merged = (jax.nn.sigmoid(gate_ref[:, :D_MODEL]) * a_out
              + jax.nn.sigmoid(gate_ref[:, D_MODEL:]) * bout_ref[...])
    m = _dot(merged.astype(BF16), wo_ref[...])
    return xprev_ref[...] + _rms(m, npost_ref[...]), mid_out


def _mixer_prompt_kernel(x_ref, cos_ref, sin_ref, npre_ref, npost_ref, win_ref, wdw_ref,
                         bdw_ref, lnw_ref, lnb_ref, wco_ref, wro_ref, wo_ref, wup_ref, wdn_ref,
                         h_ref, nc_ref, sout_ref, wup_bf_ref, wdn_bf_ref,
                         fb_ref, u_ref, y_ref, dec_ref, qd_ref, kd_ref, wb_ref, s_ref,
                         aact_ref, gate_ref, bout_ref, xprev_ref, *, tiles_per_seq, n_tiles):
    g = pl.program_id(0)

    @pl.when(g < n_tiles)
    def _pipelined_step():
        wup_bf_ref[...] = wup_ref[...].astype(BF16)
        wdn_bf_ref[...] = wdn_ref[...].astype(BF16)
        _mixer_step(x_ref, cos_ref, sin_ref, npre_ref, npost_ref, win_ref, wdw_ref,
                    bdw_ref, lnw_ref, lnb_ref, wco_ref, wro_ref, wo_ref,
                    h_ref, nc_ref, sout_ref,
                    fb_ref, u_ref, y_ref, dec_ref, qd_ref, kd_ref, wb_ref, s_ref,
                    aact_ref, gate_ref, bout_ref, xprev_ref, tiles_per_seq=tiles_per_seq)

    @pl.when(g == n_tiles)
    def _drain():
        h_ref[0], _ = _merge_out(aact_ref, gate_ref, bout_ref, xprev_ref, wco_ref, wo_ref,
                                 npost_ref)


def _mixer_step(x_ref, cos_ref, sin_ref, npre_ref, npost_ref, win_ref, wdw_ref,
                bdw_ref, lnw_ref, lnb_ref, wco_ref, wro_ref, wo_ref,
                h_ref, nc_ref, sout_ref,
                fb_ref, u_ref, y_ref, dec_ref, qd_ref, kd_ref, wb_ref, s_ref,
                aact_ref, gate_ref, bout_ref, xprev_ref, *, tiles_per_seq):
    g = pl.program_id(0)
    i = g % tiles_per_seq
    rows = x_ref.shape[1]

    @pl.when(g == 0)
    def _init_tables():
        aact_ref[...] = jnp.zeros(aact_ref.shape, BF16)
        gate_ref[...] = jnp.zeros(gate_ref.shape, F32)
        bout_ref[...] = jnp.zeros(bout_ref.shape, F32)
        xprev_ref[...] = jnp.zeros(xprev_ref.shape, F32)
        ii = lax.broadcasted_iota(jnp.int32, (CHUNK, CHUNK), 0)
        jj = lax.broadcasted_iota(jnp.int32, (CHUNK, CHUNK), 1)
        diff = (ii - jj).astype(F32)
        row_k = lax.broadcasted_iota(jnp.int32, (CHUNK, DK), 0).astype(F32)
        for h in range(HEADS):
            lg = LOG_GAMMA[h]
            dec_ref[h] = jnp.where(diff >= 0.0, jnp.exp(jnp.maximum(diff, 0.0) * lg), 0.0)
            qd_ref[h] = jnp.exp((row_k + 1.0) * lg)
            kd_ref[h] = jnp.exp((CHUNK - 1.0 - row_k) * lg)
        for j in range(CONV_WIDTH):
            wb_ref[j] = jnp.broadcast_to(wdw_ref[j], (SUBLANES, CONV_DIM))

    @pl.when(i == 0)
    def _start_sequence():
        fb_ref[0, 0:HALO_PAD, :] = jnp.zeros((HALO_PAD, CONV_DIM), F32)
        s_ref[...] = jnp.zeros(s_ref.shape, F32)

    x = x_ref[0]
    xn = _rms(x, npre_ref[...]).astype(BF16)
    h_ref[0], pc = _merge_out(
        aact_ref, gate_ref, bout_ref, xprev_ref, wco_ref, wo_ref, npost_ref,
        mid=lambda: _dot(xn, win_ref[:, C_CONV:C_CONV + 2 * CONV_DIM]))

    xprev_ref[...] = x

    u_ref[...] = pc[:, :CONV_DIM] * jax.nn.sigmoid(pc[:, CONV_DIM:])

    pqk = _dot(xn, win_ref[:, C_Q:C_Q + 2 * QK_W])
    pv = _dot(xn, win_ref[:, C_V:C_V + V_W]).astype(BF16)
    pg = _dot(xn, win_ref[:, C_G:C_G + V_W])

    first = HALO_PAD - HALO
    span = CONV_TILE + HALO_PAD - SUBLANES
    for t in range(rows // CONV_TILE):
        fb_ref[0, HALO_PAD:HALO_PAD + CONV_TILE, :] = u_ref[t * CONV_TILE:(t + 1) * CONV_TILE, :]
        for r in range(1, SUBLANES):
            fb_ref[r, 0:span, :] = fb_ref[0, r:r + span, :]
        for rb in range(CONV_TILE // CONV_ROWS):
            acc = jnp.broadcast_to(bdw_ref[...], (CONV_ROWS, CONV_DIM))
            for j in range(CONV_WIDTH):
                off = first + j
                base = rb * CONV_ROWS + off - off % SUBLANES
                slab = fb_ref[off % SUBLANES, base:base + CONV_ROWS, :]
                acc = acc + (slab.reshape(CONV_ROWS // SUBLANES, SUBLANES, CONV_DIM)
                             * wb_ref[j][None]).reshape(CONV_ROWS, CONV_DIM)
            r0 = t * CONV_TILE + rb * CONV_ROWS
            aact_ref[r0:r0 + CONV_ROWS, :] = _silu(
                _layer_norm(acc, lnw_ref[...], lnb_ref[...])).astype(BF16)
        fb_ref[0, 0:HALO_PAD, :] = fb_ref[0, CONV_TILE:CONV_TILE + HALO_PAD, :]

    cos2 = cos_ref[...]
    sin2 = sin_ref[...]

    def rot(t):
        return t * cos2 + pltpu.roll(t, DK // 2, 1) * sin2

    heads = range(HEADS)
    q_rot = [rot(pqk[:, h * DK:(h + 1) * DK]) for h in heads]
    q_bf = [q.astype(BF16) for q in q_rot]
    k_rot = [rot(pqk[:, QK_W + h * DK:QK_W + (h + 1) * DK]) * K_SCALE for h in heads]
    k_bf = [k.astype(BF16) for k in k_rot]
    g_act = [_silu(pg[:, h * DV:(h + 1) * DV]) for h in heads]

    n_chunks = rows // CHUNK
    n_fill = 2 * n_chunks
    gate_cols = 2 * D_MODEL // n_fill

    def gate_slice(f):
        gate_ref[:, f * gate_cols:(f + 1) * gate_cols] = _dot(
            xn, win_ref[:, C_GATES + f * gate_cols:C_GATES + (f + 1) * gate_cols])

    for c in range(n_chunks):
        rs = slice(c * CHUNK, (c + 1) * CHUNK)
        v_c = [pv[rs, h * DV:(h + 1) * DV] for h in heads]
        state = [s_ref[h] for h in heads]
        scores = []
        zeros = jnp.zeros((CHUNK, DK), BF16)
        for a in range(0, HEADS, 2):
            q_pair = jnp.concatenate([q_bf[a][rs], q_bf[a + 1][rs]], axis=1)
            k_pair = jnp.concatenate(
                [jnp.concatenate([k_bf[a][rs], zeros], axis=1),
                 jnp.concatenate([zeros, k_bf[a + 1][rs]], axis=1)], axis=0)
            s_pair = lax.dot_general(q_pair, k_pair, NT_DIMS, preferred_element_type=F32)
            scores += [s_pair[:, :CHUNK], s_pair[:, CHUNK:]]
        kv = [lax.dot_general((k_rot[h][rs] * kd_ref[h]).astype(BF16), v_c[h], TN_DIMS,
                              preferred_element_type=F32) for h in heads]
        gate_slice(2 * c)
        lhs = [jnp.concatenate([(scores[h] * dec_ref[h]).astype(BF16),
                                (q_rot[h][rs] * qd_ref[h]).astype(BF16)], axis=1)
               for h in heads]
        rhs = [jnp.concatenate([v_c[h], state[h].astype(BF16)], axis=0) for h in heads]
        out = [_dot(lhs[h], rhs[h]) for h in heads]
        gate_slice(2 * c + 1)
        for h in heads:
            s_ref[h] = GAMMA_CHUNK[h] * state[h] + kv[h]
            y_ref[rs, h * DV:(h + 1) * DV] = (g_act[h][rs] * _group_norm(out[h])).astype(BF16)
    bout_ref[...] = _dot(y_ref[...], wro_ref[...])

    @pl.when(i == tiles_per_seq - 1)
    def _emit_sequence_state():
        seq = g // tiles_per_seq
        for j in range(HALO):
            nc_ref[j, pl.ds(seq, 1), :] = fb_ref[0, HALO_PAD - HALO + j:HALO_PAD - HALO + j + 1, :]
        sout_ref[0] = s_ref[...]


def _mixer_prompt(x, cos2, sin2, npre, npost, win, wdw, bdw, lnw, lnb, wco, wro, wo,
                  wup_f32, wdn_f32):
    batch, seq, _ = x.shape
    rows = MIX_ROWS
    nt = seq // rows
    n_tiles = batch * nt
    assert D_MODEL % n_tiles == 0 and FFN_DIM % n_tiles == 0
    up_rows, dn_rows = D_MODEL // n_tiles, FFN_DIM // n_tiles

    def front(g):
        return jnp.minimum(g, n_tiles - 1)

    def back(g):
        return jnp.maximum(g - 1, 0)

    side = [pl.BlockSpec((up_rows, FFN_DIM), lambda g: (front(g), 0)),
            pl.BlockSpec((dn_rows, D_MODEL), lambda g: (front(g), 0))]

    in_specs = [
        pl.BlockSpec((1, rows, D_MODEL), lambda g: (front(g) // nt, front(g) % nt, 0)),
        pl.BlockSpec((rows, DK), lambda g: (front(g) % nt, 0)),
        pl.BlockSpec((rows, DK), lambda g: (front(g) % nt, 0)),
        _const_spec((1, D_MODEL)),
        _const_spec((1, D_MODEL)),
        _const_spec((D_MODEL, IN_COLS)),
        _const_spec((CONV_WIDTH, 1, CONV_DIM)),
        _const_spec((1, CONV_DIM)),
        _const_spec((1, CONV_DIM)),
        _const_spec((1, CONV_DIM)),
        _const_spec((CONV_DIM, D_MODEL)),
        _const_spec((V_W, D_MODEL)),
        _const_spec((D_MODEL, D_MODEL)),
    ] + side
    out_specs = [
        pl.BlockSpec((1, rows, D_MODEL), lambda g: (back(g) // nt, back(g) % nt, 0)),
        pl.BlockSpec((HALO, batch, CONV_DIM), lambda g: (0, 0, 0)),
        pl.BlockSpec((1, HEADS, DK, DV), lambda g: (front(g) // nt, 0, 0, 0)),
    ] + side
    out_shape = [
        jax.ShapeDtypeStruct((batch, seq, D_MODEL), F32),
        jax.ShapeDtypeStruct((HALO, batch, CONV_DIM), F32),
        jax.ShapeDtypeStruct((batch, HEADS, DK, DV), F32),
        jax.ShapeDtypeStruct((D_MODEL, FFN_DIM), BF16),
        jax.ShapeDtypeStruct((FFN_DIM, D_MODEL), BF16),
    ]
    scratch = [
        pltpu.VMEM((SUBLANES, HALO_PAD + CONV_TILE, CONV_DIM), F32),
        pltpu.VMEM((rows, CONV_DIM), F32),
        pltpu.VMEM((rows, V_W), BF16),
        pltpu.VMEM((HEADS, CHUNK, CHUNK), F32),
        pltpu.VMEM((HEADS, CHUNK, DK), F32),
        pltpu.VMEM((HEADS, CHUNK, DK), F32),
        pltpu.VMEM((CONV_WIDTH, SUBLANES, CONV_DIM), F32),
        pltpu.VMEM((HEADS, DK, DV), F32),
        pltpu.VMEM((rows, CONV_DIM), BF16),
        pltpu.VMEM((rows, 2 * D_MODEL), F32),
        pltpu.VMEM((rows, D_MODEL), F32),
        pltpu.VMEM((rows, D_MODEL), F32),
    ]
    return pl.pallas_call(
        functools.partial(_mixer_prompt_kernel, tiles_per_seq=nt, n_tiles=n_tiles),
        grid=(n_tiles + 1,), in_specs=in_specs, out_specs=out_specs, out_shape=out_shape,
        scratch_shapes=scratch,
        compiler_params=pltpu.CompilerParams(
            dimension_semantics=("arbitrary",),
            vmem_limit_bytes=VMEM_LIMIT_BYTES),
        name="mixer_prompt",
    )(x, cos2, sin2, npre, npost, win, wdw, bdw, lnw, lnb, wco, wro, wo, wup_f32, wdn_f32)


def _ffn_body(h, npre, npost, wup_ref, wdn_ref):
    hn = _rms(h, npre).astype(BF16)
    f = None
    for c in range(FFN_DIM // FFN_CHUNK):
        cols = slice(c * FFN_CHUNK, (c + 1) * FFN_CHUNK)
        up = _dot(hn, wup_ref[:, cols])
        act = jnp.square(jnp.maximum(up, 0.0)).astype(BF16)
        part = _dot(act, wdn_ref[cols, :])
        f = part if f is None else f + part
    return h + _rms(f, npost)


def _sample_state_body(blk, cache_ref, u_ref, wdw_ref, bdw_ref, qt_ref, kt_ref, v_ref, st_ref,
                       conv_ref, nc_ref, o_ref, so_ref):
    n_seq = qt_ref.shape[1]
    u = u_ref[...]
    acc = u * wdw_ref[HALO] + bdw_ref[...]
    for j in range(HALO):
        acc = acc + cache_ref[j] * wdw_ref[j]
    conv_ref[...] = acc
    nc_ref[0:HALO - 1] = cache_ref[1:HALO]
    nc_ref[HALO - 1] = u

    shift = (n_seq - blk * SAMPLE_BLOCK) % n_seq
    qt = pltpu.roll(qt_ref[...], shift, 1)
    kt = pltpu.roll(kt_ref[...], shift, 1)
    for s in range(SAMPLE_BLOCK):
        for h in range(HEADS):
            k_col = kt[h * DK:(h + 1) * DK, s:s + 1]
            q_col = qt[h * DK:(h + 1) * DK, s:s + 1]
            v_row = v_ref[s:s + 1, h * DV:(h + 1) * DV]
            new_state = GAMMA[h] * st_ref[s, h] + k_col * v_row
            so_ref[s, h] = new_state
            o_ref[s:s + 1, h * DV:(h + 1) * DV] = jnp.sum(new_state * q_col, axis=0,
                                                          keepdims=True)


def _ffn_state_kernel(h_ref, npre_ref, npost_ref, wup_ref, wdn_ref,
                      cache_ref, u_ref, wdw_ref, bdw_ref, qt_ref, kt_ref, v_ref, st_ref,
                      y_ref, conv_ref, nc_ref, o_ref, so_ref, *, steps_per_block):
    y_ref[...] = _ffn_body(h_ref[...], npre_ref[...], npost_ref[...], wup_ref, wdn_ref)
    _sample_state_body(pl.program_id(0) // steps_per_block, cache_ref, u_ref, wdw_ref, bdw_ref,
                       qt_ref, kt_ref, v_ref, st_ref, conv_ref, nc_ref, o_ref, so_ref)


def _ffn_state(h, npre, npost, wup, wdn, cache, u, wdw, bdw, qt, kt, v, state):
    n = h.shape[0]
    rows = FFN_ROWS
    steps = n // rows
    n_seq = u.shape[0]
    sb = SAMPLE_BLOCK
    spb = steps // (n_seq // sb)
    assert spb * (n_seq // sb) == steps
    in_specs = [
        pl.BlockSpec((rows, D_MODEL), lambda i: (i, 0)),
        _const_spec((1, D_MODEL)),
        _const_spec((1, D_MODEL)),
        _const_spec((D_MODEL, FFN_DIM)),
        _const_spec((FFN_DIM, D_MODEL)),
        pl.BlockSpec((HALO, sb, CONV_DIM), lambda i: (0, i // spb, 0)),
        pl.BlockSpec((sb, CONV_DIM), lambda i: (i // spb, 0)),
        _const_spec((CONV_WIDTH, 1, CONV_DIM)),
        _const_spec((1, CONV_DIM)),
        _const_spec((QK_W, n_seq)),
        _const_spec((QK_W, n_seq)),
        pl.BlockSpec((sb, V_W), lambda i: (i // spb, 0)),
        pl.BlockSpec((sb, HEADS, DK, DV), lambda i: (i // spb, 0, 0, 0)),
    ]
    out_specs = [
        pl.BlockSpec((rows, D_MODEL), lambda i: (i, 0)),
        pl.BlockSpec((sb, CONV_DIM), lambda i: (i // spb, 0)),
        pl.BlockSpec((HALO, sb, CONV_DIM), lambda i: (0, i // spb, 0)),
        pl.BlockSpec((sb, V_W), lambda i: (i // spb, 0)),
        pl.BlockSpec((sb, HEADS, DK, DV), lambda i: (i // spb, 0, 0, 0)),
    ]
    out_shape = [
        jax.ShapeDtypeStruct((n, D_MODEL), F32),
        jax.ShapeDtypeStruct((n_seq, CONV_DIM), F32),
        jax.ShapeDtypeStruct((HALO, n_seq, CONV_DIM), F32),
        jax.ShapeDtypeStruct((n_seq, V_W), F32),
        jax.ShapeDtypeStruct((n_seq, HEADS, DK, DV), F32),
    ]
    return pl.pallas_call(
        functools.partial(_ffn_state_kernel, steps_per_block=spb),
        grid=(steps,), in_specs=in_specs, out_specs=out_specs, out_shape=out_shape,
        compiler_params=pltpu.CompilerParams(
            dimension_semantics=("arbitrary",),
            vmem_limit_bytes=VMEM_LIMIT_BYTES),
        name="ffn_prompt_sample_state",
    )(h, npre, npost, wup, wdn, cache, u, wdw, bdw, qt, kt, v, state)


def _sample_pre_kernel(x_ref, cos_ref, sin_ref, npre_ref, win_ref, wco_ref, wro_ref, wo_ref,
                       wbf_ref, wco_bf_ref, wro_bf_ref, wo_bf_ref,
                       u_ref, qt_ref, kt_ref, v_ref, g_ref, ga_ref, gb_ref, xn_ref):
    j = pl.program_id(0)
    wco_bf_ref[...] = wco_ref[...].astype(BF16)
    wro_bf_ref[...] = wro_ref[...].astype(BF16)
    wo_bf_ref[...] = wo_ref[...].astype(BF16)

    @pl.when(j == 0)
    def _normalise():
        xn_ref[...] = _rms(x_ref[:, 0, :], npre_ref[...]).astype(BF16)

    w_slab = win_ref[...].astype(BF16)
    wbf_ref[...] = w_slab
    p = _dot(xn_ref[...], w_slab)

    @pl.when(j == C_CONV // PRE_COLS)
    def _glu():
        u_ref[...] = p[:, :CONV_DIM] * jax.nn.sigmoid(p[:, CONV_DIM:])

    @pl.when(j == C_Q // PRE_COLS)
    def _rotary():
        qkt = p.T
        cos_t = cos_ref[...]
        sin_t = sin_ref[...]
        half = DK // 2
        for g in range(2 * HEADS):
            x1 = qkt[g * DK:g * DK + half]
            x2 = qkt[g * DK + half:(g + 1) * DK]
            o1 = x1 * cos_t - x2 * sin_t
            o2 = x2 * cos_t + x1 * sin_t
            if g < HEADS:
                qt_ref[g * DK:g * DK + half, :] = o1
                qt_ref[g * DK + half:(g + 1) * DK, :] = o2
            else:
                k0 = (g - HEADS) * DK
                kt_ref[k0:k0 + half, :] = o1 * K_SCALE
                kt_ref[k0 + half:k0 + DK, :] = o2 * K_SCALE

    for out_ref, col in ((v_ref, C_V), (g_ref, C_G), (ga_ref, C_GATES),
                         (gb_ref, C_GATES + D_MODEL)):
        @pl.when(j == col // PRE_COLS)
        def _store(out_ref=out_ref):
            out_ref[...] = p


def _sample_pre(x, cos_t, sin_t, npre, win_f32, wco_f32, wro_f32, wo_f32):
    n = x.shape[0]
    assert all(c % PRE_COLS == 0 for c in (C_CONV, C_Q, C_V, C_G, C_GATES, IN_COLS))
    assert 2 * CONV_DIM == 2 * QK_W == V_W == D_MODEL == PRE_COLS
    steps = IN_COLS // PRE_COLS
    assert steps >= CAST_SLABS

    def whole(shape):
        zeros = (0,) * len(shape)
        return pl.BlockSpec(shape, lambda j: zeros)

    def row_slab(rows):
        return pl.BlockSpec((rows // CAST_SLABS, D_MODEL),
                            lambda j: (jnp.minimum(j, CAST_SLABS - 1), 0))

    side = [row_slab(CONV_DIM), row_slab(V_W), row_slab(D_MODEL)]
    in_specs = [
        whole((n, 1, D_MODEL)),
        whole((DK // 2, n)),
        whole((DK // 2, n)),
        whole((1, D_MODEL)),
        pl.BlockSpec((D_MODEL, PRE_COLS), lambda j: (0, j)),
    ] + side
    out_specs = [
        pl.BlockSpec((D_MODEL, PRE_COLS), lambda j: (0, j)),
    ] + side + [
        whole((n, CONV_DIM)),
        whole((QK_W, n)),
        whole((QK_W, n)),
        whole((n, V_W)),
        whole((n, V_W)),
        whole((n, D_MODEL)),
        whole((n, D_MODEL)),
    ]
    out_shape = [
        jax.ShapeDtypeStruct((D_MODEL, IN_COLS), BF16),
        jax.ShapeDtypeStruct((CONV_DIM, D_MODEL), BF16),
        jax.ShapeDtypeStruct((V_W, D_MODEL), BF16),
        jax.ShapeDtypeStruct((D_MODEL, D_MODEL), BF16),
        jax.ShapeDtypeStruct((n, CONV_DIM), F32),
        jax.ShapeDtypeStruct((QK_W, n), F32),
        jax.ShapeDtypeStruct((QK_W, n), F32),
        jax.ShapeDtypeStruct((n, V_W), F32),
        jax.ShapeDtypeStruct((n, V_W), F32),
        jax.ShapeDtypeStruct((n, D_MODEL), F32),
        jax.ShapeDtypeStruct((n, D_MODEL), F32),
    ]
    return pl.pallas_call(
        _sample_pre_kernel,
        grid=(steps,), in_specs=in_specs, out_specs=out_specs, out_shape=out_shape,
        scratch_shapes=[pltpu.VMEM((n, D_MODEL), BF16)],
        compiler_params=pltpu.CompilerParams(
            dimension_semantics=("arbitrary",),
            vmem_limit_bytes=VMEM_LIMIT_BYTES),
        name="sample_pre_cast_weights",
    )(x, cos_t, sin_t, npre, win_f32, wco_f32, wro_f32, wo_f32)


def _sample_post_kernel(x_ref, conv_ref, o_ref, g_ref, ga_ref, gb_ref, lnw_ref, lnb_ref,
                        wco_ref, wro_ref, wo_ref, npost_ref, nfpre_ref, nfpost_ref,
                        wup_ref, wdn_ref, y_ref, hres_ref, hn_ref, f_ref):
    c = pl.program_id(0)

    @pl.when(c == 0)
    def _merge():
        a_act = _silu(_layer_norm(conv_ref[...], lnw_ref[...], lnb_ref[...])).astype(BF16)
        a_out = _dot(a_act, wco_ref[...])
        b_out = jnp.zeros_like(a_out)
        for h in range(HEADS):
            cols = slice(h * DV, (h + 1) * DV)
            yh = (_silu(g_ref[:, cols]) * _group_norm(o_ref[:, cols])).astype(BF16)
            b_out = b_out + _dot(yh, wro_ref[cols, :])
        merged = jax.nn.sigmoid(ga_ref[...]) * a_out + jax.nn.sigmoid(gb_ref[...]) * b_out
        m = _dot(merged.astype(BF16), wo_ref[...])
        hres = x_ref[:, 0, :] + _rms(m, npost_ref[...])
        hres_ref[...] = hres
        hn_ref[...] = _rms(hres, nfpre_ref[...]).astype(BF16)
        f_ref[...] = jnp.zeros(f_ref.shape, F32)

    up = _dot(hn_ref[...], wup_ref[...])
    act = jnp.square(jnp.maximum(up, 0.0)).astype(BF16)
    f_ref[...] += _dot(act, wdn_ref[...])

    @pl.when(c == pl.num_programs(0) - 1)
    def _finish():
        y_ref[:, 0, :] = hres_ref[...] + _rms(f_ref[...], nfpost_ref[...])


def _sample_post(x, conv, o, g, ga, gb, lnw, lnb, wco, wro, wo, npost, nfpre, nfpost, wup, wdn):
    n = x.shape[0]

    def whole(a):
        zeros = (0,) * a.ndim
        return pl.BlockSpec(a.shape, lambda c: zeros)

    resident = (x, conv, o, g, ga, gb, lnw, lnb, wco, wro, wo, npost, nfpre, nfpost)
    in_specs = [whole(a) for a in resident] + [
        pl.BlockSpec((D_MODEL, FFN_CHUNK), lambda c: (0, c)),
        pl.BlockSpec((FFN_CHUNK, D_MODEL), lambda c: (c, 0)),
    ]
    return pl.pallas_call(
        _sample_post_kernel,
        grid=(FFN_DIM // FFN_CHUNK,), in_specs=in_specs,
        out_specs=pl.BlockSpec(x.shape, lambda c: (0, 0, 0)),
        out_shape=jax.ShapeDtypeStruct(x.shape, F32),
        scratch_shapes=[
            pltpu.VMEM((n, D_MODEL), F32),
            pltpu.VMEM((n, D_MODEL), BF16),
            pltpu.VMEM((n, D_MODEL), F32),
        ],
        compiler_params=pltpu.CompilerParams(
            dimension_semantics=("arbitrary",),
            vmem_limit_bytes=VMEM_LIMIT_BYTES),
        name="sample_post",
    )(x, conv, o, g, ga, gb, lnw, lnb, wco, wro, wo, npost, nfpre, nfpost, wup, wdn)


def _rope_angles(pos):
    half = DK // 2
    freqs = 1.0 / (ROPE_BASE ** jnp.linspace(0.0, 1.0, half, dtype=F32))
    return pos[:, None] * freqs[None, :]


def kernel(x_prompt, x_sample, cache_conv, state_ret, norm_mix_pre, norm_mix_post, w_in, w_dw, b_dw, conv_ln_w, conv_ln_b, w_conv_out, w_ret_out, w_o, norm_ffn_pre, norm_ffn_post, w_ffn_up, w_ffn_down):
    batch, seq, _ = x_prompt.shape
    n_seq, dec_seq, _ = x_sample.shape
    depth = w_in.shape[0]
    assert dec_seq == 1 and seq % MIX_ROWS == 0 and (batch * seq) % FFN_ROWS == 0
    assert n_seq % SAMPLE_BLOCK == 0

    assert seq % ROPE_FINE == 0
    def both_halves(t):
        return jnp.concatenate([t, t], axis=-1)

    ang_a = both_halves(_rope_angles(ROPE_FINE * jnp.arange(seq // ROPE_FINE, dtype=F32)))
    ang_b = both_halves(_rope_angles(jnp.arange(ROPE_FINE, dtype=F32)))
    cos_a, sin_a = jnp.cos(ang_a)[:, None, :], jnp.sin(ang_a)[:, None, :]
    cos_b, sin_b = jnp.cos(ang_b)[None, :, :], jnp.sin(ang_b)[None, :, :]
    sign = jnp.where(jnp.arange(DK) < DK // 2, -1.0, 1.0).astype(F32)
    cos_p = (cos_a * cos_b - sin_a * sin_b).reshape(seq, DK)
    sin_p = ((sin_a * cos_b + cos_a * sin_b) * sign).reshape(seq, DK)
    ang_s = _rope_angles(PAST_LEN + jnp.arange(dec_seq, dtype=F32))
    cos_s = jnp.broadcast_to(jnp.cos(ang_s).T, (DK // 2, n_seq))
    sin_s = jnp.broadcast_to(jnp.sin(ang_s).T, (DK // 2, n_seq))

    xp = x_prompt
    xs = x_sample
    conv_p, ret_p, conv_s, ret_s = [], [], [], []
    for l in range(depth):
        npre = norm_mix_pre[l][None]
        npost = norm_mix_post[l][None]
        nfpre = norm_ffn_pre[l][None]
        nfpost = norm_ffn_post[l][None]
        wdw = jnp.transpose(w_dw, (1, 0, 2))[:, l:l + 1, :]
        bdw = b_dw[l][None]
        lnw = conv_ln_w[l][None]
        lnb = conv_ln_b[l][None]

        win, wco, wro, wo, u, qt, kt, v, g, ga, gb = _sample_pre(
            xs, cos_s, sin_s, npre, w_in[l], w_conv_out[l], w_ret_out[l], w_o[l])
        cache_t = jnp.transpose(cache_conv[l], (1, 0, 2))
        h_p, nc_p, s_p, wup, wdn = _mixer_prompt(
            xp, cos_p, sin_p, npre, npost, win, wdw, bdw, lnw, lnb, wco, wro, wo,
            w_ffn_up[l], w_ffn_down[l])
        y_p, conv, nc_t, o, s_s = _ffn_state(h_p.reshape(batch * seq, D_MODEL), nfpre, nfpost,
                                             wup, wdn, cache_t, u, wdw, bdw, qt, kt, v,
                                             state_ret[l])
        xp = y_p.reshape(batch, seq, D_MODEL)
        nc_s = jnp.transpose(nc_t, (1, 0, 2))
        xs = _sample_post(xs, conv, o, g, ga, gb, lnw, lnb, wco, wro, wo, npost, nfpre, nfpost,
                          wup, wdn)
        conv_p.append(jnp.transpose(nc_p, (1, 0, 2)))
        ret_p.append(s_p)
        conv_s.append(nc_s)
        ret_s.append(s_s)

    return (xp, xs, jnp.stack(conv_p), jnp.stack(ret_p),
            jnp.stack(conv_s), jnp.stack(ret_s))
```

```python
import functools
import math

import jax
import jax.numpy as jnp
from jax import lax
from jax.experimental import pallas as pl
from jax.experimental.pallas import tpu as pltpu

F32 = jnp.float32
BF16 = jnp.bfloat16

D_MODEL = 1024
CONV_DIM = 512
CONV_WIDTH = 31
HALO = CONV_WIDTH - 1
HEADS = 4
DK = 128
DV = 256
QK_W = HEADS * DK
V_W = HEADS * DV
CHUNK = 128
FFN_DIM = 4 * D_MODEL
EPS = 1e-6
ROPE_BASE = 10000.0
PAST_LEN = 16384
K_SCALE = DK ** -0.5

C_CONV = 0
C_Q = 2 * CONV_DIM
C_K = C_Q + QK_W
C_V = C_K + QK_W
C_G = C_V + V_W
C_GATES = C_G + V_W
IN_COLS = C_GATES + 2 * D_MODEL

LOG_GAMMA = tuple(math.log1p(-(2.0 ** (-5 - h))) for h in range(HEADS))
GAMMA = tuple(math.exp(lg) for lg in LOG_GAMMA)
GAMMA_CHUNK = tuple(math.exp(CHUNK * lg) for lg in LOG_GAMMA)

VMEM_LIMIT_BYTES = 61 * 1024 * 1024
SUBLANES = 8
HALO_PAD = 32
MIX_ROWS = 512
FFN_ROWS = 1024
FFN_CHUNK = 1024
CONV_TILE = 256
CONV_ROWS = 32
SAMPLE_BLOCK = 8
PRE_COLS = 1024
PRE_RING = 3
CAST_SLABS = 4
ROPE_FINE = 128

NT_DIMS = (((1,), (1,)), ((), ()))
TN_DIMS = (((0,), (0,)), ((), ()))


def _dot(a, b):
    return jnp.dot(a, b, preferred_element_type=F32)


def _rms(x, w):
    return x * lax.rsqrt(jnp.mean(x * x, axis=-1, keepdims=True) + EPS) * w


def _layer_norm(x, w, b):
    mu = jnp.mean(x, axis=-1, keepdims=True)
    xc = x - mu
    return xc * lax.rsqrt(jnp.mean(xc * xc, axis=-1, keepdims=True) + EPS) * w + b


def _silu(x):
    return x * jax.nn.sigmoid(x)


def _group_norm(o):
    return o * lax.rsqrt(jnp.mean(o * o, axis=-1, keepdims=True) + EPS)


def _const_spec(shape):
    zeros = (0,) * len(shape)
    return pl.BlockSpec(shape, lambda *_: zeros, pipeline_mode=pl.Buffered(1))


def _merge_out(aact_ref, gate_ref, bout_ref, xprev_ref, wco_ref, wo_ref, npost_ref, mid=None):
    a_out = _dot(aact_ref[...], wco_ref[...])
    mid_out = mid() if mid is not None else None
    merged = (jax.nn.sigmoid(gate_ref[:, :D_MODEL]) * a_out
              + jax.nn.sigmoid(gate_ref[:, D_MODEL:]) * bout_ref[...])
    m = _dot(merged.astype(BF16), wo_ref[...])
    return xprev_ref[...] + _rms(m, npost_ref[...]), mid_out


def _mixer_prompt_kernel(x_ref, cos_ref, sin_ref, npre_ref, npost_ref, win_ref, wdw_ref,
                         bdw_ref, lnw_ref, lnb_ref, wco_ref, wro_ref, wo_ref, wup_ref, wdn_ref,
                         h_ref, nc_ref, sout_ref, wup_bf_ref, wdn_bf_ref,
                         fb_ref, u_ref, y_ref, dec_ref, qd_ref, kd_ref, wb_ref, s_ref,
                         aact_ref, gate_ref, bout_ref, xprev_ref, *, tiles_per_seq, n_tiles):
    g = pl.program_id(0)

    @pl.when(g < n_tiles)
    def _pipelined_step():
        wup_bf_ref[...] = wup_ref[...].astype(BF16)
        wdn_bf_ref[...] = wdn_ref[...].astype(BF16)
        _mixer_step(x_ref, cos_ref, sin_ref, npre_ref, npost_ref, win_ref, wdw_ref,
                    bdw_ref, lnw_ref, lnb_ref, wco_ref, wro_ref, wo_ref,
                    h_ref, nc_ref, sout_ref,
                    fb_ref, u_ref, y_ref, dec_ref, qd_ref, kd_ref, wb_ref, s_ref,
                    aact_ref, gate_ref, bout_ref, xprev_ref, tiles_per_seq=tiles_per_seq)

    @pl.when(g == n_tiles)
    def _drain():
        h_ref[0], _ = _merge_out(aact_ref, gate_ref, bout_ref, xprev_ref, wco_ref, wo_ref,
                                 npost_ref)


def _mixer_step(x_ref, cos_ref, sin_ref, npre_ref, npost_ref, win_ref, wdw_ref,
                bdw_ref, lnw_ref, lnb_ref, wco_ref, wro_ref, wo_ref,
                h_ref, nc_ref, sout_ref,
                fb_ref, u_ref, y_ref, dec_ref, qd_ref, kd_ref, wb_ref, s_ref,
                aact_ref, gate_ref, bout_ref, xprev_ref, *, tiles_per_seq):
    g = pl.program_id(0)
    i = g % tiles_per_seq
    rows = x_ref.shape[1]

    @pl.when(g == 0)
    def _init_tables():
        aact_ref[...] = jnp.zeros(aact_ref.shape, BF16)
        gate_ref[...] = jnp.zeros(gate_ref.shape, F32)
        bout_ref[...] = jnp.zeros(bout_ref.shape, F32)
        xprev_ref[...] = jnp.zeros(xprev_ref.shape, F32)
        ii = lax.broadcasted_iota(jnp.int32, (CHUNK, CHUNK), 0)
        jj = lax.broadcasted_iota(jnp.int32, (CHUNK, CHUNK), 1)
        diff = (ii - jj).astype(F32)
        row_k = lax.broadcasted_iota(jnp.int32, (CHUNK, DK), 0).astype(F32)
        for h in range(HEADS):
            lg = LOG_GAMMA[h]
            dec_ref[h] = jnp.where(diff >= 0.0, jnp.exp(jnp.maximum(diff, 0.0) * lg), 0.0)
            qd_ref[h] = jnp.exp((row_k + 1.0) * lg)
            kd_ref[h] = jnp.exp((CHUNK - 1.0 - row_k) * lg)
        for j in range(CONV_WIDTH):
            wb_ref[j] = jnp.broadcast_to(wdw_ref[j], (SUBLANES, CONV_DIM))

    @pl.when(i == 0)
    def _start_sequence():
        fb_ref[0, 0:HALO_PAD, :] = jnp.zeros((HALO_PAD, CONV_DIM), F32)
        s_ref[...] = jnp.zeros(s_ref.shape, F32)

    x = x_ref[0]
    xn = _rms(x, npre_ref[...]).astype(BF16)
    h_ref[0], pc = _merge_out(
        aact_ref, gate_ref, bout_ref, xprev_ref, wco_ref, wo_ref, npost_ref,
        mid=lambda: _dot(xn, win_ref[:, C_CONV:C_CONV + 2 * CONV_DIM]))

    xprev_ref[...] = x

    u_ref[...] = pc[:, :CONV_DIM] * jax.nn.sigmoid(pc[:, CONV_DIM:])

    pqk = _dot(xn, win_ref[:, C_Q:C_Q + 2 * QK_W])
    pv = _dot(xn, win_ref[:, C_V:C_V + V_W]).astype(BF16)
    pg = _dot(xn, win_ref[:, C_G:C_G + V_W])

    first = HALO_PAD - HALO
    span = CONV_TILE + HALO_PAD - SUBLANES
    for t in range(rows // CONV_TILE):
        fb_ref[0, HALO_PAD:HALO_PAD + CONV_TILE, :] = u_ref[t * CONV_TILE:(t + 1) * CONV_TILE, :]
        for r in range(1, SUBLANES):
            fb_ref[r, 0:span, :] = fb_ref[0, r:r + span, :]
        for rb in range(CONV_TILE // CONV_ROWS):
            acc = jnp.broadcast_to(bdw_ref[...], (CONV_ROWS, CONV_DIM))
            for j in range(CONV_WIDTH):
                off = first + j
                base = rb * CONV_ROWS + off - off % SUBLANES
                slab = fb_ref[off % SUBLANES, base:base + CONV_ROWS, :]
                acc = acc + (slab.reshape(CONV_ROWS // SUBLANES, SUBLANES, CONV_DIM)
                             * wb_ref[j][None]).reshape(CONV_ROWS, CONV_DIM)
            r0 = t * CONV_TILE + rb * CONV_ROWS
            aact_ref[r0:r0 + CONV_ROWS, :] = _silu(
                _layer_norm(acc, lnw_ref[...], lnb_ref[...])).astype(BF16)
        fb_ref[0, 0:HALO_PAD, :] = fb_ref[0, CONV_TILE:CONV_TILE + HALO_PAD, :]

    cos2 = cos_ref[...]
    sin2 = sin_ref[...]

    def rot(t):
        return t * cos2 + pltpu.roll(t, DK // 2, 1) * sin2

    heads = range(HEADS)
    q_rot = [rot(pqk[:, h * DK:(h + 1) * DK]) for h in heads]
    q_bf = [q.astype(BF16) for q in q_rot]
    k_rot = [rot(pqk[:, QK_W + h * DK:QK_W + (h + 1) * DK]) * K_SCALE for h in heads]
    k_bf = [k.astype(BF16) for k in k_rot]
    g_act = [_silu(pg[:, h * DV:(h + 1) * DV]) for h in heads]

    n_chunks = rows // CHUNK
    n_fill = 2 * n_chunks
    gate_cols = 2 * D_MODEL // n_fill

    def gate_slice(f):
        gate_ref[:, f * gate_cols:(f + 1) * gate_cols] = _dot(
            xn, win_ref[:, C_GATES + f * gate_cols:C_GATES + (f + 1) * gate_cols])

    for c in range(n_chunks):
        rs = slice(c * CHUNK, (c + 1) * CHUNK)
        v_c = [pv[rs, h * DV:(h + 1) * DV] for h in heads]
        state = [s_ref[h] for h in heads]
        scores = []
        zeros = jnp.zeros((CHUNK, DK), BF16)
        for a in range(0, HEADS, 2):
            q_pair = jnp.concatenate([q_bf[a][rs], q_bf[a + 1][rs]], axis=1)
            k_pair = jnp.concatenate(
                [jnp.concatenate([k_bf[a][rs], zeros], axis=1),
                 jnp.concatenate([zeros, k_bf[a + 1][rs]], axis=1)], axis=0)
            s_pair = lax.dot_general(q_pair, k_pair, NT_DIMS, preferred_element_type=F32)
            scores += [s_pair[:, :CHUNK], s_pair[:, CHUNK:]]
        kv = [lax.dot_general((k_rot[h][rs] * kd_ref[h]).astype(BF16), v_c[h], TN_DIMS,
                              preferred_element_type=F32) for h in heads]
        gate_slice(2 * c)
        lhs = [jnp.concatenate([(scores[h] * dec_ref[h]).astype(BF16),
                                (q_rot[h][rs] * qd_ref[h]).astype(BF16)], axis=1)
               for h in heads]
        rhs = [jnp.concatenate([v_c[h], state[h].astype(BF16)], axis=0) for h in heads]
        out = [_dot(lhs[h], rhs[h]) for h in heads]
        gate_slice(2 * c + 1)
        for h in heads:
            s_ref[h] = GAMMA_CHUNK[h] * state[h] + kv[h]
            y_ref[rs, h * DV:(h + 1) * DV] = (g_act[h][rs] * _group_norm(out[h])).astype(BF16)
    bout_ref[...] = _dot(y_ref[...], wro_ref[...])

    @pl.when(i == tiles_per_seq - 1)
    def _emit_sequence_state():
        seq = g // tiles_per_seq
        for j in range(HALO):
            nc_ref[j, pl.ds(seq, 1), :] = fb_ref[0, HALO_PAD - HALO + j:HALO_PAD - HALO + j + 1, :]
        sout_ref[0] = s_ref[...]


def _mixer_prompt(x, cos2, sin2, npre, npost, win, wdw, bdw, lnw, lnb, wco, wro, wo,
                  wup_f32, wdn_f32):
    batch, seq, _ = x.shape
    rows = MIX_ROWS
    nt = seq // rows
    n_tiles = batch * nt
    assert D_MODEL % n_tiles == 0 and FFN_DIM % n_tiles == 0
    up_rows, dn_rows = D_MODEL // n_tiles, FFN_DIM // n_tiles

    def front(g):
        return jnp.minimum(g, n_tiles - 1)

    def back(g):
        return jnp.maximum(g - 1, 0)

    side = [pl.BlockSpec((up_rows, FFN_DIM), lambda g: (front(g), 0)),
            pl.BlockSpec((dn_rows, D_MODEL), lambda g: (front(g), 0))]

    in_specs = [
        pl.BlockSpec((1, rows, D_MODEL), lambda g: (front(g) // nt, front(g) % nt, 0)),
        pl.BlockSpec((rows, DK), lambda g: (front(g) % nt, 0)),
        pl.BlockSpec((rows, DK), lambda g: (front(g) % nt, 0)),
        _const_spec((1, D_MODEL)),
        _const_spec((1, D_MODEL)),
        _const_spec((D_MODEL, IN_COLS)),
        _const_spec((CONV_WIDTH, 1, CONV_DIM)),
        _const_spec((1, CONV_DIM)),
        _const_spec((1, CONV_DIM)),
        _const_spec((1, CONV_DIM)),
        _const_spec((CONV_DIM, D_MODEL)),
        _const_spec((V_W, D_MODEL)),
        _const_spec((D_MODEL, D_MODEL)),
    ] + side
    out_specs = [
        pl.BlockSpec((1, rows, D_MODEL), lambda g: (back(g) // nt, back(g) % nt, 0)),
        pl.BlockSpec((HALO, batch, CONV_DIM), lambda g: (0, 0, 0)),
        pl.BlockSpec((1, HEADS, DK, DV), lambda g: (front(g) // nt, 0, 0, 0)),
    ] + side
    out_shape = [
        jax.ShapeDtypeStruct((batch, seq, D_MODEL), F32),
        jax.ShapeDtypeStruct((HALO, batch, CONV_DIM), F32),
        jax.ShapeDtypeStruct((batch, HEADS, DK, DV), F32),
        jax.ShapeDtypeStruct((D_MODEL, FFN_DIM), BF16),
        jax.ShapeDtypeStruct((FFN_DIM, D_MODEL), BF16),
    ]
    scratch = [
        pltpu.VMEM((SUBLANES, HALO_PAD + CONV_TILE, CONV_DIM), F32),
        pltpu.VMEM((rows, CONV_DIM), F32),
        pltpu.VMEM((rows, V_W), BF16),
        pltpu.VMEM((HEADS, CHUNK, CHUNK), F32),
        pltpu.VMEM((HEADS, CHUNK, DK), F32),
        pltpu.VMEM((HEADS, CHUNK, DK), F32),
        pltpu.VMEM((CONV_WIDTH, SUBLANES, CONV_DIM), F32),
        pltpu.VMEM((HEADS, DK, DV), F32),
        pltpu.VMEM((rows, CONV_DIM), BF16),
        pltpu.VMEM((rows, 2 * D_MODEL), F32),
        pltpu.VMEM((rows, D_MODEL), F32),
        pltpu.VMEM((rows, D_MODEL), F32),
    ]
    return pl.pallas_call(
        functools.partial(_mixer_prompt_kernel, tiles_per_seq=nt, n_tiles=n_tiles),
        grid=(n_tiles + 1,), in_specs=in_specs, out_specs=out_specs, out_shape=out_shape,
        scratch_shapes=scratch,
        compiler_params=pltpu.CompilerParams(
            dimension_semantics=("arbitrary",),
            vmem_limit_bytes=VMEM_LIMIT_BYTES),
        name="mixer_prompt",
    )(x, cos2, sin2, npre, npost, win, wdw, bdw, lnw, lnb, wco, wro, wo, wup_f32, wdn_f32)


def _ffn_body(h, npre, npost, wup_ref, wdn_ref):
    hn = _rms(h, npre).astype(BF16)
    f = None
    for c in range(FFN_DIM // FFN_CHUNK):
        cols = slice(c * FFN_CHUNK, (c + 1) * FFN_CHUNK)
        up = _dot(hn, wup_ref[:, cols])
        act = jnp.square(jnp.maximum(up, 0.0)).astype(BF16)
        part = _dot(act, wdn_ref[cols, :])
        f = part if f is None else f + part
    return h + _rms(f, npost)


def _sample_state_body(blk, cache_ref, u_ref, wdw_ref, bdw_ref, qt_ref, kt_ref, v_ref, st_ref,
                       conv_ref, nc_ref, o_ref, so_ref):
    n_seq = qt_ref.shape[1]
    u = u_ref[...]
    acc = u * wdw_ref[HALO] + bdw_ref[...]
    for j in range(HALO):
        acc = acc + cache_ref[j] * wdw_ref[j]
    conv_ref[...] = acc
    nc_ref[0:HALO - 1] = cache_ref[1:HALO]
    nc_ref[HALO - 1] = u

    shift = (n_seq - blk * SAMPLE_BLOCK) % n_seq
    qt = pltpu.roll(qt_ref[...], shift, 1)
    kt = pltpu.roll(kt_ref[...], shift, 1)
    for s in range(SAMPLE_BLOCK):
        for h in range(HEADS):
            k_col = kt[h * DK:(h + 1) * DK, s:s + 1]
            q_col = qt[h * DK:(h + 1) * DK, s:s + 1]
            v_row = v_ref[s:s + 1, h * DV:(h + 1) * DV]
            new_state = GAMMA[h] * st_ref[s, h] + k_col * v_row
            so_ref[s, h] = new_state
            o_ref[s:s + 1, h * DV:(h + 1) * DV] = jnp.sum(new_state * q_col, axis=0,
                                                          keepdims=True)


def _ffn_state_kernel(h_ref, npre_ref, npost_ref, wup_ref, wdn_ref,
                      cache_ref, u_ref, wdw_ref, bdw_ref, qt_ref, kt_ref, v_ref, st_ref,
                      y_ref, conv_ref, nc_ref, o_ref, so_ref, *, steps_per_block):
    y_ref[...] = _ffn_body(h_ref[...], npre_ref[...], npost_ref[...], wup_ref, wdn_ref)
    _sample_state_body(pl.program_id(0) // steps_per_block, cache_ref, u_ref, wdw_ref, bdw_ref,
                       qt_ref, kt_ref, v_ref, st_ref, conv_ref, nc_ref, o_ref, so_ref)


def _ffn_state(h, npre, npost, wup, wdn, cache, u, wdw, bdw, qt, kt, v, state):
    n = h.shape[0]
    rows = FFN_ROWS
    steps = n // rows
    n_seq = u.shape[0]
    sb = SAMPLE_BLOCK
    spb = steps // (n_seq // sb)
    assert spb * (n_seq // sb) == steps
    in_specs = [
        pl.BlockSpec((rows, D_MODEL), lambda i: (i, 0)),
        _const_spec((1, D_MODEL)),
        _const_spec((1, D_MODEL)),
        _const_spec((D_MODEL, FFN_DIM)),
        _const_spec((FFN_DIM, D_MODEL)),
        pl.BlockSpec((HALO, sb, CONV_DIM), lambda i: (0, i // spb, 0)),
        pl.BlockSpec((sb, CONV_DIM), lambda i: (i // spb, 0)),
        _const_spec((CONV_WIDTH, 1, CONV_DIM)),
        _const_spec((1, CONV_DIM)),
        _const_spec((QK_W, n_seq)),
        _const_spec((QK_W, n_seq)),
        pl.BlockSpec((sb, V_W), lambda i: (i // spb, 0)),
        pl.BlockSpec((sb, HEADS, DK, DV), lambda i: (i // spb, 0, 0, 0)),
    ]
    out_specs = [
        pl.BlockSpec((rows, D_MODEL), lambda i: (i, 0)),
        pl.BlockSpec((sb, CONV_DIM), lambda i: (i // spb, 0)),
        pl.BlockSpec((HALO, sb, CONV_DIM), lambda i: (0, i // spb, 0)),
        pl.BlockSpec((sb, V_W), lambda i: (i // spb, 0)),
        pl.BlockSpec((sb, HEADS, DK, DV), lambda i: (i // spb, 0, 0, 0)),
    ]
    out_shape = [
        jax.ShapeDtypeStruct((n, D_MODEL), F32),
        jax.ShapeDtypeStruct((n_seq, CONV_DIM), F32),
        jax.ShapeDtypeStruct((HALO, n_seq, CONV_DIM), F32),
        jax.ShapeDtypeStruct((n_seq, V_W), F32),
        jax.ShapeDtypeStruct((n_seq, HEADS, DK, DV), F32),
    ]
    return pl.pallas_call(
        functools.partial(_ffn_state_kernel, steps_per_block=spb),
        grid=(steps,), in_specs=in_specs, out_specs=out_specs, out_shape=out_shape,
        compiler_params=pltpu.CompilerParams(
            dimension_semantics=("arbitrary",),
            vmem_limit_bytes=VMEM_LIMIT_BYTES),
        name="ffn_prompt_sample_state",
    )(h, npre, npost, wup, wdn, cache, u, wdw, bdw, qt, kt, v, state)


def _sample_pre_kernel(x_ref, cos_ref, sin_ref, npre_ref, win_hbm, wco_ref, wro_ref, wo_ref,
                       wbf_ref, wco_bf_ref, wro_bf_ref, wo_bf_ref,
                       u_ref, qt_ref, kt_ref, v_ref, g_ref, ga_ref, gb_ref,
                       xn_ref, ring_ref, sem_ref):
    j = pl.program_id(0)
    steps = pl.num_programs(0)

    def slab_copy(s, slot):
        cols = pl.ds(pl.multiple_of(s * PRE_COLS, PRE_COLS), PRE_COLS)
        return pltpu.make_async_copy(win_hbm.at[:, cols], ring_ref.at[slot], sem_ref.at[slot])

    @pl.when(j == 0)
    def _prime():
        slab_copy(0, 0).start()
        slab_copy(1, 1).start()
        xn_ref[...] = _rms(x_ref[:, 0, :], npre_ref[...]).astype(BF16)

    @pl.when(j + 2 < steps)
    def _prefetch():
        slab_copy(j + 2, (j + 2) % PRE_RING).start()

    wco_bf_ref[...] = wco_ref[...].astype(BF16)
    wro_bf_ref[...] = wro_ref[...].astype(BF16)
    wo_bf_ref[...] = wo_ref[...].astype(BF16)

    slot = j % PRE_RING
    slab_copy(j, slot).wait()
    w_slab = ring_ref[slot].astype(BF16)
    wbf_ref[...] = w_slab
    p = _dot(xn_ref[...], w_slab)

    @pl.when(j == C_CONV // PRE_COLS)
    def _glu():
        u_ref[...] = p[:, :CONV_DIM] * jax.nn.sigmoid(p[:, CONV_DIM:])

    @pl.when(j == C_Q // PRE_COLS)
    def _rotary():
        qkt = p.T
        cos_t = cos_ref[...]
        sin_t = sin_ref[...]
        half = DK // 2
        for g in range(2 * HEADS):
            x1 = qkt[g * DK:g * DK + half]
            x2 = qkt[g * DK + half:(g + 1) * DK]
            o1 = x1 * cos_t - x2 * sin_t
            o2 = x2 * cos_t + x1 * sin_t
            if g < HEADS:
                qt_ref[g * DK:g * DK + half, :] = o1
                qt_ref[g * DK + half:(g + 1) * DK, :] = o2
            else:
                k0 = (g - HEADS) * DK
                kt_ref[k0:k0 + half, :] = o1 * K_SCALE
                kt_ref[k0 + half:k0 + DK, :] = o2 * K_SCALE

    for out_ref, col in ((v_ref, C_V), (g_ref, C_G), (ga_ref, C_GATES),
                         (gb_ref, C_GATES + D_MODEL)):
        @pl.when(j == col // PRE_COLS)
        def _store(out_ref=out_ref):
            out_ref[...] = p


def _sample_pre(x, cos_t, sin_t, npre, win_f32, wco_f32, wro_f32, wo_f32):
    n = x.shape[0]
    assert all(c % PRE_COLS == 0 for c in (C_CONV, C_Q, C_V, C_G, C_GATES, IN_COLS))
    assert 2 * CONV_DIM == 2 * QK_W == V_W == D_MODEL == PRE_COLS
    steps = IN_COLS // PRE_COLS
    assert steps >= CAST_SLABS

    def whole(shape):
        zeros = (0,) * len(shape)
        return pl.BlockSpec(shape, lambda j: zeros)

    def row_slab(rows):
        return pl.BlockSpec((rows // CAST_SLABS, D_MODEL),
                            lambda j: (jnp.minimum(j, CAST_SLABS - 1), 0))

    side = [row_slab(CONV_DIM), row_slab(V_W), row_slab(D_MODEL)]
    in_specs = [
        whole((n, 1, D_MODEL)),
        whole((DK // 2, n)),
        whole((DK // 2, n)),
        whole((1, D_MODEL)),
        pl.BlockSpec(memory_space=pl.ANY),
    ] + side
    out_specs = [
        pl.BlockSpec((D_MODEL, PRE_COLS), lambda j: (0, j)),
    ] + side + [
        whole((n, CONV_DIM)),
        whole((QK_W, n)),
        whole((QK_W, n)),
        whole((n, V_W)),
        whole((n, V_W)),
        whole((n, D_MODEL)),
        whole((n, D_MODEL)),
    ]
    out_shape = [
        jax.ShapeDtypeStruct((D_MODEL, IN_COLS), BF16),
        jax.ShapeDtypeStruct((CONV_DIM, D_MODEL), BF16),
        jax.ShapeDtypeStruct((V_W, D_MODEL), BF16),
        jax.ShapeDtypeStruct((D_MODEL, D_MODEL), BF16),
        jax.ShapeDtypeStruct((n, CONV_DIM), F32),
        jax.ShapeDtypeStruct((QK_W, n), F32),
        jax.ShapeDtypeStruct((QK_W, n), F32),
        jax.ShapeDtypeStruct((n, V_W), F32),
        jax.ShapeDtypeStruct((n, V_W), F32),
        jax.ShapeDtypeStruct((n, D_MODEL), F32),
        jax.ShapeDtypeStruct((n, D_MODEL), F32),
    ]
    return pl.pallas_call(
        _sample_pre_kernel,
        grid=(steps,), in_specs=in_specs, out_specs=out_specs, out_shape=out_shape,
        scratch_shapes=[
            pltpu.VMEM((n, D_MODEL), BF16),
            pltpu.VMEM((PRE_RING, D_MODEL, PRE_COLS), F32),
            pltpu.SemaphoreType.DMA((PRE_RING,)),
        ],
        compiler_params=pltpu.CompilerParams(
            dimension_semantics=("arbitrary",),
            vmem_limit_bytes=VMEM_LIMIT_BYTES),
        name="sample_pre_cast_weights",
    )(x, cos_t, sin_t, npre, win_f32, wco_f32, wro_f32, wo_f32)


def _sample_post_kernel(x_ref, conv_ref, o_ref, g_ref, ga_ref, gb_ref, lnw_ref, lnb_ref,
                        wco_ref, wro_ref, wo_ref, npost_ref, nfpre_ref, nfpost_ref,
                        wup_ref, wdn_ref, y_ref, hres_ref, hn_ref, f_ref):
    c = pl.program_id(0)

    @pl.when(c == 0)
    def _merge():
        a_act = _silu(_layer_norm(conv_ref[...], lnw_ref[...], lnb_ref[...])).astype(BF16)
        a_out = _dot(a_act, wco_ref[...])
        b_out = jnp.zeros_like(a_out)
        for h in range(HEADS):
            cols = slice(h * DV, (h + 1) * DV)
            yh = (_silu(g_ref[:, cols]) * _group_norm(o_ref[:, cols])).astype(BF16)
            b_out = b_out + _dot(yh, wro_ref[cols, :])
        merged = jax.nn.sigmoid(ga_ref[...]) * a_out + jax.nn.sigmoid(gb_ref[...]) * b_out
        m = _dot(merged.astype(BF16), wo_ref[...])
        hres = x_ref[:, 0, :] + _rms(m, npost_ref[...])
        hres_ref[...] = hres
        hn_ref[...] = _rms(hres, nfpre_ref[...]).astype(BF16)
        f_ref[...] = jnp.zeros(f_ref.shape, F32)

    up = _dot(hn_ref[...], wup_ref[...])
    act = jnp.square(jnp.maximum(up, 0.0)).astype(BF16)
    f_ref[...] += _dot(act, wdn_ref[...])

    @pl.when(c == pl.num_programs(0) - 1)
    def _finish():
        y_ref[:, 0, :] = hres_ref[...] + _rms(f_ref[...], nfpost_ref[...])


def _sample_post(x, conv, o, g, ga, gb, lnw, lnb, wco, wro, wo, npost, nfpre, nfpost, wup, wdn):
    n = x.shape[0]

    def whole(a):
        zeros = (0,) * a.ndim
        return pl.BlockSpec(a.shape, lambda c: zeros)

    resident = (x, conv, o, g, ga, gb, lnw, lnb, wco, wro, wo, npost, nfpre, nfpost)
    in_specs = [whole(a) for a in resident] + [
        pl.BlockSpec((D_MODEL, FFN_CHUNK), lambda c: (0, c)),
        pl.BlockSpec((FFN_CHUNK, D_MODEL), lambda c: (c, 0)),
    ]
    return pl.pallas_call(
        _sample_post_kernel,
        grid=(FFN_DIM // FFN_CHUNK,), in_specs=in_specs,
        out_specs=pl.BlockSpec(x.shape, lambda c: (0, 0, 0)),
        out_shape=jax.ShapeDtypeStruct(x.shape, F32),
        scratch_shapes=[
            pltpu.VMEM((n, D_MODEL), F32),
            pltpu.VMEM((n, D_MODEL), BF16),
            pltpu.VMEM((n, D_MODEL), F32),
        ],
        compiler_params=pltpu.CompilerParams(
            dimension_semantics=("arbitrary",),
            vmem_limit_bytes=VMEM_LIMIT_BYTES),
        name="sample_post",
    )(x, conv, o, g, ga, gb, lnw, lnb, wco, wro, wo, npost, nfpre, nfpost, wup, wdn)


def _rope_angles(pos):
    half = DK // 2
    freqs = 1.0 / (ROPE_BASE ** jnp.linspace(0.0, 1.0, half, dtype=F32))
    return pos[:, None] * freqs[None, :]


def kernel(x_prompt, x_sample, cache_conv, state_ret, norm_mix_pre, norm_mix_post, w_in, w_dw, b_dw, conv_ln_w, conv_ln_b, w_conv_out, w_ret_out, w_o, norm_ffn_pre, norm_ffn_post, w_ffn_up, w_ffn_down):
    batch, seq, _ = x_prompt.shape
    n_seq, dec_seq, _ = x_sample.shape
    depth = w_in.shape[0]
    assert dec_seq == 1 and seq % MIX_ROWS == 0 and (batch * seq) % FFN_ROWS == 0
    assert n_seq % SAMPLE_BLOCK == 0

    assert seq % ROPE_FINE == 0
    def both_halves(t):
        return jnp.concatenate([t, t], axis=-1)

    ang_a = both_halves(_rope_angles(ROPE_FINE * jnp.arange(seq // ROPE_FINE, dtype=F32)))
    ang_b = both_halves(_rope_angles(jnp.arange(ROPE_FINE, dtype=F32)))
    cos_a, sin_a = jnp.cos(ang_a)[:, None, :], jnp.sin(ang_a)[:, None, :]
    cos_b, sin_b = jnp.cos(ang_b)[None, :, :], jnp.sin(ang_b)[None, :, :]
    sign = jnp.where(jnp.arange(DK) < DK // 2, -1.0, 1.0).astype(F32)
    cos_p = (cos_a * cos_b - sin_a * sin_b).reshape(seq, DK)
    sin_p = ((sin_a * cos_b + cos_a * sin_b) * sign).reshape(seq, DK)
    ang_s = _rope_angles(PAST_LEN + jnp.arange(dec_seq, dtype=F32))
    cos_s = jnp.broadcast_to(jnp.cos(ang_s).T, (DK // 2, n_seq))
    sin_s = jnp.broadcast_to(jnp.sin(ang_s).T, (DK // 2, n_seq))

    xp = x_prompt
    xs = x_sample
    conv_p, ret_p, conv_s, ret_s = [], [], [], []
    for l in range(depth):
        npre = norm_mix_pre[l][None]
        npost = norm_mix_post[l][None]
        nfpre = norm_ffn_pre[l][None]
        nfpost = norm_ffn_post[l][None]
        wdw = jnp.transpose(w_dw, (1, 0, 2))[:, l:l + 1, :]
        bdw = b_dw[l][None]
        lnw = conv_ln_w[l][None]
        lnb = conv_ln_b[l][None]

        win, wco, wro, wo, u, qt, kt, v, g, ga, gb = _sample_pre(
            xs, cos_s, sin_s, npre, w_in[l], w_conv_out[l], w_ret_out[l], w_o[l])
        cache_t = jnp.transpose(cache_conv[l], (1, 0, 2))
        h_p, nc_p, s_p, wup, wdn = _mixer_prompt(
            xp, cos_p, sin_p, npre, npost, win, wdw, bdw, lnw, lnb, wco, wro, wo,
            w_ffn_up[l], w_ffn_down[l])
        y_p, conv, nc_t, o, s_s = _ffn_state(h_p.reshape(batch * seq, D_MODEL), nfpre, nfpost,
                                             wup, wdn, cache_t, u, wdw, bdw, qt, kt, v,
                                             state_ret[l])
        xp = y_p.reshape(batch, seq, D_MODEL)
        nc_s = jnp.transpose(nc_t, (1, 0, 2))
        xs = _sample_post(xs, conv, o, g, ga, gb, lnw, lnb, wco, wro, wo, npost, nfpre, nfpost,
                          wup, wdn)
        conv_p.append(jnp.transpose(nc_p, (1, 0, 2)))
        ret_p.append(s_p)
        conv_s.append(nc_s)
        ret_s.append(s_s)

    return (xp, xs, jnp.stack(conv_p), jnp.stack(ret_p),
            jnp.stack(conv_s), jnp.stack(ret_s))
```

```python
import functools
import math

import jax
import jax.numpy as jnp
from jax import lax
from jax.experimental import pallas as pl
from jax.experimental.pallas import tpu as pltpu

F32 = jnp.float32
BF16 = jnp.bfloat16

D_MODEL = 1024
CONV_DIM = 512
CONV_WIDTH = 31
HALO = CONV_WIDTH - 1
HEADS = 4
DK = 128
DV = 256
QK_W = HEADS * DK
V_W = HEADS * DV
CHUNK = 128
FFN_DIM = 4 * D_MODEL
EPS = 1e-6
ROPE_BASE = 10000.0
PAST_LEN = 16384
K_SCALE = DK ** -0.5

C_CONV = 0
C_Q = 2 * CONV_DIM
C_K = C_Q + QK_W
C_V = C_K + QK_W
C_G = C_V + V_W
C_GATES = C_G + V_W
IN_COLS = C_GATES + 2 * D_MODEL

LOG_GAMMA = tuple(math.log1p(-(2.0 ** (-5 - h))) for h in range(HEADS))
GAMMA = tuple(math.exp(lg) for lg in LOG_GAMMA)
GAMMA_CHUNK = tuple(math.exp(CHUNK * lg) for lg in LOG_GAMMA)

VMEM_LIMIT_BYTES = 61 * 1024 * 1024
SUBLANES = 8
HALO_PAD = 32
MIX_ROWS = 512
FFN_ROWS = 1024
FFN_CHUNK = 1024
CONV_TILE = 256
CONV_ROWS = 32
SAMPLE_BLOCK = 8
PRE_COLS = 1024
PRE_RING = 3
CAST_SLABS = 4
ROPE_FINE = 128

NT_DIMS = (((1,), (1,)), ((), ()))
TN_DIMS = (((0,), (0,)), ((), ()))


def _dot(a, b):
    return jnp.dot(a, b, preferred_element_type=F32)


def _rms(x, w):
    return x * lax.rsqrt(jnp.mean(x * x, axis=-1, keepdims=True) + EPS) * w


def _layer_norm(x, w, b):
    mu = jnp.mean(x, axis=-1, keepdims=True)
    xc = x - mu
    return xc * lax.rsqrt(jnp.mean(xc * xc, axis=-1, keepdims=True) + EPS) * w + b


def _silu(x):
    return x * jax.nn.sigmoid(x)


def _group_norm(o):
    return o * lax.rsqrt(jnp.mean(o * o, axis=-1, keepdims=True) + EPS)


def _const_spec(shape):
    zeros = (0,) * len(shape)
    return pl.BlockSpec(shape, lambda *_: zeros, pipeline_mode=pl.Buffered(1))


def _merge_out(aact_ref, gate_ref, bout_ref, xprev_ref, wco_ref, wo_ref, npost_ref, mid=None):
    a_out = _dot(aact_ref[...], wco_ref[...])
    mid_out = mid() if mid is not None else None
    merged = (jax.nn.sigmoid(gate_ref[:, :D_MODEL]) * a_out
              + jax.nn.sigmoid(gate_ref[:, D_MODEL:]) * bout_ref[...])
    m = _dot(merged.astype(BF16), wo_ref[...])
    return xprev_ref[...] + _rms(m, npost_ref[...]), mid_out


def _mixer_prompt_kernel(x_ref, cos_ref, sin_ref, npre_ref, npost_ref, win_ref, wdw_ref,
                         bdw_ref, lnw_ref, lnb_ref, wco_ref, wro_ref, wo_ref, wup_ref, wdn_ref,
                         h_ref, nc_ref, sout_ref, wup_bf_ref, wdn_bf_ref,
                         fb_ref, u_ref, y_ref, dec_ref, qd_ref, kd_ref, wb_ref, s_ref,
                         aact_ref, gate_ref, bout_ref, xprev_ref, *, tiles_per_seq, n_tiles):
    g = pl.program_id(0)

    @pl.when(g < n_tiles)
    def _pipelined_step():
        wup_bf_ref[...] = wup_ref[...].astype(BF16)
        wdn_bf_ref[...] = wdn_ref[...].astype(BF16)
        _mixer_step(x_ref, cos_ref, sin_ref, npre_ref, npost_ref, win_ref, wdw_ref,
                    bdw_ref, lnw_ref, lnb_ref, wco_ref, wro_ref, wo_ref,
                    h_ref, nc_ref, sout_ref,
                    fb_ref, u_ref, y_ref, dec_ref, qd_ref, kd_ref, wb_ref, s_ref,
                    aact_ref, gate_ref, bout_ref, xprev_ref, tiles_per_seq=tiles_per_seq)

    @pl.when(g == n_tiles)
    def _drain():
        h_ref[0], _ = _merge_out(aact_ref, gate_ref, bout_ref, xprev_ref, wco_ref, wo_ref,
                                 npost_ref)


def _mixer_step(x_ref, cos_ref, sin_ref, npre_ref, npost_ref, win_ref, wdw_ref,
                bdw_ref, lnw_ref, lnb_ref, wco_ref, wro_ref, wo_ref,
                h_ref, nc_ref, sout_ref,
                fb_ref, u_ref, y_ref, dec_ref, qd_ref, kd_ref, wb_ref, s_ref,
                aact_ref, gate_ref, bout_ref, xprev_ref, *, tiles_per_seq):
    g = pl.program_id(0)
    i = g % tiles_per_seq
    rows = x_ref.shape[1]

    @pl.when(g == 0)
    def _init_tables():
        aact_ref[...] = jnp.zeros(aact_ref.shape, BF16)
        gate_ref[...] = jnp.zeros(gate_ref.shape, F32)
        bout_ref[...] = jnp.zeros(bout_ref.shape, F32)
        xprev_ref[...] = jnp.zeros(xprev_ref.shape, F32)
        ii = lax.broadcasted_iota(jnp.int32, (CHUNK, CHUNK), 0)
        jj = lax.broadcasted_iota(jnp.int32, (CHUNK, CHUNK), 1)
        diff = (ii - jj).astype(F32)
        row_k = lax.broadcasted_iota(jnp.int32, (CHUNK, DK), 0).astype(F32)
        for h in range(HEADS):
            lg = LOG_GAMMA[h]
            dec_ref[h] = jnp.where(diff >= 0.0, jnp.exp(jnp.maximum(diff, 0.0) * lg), 0.0)
            qd_ref[h] = jnp.exp((row_k + 1.0) * lg)
            kd_ref[h] = jnp.exp((CHUNK - 1.0 - row_k) * lg)
        for j in range(CONV_WIDTH):
            wb_ref[j] = jnp.broadcast_to(wdw_ref[j], (SUBLANES, CONV_DIM))

    @pl.when(i == 0)
    def _start_sequence():
        fb_ref[0, 0:HALO_PAD, :] = jnp.zeros((HALO_PAD, CONV_DIM), F32)
        s_ref[...] = jnp.zeros(s_ref.shape, F32)

    x = x_ref[0]
    xn = _rms(x, npre_ref[...]).astype(BF16)
    h_ref[0], pc = _merge_out(
        aact_ref, gate_ref, bout_ref, xprev_ref, wco_ref, wo_ref, npost_ref,
        mid=lambda: _dot(xn, win_ref[:, C_CONV:C_CONV + 2 * CONV_DIM]))

    xprev_ref[...] = x

    u_ref[...] = pc[:, :CONV_DIM] * jax.nn.sigmoid(pc[:, CONV_DIM:])

    pqk = _dot(xn, win_ref[:, C_Q:C_Q + 2 * QK_W])
    pv = _dot(xn, win_ref[:, C_V:C_V + V_W]).astype(BF16)
    pg = _dot(xn, win_ref[:, C_G:C_G + V_W])

    first = HALO_PAD - HALO
    span = CONV_TILE + HALO_PAD - SUBLANES
    for t in range(rows // CONV_TILE):
        fb_ref[0, HALO_PAD:HALO_PAD + CONV_TILE, :] = u_ref[t * CONV_TILE:(t + 1) * CONV_TILE, :]
        for r in range(1, SUBLANES):
            fb_ref[r, 0:span, :] = fb_ref[0, r:r + span, :]
        for rb in range(CONV_TILE // CONV_ROWS):
            acc = jnp.broadcast_to(bdw_ref[...], (CONV_ROWS, CONV_DIM))
            for j in range(CONV_WIDTH):
                off = first + j
                base = rb * CONV_ROWS + off - off % SUBLANES
                slab = fb_ref[off % SUBLANES, base:base + CONV_ROWS, :]
                acc = acc + (slab.reshape(CONV_ROWS // SUBLANES, SUBLANES, CONV_DIM)
                             * wb_ref[j][None]).reshape(CONV_ROWS, CONV_DIM)
            r0 = t * CONV_TILE + rb * CONV_ROWS
            aact_ref[r0:r0 + CONV_ROWS, :] = _silu(
                _layer_norm(acc, lnw_ref[...], lnb_ref[...])).astype(BF16)
        fb_ref[0, 0:HALO_PAD, :] = fb_ref[0, CONV_TILE:CONV_TILE + HALO_PAD, :]

    cos2 = cos_ref[...]
    sin2 = sin_ref[...]

    def rot(t):
        return t * cos2 + pltpu.roll(t, DK // 2, 1) * sin2

    heads = range(HEADS)
    q_rot = [rot(pqk[:, h * DK:(h + 1) * DK]) for h in heads]
    q_bf = [q.astype(BF16) for q in q_rot]
    k_rot = [rot(pqk[:, QK_W + h * DK:QK_W + (h + 1) * DK]) * K_SCALE for h in heads]
    k_bf = [k.astype(BF16) for k in k_rot]
    g_act = [_silu(pg[:, h * DV:(h + 1) * DV]) for h in heads]

    n_chunks = rows // CHUNK
    n_fill = 2 * n_chunks
    gate_cols = 2 * D_MODEL // n_fill

    def gate_slice(f):
        gate_ref[:, f * gate_cols:(f + 1) * gate_cols] = _dot(
            xn, win_ref[:, C_GATES + f * gate_cols:C_GATES + (f + 1) * gate_cols])

    for c in range(n_chunks):
        rs = slice(c * CHUNK, (c + 1) * CHUNK)
        v_c = [pv[rs, h * DV:(h + 1) * DV] for h in heads]
        state = [s_ref[h] for h in heads]
        scores = []
        zeros = jnp.zeros((CHUNK, DK), BF16)
        for a in range(0, HEADS, 2):
            q_pair = jnp.concatenate([q_bf[a][rs], q_bf[a + 1][rs]], axis=1)
            k_pair = jnp.concatenate(
                [jnp.concatenate([k_bf[a][rs], zeros], axis=1),
                 jnp.concatenate([zeros, k_bf[a + 1][rs]], axis=1)], axis=0)
            s_pair = lax.dot_general(q_pair, k_pair, NT_DIMS, preferred_element_type=F32)
            scores += [s_pair[:, :CHUNK], s_pair[:, CHUNK:]]
        kv = [lax.dot_general((k_rot[h][rs] * kd_ref[h]).astype(BF16), v_c[h], TN_DIMS,
                              preferred_element_type=F32) for h in heads]
        gate_slice(2 * c)
        lhs = [jnp.concatenate([(scores[h] * dec_ref[h]).astype(BF16),
                                (q_rot[h][rs] * qd_ref[h]).astype(BF16)], axis=1)
               for h in heads]
        rhs = [jnp.concatenate([v_c[h], state[h].astype(BF16)], axis=0) for h in heads]
        out = [_dot(lhs[h], rhs[h]) for h in heads]
        gate_slice(2 * c + 1)
        for h in heads:
            s_ref[h] = GAMMA_CHUNK[h] * state[h] + kv[h]
            y_ref[rs, h * DV:(h + 1) * DV] = (g_act[h][rs] * _group_norm(out[h])).astype(BF16)
    bout_ref[...] = _dot(y_ref[...], wro_ref[...])

    @pl.when(i == tiles_per_seq - 1)
    def _emit_sequence_state():
        seq = g // tiles_per_seq
        for j in range(HALO):
            nc_ref[j, pl.ds(seq, 1), :] = fb_ref[0, HALO_PAD - HALO + j:HALO_PAD - HALO + j + 1, :]
        sout_ref[0] = s_ref[...]


def _mixer_prompt(x, cos2, sin2, npre, npost, win, wdw, bdw, lnw, lnb, wco, wro, wo,
                  wup_f32, wdn_f32):
    batch, seq, _ = x.shape
    rows = MIX_ROWS
    nt = seq // rows
    n_tiles = batch * nt
    assert D_MODEL % n_tiles == 0 and FFN_DIM % n_tiles == 0
    up_rows, dn_rows = D_MODEL // n_tiles, FFN_DIM // n_tiles

    def front(g):
        return jnp.minimum(g, n_tiles - 1)

    def back(g):
        return jnp.maximum(g - 1, 0)

    side = [pl.BlockSpec((up_rows, FFN_DIM), lambda g: (front(g), 0)),
            pl.BlockSpec((dn_rows, D_MODEL), lambda g: (front(g), 0))]

    in_specs = [
        pl.BlockSpec((1, rows, D_MODEL), lambda g: (front(g) // nt, front(g) % nt, 0)),
        pl.BlockSpec((rows, DK), lambda g: (front(g) % nt, 0)),
        pl.BlockSpec((rows, DK), lambda g: (front(g) % nt, 0)),
        _const_spec((1, D_MODEL)),
        _const_spec((1, D_MODEL)),
        _const_spec((D_MODEL, IN_COLS)),
        _const_spec((CONV_WIDTH, 1, CONV_DIM)),
        _const_spec((1, CONV_DIM)),
        _const_spec((1, CONV_DIM)),
        _const_spec((1, CONV_DIM)),
        _const_spec((CONV_DIM, D_MODEL)),
        _const_spec((V_W, D_MODEL)),
        _const_spec((D_MODEL, D_MODEL)),
    ] + side
    out_specs = [
        pl.BlockSpec((1, rows, D_MODEL), lambda g: (back(g) // nt, back(g) % nt, 0)),
        pl.BlockSpec((HALO, batch, CONV_DIM), lambda g: (0, 0, 0)),
        pl.BlockSpec((1, HEADS, DK, DV), lambda g: (front(g) // nt, 0, 0, 0)),
    ] + side
    out_shape = [
        jax.ShapeDtypeStruct((batch, seq, D_MODEL), F32),
        jax.ShapeDtypeStruct((HALO, batch, CONV_DIM), F32),
        jax.ShapeDtypeStruct((batch, HEADS, DK, DV), F32),
        jax.ShapeDtypeStruct((D_MODEL, FFN_DIM), BF16),
        jax.ShapeDtypeStruct((FFN_DIM, D_MODEL), BF16),
    ]
    scratch = [
        pltpu.VMEM((SUBLANES, HALO_PAD + CONV_TILE, CONV_DIM), F32),
        pltpu.VMEM((rows, CONV_DIM), F32),
        pltpu.VMEM((rows, V_W), BF16),
        pltpu.VMEM((HEADS, CHUNK, CHUNK), F32),
        pltpu.VMEM((HEADS, CHUNK, DK), F32),
        pltpu.VMEM((HEADS, CHUNK, DK), F32),
        pltpu.VMEM((CONV_WIDTH, SUBLANES, CONV_DIM), F32),
        pltpu.VMEM((HEADS, DK, DV), F32),
        pltpu.VMEM((rows, CONV_DIM), BF16),
        pltpu.VMEM((rows, 2 * D_MODEL), F32),
        pltpu.VMEM((rows, D_MODEL), F32),
        pltpu.VMEM((rows, D_MODEL), F32),
    ]
    return pl.pallas_call(
        functools.partial(_mixer_prompt_kernel, tiles_per_seq=nt, n_tiles=n_tiles),
        grid=(n_tiles + 1,), in_specs=in_specs, out_specs=out_specs, out_shape=out_shape,
        scratch_shapes=scratch,
        compiler_params=pltpu.CompilerParams(
            dimension_semantics=("arbitrary",),
            vmem_limit_bytes=VMEM_LIMIT_BYTES),
        name="mixer_prompt",
    )(x, cos2, sin2, npre, npost, win, wdw, bdw, lnw, lnb, wco, wro, wo, wup_f32, wdn_f32)


def _ffn_body(h, npre, npost, wup_ref, wdn_ref):
    hn = _rms(h, npre).astype(BF16)
    f = None
    for c in range(FFN_DIM // FFN_CHUNK):
        cols = slice(c * FFN_CHUNK, (c + 1) * FFN_CHUNK)
        up = _dot(hn, wup_ref[:, cols])
        act = jnp.square(jnp.maximum(up, 0.0)).astype(BF16)
        part = _dot(act, wdn_ref[cols, :])
        f = part if f is None else f + part
    return h + _rms(f, npost)


def _sample_state_body(blk, cache_ref, u_ref, wdw_ref, bdw_ref, qt_ref, kt_ref, v_ref, st_ref,
                       conv_ref, nc_ref, o_ref, so_ref):
    n_seq = qt_ref.shape[1]
    u = u_ref[...]
    acc = u * wdw_ref[HALO] + bdw_ref[...]
    for j in range(HALO):
        acc = acc + cache_ref[j] * wdw_ref[j]
    conv_ref[...] = acc
    nc_ref[0:HALO - 1] = cache_ref[1:HALO]
    nc_ref[HALO - 1] = u

    shift = (n_seq - blk * SAMPLE_BLOCK) % n_seq
    qt = pltpu.roll(qt_ref[...], shift, 1)
    kt = pltpu.roll(kt_ref[...], shift, 1)
    for s in range(SAMPLE_BLOCK):
        for h in range(HEADS):
            k_col = kt[h * DK:(h + 1) * DK, s:s + 1]
            q_col = qt[h * DK:(h + 1) * DK, s:s + 1]
            v_row = v_ref[s:s + 1, h * DV:(h + 1) * DV]
            new_state = GAMMA[h] * st_ref[s, h] + k_col * v_row
            so_ref[s, h] = new_state
            o_ref[s:s + 1, h * DV:(h + 1) * DV] = jnp.sum(new_state * q_col, axis=0,
                                                          keepdims=True)


def _ffn_state_kernel(h_ref, npre_ref, npost_ref, wup_ref, wdn_ref,
                      cache_ref, u_ref, wdw_ref, bdw_ref, qt_ref, kt_ref, v_ref, st_ref,
                      y_ref, conv_ref, nc_ref, o_ref, so_ref, *, steps_per_block):
    y_ref[...] = _ffn_body(h_ref[...], npre_ref[...], npost_ref[...], wup_ref, wdn_ref)
    _sample_state_body(pl.program_id(0) // steps_per_block, cache_ref, u_ref, wdw_ref, bdw_ref,
                       qt_ref, kt_ref, v_ref, st_ref, conv_ref, nc_ref, o_ref, so_ref)


def _ffn_state(h, npre, npost, wup, wdn, cache, u, wdw, bdw, qt, kt, v, state):
    n = h.shape[0]
    rows = FFN_ROWS
    steps = n // rows
    n_seq = u.shape[0]
    sb = SAMPLE_BLOCK
    spb = steps // (n_seq // sb)
    assert spb * (n_seq // sb) == steps
    in_specs = [
        pl.BlockSpec((rows, D_MODEL), lambda i: (i, 0)),
        _const_spec((1, D_MODEL)),
        _const_spec((1, D_MODEL)),
        _const_spec((D_MODEL, FFN_DIM)),
        _const_spec((FFN_DIM, D_MODEL)),
        pl.BlockSpec((HALO, sb, CONV_DIM), lambda i: (0, i // spb, 0)),
        pl.BlockSpec((sb, CONV_DIM), lambda i: (i // spb, 0)),
        _const_spec((CONV_WIDTH, 1, CONV_DIM)),
        _const_spec((1, CONV_DIM)),
        _const_spec((QK_W, n_seq)),
        _const_spec((QK_W, n_seq)),
        pl.BlockSpec((sb, V_W), lambda i: (i // spb, 0)),
        pl.BlockSpec((sb, HEADS, DK, DV), lambda i: (i // spb, 0, 0, 0)),
    ]
    out_specs = [
        pl.BlockSpec((rows, D_MODEL), lambda i: (i, 0)),
        pl.BlockSpec((sb, CONV_DIM), lambda i: (i // spb, 0)),
        pl.BlockSpec((HALO, sb, CONV_DIM), lambda i: (0, i // spb, 0)),
        pl.BlockSpec((sb, V_W), lambda i: (i // spb, 0)),
        pl.BlockSpec((sb, HEADS, DK, DV), lambda i: (i // spb, 0, 0, 0)),
    ]
    out_shape = [
        jax.ShapeDtypeStruct((n, D_MODEL), F32),
        jax.ShapeDtypeStruct((n_seq, CONV_DIM), F32),
        jax.ShapeDtypeStruct((HALO, n_seq, CONV_DIM), F32),
        jax.ShapeDtypeStruct((n_seq, V_W), F32),
        jax.ShapeDtypeStruct((n_seq, HEADS, DK, DV), F32),
    ]
    return pl.pallas_call(
        functools.partial(_ffn_state_kernel, steps_per_block=spb),
        grid=(steps,), in_specs=in_specs, out_specs=out_specs, out_shape=out_shape,
        compiler_params=pltpu.CompilerParams(
            dimension_semantics=("arbitrary",),
            vmem_limit_bytes=VMEM_LIMIT_BYTES),
        name="ffn_prompt_sample_state",
    )(h, npre, npost, wup, wdn, cache, u, wdw, bdw, qt, kt, v, state)


def _sample_pre_kernel(x_ref, cos_ref, sin_ref, npre_ref, win_hbm, wco_ref, wro_ref, wo_ref,
                       wbf_ref, wco_bf_ref, wro_bf_ref, wo_bf_ref,
                       u_ref, qt_ref, kt_ref, v_ref, g_ref, ga_ref, gb_ref,
                       xn_ref, ring_ref, sem_ref):
    j = pl.program_id(0)
    steps = pl.num_programs(0)

    def slab_copy(s, slot):
        cols = pl.ds(pl.multiple_of(s * PRE_COLS, PRE_COLS), PRE_COLS)
        return pltpu.make_async_copy(win_hbm.at[:, cols], ring_ref.at[slot], sem_ref.at[slot])

    @pl.when(j == 0)
    def _prime():
        slab_copy(0, 0).start()
        slab_copy(1, 1).start()
        xn_ref[...] = _rms(x_ref[:, 0, :], npre_ref[...]).astype(BF16)

    @pl.when(j + 2 < steps)
    def _prefetch():
        slab_copy(j + 2, (j + 2) % PRE_RING).start()

    wco_bf_ref[...] = wco_ref[...].astype(BF16)
    wro_bf_ref[...] = wro_ref[...].astype(BF16)
    wo_bf_ref[...] = wo_ref[...].astype(BF16)

    slot = j % PRE_RING
    slab_copy(j, slot).wait()
    w_slab = ring_ref[slot].astype(BF16)
    wbf_ref[...] = w_slab
    p = _dot(xn_ref[...], w_slab)

    @pl.when(j == C_CONV // PRE_COLS)
    def _glu():
        u_ref[...] = p[:, :CONV_DIM] * jax.nn.sigmoid(p[:, CONV_DIM:])

    @pl.when(j == C_Q // PRE_COLS)
    def _rotary():
        qkt = p.T
        cos_t = cos_ref[...]
        sin_t = sin_ref[...]
        half = DK // 2
        for g in range(2 * HEADS):
            x1 = qkt[g * DK:g * DK + half]
            x2 = qkt[g * DK + half:(g + 1) * DK]
            o1 = x1 * cos_t - x2 * sin_t
            o2 = x2 * cos_t + x1 * sin_t
            if g < HEADS:
                qt_ref[g * DK:g * DK + half, :] = o1
                qt_ref[g * DK + half:(g + 1) * DK, :] = o2
            else:
                k0 = (g - HEADS) * DK
                kt_ref[k0:k0 + half, :] = o1 * K_SCALE
                kt_ref[k0 + half:k0 + DK, :] = o2 * K_SCALE

    for out_ref, col in ((v_ref, C_V), (g_ref, C_G), (ga_ref, C_GATES),
                         (gb_ref, C_GATES + D_MODEL)):
        @pl.when(j == col // PRE_COLS)
        def _store(out_ref=out_ref):
            out_ref[...] = p


def _sample_pre(x, cos_t, sin_t, npre, win_f32, wco_f32, wro_f32, wo_f32):
    n = x.shape[0]
    assert all(c % PRE_COLS == 0 for c in (C_CONV, C_Q, C_V, C_G, C_GATES, IN_COLS))
    assert 2 * CONV_DIM == 2 * QK_W == V_W == D_MODEL == PRE_COLS
    steps = IN_COLS // PRE_COLS
    assert steps >= CAST_SLABS

    def whole(shape):
        zeros = (0,) * len(shape)
        return pl.BlockSpec(shape, lambda j: zeros)

    def row_slab(rows):
        return pl.BlockSpec((rows // CAST_SLABS, D_MODEL),
                            lambda j: (jnp.minimum(j, CAST_SLABS - 1), 0))

    side = [row_slab(CONV_DIM), row_slab(V_W), row_slab(D_MODEL)]
    in_specs = [
        whole((n, 1, D_MODEL)),
        whole((DK // 2, n)),
        whole((DK // 2, n)),
        whole((1, D_MODEL)),
        pl.BlockSpec(memory_space=pl.ANY),
    ] + side
    out_specs = [
        pl.BlockSpec((D_MODEL, PRE_COLS), lambda j: (0, j)),
    ] + side + [
        whole((n, CONV_DIM)),
        whole((QK_W, n)),
        whole((QK_W, n)),
        whole((n, V_W)),
        whole((n, V_W)),
        whole((n, D_MODEL)),
        whole((n, D_MODEL)),
    ]
    out_shape = [
        jax.ShapeDtypeStruct((D_MODEL, IN_COLS), BF16),
        jax.ShapeDtypeStruct((CONV_DIM, D_MODEL), BF16),
        jax.ShapeDtypeStruct((V_W, D_MODEL), BF16),
        jax.ShapeDtypeStruct((D_MODEL, D_MODEL), BF16),
        jax.ShapeDtypeStruct((n, CONV_DIM), F32),
        jax.ShapeDtypeStruct((QK_W, n), F32),
        jax.ShapeDtypeStruct((QK_W, n), F32),
        jax.ShapeDtypeStruct((n, V_W), F32),
        jax.ShapeDtypeStruct((n, V_W), F32),
        jax.ShapeDtypeStruct((n, D_MODEL), F32),
        jax.ShapeDtypeStruct((n, D_MODEL), F32),
    ]
    return pl.pallas_call(
        _sample_pre_kernel,
        grid=(steps,), in_specs=in_specs, out_specs=out_specs, out_shape=out_shape,
        scratch_shapes=[
            pltpu.VMEM((n, D_MODEL), BF16),
            pltpu.VMEM((PRE_RING, D_MODEL, PRE_COLS), F32),
            pltpu.SemaphoreType.DMA((PRE_RING,)),
        ],
        compiler_params=pltpu.CompilerParams(
            dimension_semantics=("arbitrary",),
            vmem_limit_bytes=VMEM_LIMIT_BYTES),
        name="sample_pre_cast_weights",
    )(x, cos_t, sin_t, npre, win_f32, wco_f32, wro_f32, wo_f32)


def _sample_post_kernel(x_ref, conv_ref, o_ref, g_ref, ga_ref, gb_ref, lnw_ref, lnb_ref,
                        npost_ref, nfpre_ref, nfpost_ref,
                        wco_hbm, wro_hbm, wo_hbm, wup_hbm, wdn_hbm, y_ref,
                        wco_ref, wro_ref, wo_ref, wup_ref, wdn_ref, sem_ref):
    n_chunks = FFN_DIM // FFN_CHUNK

    def up_copy(c):
        cols = pl.ds(c * FFN_CHUNK, FFN_CHUNK)
        return pltpu.make_async_copy(wup_hbm.at[:, cols], wup_ref.at[:, cols],
                                     sem_ref.at[3 + c])

    def down_copy(c):
        rows = pl.ds(c * FFN_CHUNK, FFN_CHUNK)
        return pltpu.make_async_copy(wdn_hbm.at[rows, :], wdn_ref.at[rows, :],
                                     sem_ref.at[3 + n_chunks + c])

    mix_copies = [pltpu.make_async_copy(src, dst, sem_ref.at[k]) for k, (src, dst) in
                  enumerate(((wco_hbm, wco_ref), (wro_hbm, wro_ref), (wo_hbm, wo_ref)))]
    for cp in mix_copies:
        cp.start()
    for c in range(n_chunks):
        up_copy(c).start()
        down_copy(c).start()

    a_act = _silu(_layer_norm(conv_ref[...], lnw_ref[...], lnb_ref[...])).astype(BF16)
    mix_copies[0].wait()
    a_out = _dot(a_act, wco_ref[...])
    mix_copies[1].wait()
    b_out = jnp.zeros_like(a_out)
    for h in range(HEADS):
        cols = slice(h * DV, (h + 1) * DV)
        yh = (_silu(g_ref[:, cols]) * _group_norm(o_ref[:, cols])).astype(BF16)
        b_out = b_out + _dot(yh, wro_ref[cols, :])
    merged = jax.nn.sigmoid(ga_ref[...]) * a_out + jax.nn.sigmoid(gb_ref[...]) * b_out
    mix_copies[2].wait()
    m = _dot(merged.astype(BF16), wo_ref[...])
    hres = x_ref[:, 0, :] + _rms(m, npost_ref[...])

    hn = _rms(hres, nfpre_ref[...]).astype(BF16)
    f = None
    for c in range(n_chunks):
        cols = slice(c * FFN_CHUNK, (c + 1) * FFN_CHUNK)
        up_copy(c).wait()
        up = _dot(hn, wup_ref[:, cols])
        act = jnp.square(jnp.maximum(up, 0.0)).astype(BF16)
        down_copy(c).wait()
        part = _dot(act, wdn_ref[cols, :])
        f = part if f is None else f + part
    y_ref[:, 0, :] = hres + _rms(f, nfpost_ref[...])


def _sample_post(x, conv, o, g, ga, gb, lnw, lnb, wco, wro, wo, npost, nfpre, nfpost, wup, wdn):
    vmem = pl.BlockSpec(memory_space=pltpu.VMEM)
    hbm = pl.BlockSpec(memory_space=pl.ANY)
    return pl.pallas_call(
        _sample_post_kernel,
        in_specs=[vmem] * 11 + [hbm] * 5,
        out_specs=vmem,
        out_shape=jax.ShapeDtypeStruct(x.shape, F32),
        scratch_shapes=[
            pltpu.VMEM(wco.shape, BF16),
            pltpu.VMEM(wro.shape, BF16),
            pltpu.VMEM(wo.shape, BF16),
            pltpu.VMEM(wup.shape, BF16),
            pltpu.VMEM(wdn.shape, BF16),
            pltpu.SemaphoreType.DMA((3 + 2 * (FFN_DIM // FFN_CHUNK),)),
        ],
        compiler_params=pltpu.CompilerParams(vmem_limit_bytes=VMEM_LIMIT_BYTES),
        name="sample_post",
    )(x, conv, o, g, ga, gb, lnw, lnb, npost, nfpre, nfpost, wco, wro, wo, wup, wdn)


def _rope_angles(pos):
    half = DK // 2
    freqs = 1.0 / (ROPE_BASE ** jnp.linspace(0.0, 1.0, half, dtype=F32))
    return pos[:, None] * freqs[None, :]


def kernel(x_prompt, x_sample, cache_conv, state_ret, norm_mix_pre, norm_mix_post, w_in, w_dw, b_dw, conv_ln_w, conv_ln_b, w_conv_out, w_ret_out, w_o, norm_ffn_pre, norm_ffn_post, w_ffn_up, w_ffn_down):
    batch, seq, _ = x_prompt.shape
    n_seq, dec_seq, _ = x_sample.shape
    depth = w_in.shape[0]
    assert dec_seq == 1 and seq % MIX_ROWS == 0 and (batch * seq) % FFN_ROWS == 0
    assert n_seq % SAMPLE_BLOCK == 0

    assert seq % ROPE_FINE == 0
    def both_halves(t):
        return jnp.concatenate([t, t], axis=-1)

    ang_a = both_halves(_rope_angles(ROPE_FINE * jnp.arange(seq // ROPE_FINE, dtype=F32)))
    ang_b = both_halves(_rope_angles(jnp.arange(ROPE_FINE, dtype=F32)))
    cos_a, sin_a = jnp.cos(ang_a)[:, None, :], jnp.sin(ang_a)[:, None, :]
    cos_b, sin_b = jnp.cos(ang_b)[None, :, :], jnp.sin(ang_b)[None, :, :]
    sign = jnp.where(jnp.arange(DK) < DK // 2, -1.0, 1.0).astype(F32)
    cos_p = (cos_a * cos_b - sin_a * sin_b).reshape(seq, DK)
    sin_p = ((sin_a * cos_b + cos_a * sin_b) * sign).reshape(seq, DK)
    ang_s = _rope_angles(PAST_LEN + jnp.arange(dec_seq, dtype=F32))
    cos_s = jnp.broadcast_to(jnp.cos(ang_s).T, (DK // 2, n_seq))
    sin_s = jnp.broadcast_to(jnp.sin(ang_s).T, (DK // 2, n_seq))

    xp = x_prompt
    xs = x_sample
    conv_p, ret_p, conv_s, ret_s = [], [], [], []
    for l in range(depth):
        npre = norm_mix_pre[l][None]
        npost = norm_mix_post[l][None]
        nfpre = norm_ffn_pre[l][None]
        nfpost = norm_ffn_post[l][None]
        wdw = jnp.transpose(w_dw, (1, 0, 2))[:, l:l + 1, :]
        bdw = b_dw[l][None]
        lnw = conv_ln_w[l][None]
        lnb = conv_ln_b[l][None]

        win, wco, wro, wo, u, qt, kt, v, g, ga, gb = _sample_pre(
            xs, cos_s, sin_s, npre, w_in[l], w_conv_out[l], w_ret_out[l], w_o[l])
        cache_t = jnp.transpose(cache_conv[l], (1, 0, 2))
        h_p, nc_p, s_p, wup, wdn = _mixer_prompt(
            xp, cos_p, sin_p, npre, npost, win, wdw, bdw, lnw, lnb, wco, wro, wo,
            w_ffn_up[l], w_ffn_down[l])
        y_p, conv, nc_t, o, s_s = _ffn_state(h_p.reshape(batch * seq, D_MODEL), nfpre, nfpost,
                                             wup, wdn, cache_t, u, wdw, bdw, qt, kt, v,
                                             state_ret[l])
        xp = y_p.reshape(batch, seq, D_MODEL)
        nc_s = jnp.transpose(nc_t, (1, 0, 2))
        xs = _sample_post(xs, conv, o, g, ga, gb, lnw, lnb, wco, wro, wo, npost, nfpre, nfpost,
                          wup, wdn)
        conv_p.append(jnp.transpose(nc_p, (1, 0, 2)))
        ret_p.append(s_p)
        conv_s.append(nc_s)
        ret_s.append(s_s)

    return (xp, xs, jnp.stack(conv_p), jnp.stack(ret_p),
            jnp.stack(conv_s), jnp.stack(ret_s))
```

```python
import functools
import math

import jax
import jax.numpy as jnp
from jax import lax
from jax.experimental import pallas as pl
from jax.experimental.pallas import tpu as pltpu

F32 = jnp.float32
BF16 = jnp.bfloat16

D_MODEL = 1024
CONV_DIM = 512
CONV_WIDTH = 31
HALO = CONV_WIDTH - 1
HEADS = 4
DK = 128
DV = 256
QK_W = HEADS * DK
V_W = HEADS * DV
CHUNK = 128
FFN_DIM = 4 * D_MODEL
EPS = 1e-6
ROPE_BASE = 10000.0
PAST_LEN = 16384
K_SCALE = DK ** -0.5

C_CONV = 0
C_Q = 2 * CONV_DIM
C_K = C_Q + QK_W
C_V = C_K + QK_W
C_G = C_V + V_W
C_GATES = C_G + V_W
IN_COLS = C_GATES + 2 * D_MODEL

LOG_GAMMA = tuple(math.log1p(-(2.0 ** (-5 - h))) for h in range(HEADS))
GAMMA = tuple(math.exp(lg) for lg in LOG_GAMMA)
GAMMA_CHUNK = tuple(math.exp(CHUNK * lg) for lg in LOG_GAMMA)

VMEM_LIMIT_BYTES = 61 * 1024 * 1024
SUBLANES = 8
HALO_PAD = 32
MIX_ROWS = 512
FFN_ROWS = 1024
FFN_CHUNK = 1024
CONV_TILE = 256
CONV_ROWS = 32
SAMPLE_BLOCK = 8
PRE_COLS = 1024
PRE_RING = 3
CAST_SLABS = 4
ROPE_FINE = 128

NT_DIMS = (((1,), (1,)), ((), ()))
TN_DIMS = (((0,), (0,)), ((), ()))


def _dot(a, b):
    return jnp.dot(a, b, preferred_element_type=F32)


def _rms(x, w):
    return x * lax.rsqrt(jnp.mean(x * x, axis=-1, keepdims=True) + EPS) * w


def _layer_norm(x, w, b):
    mu = jnp.mean(x, axis=-1, keepdims=True)
    xc = x - mu
    return xc * lax.rsqrt(jnp.mean(xc * xc, axis=-1, keepdims=True) + EPS) * w + b


def _silu(x):
    return x * jax.nn.sigmoid(x)


def _group_norm(o):
    return o * lax.rsqrt(jnp.mean(o * o, axis=-1, keepdims=True) + EPS)


def _const_spec(shape):
    zeros = (0,) * len(shape)
    return pl.BlockSpec(shape, lambda *_: zeros, pipeline_mode=pl.Buffered(1))


def _merge_out(aact_ref, gate_ref, bout_ref, xprev_ref, wco_ref, wo_ref, npost_ref, mid=None):
    a_out = _dot(aact_ref[...], wco_ref[...])
    mid_out = mid() if mid is not None else None
    merged = (jax.nn.sigmoid(gate_ref[:, :D_MODEL]) * a_out
              + jax.nn.sigmoid(gate_ref[:, D_MODEL:]) * bout_ref[...])
    m = _dot(merged.astype(BF16), wo_ref[...])
    return xprev_ref[...] + _rms(m, npost_ref[...]), mid_out


def _mixer_prompt_kernel(x_ref, cos_ref, sin_ref, npre_ref, npost_ref, win_ref, wdw_ref,
                         bdw_ref, lnw_ref, lnb_ref, wco_ref, wro_ref, wo_ref, wup_ref, wdn_ref,
                         h_ref, nc_ref, sout_ref, wup_bf_ref, wdn_bf_ref,
                         fb_ref, u_ref, y_ref, dec_ref, qd_ref, kd_ref, wb_ref, s_ref,
                         aact_ref, gate_ref, bout_ref, xprev_ref, *, tiles_per_seq, n_tiles):
    g = pl.program_id(0)

    @pl.when(g < n_tiles)
    def _pipelined_step():
        wup_bf_ref[...] = wup_ref[...].astype(BF16)
        wdn_bf_ref[...] = wdn_ref[...].astype(BF16)
        _mixer_step(x_ref, cos_ref, sin_ref, npre_ref, npost_ref, win_ref, wdw_ref,
                    bdw_ref, lnw_ref, lnb_ref, wco_ref, wro_ref, wo_ref,
                    h_ref, nc_ref, sout_ref,
                    fb_ref, u_ref, y_ref, dec_ref, qd_ref, kd_ref, wb_ref, s_ref,
                    aact_ref, gate_ref, bout_ref, xprev_ref, tiles_per_seq=tiles_per_seq)

    @pl.when(g == n_tiles)
    def _drain():
        h_ref[0], _ = _merge_out(aact_ref, gate_ref, bout_ref, xprev_ref, wco_ref, wo_ref,
                                 npost_ref)


def _mixer_step(x_ref, cos_ref, sin_ref, npre_ref, npost_ref, win_ref, wdw_ref,
                bdw_ref, lnw_ref, lnb_ref, wco_ref, wro_ref, wo_ref,
                h_ref, nc_ref, sout_ref,
                fb_ref, u_ref, y_ref, dec_ref, qd_ref, kd_ref, wb_ref, s_ref,
                aact_ref, gate_ref, bout_ref, xprev_ref, *, tiles_per_seq):
    g = pl.program_id(0)
    i = g % tiles_per_seq
    rows = x_ref.shape[1]

    @pl.when(g == 0)
    def _init_tables():
        aact_ref[...] = jnp.zeros(aact_ref.shape, BF16)
        gate_ref[...] = jnp.zeros(gate_ref.shape, F32)
        bout_ref[...] = jnp.zeros(bout_ref.shape, F32)
        xprev_ref[...] = jnp.zeros(xprev_ref.shape, F32)
        ii = lax.broadcasted_iota(jnp.int32, (CHUNK, CHUNK), 0)
        jj = lax.broadcasted_iota(jnp.int32, (CHUNK, CHUNK), 1)
        diff = (ii - jj).astype(F32)
        row_k = lax.broadcasted_iota(jnp.int32, (CHUNK, DK), 0).astype(F32)
        for h in range(HEADS):
            lg = LOG_GAMMA[h]
            dec_ref[h] = jnp.where(diff >= 0.0, jnp.exp(jnp.maximum(diff, 0.0) * lg), 0.0)
            qd_ref[h] = jnp.exp((row_k + 1.0) * lg)
            kd_ref[h] = jnp.exp((CHUNK - 1.0 - row_k) * lg)
        for j in range(CONV_WIDTH):
            wb_ref[j] = jnp.broadcast_to(wdw_ref[j], (SUBLANES, CONV_DIM))

    @pl.when(i == 0)
    def _start_sequence():
        fb_ref[0, 0:HALO_PAD, :] = jnp.zeros((HALO_PAD, CONV_DIM), F32)
        s_ref[...] = jnp.zeros(s_ref.shape, F32)

    x = x_ref[0]
    xn = _rms(x, npre_ref[...]).astype(BF16)
    h_ref[0], pc = _merge_out(
        aact_ref, gate_ref, bout_ref, xprev_ref, wco_ref, wo_ref, npost_ref,
        mid=lambda: _dot(xn, win_ref[:, C_CONV:C_CONV + 2 * CONV_DIM]))

    xprev_ref[...] = x

    u_ref[...] = pc[:, :CONV_DIM] * jax.nn.sigmoid(pc[:, CONV_DIM:])

    pqk = _dot(xn, win_ref[:, C_Q:C_Q + 2 * QK_W])
    pv = _dot(xn, win_ref[:, C_V:C_V + V_W]).astype(BF16)
    pg = _dot(xn, win_ref[:, C_G:C_G + V_W])

    first = HALO_PAD - HALO
    span = CONV_TILE + HALO_PAD - SUBLANES
    for t in range(rows // CONV_TILE):
        fb_ref[0, HALO_PAD:HALO_PAD + CONV_TILE, :] = u_ref[t * CONV_TILE:(t + 1) * CONV_TILE, :]
        for r in range(1, SUBLANES):
            fb_ref[r, 0:span, :] = fb_ref[0, r:r + span, :]
        for rb in range(CONV_TILE // CONV_ROWS):
            acc = jnp.broadcast_to(bdw_ref[...], (CONV_ROWS, CONV_DIM))
            for j in range(CONV_WIDTH):
                off = first + j
                base = rb * CONV_ROWS + off - off % SUBLANES
                slab = fb_ref[off % SUBLANES, base:base + CONV_ROWS, :]
                acc = acc + (slab.reshape(CONV_ROWS // SUBLANES, SUBLANES, CONV_DIM)
                             * wb_ref[j][None]).reshape(CONV_ROWS, CONV_DIM)
            r0 = t * CONV_TILE + rb * CONV_ROWS
            aact_ref[r0:r0 + CONV_ROWS, :] = _silu(
                _layer_norm(acc, lnw_ref[...], lnb_ref[...])).astype(BF16)
        fb_ref[0, 0:HALO_PAD, :] = fb_ref[0, CONV_TILE:CONV_TILE + HALO_PAD, :]

    cos2 = cos_ref[...]
    sin2 = sin_ref[...]

    def rot(t):
        return t * cos2 + pltpu.roll(t, DK // 2, 1) * sin2

    heads = range(HEADS)
    q_rot = [rot(pqk[:, h * DK:(h + 1) * DK]) for h in heads]
    q_bf = [q.astype(BF16) for q in q_rot]
    k_rot = [rot(pqk[:, QK_W + h * DK:QK_W + (h + 1) * DK]) * K_SCALE for h in heads]
    k_bf = [k.astype(BF16) for k in k_rot]
    g_act = [_silu(pg[:, h * DV:(h + 1) * DV]) for h in heads]

    n_chunks = rows // CHUNK
    n_fill = 2 * n_chunks
    gate_cols = 2 * D_MODEL // n_fill

    def gate_slice(f):
        gate_ref[:, f * gate_cols:(f + 1) * gate_cols] = _dot(
            xn, win_ref[:, C_GATES + f * gate_cols:C_GATES + (f + 1) * gate_cols])

    for c in range(n_chunks):
        rs = slice(c * CHUNK, (c + 1) * CHUNK)
        v_c = [pv[rs, h * DV:(h + 1) * DV] for h in heads]
        state = [s_ref[h] for h in heads]
        scores = []
        zeros = jnp.zeros((CHUNK, DK), BF16)
        for a in range(0, HEADS, 2):
            q_pair = jnp.concatenate([q_bf[a][rs], q_bf[a + 1][rs]], axis=1)
            k_pair = jnp.concatenate(
                [jnp.concatenate([k_bf[a][rs], zeros], axis=1),
                 jnp.concatenate([zeros, k_bf[a + 1][rs]], axis=1)], axis=0)
            s_pair = lax.dot_general(q_pair, k_pair, NT_DIMS, preferred_element_type=F32)
            scores += [s_pair[:, :CHUNK], s_pair[:, CHUNK:]]
        kv = [lax.dot_general((k_rot[h][rs] * kd_ref[h]).astype(BF16), v_c[h], TN_DIMS,
                              preferred_element_type=F32) for h in heads]
        gate_slice(2 * c)
        lhs = [jnp.concatenate([(scores[h] * dec_ref[h]).astype(BF16),
                                (q_rot[h][rs] * qd_ref[h]).astype(BF16)], axis=1)
               for h in heads]
        rhs = [jnp.concatenate([v_c[h], state[h].astype(BF16)], axis=0) for h in heads]
        out = [_dot(lhs[h], rhs[h]) for h in heads]
        gate_slice(2 * c + 1)
        for h in heads:
            s_ref[h] = GAMMA_CHUNK[h] * state[h] + kv[h]
            y_ref[rs, h * DV:(h + 1) * DV] = (g_act[h][rs] * _group_norm(out[h])).astype(BF16)
    bout_ref[...] = _dot(y_ref[...], wro_ref[...])

    @pl.when(i == tiles_per_seq - 1)
    def _emit_sequence_state():
        seq = g // tiles_per_seq
        for j in range(HALO):
            nc_ref[j, pl.ds(seq, 1), :] = fb_ref[0, HALO_PAD - HALO + j:HALO_PAD - HALO + j + 1, :]
        sout_ref[0] = s_ref[...]


def _mixer_prompt(x, cos2, sin2, npre, npost, win, wdw, bdw, lnw, lnb, wco, wro, wo,
                  wup_f32, wdn_f32):
    batch, seq, _ = x.shape
    rows = MIX_ROWS
    nt = seq // rows
    n_tiles = batch * nt
    assert D_MODEL % n_tiles == 0 and FFN_DIM % n_tiles == 0
    up_rows, dn_rows = D_MODEL // n_tiles, FFN_DIM // n_tiles

    def front(g):
        return jnp.minimum(g, n_tiles - 1)

    def back(g):
        return jnp.maximum(g - 1, 0)

    side = [pl.BlockSpec((up_rows, FFN_DIM), lambda g: (front(g), 0)),
            pl.BlockSpec((dn_rows, D_MODEL), lambda g: (front(g), 0))]

    in_specs = [
        pl.BlockSpec((1, rows, D_MODEL), lambda g: (front(g) // nt, front(g) % nt, 0)),
        pl.BlockSpec((rows, DK), lambda g: (front(g) % nt, 0)),
        pl.BlockSpec((rows, DK), lambda g: (front(g) % nt, 0)),
        _const_spec((1, D_MODEL)),
        _const_spec((1, D_MODEL)),
        _const_spec((D_MODEL, IN_COLS)),
        _const_spec((CONV_WIDTH, 1, CONV_DIM)),
        _const_spec((1, CONV_DIM)),
        _const_spec((1, CONV_DIM)),
        _const_spec((1, CONV_DIM)),
        _const_spec((CONV_DIM, D_MODEL)),
        _const_spec((V_W, D_MODEL)),
        _const_spec((D_MODEL, D_MODEL)),
    ] + side
    out_specs = [
        pl.BlockSpec((1, rows, D_MODEL), lambda g: (back(g) // nt, back(g) % nt, 0)),
        pl.BlockSpec((HALO, batch, CONV_DIM), lambda g: (0, 0, 0)),
        pl.BlockSpec((1, HEADS, DK, DV), lambda g: (front(g) // nt, 0, 0, 0)),
    ] + side
    out_shape = [
        jax.ShapeDtypeStruct((batch, seq, D_MODEL), F32),
        jax.ShapeDtypeStruct((HALO, batch, CONV_DIM), F32),
        jax.ShapeDtypeStruct((batch, HEADS, DK, DV), F32),
        jax.ShapeDtypeStruct((D_MODEL, FFN_DIM), BF16),
        jax.ShapeDtypeStruct((FFN_DIM, D_MODEL), BF16),
    ]
    scratch = [
        pltpu.VMEM((SUBLANES, HALO_PAD + CONV_TILE, CONV_DIM), F32),
        pltpu.VMEM((rows, CONV_DIM), F32),
        pltpu.VMEM((rows, V_W), BF16),
        pltpu.VMEM((HEADS, CHUNK, CHUNK), F32),
        pltpu.VMEM((HEADS, CHUNK, DK), F32),
        pltpu.VMEM((HEADS, CHUNK, DK), F32),
        pltpu.VMEM((CONV_WIDTH, SUBLANES, CONV_DIM), F32),
        pltpu.VMEM((HEADS, DK, DV), F32),
        pltpu.VMEM((rows, CONV_DIM), BF16),
        pltpu.VMEM((rows, 2 * D_MODEL), F32),
        pltpu.VMEM((rows, D_MODEL), F32),
        pltpu.VMEM((rows, D_MODEL), F32),
    ]
    return pl.pallas_call(
        functools.partial(_mixer_prompt_kernel, tiles_per_seq=nt, n_tiles=n_tiles),
        grid=(n_tiles + 1,), in_specs=in_specs, out_specs=out_specs, out_shape=out_shape,
        scratch_shapes=scratch,
        compiler_params=pltpu.CompilerParams(
            dimension_semantics=("arbitrary",),
            vmem_limit_bytes=VMEM_LIMIT_BYTES),
        name="mixer_prompt",
    )(x, cos2, sin2, npre, npost, win, wdw, bdw, lnw, lnb, wco, wro, wo, wup_f32, wdn_f32)


def _ffn_body(h, npre, npost, wup_ref, wdn_ref):
    hn = _rms(h, npre).astype(BF16)
    f = None
    for c in range(FFN_DIM // FFN_CHUNK):
        cols = slice(c * FFN_CHUNK, (c + 1) * FFN_CHUNK)
        up = _dot(hn, wup_ref[:, cols])
        act = jnp.square(jnp.maximum(up, 0.0)).astype(BF16)
        part = _dot(act, wdn_ref[cols, :])
        f = part if f is None else f + part
    return h + _rms(f, npost)


def _sample_state_body(blk, cache_ref, u_ref, wdw_ref, bdw_ref, qt_ref, kt_ref, v_ref, st_ref,
                       conv_ref, nc_ref, o_ref, so_ref):
    n_seq = qt_ref.shape[1]
    u = u_ref[...]
    acc = u * wdw_ref[HALO] + bdw_ref[...]
    for j in range(HALO):
        acc = acc + cache_ref[j] * wdw_ref[j]
    conv_ref[...] = acc
    nc_ref[0:HALO - 1] = cache_ref[1:HALO]
    nc_ref[HALO - 1] = u

    shift = (n_seq - blk * SAMPLE_BLOCK) % n_seq
    qt = pltpu.roll(qt_ref[...], shift, 1)
    kt = pltpu.roll(kt_ref[...], shift, 1)
    for s in range(SAMPLE_BLOCK):
        for h in range(HEADS):
            k_col = kt[h * DK:(h + 1) * DK, s:s + 1]
            q_col = qt[h * DK:(h + 1) * DK, s:s + 1]
            v_row = v_ref[s:s + 1, h * DV:(h + 1) * DV]
            new_state = GAMMA[h] * st_ref[s, h] + k_col * v_row
            so_ref[s, h] = new_state
            o_ref[s:s + 1, h * DV:(h + 1) * DV] = jnp.sum(new_state * q_col, axis=0,
                                                          keepdims=True)


def _ffn_state_kernel(h_ref, npre_ref, npost_ref, wup_ref, wdn_ref,
                      cache_ref, u_ref, wdw_ref, bdw_ref, qt_ref, kt_ref, v_ref, st_ref,
                      y_ref, conv_ref, nc_ref, o_ref, so_ref, *, steps_per_block):
    y_ref[...] = _ffn_body(h_ref[...], npre_ref[...], npost_ref[...], wup_ref, wdn_ref)
    _sample_state_body(pl.program_id(0) // steps_per_block, cache_ref, u_ref, wdw_ref, bdw_ref,
                       qt_ref, kt_ref, v_ref, st_ref, conv_ref, nc_ref, o_ref, so_ref)


def _ffn_state(h, npre, npost, wup, wdn, cache, u, wdw, bdw, qt, kt, v, state):
    n = h.shape[0]
    rows = FFN_ROWS
    steps = n // rows
    n_seq = u.shape[0]
    sb = SAMPLE_BLOCK
    spb = steps // (n_seq // sb)
    assert spb * (n_seq // sb) == steps
    in_specs = [
        pl.BlockSpec((rows, D_MODEL), lambda i: (i, 0)),
        _const_spec((1, D_MODEL)),
        _const_spec((1, D_MODEL)),
        _const_spec((D_MODEL, FFN_DIM)),
        _const_spec((FFN_DIM, D_MODEL)),
        pl.BlockSpec((HALO, sb, CONV_DIM), lambda i: (0, i // spb, 0)),
        pl.BlockSpec((sb, CONV_DIM), lambda i: (i // spb, 0)),
        _const_spec((CONV_WIDTH, 1, CONV_DIM)),
        _const_spec((1, CONV_DIM)),
        _const_spec((QK_W, n_seq)),
        _const_spec((QK_W, n_seq)),
        pl.BlockSpec((sb, V_W), lambda i: (i // spb, 0)),
        pl.BlockSpec((sb, HEADS, DK, DV), lambda i: (i // spb, 0, 0, 0)),
    ]
    out_specs = [
        pl.BlockSpec((rows, D_MODEL), lambda i: (i, 0)),
        pl.BlockSpec((sb, CONV_DIM), lambda i: (i // spb, 0)),
        pl.BlockSpec((HALO, sb, CONV_DIM), lambda i: (0, i // spb, 0)),
        pl.BlockSpec((sb, V_W), lambda i: (i // spb, 0)),
        pl.BlockSpec((sb, HEADS, DK, DV), lambda i: (i // spb, 0, 0, 0)),
    ]
    out_shape = [
        jax.ShapeDtypeStruct((n, D_MODEL), F32),
        jax.ShapeDtypeStruct((n_seq, CONV_DIM), F32),
        jax.ShapeDtypeStruct((HALO, n_seq, CONV_DIM), F32),
        jax.ShapeDtypeStruct((n_seq, V_W), F32),
        jax.ShapeDtypeStruct((n_seq, HEADS, DK, DV), F32),
    ]
    return pl.pallas_call(
        functools.partial(_ffn_state_kernel, steps_per_block=spb),
        grid=(steps,), in_specs=in_specs, out_specs=out_specs, out_shape=out_shape,
        compiler_params=pltpu.CompilerParams(
            dimension_semantics=("arbitrary",),
            vmem_limit_bytes=VMEM_LIMIT_BYTES),
        name="ffn_prompt_sample_state",
    )(h, npre, npost, wup, wdn, cache, u, wdw, bdw, qt, kt, v, state)


def _sample_pre_kernel(x_ref, cos_ref, sin_ref, npre_ref, win_hbm, wco_ref, wro_ref, wo_ref,
                       wbf_ref, wco_bf_ref, wro_bf_ref, wo_bf_ref,
                       u_ref, qt_ref, kt_ref, v_ref, g_ref, ga_ref, gb_ref,
                       xn_ref, ring_ref, sem_ref):
    j = pl.program_id(0)
    steps = pl.num_programs(0)

    def slab_copy(s, slot):
        cols = pl.ds(pl.multiple_of(s * PRE_COLS, PRE_COLS), PRE_COLS)
        return pltpu.make_async_copy(win_hbm.at[:, cols], ring_ref.at[slot], sem_ref.at[slot])

    @pl.when(j == 0)
    def _prime():
        slab_copy(0, 0).start()
        slab_copy(1, 1).start()
        xn_ref[...] = _rms(x_ref[:, 0, :], npre_ref[...]).astype(BF16)

    @pl.when(j + 2 < steps)
    def _prefetch():
        slab_copy(j + 2, (j + 2) % PRE_RING).start()

    wco_bf_ref[...] = wco_ref[...].astype(BF16)
    wro_bf_ref[...] = wro_ref[...].astype(BF16)
    wo_bf_ref[...] = wo_ref[...].astype(BF16)

    slot = j % PRE_RING
    slab_copy(j, slot).wait()
    w_slab = ring_ref[slot].astype(BF16)
    wbf_ref[...] = w_slab
    p = _dot(xn_ref[...], w_slab)

    @pl.when(j == C_CONV // PRE_COLS)
    def _glu():
        u_ref[...] = p[:, :CONV_DIM] * jax.nn.sigmoid(p[:, CONV_DIM:])

    @pl.when(j == C_Q // PRE_COLS)
    def _rotary():
        qkt = p.T
        cos_t = cos_ref[...]
        sin_t = sin_ref[...]
        half = DK // 2
        for g in range(2 * HEADS):
            x1 = qkt[g * DK:g * DK + half]
            x2 = qkt[g * DK + half:(g + 1) * DK]
            o1 = x1 * cos_t - x2 * sin_t
            o2 = x2 * cos_t + x1 * sin_t
            if g < HEADS:
                qt_ref[g * DK:g * DK + half, :] = o1
                qt_ref[g * DK + half:(g + 1) * DK, :] = o2
            else:
                k0 = (g - HEADS) * DK
                kt_ref[k0:k0 + half, :] = o1 * K_SCALE
                kt_ref[k0 + half:k0 + DK, :] = o2 * K_SCALE

    for out_ref, col in ((v_ref, C_V), (g_ref, C_G), (ga_ref, C_GATES),
                         (gb_ref, C_GATES + D_MODEL)):
        @pl.when(j == col // PRE_COLS)
        def _store(out_ref=out_ref):
            out_ref[...] = p


def _sample_pre(x, cos_t, sin_t, npre, win_f32, wco_f32, wro_f32, wo_f32):
    n = x.shape[0]
    assert all(c % PRE_COLS == 0 for c in (C_CONV, C_Q, C_V, C_G, C_GATES, IN_COLS))
    assert 2 * CONV_DIM == 2 * QK_W == V_W == D_MODEL == PRE_COLS
    steps = IN_COLS // PRE_COLS
    assert steps >= CAST_SLABS

    def whole(shape):
        zeros = (0,) * len(shape)
        return pl.BlockSpec(shape, lambda j: zeros)

    def row_slab(rows):
        return pl.BlockSpec((rows // CAST_SLABS, D_MODEL),
                            lambda j: (jnp.minimum(j, CAST_SLABS - 1), 0))

    side = [row_slab(CONV_DIM), row_slab(V_W), row_slab(D_MODEL)]
    in_specs = [
        whole((n, 1, D_MODEL)),
        whole((DK // 2, n)),
        whole((DK // 2, n)),
        whole((1, D_MODEL)),
        pl.BlockSpec(memory_space=pl.ANY),
    ] + side
    out_specs = [
        pl.BlockSpec((D_MODEL, PRE_COLS), lambda j: (0, j)),
    ] + side + [
        whole((n, CONV_DIM)),
        whole((QK_W, n)),
        whole((QK_W, n)),
        whole((n, V_W)),
        whole((n, V_W)),
        whole((n, D_MODEL)),
        whole((n, D_MODEL)),
    ]
    out_shape = [
        jax.ShapeDtypeStruct((D_MODEL, IN_COLS), BF16),
        jax.ShapeDtypeStruct((CONV_DIM, D_MODEL), BF16),
        jax.ShapeDtypeStruct((V_W, D_MODEL), BF16),
        jax.ShapeDtypeStruct((D_MODEL, D_MODEL), BF16),
        jax.ShapeDtypeStruct((n, CONV_DIM), F32),
        jax.ShapeDtypeStruct((QK_W, n), F32),
        jax.ShapeDtypeStruct((QK_W, n), F32),
        jax.ShapeDtypeStruct((n, V_W), F32),
        jax.ShapeDtypeStruct((n, V_W), F32),
        jax.ShapeDtypeStruct((n, D_MODEL), F32),
        jax.ShapeDtypeStruct((n, D_MODEL), F32),
    ]
    return pl.pallas_call(
        _sample_pre_kernel,
        grid=(steps,), in_specs=in_specs, out_specs=out_specs, out_shape=out_shape,
        scratch_shapes=[
            pltpu.VMEM((n, D_MODEL), BF16),
            pltpu.VMEM((PRE_RING, D_MODEL, PRE_COLS), F32),
            pltpu.SemaphoreType.DMA((PRE_RING,)),
        ],
        compiler_params=pltpu.CompilerParams(
            dimension_semantics=("arbitrary",),
            vmem_limit_bytes=VMEM_LIMIT_BYTES),
        name="sample_pre_cast_weights",
    )(x, cos_t, sin_t, npre, win_f32, wco_f32, wro_f32, wo_f32)


def _sample_post_kernel(x_ref, conv_ref, o_ref, g_ref, ga_ref, gb_ref, lnw_ref, lnb_ref,
                        npost_ref, nfpre_ref, nfpost_ref,
                        wco_hbm, wro_hbm, wo_hbm, wup_hbm, wdn_hbm, y_ref,
                        wco_ref, wro_ref, wo_ref, wup_ref, wdn_ref, sem_ref):
    n_chunks = FFN_DIM // FFN_CHUNK

    def up_copy(c):
        cols = pl.ds(c * FFN_CHUNK, FFN_CHUNK)
        return pltpu.make_async_copy(wup_hbm.at[:, cols], wup_ref.at[:, cols],
                                     sem_ref.at[3 + c])

    def down_copy(c):
        rows = pl.ds(c * FFN_CHUNK, FFN_CHUNK)
        return pltpu.make_async_copy(wdn_hbm.at[rows, :], wdn_ref.at[rows, :],
                                     sem_ref.at[3 + n_chunks + c])

    mix_copies = [pltpu.make_async_copy(src, dst, sem_ref.at[k]) for k, (src, dst) in
                  enumerate(((wco_hbm, wco_ref), (wro_hbm, wro_ref), (wo_hbm, wo_ref)))]
    in_order = list(mix_copies)
    for c in range(n_chunks):
        in_order += [up_copy(c), down_copy(c)]
    for k, cp in enumerate(in_order):
        cp.start(priority=k % 2)

    a_act = _silu(_layer_norm(conv_ref[...], lnw_ref[...], lnb_ref[...])).astype(BF16)
    mix_copies[0].wait()
    a_out = _dot(a_act, wco_ref[...])
    mix_copies[1].wait()
    b_out = jnp.zeros_like(a_out)
    for h in range(HEADS):
        cols = slice(h * DV, (h + 1) * DV)
        yh = (_silu(g_ref[:, cols]) * _group_norm(o_ref[:, cols])).astype(BF16)
        b_out = b_out + _dot(yh, wro_ref[cols, :])
    merged = jax.nn.sigmoid(ga_ref[...]) * a_out + jax.nn.sigmoid(gb_ref[...]) * b_out
    mix_copies[2].wait()
    m = _dot(merged.astype(BF16), wo_ref[...])
    hres = x_ref[:, 0, :] + _rms(m, npost_ref[...])

    hn = _rms(hres, nfpre_ref[...]).astype(BF16)
    f = None
    for c in range(n_chunks):
        cols = slice(c * FFN_CHUNK, (c + 1) * FFN_CHUNK)
        up_copy(c).wait()
        up = _dot(hn, wup_ref[:, cols])
        act = jnp.square(jnp.maximum(up, 0.0)).astype(BF16)
        down_copy(c).wait()
        part = _dot(act, wdn_ref[cols, :])
        f = part if f is None else f + part
    y_ref[:, 0, :] = hres + _rms(f, nfpost_ref[...])


def _sample_post(x, conv, o, g, ga, gb, lnw, lnb, wco, wro, wo, npost, nfpre, nfpost, wup, wdn):
    vmem = pl.BlockSpec(memory_space=pltpu.VMEM)
    hbm = pl.BlockSpec(memory_space=pl.ANY)
    return pl.pallas_call(
        _sample_post_kernel,
        in_specs=[vmem] * 11 + [hbm] * 5,
        out_specs=vmem,
        out_shape=jax.ShapeDtypeStruct(x.shape, F32),
        scratch_shapes=[
            pltpu.VMEM(wco.shape, BF16),
            pltpu.VMEM(wro.shape, BF16),
            pltpu.VMEM(wo.shape, BF16),
            pltpu.VMEM(wup.shape, BF16),
            pltpu.VMEM(wdn.shape, BF16),
            pltpu.SemaphoreType.DMA((3 + 2 * (FFN_DIM // FFN_CHUNK),)),
        ],
        compiler_params=pltpu.CompilerParams(vmem_limit_bytes=VMEM_LIMIT_BYTES),
        name="sample_post",
    )(x, conv, o, g, ga, gb, lnw, lnb, npost, nfpre, nfpost, wco, wro, wo, wup, wdn)


def _rope_angles(pos):
    half = DK // 2
    freqs = 1.0 / (ROPE_BASE ** jnp.linspace(0.0, 1.0, half, dtype=F32))
    return pos[:, None] * freqs[None, :]


def kernel(x_prompt, x_sample, cache_conv, state_ret, norm_mix_pre, norm_mix_post, w_in, w_dw, b_dw, conv_ln_w, conv_ln_b, w_conv_out, w_ret_out, w_o, norm_ffn_pre, norm_ffn_post, w_ffn_up, w_ffn_down):
    batch, seq, _ = x_prompt.shape
    n_seq, dec_seq, _ = x_sample.shape
    depth = w_in.shape[0]
    assert dec_seq == 1 and seq % MIX_ROWS == 0 and (batch * seq) % FFN_ROWS == 0
    assert n_seq % SAMPLE_BLOCK == 0

    assert seq % ROPE_FINE == 0
    def both_halves(t):
        return jnp.concatenate([t, t], axis=-1)

    ang_a = both_halves(_rope_angles(ROPE_FINE * jnp.arange(seq // ROPE_FINE, dtype=F32)))
    ang_b = both_halves(_rope_angles(jnp.arange(ROPE_FINE, dtype=F32)))
    cos_a, sin_a = jnp.cos(ang_a)[:, None, :], jnp.sin(ang_a)[:, None, :]
    cos_b, sin_b = jnp.cos(ang_b)[None, :, :], jnp.sin(ang_b)[None, :, :]
    sign = jnp.where(jnp.arange(DK) < DK // 2, -1.0, 1.0).astype(F32)
    cos_p = (cos_a * cos_b - sin_a * sin_b).reshape(seq, DK)
    sin_p = ((sin_a * cos_b + cos_a * sin_b) * sign).reshape(seq, DK)
    ang_s = _rope_angles(PAST_LEN + jnp.arange(dec_seq, dtype=F32))
    cos_s = jnp.broadcast_to(jnp.cos(ang_s).T, (DK // 2, n_seq))
    sin_s = jnp.broadcast_to(jnp.sin(ang_s).T, (DK // 2, n_seq))

    xp = x_prompt
    xs = x_sample
    conv_p, ret_p, conv_s, ret_s = [], [], [], []
    for l in range(depth):
        npre = norm_mix_pre[l][None]
        npost = norm_mix_post[l][None]
        nfpre = norm_ffn_pre[l][None]
        nfpost = norm_ffn_post[l][None]
        wdw = jnp.transpose(w_dw, (1, 0, 2))[:, l:l + 1, :]
        bdw = b_dw[l][None]
        lnw = conv_ln_w[l][None]
        lnb = conv_ln_b[l][None]

        win, wco, wro, wo, u, qt, kt, v, g, ga, gb = _sample_pre(
            xs, cos_s, sin_s, npre, w_in[l], w_conv_out[l], w_ret_out[l], w_o[l])
        cache_t = jnp.transpose(cache_conv[l], (1, 0, 2))
        h_p, nc_p, s_p, wup, wdn = _mixer_prompt(
            xp, cos_p, sin_p, npre, npost, win, wdw, bdw, lnw, lnb, wco, wro, wo,
            w_ffn_up[l], w_ffn_down[l])
        y_p, conv, nc_t, o, s_s = _ffn_state(h_p.reshape(batch * seq, D_MODEL), nfpre, nfpost,
                                             wup, wdn, cache_t, u, wdw, bdw, qt, kt, v,
                                             state_ret[l])
        xp = y_p.reshape(batch, seq, D_MODEL)
        nc_s = jnp.transpose(nc_t, (1, 0, 2))
        xs = _sample_post(xs, conv, o, g, ga, gb, lnw, lnb, wco, wro, wo, npost, nfpre, nfpost,
                          wup, wdn)
        conv_p.append(jnp.transpose(nc_p, (1, 0, 2)))
        ret_p.append(s_p)
        conv_s.append(nc_s)
        ret_s.append(s_s)

    return (xp, xs, jnp.stack(conv_p), jnp.stack(ret_p),
            jnp.stack(conv_s), jnp.stack(ret_s))
```

```python
import functools
import math

import jax
import jax.numpy as jnp
from jax import lax
from jax.experimental import pallas as pl
from jax.experimental.pallas import tpu as pltpu

F32 = jnp.float32
BF16 = jnp.bfloat16

D_MODEL = 1024
CONV_DIM = 512
CONV_WIDTH = 31
HALO = CONV_WIDTH - 1
HEADS = 4
DK = 128
DV = 256
QK_W = HEADS * DK
V_W = HEADS * DV
CHUNK = 128
FFN_DIM = 4 * D_MODEL
EPS = 1e-6
ROPE_BASE = 10000.0
PAST_LEN = 16384
K_SCALE = DK ** -0.5

C_CONV = 0
C_Q = 2 * CONV_DIM
C_K = C_Q + QK_W
C_V = C_K + QK_W
C_G = C_V + V_W
C_GATES = C_G + V_W
IN_COLS = C_GATES + 2 * D_MODEL

LOG_GAMMA = tuple(math.log1p(-(2.0 ** (-5 - h))) for h in range(HEADS))
GAMMA = tuple(math.exp(lg) for lg in LOG_GAMMA)
GAMMA_CHUNK = tuple(math.exp(CHUNK * lg) for lg in LOG_GAMMA)

VMEM_LIMIT_BYTES = 61 * 1024 * 1024
SUBLANES = 8
HALO_PAD = 32
MIX_ROWS = 512
FFN_ROWS = 1024
FFN_CHUNK = 1024
CONV_TILE = 256
CONV_ROWS = 32
SAMPLE_BLOCK = 8
PRE_COLS = 1024
PRE_RING = 4
CAST_SLABS = 4
ROPE_FINE = 128

NT_DIMS = (((1,), (1,)), ((), ()))
TN_DIMS = (((0,), (0,)), ((), ()))


def _dot(a, b):
    return jnp.dot(a, b, preferred_element_type=F32)


def _rms(x, w):
    return x * lax.rsqrt(jnp.mean(x * x, axis=-1, keepdims=True) + EPS) * w


def _layer_norm(x, w, b):
    mu = jnp.mean(x, axis=-1, keepdims=True)
    xc = x - mu
    return xc * lax.rsqrt(jnp.mean(xc * xc, axis=-1, keepdims=True) + EPS) * w + b


def _silu(x):
    return x * jax.nn.sigmoid(x)


def _group_norm(o):
    return o * lax.rsqrt(jnp.mean(o * o, axis=-1, keepdims=True) + EPS)


def _const_spec(shape):
    zeros = (0,) * len(shape)
    return pl.BlockSpec(shape, lambda *_: zeros, pipeline_mode=pl.Buffered(1))


def _merge_out(aact_ref, gate_ref, bout_ref, xprev_ref, wco_ref, wo_ref, npost_ref, mid=None):
    a_out = _dot(aact_ref[...], wco_ref[...])
    mid_out = mid() if mid is not None else None
    merged = (jax.nn.sigmoid(gate_ref[:, :D_MODEL]) * a_out
              + jax.nn.sigmoid(gate_ref[:, D_MODEL:]) * bout_ref[...])
    m = _dot(merged.astype(BF16), wo_ref[...])
    return xprev_ref[...] + _rms(m, npost_ref[...]), mid_out


def _mixer_prompt_kernel(x_ref, cos_ref, sin_ref, npre_ref, npost_ref, win_ref, wdw_ref,
                         bdw_ref, lnw_ref, lnb_ref, wco_ref, wro_ref, wo_ref, wup_ref, wdn_ref,
                         h_ref, nc_ref, sout_ref, wup_bf_ref, wdn_bf_ref,
                         fb_ref, u_ref, y_ref, dec_ref, qd_ref, kd_ref, wb_ref, s_ref,
                         aact_ref, gate_ref, bout_ref, xprev_ref, *, tiles_per_seq, n_tiles):
    g = pl.program_id(0)

    @pl.when(g < n_tiles)
    def _pipelined_step():
        wup_bf_ref[...] = wup_ref[...].astype(BF16)
        wdn_bf_ref[...] = wdn_ref[...].astype(BF16)
        _mixer_step(x_ref, cos_ref, sin_ref, npre_ref, npost_ref, win_ref, wdw_ref,
                    bdw_ref, lnw_ref, lnb_ref, wco_ref, wro_ref, wo_ref,
                    h_ref, nc_ref, sout_ref,
                    fb_ref, u_ref, y_ref, dec_ref, qd_ref, kd_ref, wb_ref, s_ref,
                    aact_ref, gate_ref, bout_ref, xprev_ref, tiles_per_seq=tiles_per_seq)

    @pl.when(g == n_tiles)
    def _drain():
        h_ref[0], _ = _merge_out(aact_ref, gate_ref, bout_ref, xprev_ref, wco_ref, wo_ref,
                                 npost_ref)


def _mixer_step(x_ref, cos_ref, sin_ref, npre_ref, npost_ref, win_ref, wdw_ref,
                bdw_ref, lnw_ref, lnb_ref, wco_ref, wro_ref, wo_ref,
                h_ref, nc_ref, sout_ref,
                fb_ref, u_ref, y_ref, dec_ref, qd_ref, kd_ref, wb_ref, s_ref,
                aact_ref, gate_ref, bout_ref, xprev_ref, *, tiles_per_seq):
    g = pl.program_id(0)
    i = g % tiles_per_seq
    rows = x_ref.shape[1]

    @pl.when(g == 0)
    def _init_tables():
        aact_ref[...] = jnp.zeros(aact_ref.shape, BF16)
        gate_ref[...] = jnp.zeros(gate_ref.shape, F32)
        bout_ref[...] = jnp.zeros(bout_ref.shape, F32)
        xprev_ref[...] = jnp.zeros(xprev_ref.shape, F32)
        ii = lax.broadcasted_iota(jnp.int32, (CHUNK, CHUNK), 0)
        jj = lax.broadcasted_iota(jnp.int32, (CHUNK, CHUNK), 1)
        diff = (ii - jj).astype(F32)
        row_k = lax.broadcasted_iota(jnp.int32, (CHUNK, DK), 0).astype(F32)
        for h in range(HEADS):
            lg = LOG_GAMMA[h]
            dec_ref[h] = jnp.where(diff >= 0.0, jnp.exp(jnp.maximum(diff, 0.0) * lg), 0.0)
            qd_ref[h] = jnp.exp((row_k + 1.0) * lg)
            kd_ref[h] = jnp.exp((CHUNK - 1.0 - row_k) * lg)
        for j in range(CONV_WIDTH):
            wb_ref[j] = jnp.broadcast_to(wdw_ref[j], (SUBLANES, CONV_DIM))

    @pl.when(i == 0)
    def _start_sequence():
        fb_ref[0, 0:HALO_PAD, :] = jnp.zeros((HALO_PAD, CONV_DIM), F32)
        s_ref[...] = jnp.zeros(s_ref.shape, F32)

    x = x_ref[0]
    xn = _rms(x, npre_ref[...]).astype(BF16)
    h_ref[0], pc = _merge_out(
        aact_ref, gate_ref, bout_ref, xprev_ref, wco_ref, wo_ref, npost_ref,
        mid=lambda: _dot(xn, win_ref[:, C_CONV:C_CONV + 2 * CONV_DIM]))

    xprev_ref[...] = x

    u_ref[...] = pc[:, :CONV_DIM] * jax.nn.sigmoid(pc[:, CONV_DIM:])

    pqk = _dot(xn, win_ref[:, C_Q:C_Q + 2 * QK_W])
    pv = _dot(xn, win_ref[:, C_V:C_V + V_W]).astype(BF16)
    pg = _dot(xn, win_ref[:, C_G:C_G + V_W])

    first = HALO_PAD - HALO
    span = CONV_TILE + HALO_PAD - SUBLANES
    for t in range(rows // CONV_TILE):
        fb_ref[0, HALO_PAD:HALO_PAD + CONV_TILE, :] = u_ref[t * CONV_TILE:(t + 1) * CONV_TILE, :]
        for r in range(1, SUBLANES):
            fb_ref[r, 0:span, :] = fb_ref[0, r:r + span, :]
        for rb in range(CONV_TILE // CONV_ROWS):
            acc = jnp.broadcast_to(bdw_ref[...], (CONV_ROWS, CONV_DIM))
            for j in range(CONV_WIDTH):
                off = first + j
                base = rb * CONV_ROWS + off - off % SUBLANES
                slab = fb_ref[off % SUBLANES, base:base + CONV_ROWS, :]
                acc = acc + (slab.reshape(CONV_ROWS // SUBLANES, SUBLANES, CONV_DIM)
                             * wb_ref[j][None]).reshape(CONV_ROWS, CONV_DIM)
            r0 = t * CONV_TILE + rb * CONV_ROWS
            aact_ref[r0:r0 + CONV_ROWS, :] = _silu(
                _layer_norm(acc, lnw_ref[...], lnb_ref[...])).astype(BF16)
        fb_ref[0, 0:HALO_PAD, :] = fb_ref[0, CONV_TILE:CONV_TILE + HALO_PAD, :]

    cos2 = cos_ref[...]
    sin2 = sin_ref[...]

    def rot(t):
        return t * cos2 + pltpu.roll(t, DK // 2, 1) * sin2

    heads = range(HEADS)
    q_rot = [rot(pqk[:, h * DK:(h + 1) * DK]) for h in heads]
    q_bf = [q.astype(BF16) for q in q_rot]
    k_rot = [rot(pqk[:, QK_W + h * DK:QK_W + (h + 1) * DK]) * K_SCALE for h in heads]
    k_bf = [k.astype(BF16) for k in k_rot]
    g_act = [_silu(pg[:, h * DV:(h + 1) * DV]) for h in heads]

    n_chunks = rows // CHUNK
    n_fill = 2 * n_chunks
    gate_cols = 2 * D_MODEL // n_fill

    def gate_slice(f):
        gate_ref[:, f * gate_cols:(f + 1) * gate_cols] = _dot(
            xn, win_ref[:, C_GATES + f * gate_cols:C_GATES + (f + 1) * gate_cols])

    for c in range(n_chunks):
        rs = slice(c * CHUNK, (c + 1) * CHUNK)
        v_c = [pv[rs, h * DV:(h + 1) * DV] for h in heads]
        state = [s_ref[h] for h in heads]
        scores = []
        zeros = jnp.zeros((CHUNK, DK), BF16)
        for a in range(0, HEADS, 2):
            q_pair = jnp.concatenate([q_bf[a][rs], q_bf[a + 1][rs]], axis=1)
            k_pair = jnp.concatenate(
                [jnp.concatenate([k_bf[a][rs], zeros], axis=1),
                 jnp.concatenate([zeros, k_bf[a + 1][rs]], axis=1)], axis=0)
            s_pair = lax.dot_general(q_pair, k_pair, NT_DIMS, preferred_element_type=F32)
            scores += [s_pair[:, :CHUNK], s_pair[:, CHUNK:]]
        kv = [lax.dot_general((k_rot[h][rs] * kd_ref[h]).astype(BF16), v_c[h], TN_DIMS,
                              preferred_element_type=F32) for h in heads]
        gate_slice(2 * c)
        lhs = [jnp.concatenate([(scores[h] * dec_ref[h]).astype(BF16),
                                (q_rot[h][rs] * qd_ref[h]).astype(BF16)], axis=1)
               for h in heads]
        rhs = [jnp.concatenate([v_c[h], state[h].astype(BF16)], axis=0) for h in heads]
        out = [_dot(lhs[h], rhs[h]) for h in heads]
        gate_slice(2 * c + 1)
        for h in heads:
            s_ref[h] = GAMMA_CHUNK[h] * state[h] + kv[h]
            y_ref[rs, h * DV:(h + 1) * DV] = (g_act[h][rs] * _group_norm(out[h])).astype(BF16)
    bout_ref[...] = _dot(y_ref[...], wro_ref[...])

    @pl.when(i == tiles_per_seq - 1)
    def _emit_sequence_state():
        seq = g // tiles_per_seq
        for j in range(HALO):
            nc_ref[j, pl.ds(seq, 1), :] = fb_ref[0, HALO_PAD - HALO + j:HALO_PAD - HALO + j + 1, :]
        sout_ref[0] = s_ref[...]


def _mixer_prompt(x, cos2, sin2, npre, npost, win, wdw, bdw, lnw, lnb, wco, wro, wo,
                  wup_f32, wdn_f32):
    batch, seq, _ = x.shape
    rows = MIX_ROWS
    nt = seq // rows
    n_tiles = batch * nt
    assert D_MODEL % n_tiles == 0 and FFN_DIM % n_tiles == 0
    up_rows, dn_rows = D_MODEL // n_tiles, FFN_DIM // n_tiles

    def front(g):
        return jnp.minimum(g, n_tiles - 1)

    def back(g):
        return jnp.maximum(g - 1, 0)

    side = [pl.BlockSpec((up_rows, FFN_DIM), lambda g: (front(g), 0)),
            pl.BlockSpec((dn_rows, D_MODEL), lambda g: (front(g), 0))]

    in_specs = [
        pl.BlockSpec((1, rows, D_MODEL), lambda g: (front(g) // nt, front(g) % nt, 0)),
        pl.BlockSpec((rows, DK), lambda g: (front(g) % nt, 0)),
        pl.BlockSpec((rows, DK), lambda g: (front(g) % nt, 0)),
        _const_spec((1, D_MODEL)),
        _const_spec((1, D_MODEL)),
        _const_spec((D_MODEL, IN_COLS)),
        _const_spec((CONV_WIDTH, 1, CONV_DIM)),
        _const_spec((1, CONV_DIM)),
        _const_spec((1, CONV_DIM)),
        _const_spec((1, CONV_DIM)),
        _const_spec((CONV_DIM, D_MODEL)),
        _const_spec((V_W, D_MODEL)),
        _const_spec((D_MODEL, D_MODEL)),
    ] + side
    out_specs = [
        pl.BlockSpec((1, rows, D_MODEL), lambda g: (back(g) // nt, back(g) % nt, 0)),
        pl.BlockSpec((HALO, batch, CONV_DIM), lambda g: (0, 0, 0)),
        pl.BlockSpec((1, HEADS, DK, DV), lambda g: (front(g) // nt, 0, 0, 0)),
    ] + side
    out_shape = [
        jax.ShapeDtypeStruct((batch, seq, D_MODEL), F32),
        jax.ShapeDtypeStruct((HALO, batch, CONV_DIM), F32),
        jax.ShapeDtypeStruct((batch, HEADS, DK, DV), F32),
        jax.ShapeDtypeStruct((D_MODEL, FFN_DIM), BF16),
        jax.ShapeDtypeStruct((FFN_DIM, D_MODEL), BF16),
    ]
    scratch = [
        pltpu.VMEM((SUBLANES, HALO_PAD + CONV_TILE, CONV_DIM), F32),
        pltpu.VMEM((rows, CONV_DIM), F32),
        pltpu.VMEM((rows, V_W), BF16),
        pltpu.VMEM((HEADS, CHUNK, CHUNK), F32),
        pltpu.VMEM((HEADS, CHUNK, DK), F32),
        pltpu.VMEM((HEADS, CHUNK, DK), F32),
        pltpu.VMEM((CONV_WIDTH, SUBLANES, CONV_DIM), F32),
        pltpu.VMEM((HEADS, DK, DV), F32),
        pltpu.VMEM((rows, CONV_DIM), BF16),
        pltpu.VMEM((rows, 2 * D_MODEL), F32),
        pltpu.VMEM((rows, D_MODEL), F32),
        pltpu.VMEM((rows, D_MODEL), F32),
    ]
    return pl.pallas_call(
        functools.partial(_mixer_prompt_kernel, tiles_per_seq=nt, n_tiles=n_tiles),
        grid=(n_tiles + 1,), in_specs=in_specs, out_specs=out_specs, out_shape=out_shape,
        scratch_shapes=scratch,
        compiler_params=pltpu.CompilerParams(
            dimension_semantics=("arbitrary",),
            vmem_limit_bytes=VMEM_LIMIT_BYTES),
        name="mixer_prompt",
    )(x, cos2, sin2, npre, npost, win, wdw, bdw, lnw, lnb, wco, wro, wo, wup_f32, wdn_f32)


def _ffn_body(h, npre, npost, wup_ref, wdn_ref):
    hn = _rms(h, npre).astype(BF16)
    f = None
    for c in range(FFN_DIM // FFN_CHUNK):
        cols = slice(c * FFN_CHUNK, (c + 1) * FFN_CHUNK)
        up = _dot(hn, wup_ref[:, cols])
        act = jnp.square(jnp.maximum(up, 0.0)).astype(BF16)
        part = _dot(act, wdn_ref[cols, :])
        f = part if f is None else f + part
    return h + _rms(f, npost)


def _sample_state_body(blk, cache_ref, u_ref, wdw_ref, bdw_ref, qt_ref, kt_ref, v_ref, st_ref,
                       conv_ref, nc_ref, o_ref, so_ref):
    n_seq = qt_ref.shape[1]
    u = u_ref[...]
    acc = u * wdw_ref[HALO] + bdw_ref[...]
    for j in range(HALO):
        acc = acc + cache_ref[j] * wdw_ref[j]
    conv_ref[...] = acc
    nc_ref[0:HALO - 1] = cache_ref[1:HALO]
    nc_ref[HALO - 1] = u

    shift = (n_seq - blk * SAMPLE_BLOCK) % n_seq
    qt = pltpu.roll(qt_ref[...], shift, 1)
    kt = pltpu.roll(kt_ref[...], shift, 1)
    for s in range(SAMPLE_BLOCK):
        for h in range(HEADS):
            k_col = kt[h * DK:(h + 1) * DK, s:s + 1]
            q_col = qt[h * DK:(h + 1) * DK, s:s + 1]
            v_row = v_ref[s:s + 1, h * DV:(h + 1) * DV]
            new_state = GAMMA[h] * st_ref[s, h] + k_col * v_row
            so_ref[s, h] = new_state
            o_ref[s:s + 1, h * DV:(h + 1) * DV] = jnp.sum(new_state * q_col, axis=0,
                                                          keepdims=True)


def _ffn_state_kernel(h_ref, npre_ref, npost_ref, wup_ref, wdn_ref,
                      cache_ref, u_ref, wdw_ref, bdw_ref, qt_ref, kt_ref, v_ref, st_ref,
                      y_ref, conv_ref, nc_ref, o_ref, so_ref, *, steps_per_block):
    y_ref[...] = _ffn_body(h_ref[...], npre_ref[...], npost_ref[...], wup_ref, wdn_ref)
    _sample_state_body(pl.program_id(0) // steps_per_block, cache_ref, u_ref, wdw_ref, bdw_ref,
                       qt_ref, kt_ref, v_ref, st_ref, conv_ref, nc_ref, o_ref, so_ref)


def _ffn_state(h, npre, npost, wup, wdn, cache, u, wdw, bdw, qt, kt, v, state):
    n = h.shape[0]
    rows = FFN_ROWS
    steps = n // rows
    n_seq = u.shape[0]
    sb = SAMPLE_BLOCK
    spb = steps // (n_seq // sb)
    assert spb * (n_seq // sb) == steps
    in_specs = [
        pl.BlockSpec((rows, D_MODEL), lambda i: (i, 0)),
        _const_spec((1, D_MODEL)),
        _const_spec((1, D_MODEL)),
        _const_spec((D_MODEL, FFN_DIM)),
        _const_spec((FFN_DIM, D_MODEL)),
        pl.BlockSpec((HALO, sb, CONV_DIM), lambda i: (0, i // spb, 0)),
        pl.BlockSpec((sb, CONV_DIM), lambda i: (i // spb, 0)),
        _const_spec((CONV_WIDTH, 1, CONV_DIM)),
        _const_spec((1, CONV_DIM)),
        _const_spec((QK_W, n_seq)),
        _const_spec((QK_W, n_seq)),
        pl.BlockSpec((sb, V_W), lambda i: (i // spb, 0)),
        pl.BlockSpec((sb, HEADS, DK, DV), lambda i: (i // spb, 0, 0, 0)),
    ]
    out_specs = [
        pl.BlockSpec((rows, D_MODEL), lambda i: (i, 0)),
        pl.BlockSpec((sb, CONV_DIM), lambda i: (i // spb, 0)),
        pl.BlockSpec((HALO, sb, CONV_DIM), lambda i: (0, i // spb, 0)),
        pl.BlockSpec((sb, V_W), lambda i: (i // spb, 0)),
        pl.BlockSpec((sb, HEADS, DK, DV), lambda i: (i // spb, 0, 0, 0)),
    ]
    out_shape = [
        jax.ShapeDtypeStruct((n, D_MODEL), F32),
        jax.ShapeDtypeStruct((n_seq, CONV_DIM), F32),
        jax.ShapeDtypeStruct((HALO, n_seq, CONV_DIM), F32),
        jax.ShapeDtypeStruct((n_seq, V_W), F32),
        jax.ShapeDtypeStruct((n_seq, HEADS, DK, DV), F32),
    ]
    return pl.pallas_call(
        functools.partial(_ffn_state_kernel, steps_per_block=spb),
        grid=(steps,), in_specs=in_specs, out_specs=out_specs, out_shape=out_shape,
        compiler_params=pltpu.CompilerParams(
            dimension_semantics=("arbitrary",),
            vmem_limit_bytes=VMEM_LIMIT_BYTES),
        name="ffn_prompt_sample_state",
    )(h, npre, npost, wup, wdn, cache, u, wdw, bdw, qt, kt, v, state)


def _sample_pre_kernel(x_ref, cos_ref, sin_ref, npre_ref, win_hbm, wco_ref, wro_ref, wo_ref,
                       wbf_ref, wco_bf_ref, wro_bf_ref, wo_bf_ref,
                       u_ref, qt_ref, kt_ref, v_ref, g_ref, ga_ref, gb_ref,
                       xn_ref, ring_ref, sem_ref):
    j = pl.program_id(0)
    steps = pl.num_programs(0)
    ahead = PRE_RING - 1

    def slab_copy(s, slot):
        cols = pl.ds(pl.multiple_of(s * PRE_COLS, PRE_COLS), PRE_COLS)
        return pltpu.make_async_copy(win_hbm.at[:, cols], ring_ref.at[slot], sem_ref.at[slot])

    @pl.when(j == 0)
    def _prime():
        for s in range(ahead):
            slab_copy(s, s).start()
        xn_ref[...] = _rms(x_ref[:, 0, :], npre_ref[...]).astype(BF16)

    @pl.when(j + ahead < steps)
    def _prefetch():
        slab_copy(j + ahead, (j + ahead) % PRE_RING).start()

    wco_bf_ref[...] = wco_ref[...].astype(BF16)
    wro_bf_ref[...] = wro_ref[...].astype(BF16)
    wo_bf_ref[...] = wo_ref[...].astype(BF16)

    slot = j % PRE_RING
    slab_copy(j, slot).wait()
    w_slab = ring_ref[slot].astype(BF16)
    wbf_ref[...] = w_slab
    p = _dot(xn_ref[...], w_slab)

    @pl.when(j == C_CONV // PRE_COLS)
    def _glu():
        u_ref[...] = p[:, :CONV_DIM] * jax.nn.sigmoid(p[:, CONV_DIM:])

    @pl.when(j == C_Q // PRE_COLS)
    def _rotary():
        qkt = p.T
        cos_t = cos_ref[...]
        sin_t = sin_ref[...]
        half = DK // 2
        for g in range(2 * HEADS):
            x1 = qkt[g * DK:g * DK + half]
            x2 = qkt[g * DK + half:(g + 1) * DK]
            o1 = x1 * cos_t - x2 * sin_t
            o2 = x2 * cos_t + x1 * sin_t
            if g < HEADS:
                qt_ref[g * DK:g * DK + half, :] = o1
                qt_ref[g * DK + half:(g + 1) * DK, :] = o2
            else:
                k0 = (g - HEADS) * DK
                kt_ref[k0:k0 + half, :] = o1 * K_SCALE
                kt_ref[k0 + half:k0 + DK, :] = o2 * K_SCALE

    for out_ref, col in ((v_ref, C_V), (g_ref, C_G), (ga_ref, C_GATES),
                         (gb_ref, C_GATES + D_MODEL)):
        @pl.when(j == col // PRE_COLS)
        def _store(out_ref=out_ref):
            out_ref[...] = p


def _sample_pre(x, cos_t, sin_t, npre, win_f32, wco_f32, wro_f32, wo_f32):
    n = x.shape[0]
    assert all(c % PRE_COLS == 0 for c in (C_CONV, C_Q, C_V, C_G, C_GATES, IN_COLS))
    assert 2 * CONV_DIM == 2 * QK_W == V_W == D_MODEL == PRE_COLS
    steps = IN_COLS // PRE_COLS
    assert steps >= CAST_SLABS

    def whole(shape):
        zeros = (0,) * len(shape)
        return pl.BlockSpec(shape, lambda j: zeros)

    def row_slab(rows):
        return pl.BlockSpec((rows // CAST_SLABS, D_MODEL),
                            lambda j: (jnp.minimum(j, CAST_SLABS - 1), 0))

    side = [row_slab(CONV_DIM), row_slab(V_W), row_slab(D_MODEL)]
    in_specs = [
        whole((n, 1, D_MODEL)),
        whole((DK // 2, n)),
        whole((DK // 2, n)),
        whole((1, D_MODEL)),
        pl.BlockSpec(memory_space=pl.ANY),
    ] + side
    out_specs = [
        pl.BlockSpec((D_MODEL, PRE_COLS), lambda j: (0, j)),
    ] + side + [
        whole((n, CONV_DIM)),
        whole((QK_W, n)),
        whole((QK_W, n)),
        whole((n, V_W)),
        whole((n, V_W)),
        whole((n, D_MODEL)),
        whole((n, D_MODEL)),
    ]
    out_shape = [
        jax.ShapeDtypeStruct((D_MODEL, IN_COLS), BF16),
        jax.ShapeDtypeStruct((CONV_DIM, D_MODEL), BF16),
        jax.ShapeDtypeStruct((V_W, D_MODEL), BF16),
        jax.ShapeDtypeStruct((D_MODEL, D_MODEL), BF16),
        jax.ShapeDtypeStruct((n, CONV_DIM), F32),
        jax.ShapeDtypeStruct((QK_W, n), F32),
        jax.ShapeDtypeStruct((QK_W, n), F32),
        jax.ShapeDtypeStruct((n, V_W), F32),
        jax.ShapeDtypeStruct((n, V_W), F32),
        jax.ShapeDtypeStruct((n, D_MODEL), F32),
        jax.ShapeDtypeStruct((n, D_MODEL), F32),
    ]
    return pl.pallas_call(
        _sample_pre_kernel,
        grid=(steps,), in_specs=in_specs, out_specs=out_specs, out_shape=out_shape,
        scratch_shapes=[
            pltpu.VMEM((n, D_MODEL), BF16),
            pltpu.VMEM((PRE_RING, D_MODEL, PRE_COLS), F32),
            pltpu.SemaphoreType.DMA((PRE_RING,)),
        ],
        compiler_params=pltpu.CompilerParams(
            dimension_semantics=("arbitrary",),
            vmem_limit_bytes=VMEM_LIMIT_BYTES),
        name="sample_pre_cast_weights",
    )(x, cos_t, sin_t, npre, win_f32, wco_f32, wro_f32, wo_f32)


def _sample_post_kernel(x_ref, conv_ref, o_ref, g_ref, ga_ref, gb_ref, lnw_ref, lnb_ref,
                        wco_ref, wro_ref, wo_ref, npost_ref, nfpre_ref, nfpost_ref,
                        wup_ref, wdn_ref, y_ref, hres_ref, hn_ref, f_ref):
    c = pl.program_id(0)

    @pl.when(c == 0)
    def _merge():
        a_act = _silu(_layer_norm(conv_ref[...], lnw_ref[...], lnb_ref[...])).astype(BF16)
        a_out = _dot(a_act, wco_ref[...])
        b_out = jnp.zeros_like(a_out)
        for h in range(HEADS):
            cols = slice(h * DV, (h + 1) * DV)
            yh = (_silu(g_ref[:, cols]) * _group_norm(o_ref[:, cols])).astype(BF16)
            b_out = b_out + _dot(yh, wro_ref[cols, :])
        merged = jax.nn.sigmoid(ga_ref[...]) * a_out + jax.nn.sigmoid(gb_ref[...]) * b_out
        m = _dot(merged.astype(BF16), wo_ref[...])
        hres = x_ref[:, 0, :] + _rms(m, npost_ref[...])
        hres_ref[...] = hres
        hn_ref[...] = _rms(hres, nfpre_ref[...]).astype(BF16)
        f_ref[...] = jnp.zeros(f_ref.shape, F32)

    up = _dot(hn_ref[...], wup_ref[...])
    act = jnp.square(jnp.maximum(up, 0.0)).astype(BF16)
    f_ref[...] += _dot(act, wdn_ref[...])

    @pl.when(c == pl.num_programs(0) - 1)
    def _finish():
        y_ref[:, 0, :] = hres_ref[...] + _rms(f_ref[...], nfpost_ref[...])


def _sample_post(x, conv, o, g, ga, gb, lnw, lnb, wco, wro, wo, npost, nfpre, nfpost, wup, wdn):
    n = x.shape[0]

    def whole(a):
        zeros = (0,) * a.ndim
        return pl.BlockSpec(a.shape, lambda c: zeros)

    resident = (x, conv, o, g, ga, gb, lnw, lnb, wco, wro, wo, npost, nfpre, nfpost)
    in_specs = [whole(a) for a in resident] + [
        pl.BlockSpec((D_MODEL, FFN_CHUNK), lambda c: (0, c)),
        pl.BlockSpec((FFN_CHUNK, D_MODEL), lambda c: (c, 0)),
    ]
    return pl.pallas_call(
        _sample_post_kernel,
        grid=(FFN_DIM // FFN_CHUNK,), in_specs=in_specs,
        out_specs=pl.BlockSpec(x.shape, lambda c: (0, 0, 0)),
        out_shape=jax.ShapeDtypeStruct(x.shape, F32),
        scratch_shapes=[
            pltpu.VMEM((n, D_MODEL), F32),
            pltpu.VMEM((n, D_MODEL), BF16),
            pltpu.VMEM((n, D_MODEL), F32),
        ],
        compiler_params=pltpu.CompilerParams(
            dimension_semantics=("arbitrary",),
            vmem_limit_bytes=VMEM_LIMIT_BYTES),
        name="sample_post",
    )(x, conv, o, g, ga, gb, lnw, lnb, wco, wro, wo, npost, nfpre, nfpost, wup, wdn)


def _rope_angles(pos):
    half = DK // 2
    freqs = 1.0 / (ROPE_BASE ** jnp.linspace(0.0, 1.0, half, dtype=F32))
    return pos[:, None] * freqs[None, :]


def kernel(x_prompt, x_sample, cache_conv, state_ret, norm_mix_pre, norm_mix_post, w_in, w_dw, b_dw, conv_ln_w, conv_ln_b, w_conv_out, w_ret_out, w_o, norm_ffn_pre, norm_ffn_post, w_ffn_up, w_ffn_down):
    batch, seq, _ = x_prompt.shape
    n_seq, dec_seq, _ = x_sample.shape
    depth = w_in.shape[0]
    assert dec_seq == 1 and seq % MIX_ROWS == 0 and (batch * seq) % FFN_ROWS == 0
    assert n_seq % SAMPLE_BLOCK == 0

    assert seq % ROPE_FINE == 0
    def both_halves(t):
        return jnp.concatenate([t, t], axis=-1)

    ang_a = both_halves(_rope_angles(ROPE_FINE * jnp.arange(seq // ROPE_FINE, dtype=F32)))
    ang_b = both_halves(_rope_angles(jnp.arange(ROPE_FINE, dtype=F32)))
    cos_a, sin_a = jnp.cos(ang_a)[:, None, :], jnp.sin(ang_a)[:, None, :]
    cos_b, sin_b = jnp.cos(ang_b)[None, :, :], jnp.sin(ang_b)[None, :, :]
    sign = jnp.where(jnp.arange(DK) < DK // 2, -1.0, 1.0).astype(F32)
    cos_p = (cos_a * cos_b - sin_a * sin_b).reshape(seq, DK)
    sin_p = ((sin_a * cos_b + cos_a * sin_b) * sign).reshape(seq, DK)
    ang_s = _rope_angles(PAST_LEN + jnp.arange(dec_seq, dtype=F32))
    cos_s = jnp.broadcast_to(jnp.cos(ang_s).T, (DK // 2, n_seq))
    sin_s = jnp.broadcast_to(jnp.sin(ang_s).T, (DK // 2, n_seq))

    xp = x_prompt
    xs = x_sample
    conv_p, ret_p, conv_s, ret_s = [], [], [], []
    for l in range(depth):
        npre = norm_mix_pre[l][None]
        npost = norm_mix_post[l][None]
        nfpre = norm_ffn_pre[l][None]
        nfpost = norm_ffn_post[l][None]
        wdw = jnp.transpose(w_dw, (1, 0, 2))[:, l:l + 1, :]
        bdw = b_dw[l][None]
        lnw = conv_ln_w[l][None]
        lnb = conv_ln_b[l][None]

        win, wco, wro, wo, u, qt, kt, v, g, ga, gb = _sample_pre(
            xs, cos_s, sin_s, npre, w_in[l], w_conv_out[l], w_ret_out[l], w_o[l])
        cache_t = jnp.transpose(cache_conv[l], (1, 0, 2))
        h_p, nc_p, s_p, wup, wdn = _mixer_prompt(
            xp, cos_p, sin_p, npre, npost, win, wdw, bdw, lnw, lnb, wco, wro, wo,
            w_ffn_up[l], w_ffn_down[l])
        y_p, conv, nc_t, o, s_s = _ffn_state(h_p.reshape(batch * seq, D_MODEL), nfpre, nfpost,
                                             wup, wdn, cache_t, u, wdw, bdw, qt, kt, v,
                                             state_ret[l])
        xp = y_p.reshape(batch, seq, D_MODEL)
        nc_s = jnp.transpose(nc_t, (1, 0, 2))
        xs = _sample_post(xs, conv, o, g, ga, gb, lnw, lnb, wco, wro, wo, npost, nfpre, nfpost,
                          wup, wdn)
        conv_p.append(jnp.transpose(nc_p, (1, 0, 2)))
        ret_p.append(s_p)
        conv_s.append(nc_s)
        ret_s.append(s_s)

    return (xp, xs, jnp.stack(conv_p), jnp.stack(ret_p),
            jnp.stack(conv_s), jnp.stack(ret_s))
```

```python
import functools
import math

import jax
import jax.numpy as jnp
from jax import lax
from jax.experimental import pallas as pl
from jax.experimental.pallas import tpu as pltpu

F32 = jnp.float32
BF16 = jnp.bfloat16

D_MODEL = 1024
CONV_DIM = 512
CONV_WIDTH = 31
HALO = CONV_WIDTH - 1
HEADS = 4
DK = 128
DV = 256
QK_W = HEADS * DK
V_W = HEADS * DV
CHUNK = 128
FFN_DIM = 4 * D_MODEL
EPS = 1e-6
ROPE_BASE = 10000.0
PAST_LEN = 16384
K_SCALE = DK ** -0.5

C_CONV = 0
C_Q = 2 * CONV_DIM
C_K = C_Q + QK_W
C_V = C_K + QK_W
C_G = C_V + V_W
C_GATES = C_G + V_W
IN_COLS = C_GATES + 2 * D_MODEL

LOG_GAMMA = tuple(math.log1p(-(2.0 ** (-5 - h))) for h in range(HEADS))
GAMMA = tuple(math.exp(lg) for lg in LOG_GAMMA)
GAMMA_CHUNK = tuple(math.exp(CHUNK * lg) for lg in LOG_GAMMA)

VMEM_LIMIT_BYTES = 61 * 1024 * 1024
SUBLANES = 8
HALO_PAD = 32
MIX_ROWS = 512
FFN_ROWS = 1024
FFN_CHUNK = 1024
CONV_TILE = 256
CONV_ROWS = 32
SAMPLE_BLOCK = 8
PRE_COLS = 1024
PRE_RING = 3
CAST_SLABS = 4
ROPE_FINE = 128

NT_DIMS = (((1,), (1,)), ((), ()))
TN_DIMS = (((0,), (0,)), ((), ()))


def _dot(a, b):
    return jnp.dot(a, b, preferred_element_type=F32)


def _rms(x, w):
    return x * lax.rsqrt(jnp.mean(x * x, axis=-1, keepdims=True) + EPS) * w


def _layer_norm(x, w, b):
    mu = jnp.mean(x, axis=-1, keepdims=True)
    xc = x - mu
    return xc * lax.rsqrt(jnp.mean(xc * xc, axis=-1, keepdims=True) + EPS) * w + b


def _silu(x):
    return x * jax.nn.sigmoid(x)


def _group_norm(o):
    return o * lax.rsqrt(jnp.mean(o * o, axis=-1, keepdims=True) + EPS)


def _const_spec(shape):
    zeros = (0,) * len(shape)
    return pl.BlockSpec(shape, lambda *_: zeros, pipeline_mode=pl.Buffered(1))


def _merge_out(aact_ref, gate_ref, bout_ref, xprev_ref, wco_ref, wo_ref, npost_ref, mid=None):
    a_out = _dot(aact_ref[...], wco_ref[...])
    mid_out = mid() if mid is not None else None
    merged = (jax.nn.sigmoid(gate_ref[:, :D_MODEL]) * a_out
              + jax.nn.sigmoid(gate_ref[:, D_MODEL:]) * bout_ref[...])
    m = _dot(merged.astype(BF16), wo_ref[...])
    return xprev_ref[...] + _rms(m, npost_ref[...]), mid_out


def _mixer_prompt_kernel(x_ref, cos_ref, sin_ref, npre_ref, npost_ref, win_ref, wdw_ref,
                         bdw_ref, lnw_ref, lnb_ref, wco_ref, wro_ref, wo_ref, wup_ref, wdn_ref,
                         h_ref, nc_ref, sout_ref, wup_bf_ref, wdn_bf_ref,
                         fb_ref, u_ref, y_ref, dec_ref, qd_ref, kd_ref, wb_ref, s_ref,
                         aact_ref, gate_ref, bout_ref, xprev_ref, *, tiles_per_seq, n_tiles):
    g = pl.program_id(0)

    @pl.when(g < n_tiles)
    def _pipelined_step():
        wup_bf_ref[...] = wup_ref[...].astype(BF16)
        wdn_bf_ref[...] = wdn_ref[...].astype(BF16)
        _mixer_step(x_ref, cos_ref, sin_ref, npre_ref, npost_ref, win_ref, wdw_ref,
                    bdw_ref, lnw_ref, lnb_ref, wco_ref, wro_ref, wo_ref,
                    h_ref, nc_ref, sout_ref,
                    fb_ref, u_ref, y_ref, dec_ref, qd_ref, kd_ref, wb_ref, s_ref,
                    aact_ref, gate_ref, bout_ref, xprev_ref, tiles_per_seq=tiles_per_seq)

    @pl.when(g == n_tiles)
    def _drain():
        h_ref[0], _ = _merge_out(aact_ref, gate_ref, bout_ref, xprev_ref, wco_ref, wo_ref,
                                 npost_ref)


def _mixer_step(x_ref, cos_ref, sin_ref, npre_ref, npost_ref, win_ref, wdw_ref,
                bdw_ref, lnw_ref, lnb_ref, wco_ref, wro_ref, wo_ref,
                h_ref, nc_ref, sout_ref,
                fb_ref, u_ref, y_ref, dec_ref, qd_ref, kd_ref, wb_ref, s_ref,
                aact_ref, gate_ref, bout_ref, xprev_ref, *, tiles_per_seq):
    g = pl.program_id(0)
    i = g % tiles_per_seq
    rows = x_ref.shape[1]

    @pl.when(g == 0)
    def _init_tables():
        aact_ref[...] = jnp.zeros(aact_ref.shape, BF16)
        gate_ref[...] = jnp.zeros(gate_ref.shape, F32)
        bout_ref[...] = jnp.zeros(bout_ref.shape, F32)
        xprev_ref[...] = jnp.zeros(xprev_ref.shape, F32)
        ii = lax.broadcasted_iota(jnp.int32, (CHUNK, CHUNK), 0)
        jj = lax.broadcasted_iota(jnp.int32, (CHUNK, CHUNK), 1)
        diff = (ii - jj).astype(F32)
        row_k = lax.broadcasted_iota(jnp.int32, (CHUNK, DK), 0).astype(F32)
        for h in range(HEADS):
            lg = LOG_GAMMA[h]
            dec_ref[h] = jnp.where(diff >= 0.0, jnp.exp(jnp.maximum(diff, 0.0) * lg), 0.0)
            qd_ref[h] = jnp.exp((row_k + 1.0) * lg)
            kd_ref[h] = jnp.exp((CHUNK - 1.0 - row_k) * lg)
        for j in range(CONV_WIDTH):
            wb_ref[j] = jnp.broadcast_to(wdw_ref[j], (SUBLANES, CONV_DIM))

    @pl.when(i == 0)
    def _start_sequence():
        fb_ref[0, 0:HALO_PAD, :] = jnp.zeros((HALO_PAD, CONV_DIM), F32)
        s_ref[...] = jnp.zeros(s_ref.shape, F32)

    x = x_ref[0]
    xn = _rms(x, npre_ref[...]).astype(BF16)
    h_ref[0], pc = _merge_out(
        aact_ref, gate_ref, bout_ref, xprev_ref, wco_ref, wo_ref, npost_ref,
        mid=lambda: _dot(xn, win_ref[:, C_CONV:C_CONV + 2 * CONV_DIM]))

    xprev_ref[...] = x

    u_ref[...] = pc[:, :CONV_DIM] * jax.nn.sigmoid(pc[:, CONV_DIM:])

    pqk = _dot(xn, win_ref[:, C_Q:C_Q + 2 * QK_W])
    pv = _dot(xn, win_ref[:, C_V:C_V + V_W]).astype(BF16)
    pg = _dot(xn, win_ref[:, C_G:C_G + V_W])

    first = HALO_PAD - HALO
    span = CONV_TILE + HALO_PAD - SUBLANES
    for t in range(rows // CONV_TILE):
        fb_ref[0, HALO_PAD:HALO_PAD + CONV_TILE, :] = u_ref[t * CONV_TILE:(t + 1) * CONV_TILE, :]
        for r in range(1, SUBLANES):
            fb_ref[r, 0:span, :] = fb_ref[0, r:r + span, :]
        for rb in range(CONV_TILE // CONV_ROWS):
            acc = jnp.broadcast_to(bdw_ref[...], (CONV_ROWS, CONV_DIM))
            for j in range(CONV_WIDTH):
                off = first + j
                base = rb * CONV_ROWS + off - off % SUBLANES
                slab = fb_ref[off % SUBLANES, base:base + CONV_ROWS, :]
                acc = acc + (slab.reshape(CONV_ROWS // SUBLANES, SUBLANES, CONV_DIM)
                             * wb_ref[j][None]).reshape(CONV_ROWS, CONV_DIM)
            r0 = t * CONV_TILE + rb * CONV_ROWS
            aact_ref[r0:r0 + CONV_ROWS, :] = _silu(
                _layer_norm(acc, lnw_ref[...], lnb_ref[...])).astype(BF16)
        fb_ref[0, 0:HALO_PAD, :] = fb_ref[0, CONV_TILE:CONV_TILE + HALO_PAD, :]

    cos2 = cos_ref[...]
    sin2 = sin_ref[...]

    def rot(t):
        return t * cos2 + pltpu.roll(t, DK // 2, 1) * sin2

    heads = range(HEADS)
    q_rot = [rot(pqk[:, h * DK:(h + 1) * DK]) for h in heads]
    q_bf = [q.astype(BF16) for q in q_rot]
    k_rot = [rot(pqk[:, QK_W + h * DK:QK_W + (h + 1) * DK]) * K_SCALE for h in heads]
    k_bf = [k.astype(BF16) for k in k_rot]
    g_act = [_silu(pg[:, h * DV:(h + 1) * DV]) for h in heads]

    n_chunks = rows // CHUNK
    n_fill = 2 * n_chunks
    gate_cols = 2 * D_MODEL // n_fill

    def gate_slice(f):
        gate_ref[:, f * gate_cols:(f + 1) * gate_cols] = _dot(
            xn, win_ref[:, C_GATES + f * gate_cols:C_GATES + (f + 1) * gate_cols])

    for c in range(n_chunks):
        rs = slice(c * CHUNK, (c + 1) * CHUNK)
        v_c = [pv[rs, h * DV:(h + 1) * DV] for h in heads]
        state = [s_ref[h] for h in heads]
        scores = []
        zeros = jnp.zeros((CHUNK, DK), BF16)
        for a in range(0, HEADS, 2):
            q_pair = jnp.concatenate([q_bf[a][rs], q_bf[a + 1][rs]], axis=1)
            k_pair = jnp.concatenate(
                [jnp.concatenate([k_bf[a][rs], zeros], axis=1),
                 jnp.concatenate([zeros, k_bf[a + 1][rs]], axis=1)], axis=0)
            s_pair = lax.dot_general(q_pair, k_pair, NT_DIMS, preferred_element_type=F32)
            scores += [s_pair[:, :CHUNK], s_pair[:, CHUNK:]]
        kv = [lax.dot_general((k_rot[h][rs] * kd_ref[h]).astype(BF16), v_c[h], TN_DIMS,
                              preferred_element_type=F32) for h in heads]
        gate_slice(2 * c)
        lhs = [jnp.concatenate([(scores[h] * dec_ref[h]).astype(BF16),
                                (q_rot[h][rs] * qd_ref[h]).astype(BF16)], axis=1)
               for h in heads]
        rhs = [jnp.concatenate([v_c[h], state[h].astype(BF16)], axis=0) for h in heads]
        out = [_dot(lhs[h], rhs[h]) for h in heads]
        gate_slice(2 * c + 1)
        for h in heads:
            s_ref[h] = GAMMA_CHUNK[h] * state[h] + kv[h]
            y_ref[rs, h * DV:(h + 1) * DV] = (g_act[h][rs] * _group_norm(out[h])).astype(BF16)
    bout_ref[...] = _dot(y_ref[...], wro_ref[...])

    @pl.when(i == tiles_per_seq - 1)
    def _emit_sequence_state():
        seq = g // tiles_per_seq
        for j in range(HALO):
            nc_ref[j, pl.ds(seq, 1), :] = fb_ref[0, HALO_PAD - HALO + j:HALO_PAD - HALO + j + 1, :]
        sout_ref[0] = s_ref[...]


def _mixer_prompt(x, cos2, sin2, npre, npost, win, wdw, bdw, lnw, lnb, wco, wro, wo,
                  wup_f32, wdn_f32):
    batch, seq, _ = x.shape
    rows = MIX_ROWS
    nt = seq // rows
    n_tiles = batch * nt
    assert D_MODEL % n_tiles == 0 and FFN_DIM % n_tiles == 0
    up_rows, dn_rows = D_MODEL // n_tiles, FFN_DIM // n_tiles

    def front(g):
        return jnp.minimum(g, n_tiles - 1)

    def back(g):
        return jnp.maximum(g - 1, 0)

    side = [pl.BlockSpec((up_rows, FFN_DIM), lambda g: (front(g), 0)),
            pl.BlockSpec((dn_rows, D_MODEL), lambda g: (front(g), 0))]

    in_specs = [
        pl.BlockSpec((1, rows, D_MODEL), lambda g: (front(g) // nt, front(g) % nt, 0)),
        pl.BlockSpec((rows, DK), lambda g: (front(g) % nt, 0)),
        pl.BlockSpec((rows, DK), lambda g: (front(g) % nt, 0)),
        _const_spec((1, D_MODEL)),
        _const_spec((1, D_MODEL)),
        _const_spec((D_MODEL, IN_COLS)),
        _const_spec((CONV_WIDTH, 1, CONV_DIM)),
        _const_spec((1, CONV_DIM)),
        _const_spec((1, CONV_DIM)),
        _const_spec((1, CONV_DIM)),
        _const_spec((CONV_DIM, D_MODEL)),
        _const_spec((V_W, D_MODEL)),
        _const_spec((D_MODEL, D_MODEL)),
    ] + side
    out_specs = [
        pl.BlockSpec((1, rows, D_MODEL), lambda g: (back(g) // nt, back(g) % nt, 0)),
        pl.BlockSpec((HALO, batch, CONV_DIM), lambda g: (0, 0, 0)),
        pl.BlockSpec((1, HEADS, DK, DV), lambda g: (front(g) // nt, 0, 0, 0)),
    ] + side
    out_shape = [
        jax.ShapeDtypeStruct((batch, seq, D_MODEL), F32),
        jax.ShapeDtypeStruct((HALO, batch, CONV_DIM), F32),
        jax.ShapeDtypeStruct((batch, HEADS, DK, DV), F32),
        jax.ShapeDtypeStruct((D_MODEL, FFN_DIM), BF16),
        jax.ShapeDtypeStruct((FFN_DIM, D_MODEL), BF16),
    ]
    scratch = [
        pltpu.VMEM((SUBLANES, HALO_PAD + CONV_TILE, CONV_DIM), F32),
        pltpu.VMEM((rows, CONV_DIM), F32),
        pltpu.VMEM((rows, V_W), BF16),
        pltpu.VMEM((HEADS, CHUNK, CHUNK), F32),
        pltpu.VMEM((HEADS, CHUNK, DK), F32),
        pltpu.VMEM((HEADS, CHUNK, DK), F32),
        pltpu.VMEM((CONV_WIDTH, SUBLANES, CONV_DIM), F32),
        pltpu.VMEM((HEADS, DK, DV), F32),
        pltpu.VMEM((rows, CONV_DIM), BF16),
        pltpu.VMEM((rows, 2 * D_MODEL), F32),
        pltpu.VMEM((rows, D_MODEL), F32),
        pltpu.VMEM((rows, D_MODEL), F32),
    ]
    return pl.pallas_call(
        functools.partial(_mixer_prompt_kernel, tiles_per_seq=nt, n_tiles=n_tiles),
        grid=(n_tiles + 1,), in_specs=in_specs, out_specs=out_specs, out_shape=out_shape,
        scratch_shapes=scratch,
        compiler_params=pltpu.CompilerParams(
            dimension_semantics=("arbitrary",),
            vmem_limit_bytes=VMEM_LIMIT_BYTES),
        name="mixer_prompt",
    )(x, cos2, sin2, npre, npost, win, wdw, bdw, lnw, lnb, wco, wro, wo, wup_f32, wdn_f32)


def _ffn_body(h, npre, npost, wup_ref, wdn_ref):
    hn = _rms(h, npre).astype(BF16)
    f = None
    for c in range(FFN_DIM // FFN_CHUNK):
        cols = slice(c * FFN_CHUNK, (c + 1) * FFN_CHUNK)
        up = _dot(hn, wup_ref[:, cols])
        act = jnp.square(jnp.maximum(up, 0.0)).astype(BF16)
        part = _dot(act, wdn_ref[cols, :])
        f = part if f is None else f + part
    return h + _rms(f, npost)


def _sample_state_body(blk, cache_ref, u_ref, wdw_ref, bdw_ref, qt_ref, kt_ref, v_ref, st_ref,
                       conv_ref, nc_ref, o_ref, so_ref):
    n_seq = qt_ref.shape[1]
    u = u_ref[...]
    acc = u * wdw_ref[HALO] + bdw_ref[...]
    for j in range(HALO):
        acc = acc + cache_ref[j] * wdw_ref[j]
    conv_ref[...] = acc
    nc_ref[0:HALO - 1] = cache_ref[1:HALO]
    nc_ref[HALO - 1] = u

    shift = (n_seq - blk * SAMPLE_BLOCK) % n_seq
    qt = pltpu.roll(qt_ref[...], shift, 1)
    kt = pltpu.roll(kt_ref[...], shift, 1)
    for s in range(SAMPLE_BLOCK):
        for h in range(HEADS):
            k_col = kt[h * DK:(h + 1) * DK, s:s + 1]
            q_col = qt[h * DK:(h + 1) * DK, s:s + 1]
            v_row = v_ref[s:s + 1, h * DV:(h + 1) * DV]
            new_state = GAMMA[h] * st_ref[s, h] + k_col * v_row
            so_ref[s, h] = new_state
            o_ref[s:s + 1, h * DV:(h + 1) * DV] = jnp.sum(new_state * q_col, axis=0,
                                                          keepdims=True)


def _ffn_state_kernel(h_ref, npre_ref, npost_ref, wup_ref, wdn_ref,
                      cache_ref, u_ref, wdw_ref, bdw_ref, qt_ref, kt_ref, v_ref, st_ref,
                      y_ref, conv_ref, nc_ref, o_ref, so_ref, *, steps_per_block):
    y_ref[...] = _ffn_body(h_ref[...], npre_ref[...], npost_ref[...], wup_ref, wdn_ref)
    _sample_state_body(pl.program_id(0) // steps_per_block, cache_ref, u_ref, wdw_ref, bdw_ref,
                       qt_ref, kt_ref, v_ref, st_ref, conv_ref, nc_ref, o_ref, so_ref)


def _ffn_state(h, npre, npost, wup, wdn, cache, u, wdw, bdw, qt, kt, v, state):
    n = h.shape[0]
    rows = FFN_ROWS
    steps = n // rows
    n_seq = u.shape[0]
    sb = SAMPLE_BLOCK
    spb = steps // (n_seq // sb)
    assert spb * (n_seq // sb) == steps
    in_specs = [
        pl.BlockSpec((rows, D_MODEL), lambda i: (i, 0)),
        _const_spec((1, D_MODEL)),
        _const_spec((1, D_MODEL)),
        _const_spec((D_MODEL, FFN_DIM)),
        _const_spec((FFN_DIM, D_MODEL)),
        pl.BlockSpec((HALO, sb, CONV_DIM), lambda i: (0, i // spb, 0)),
        pl.BlockSpec((sb, CONV_DIM), lambda i: (i // spb, 0)),
        _const_spec((CONV_WIDTH, 1, CONV_DIM)),
        _const_spec((1, CONV_DIM)),
        _const_spec((QK_W, n_seq)),
        _const_spec((QK_W, n_seq)),
        pl.BlockSpec((sb, V_W), lambda i: (i // spb, 0)),
        pl.BlockSpec((sb, HEADS, DK, DV), lambda i: (i // spb, 0, 0, 0)),
    ]
    out_specs = [
        pl.BlockSpec((rows, D_MODEL), lambda i: (i, 0)),
        pl.BlockSpec((sb, CONV_DIM), lambda i: (i // spb, 0)),
        pl.BlockSpec((HALO, sb, CONV_DIM), lambda i: (0, i // spb, 0)),
        pl.BlockSpec((sb, V_W), lambda i: (i // spb, 0)),
        pl.BlockSpec((sb, HEADS, DK, DV), lambda i: (i // spb, 0, 0, 0)),
    ]
    out_shape = [
        jax.ShapeDtypeStruct((n, D_MODEL), F32),
        jax.ShapeDtypeStruct((n_seq, CONV_DIM), F32),
        jax.ShapeDtypeStruct((HALO, n_seq, CONV_DIM), F32),
        jax.ShapeDtypeStruct((n_seq, V_W), F32),
        jax.ShapeDtypeStruct((n_seq, HEADS, DK, DV), F32),
    ]
    return pl.pallas_call(
        functools.partial(_ffn_state_kernel, steps_per_block=spb),
        grid=(steps,), in_specs=in_specs, out_specs=out_specs, out_shape=out_shape,
        compiler_params=pltpu.CompilerParams(
            dimension_semantics=("arbitrary",),
            vmem_limit_bytes=VMEM_LIMIT_BYTES),
        name="ffn_prompt_sample_state",
    )(h, npre, npost, wup, wdn, cache, u, wdw, bdw, qt, kt, v, state)


def _sample_pre_kernel(x_ref, cos_ref, sin_ref, npre_ref, win_hbm, wco_ref, wro_ref, wo_ref,
                       wbf_ref, wco_bf_ref, wro_bf_ref, wo_bf_ref,
                       u_ref, qt_ref, kt_ref, v_ref, g_ref, ga_ref, gb_ref,
                       xn_ref, ring_ref, sem_ref):
    j = pl.program_id(0)
    steps = pl.num_programs(0)

    def slab_copy(s, slot):
        cols = pl.ds(pl.multiple_of(s * PRE_COLS, PRE_COLS), PRE_COLS)
        return pltpu.make_async_copy(win_hbm.at[:, cols], ring_ref.at[slot], sem_ref.at[slot])

    @pl.when(j == 0)
    def _prime():
        slab_copy(0, 0).start()
        slab_copy(1, 1).start()
        xn_ref[...] = _rms(x_ref[:, 0, :], npre_ref[...]).astype(BF16)

    wco_bf_ref[...] = wco_ref[...].astype(BF16)
    wro_bf_ref[...] = wro_ref[...].astype(BF16)
    wo_bf_ref[...] = wo_ref[...].astype(BF16)

    @pl.when(j + 2 < steps)
    def _prefetch():
        slab_copy(j + 2, (j + 2) % PRE_RING).start()

    slot = j % PRE_RING
    slab_copy(j, slot).wait()
    w_slab = ring_ref[slot].astype(BF16)
    wbf_ref[...] = w_slab
    p = _dot(xn_ref[...], w_slab)

    @pl.when(j == C_CONV // PRE_COLS)
    def _glu():
        u_ref[...] = p[:, :CONV_DIM] * jax.nn.sigmoid(p[:, CONV_DIM:])

    @pl.when(j == C_Q // PRE_COLS)
    def _rotary():
        qkt = p.T
        cos_t = cos_ref[...]
        sin_t = sin_ref[...]
        half = DK // 2
        for g in range(2 * HEADS):
            x1 = qkt[g * DK:g * DK + half]
            x2 = qkt[g * DK + half:(g + 1) * DK]
            o1 = x1 * cos_t - x2 * sin_t
            o2 = x2 * cos_t + x1 * sin_t
            if g < HEADS:
                qt_ref[g * DK:g * DK + half, :] = o1
                qt_ref[g * DK + half:(g + 1) * DK, :] = o2
            else:
                k0 = (g - HEADS) * DK
                kt_ref[k0:k0 + half, :] = o1 * K_SCALE
                kt_ref[k0 + half:k0 + DK, :] = o2 * K_SCALE

    for out_ref, col in ((v_ref, C_V), (g_ref, C_G), (ga_ref, C_GATES),
                         (gb_ref, C_GATES + D_MODEL)):
        @pl.when(j == col // PRE_COLS)
        def _store(out_ref=out_ref):
            out_ref[...] = p


def _sample_pre(x, cos_t, sin_t, npre, win_f32, wco_f32, wro_f32, wo_f32):
    n = x.shape[0]
    assert all(c % PRE_COLS == 0 for c in (C_CONV, C_Q, C_V, C_G, C_GATES, IN_COLS))
    assert 2 * CONV_DIM == 2 * QK_W == V_W == D_MODEL == PRE_COLS
    steps = IN_COLS // PRE_COLS
    assert steps >= CAST_SLABS

    def whole(shape):
        zeros = (0,) * len(shape)
        return pl.BlockSpec(shape, lambda j: zeros)

    def row_slab(rows):
        return pl.BlockSpec((rows // CAST_SLABS, D_MODEL),
                            lambda j: (jnp.minimum(j, CAST_SLABS - 1), 0))

    side = [row_slab(CONV_DIM), row_slab(V_W), row_slab(D_MODEL)]
    in_specs = [
        whole((n, 1, D_MODEL)),
        whole((DK // 2, n)),
        whole((DK // 2, n)),
        whole((1, D_MODEL)),
        pl.BlockSpec(memory_space=pl.ANY),
    ] + side
    out_specs = [
        pl.BlockSpec((D_MODEL, PRE_COLS), lambda j: (0, j)),
    ] + side + [
        whole((n, CONV_DIM)),
        whole((QK_W, n)),
        whole((QK_W, n)),
        whole((n, V_W)),
        whole((n, V_W)),
        whole((n, D_MODEL)),
        whole((n, D_MODEL)),
    ]
    out_shape = [
        jax.ShapeDtypeStruct((D_MODEL, IN_COLS), BF16),
        jax.ShapeDtypeStruct((CONV_DIM, D_MODEL), BF16),
        jax.ShapeDtypeStruct((V_W, D_MODEL), BF16),
        jax.ShapeDtypeStruct((D_MODEL, D_MODEL), BF16),
        jax.ShapeDtypeStruct((n, CONV_DIM), F32),
        jax.ShapeDtypeStruct((QK_W, n), F32),
        jax.ShapeDtypeStruct((QK_W, n), F32),
        jax.ShapeDtypeStruct((n, V_W), F32),
        jax.ShapeDtypeStruct((n, V_W), F32),
        jax.ShapeDtypeStruct((n, D_MODEL), F32),
        jax.ShapeDtypeStruct((n, D_MODEL), F32),
    ]
    return pl.pallas_call(
        _sample_pre_kernel,
        grid=(steps,), in_specs=in_specs, out_specs=out_specs, out_shape=out_shape,
        scratch_shapes=[
            pltpu.VMEM((n, D_MODEL), BF16),
            pltpu.VMEM((PRE_RING, D_MODEL, PRE_COLS), F32),
            pltpu.SemaphoreType.DMA((PRE_RING,)),
        ],
        compiler_params=pltpu.CompilerParams(
            dimension_semantics=("arbitrary",),
            vmem_limit_bytes=VMEM_LIMIT_BYTES),
        name="sample_pre_cast_weights",
    )(x, cos_t, sin_t, npre, win_f32, wco_f32, wro_f32, wo_f32)


def _sample_post_kernel(x_ref, conv_ref, o_ref, g_ref, ga_ref, gb_ref, lnw_ref, lnb_ref,
                        wco_ref, wro_ref, wo_ref, npost_ref, nfpre_ref, nfpost_ref,
                        wup_ref, wdn_ref, y_ref, hres_ref, hn_ref, f_ref):
    c = pl.program_id(0)

    @pl.when(c == 0)
    def _merge():
        a_act = _silu(_layer_norm(conv_ref[...], lnw_ref[...], lnb_ref[...])).astype(BF16)
        a_out = _dot(a_act, wco_ref[...])
        b_out = jnp.zeros_like(a_out)
        for h in range(HEADS):
            cols = slice(h * DV, (h + 1) * DV)
            yh = (_silu(g_ref[:, cols]) * _group_norm(o_ref[:, cols])).astype(BF16)
            b_out = b_out + _dot(yh, wro_ref[cols, :])
        merged = jax.nn.sigmoid(ga_ref[...]) * a_out + jax.nn.sigmoid(gb_ref[...]) * b_out
        m = _dot(merged.astype(BF16), wo_ref[...])
        hres = x_ref[:, 0, :] + _rms(m, npost_ref[...])
        hres_ref[...] = hres
        hn_ref[...] = _rms(hres, nfpre_ref[...]).astype(BF16)
        f_ref[...] = jnp.zeros(f_ref.shape, F32)

    up = _dot(hn_ref[...], wup_ref[...])
    act = jnp.square(jnp.maximum(up, 0.0)).astype(BF16)
    f_ref[...] += _dot(act, wdn_ref[...])

    @pl.when(c == pl.num_programs(0) - 1)
    def _finish():
        y_ref[:, 0, :] = hres_ref[...] + _rms(f_ref[...], nfpost_ref[...])


def _sample_post(x, conv, o, g, ga, gb, lnw, lnb, wco, wro, wo, npost, nfpre, nfpost, wup, wdn):
    n = x.shape[0]

    def whole(a):
        zeros = (0,) * a.ndim
        return pl.BlockSpec(a.shape, lambda c: zeros)

    resident = (x, conv, o, g, ga, gb, lnw, lnb, wco, wro, wo, npost, nfpre, nfpost)
    in_specs = [whole(a) for a in resident] + [
        pl.BlockSpec((D_MODEL, FFN_CHUNK), lambda c: (0, c)),
        pl.BlockSpec((FFN_CHUNK, D_MODEL), lambda c: (c, 0)),
    ]
    return pl.pallas_call(
        _sample_post_kernel,
        grid=(FFN_DIM // FFN_CHUNK,), in_specs=in_specs,
        out_specs=pl.BlockSpec(x.shape, lambda c: (0, 0, 0)),
        out_shape=jax.ShapeDtypeStruct(x.shape, F32),
        scratch_shapes=[
            pltpu.VMEM((n, D_MODEL), F32),
            pltpu.VMEM((n, D_MODEL), BF16),
            pltpu.VMEM((n, D_MODEL), F32),
        ],
        compiler_params=pltpu.CompilerParams(
            dimension_semantics=("arbitrary",),
            vmem_limit_bytes=VMEM_LIMIT_BYTES),
        name="sample_post",
    )(x, conv, o, g, ga, gb, lnw, lnb, wco, wro, wo, npost, nfpre, nfpost, wup, wdn)


def _rope_angles(pos):
    half = DK // 2
    freqs = 1.0 / (ROPE_BASE ** jnp.linspace(0.0, 1.0, half, dtype=F32))
    return pos[:, None] * freqs[None, :]


def kernel(x_prompt, x_sample, cache_conv, state_ret, norm_mix_pre, norm_mix_post, w_in, w_dw, b_dw, conv_ln_w, conv_ln_b, w_conv_out, w_ret_out, w_o, norm_ffn_pre, norm_ffn_post, w_ffn_up, w_ffn_down):
    batch, seq, _ = x_prompt.shape
    n_seq, dec_seq, _ = x_sample.shape
    depth = w_in.shape[0]
    assert dec_seq == 1 and seq % MIX_ROWS == 0 and (batch * seq) % FFN_ROWS == 0
    assert n_seq % SAMPLE_BLOCK == 0

    assert seq % ROPE_FINE == 0
    def both_halves(t):
        return jnp.concatenate([t, t], axis=-1)

    ang_a = both_halves(_rope_angles(ROPE_FINE * jnp.arange(seq // ROPE_FINE, dtype=F32)))
    ang_b = both_halves(_rope_angles(jnp.arange(ROPE_FINE, dtype=F32)))
    cos_a, sin_a = jnp.cos(ang_a)[:, None, :], jnp.sin(ang_a)[:, None, :]
    cos_b, sin_b = jnp.cos(ang_b)[None, :, :], jnp.sin(ang_b)[None, :, :]
    sign = jnp.where(jnp.arange(DK) < DK // 2, -1.0, 1.0).astype(F32)
    cos_p = (cos_a * cos_b - sin_a * sin_b).reshape(seq, DK)
    sin_p = ((sin_a * cos_b + cos_a * sin_b) * sign).reshape(seq, DK)
    ang_s = _rope_angles(PAST_LEN + jnp.arange(dec_seq, dtype=F32))
    cos_s = jnp.broadcast_to(jnp.cos(ang_s).T, (DK // 2, n_seq))
    sin_s = jnp.broadcast_to(jnp.sin(ang_s).T, (DK // 2, n_seq))

    xp = x_prompt
    xs = x_sample
    conv_p, ret_p, conv_s, ret_s = [], [], [], []
    for l in range(depth):
        npre = norm_mix_pre[l][None]
        npost = norm_mix_post[l][None]
        nfpre = norm_ffn_pre[l][None]
        nfpost = norm_ffn_post[l][None]
        wdw = jnp.transpose(w_dw, (1, 0, 2))[:, l:l + 1, :]
        bdw = b_dw[l][None]
        lnw = conv_ln_w[l][None]
        lnb = conv_ln_b[l][None]

        win, wco, wro, wo, u, qt, kt, v, g, ga, gb = _sample_pre(
            xs, cos_s, sin_s, npre, w_in[l], w_conv_out[l], w_ret_out[l], w_o[l])
        cache_t = jnp.transpose(cache_conv[l], (1, 0, 2))
        h_p, nc_p, s_p, wup, wdn = _mixer_prompt(
            xp, cos_p, sin_p, npre, npost, win, wdw, bdw, lnw, lnb, wco, wro, wo,
            w_ffn_up[l], w_ffn_down[l])
        y_p, conv, nc_t, o, s_s = _ffn_state(h_p.reshape(batch * seq, D_MODEL), nfpre, nfpost,
                                             wup, wdn, cache_t, u, wdw, bdw, qt, kt, v,
                                             state_ret[l])
        xp = y_p.reshape(batch, seq, D_MODEL)
        nc_s = jnp.transpose(nc_t, (1, 0, 2))
        xs = _sample_post(xs, conv, o, g, ga, gb, lnw, lnb, wco, wro, wo, npost, nfpre, nfpost,
                          wup, wdn)
        conv_p.append(jnp.transpose(nc_p, (1, 0, 2)))
        ret_p.append(s_p)
        conv_s.append(nc_s)
        ret_s.append(s_s)

    return (xp, xs, jnp.stack(conv_p), jnp.stack(ret_p),
            jnp.stack(conv_s), jnp.stack(ret_s))
```
